```python
import math
import jax, jax.numpy as jnp
from jax import lax
import numpy as np

D_MODEL = 2048
BATCH = 2
SEQ = 4096
DEPTH = 2
DEC_BATCH = 128
DEC_SEQ = 8
PAST_LEN = 8192
PAGE_SIZE = 128

N_EVEN = (DEPTH + 1) // 2
N_ODD = DEPTH // 2

SWA_HEADS = 16
SWA_KV_HEADS = 4
SWA_HEAD_DIM = 64
SWA_GROUP = SWA_HEADS // SWA_KV_HEADS
WINDOW = 128
ROPE_THETA = 500000.0
ROPE_DIM = SWA_HEAD_DIM // 4

GMLP_GROUPS = 4
GMLP_WIDTH = 1024
GMLP_GROUP_DIM = GMLP_WIDTH // GMLP_GROUPS
GMLP_CHUNK = 128

SWA_Q_W = SWA_HEADS * SWA_HEAD_DIM
SWA_KV_W = SWA_KV_HEADS * SWA_HEAD_DIM
AB_IN = SWA_Q_W + 2 * SWA_KV_W + 2 * GMLP_WIDTH
AB_OUT = SWA_Q_W + GMLP_WIDTH

RET_HEADS = 8
RET_KEY_DIM = D_MODEL // RET_HEADS
RET_VAL_DIM = 2 * RET_KEY_DIM
RET_CHUNK = 128
RET_THETA = 10000.0
RET_IN = 2 * RET_HEADS * RET_KEY_DIM + 2 * RET_HEADS * RET_VAL_DIM

MEM_LEN = 256
MEM_HEADS = 4
MEM_HEAD_DIM = 128
MEM_W = MEM_HEADS * MEM_HEAD_DIM

D_FF = 4 * D_MODEL

EPS = 1e-6
NEG_INF = -1e30

kernel_name = "hybrid_swa_gmlp_retention_decode_step"


def rms_norm(x, g):
    xf = x.astype(jnp.float32)
    y = xf * lax.rsqrt(jnp.mean(xf * xf, axis=-1, keepdims=True) + EPS)
    return (y * g.astype(jnp.float32)).astype(x.dtype)


def layer_norm(x, g, b):
    xf = x.astype(jnp.float32)
    mu = jnp.mean(xf, axis=-1, keepdims=True)
    xc = xf - mu
    y = xc * lax.rsqrt(jnp.mean(xc * xc, axis=-1, keepdims=True) + EPS)
    return (y * g.astype(jnp.float32) + b.astype(jnp.float32)).astype(x.dtype)


def rope(x, pos, rot_dim, theta):
    half = rot_dim // 2
    inv = theta ** (-jnp.arange(half, dtype=jnp.float32) / half)
    ang = pos.astype(jnp.float32)[:, None] * inv[None, :]
    cos = jnp.cos(ang)[:, None, :]
    sin = jnp.sin(ang)[:, None, :]
    xf = x[..., :rot_dim].astype(jnp.float32)
    x1, x2 = xf[..., :half], xf[..., half:]
    rot = jnp.concatenate([x1 * cos - x2 * sin, x2 * cos + x1 * sin], axis=-1).astype(x.dtype)
    return jnp.concatenate([rot, x[..., rot_dim:]], axis=-1)


def sink_softmax(scores, sinks):
    m = jnp.maximum(jnp.max(scores, axis=-1, keepdims=True), sinks)
    p = jnp.exp(scores - m)
    return p / (jnp.sum(p, axis=-1, keepdims=True) + jnp.exp(sinks - m))


def ab_project(h, pos, w_in, ln_g, ln_b):
    z = h @ w_in
    q, k, v, zu, zv = jnp.split(z, [SWA_Q_W, SWA_Q_W + SWA_KV_W, SWA_Q_W + 2 * SWA_KV_W,
                                   SWA_Q_W + 2 * SWA_KV_W + GMLP_WIDTH], axis=-1)
    lead = h.shape[:-1]
    q = rope(q.reshape(*lead, SWA_HEADS, SWA_HEAD_DIM), pos, ROPE_DIM, ROPE_THETA)
    k = rope(k.reshape(*lead, SWA_KV_HEADS, SWA_HEAD_DIM), pos, ROPE_DIM, ROPE_THETA)
    v = v.reshape(*lead, SWA_KV_HEADS, SWA_HEAD_DIM)
    zu = jax.nn.gelu(zu)
    zv = layer_norm(jax.nn.gelu(zv), ln_g, ln_b)
    return q, k, v, zu, zv


def swa_banded(q, k, v, sinks):
    b, s = q.shape[:2]
    nb = s // WINDOW
    qb = q.reshape(b, nb, WINDOW, SWA_KV_HEADS, SWA_GROUP, SWA_HEAD_DIM)
    kb = k.reshape(b, nb, WINDOW, SWA_KV_HEADS, SWA_HEAD_DIM)
    vb = v.reshape(b, nb, WINDOW, SWA_KV_HEADS, SWA_HEAD_DIM)
    pad = ((0, 0), (1, 0), (0, 0), (0, 0), (0, 0))
    kcat = jnp.concatenate([jnp.pad(kb, pad)[:, :-1], kb], axis=2)
    vcat = jnp.concatenate([jnp.pad(vb, pad)[:, :-1], vb], axis=2)
    scores = jnp.einsum('bnqkgd,bnskd->bnkgqs', qb, kcat,
                        preferred_element_type=jnp.float32) * (SWA_HEAD_DIM ** -0.5)
    t = jnp.arange(WINDOW)[:, None]
    sk = jnp.arange(2 * WINDOW)[None, :]
    band = (sk <= t + WINDOW) & (sk > t)
    not_pad = (jnp.arange(nb)[:, None, None] > 0) | (sk[None] >= WINDOW)
    valid = band[None] & not_pad
    scores = jnp.where(valid[None, :, None, None], scores, NEG_INF)
    probs = sink_softmax(scores, sinks.astype(jnp.float32).reshape(SWA_KV_HEADS, SWA_GROUP)[:, :, None, None])
    o = jnp.einsum('bnkgqs,bnskd->bnqkgd', probs.astype(v.dtype), vcat)
    return o.reshape(b, s, SWA_HEADS, SWA_HEAD_DIM)


def swa_with_buffer(q, k, v, buf_k, buf_v, sinks):
    n, t = q.shape[:2]
    kall = jnp.concatenate([buf_k.astype(k.dtype), k], axis=1)
    vall = jnp.concatenate([buf_v.astype(v.dtype), v], axis=1)
    qg = q.reshape(n, t, SWA_KV_HEADS, SWA_GROUP, SWA_HEAD_DIM)
    scores = jnp.einsum('ntkgd,nskd->nkgts', qg, kall,
                        preferred_element_type=jnp.float32) * (SWA_HEAD_DIM ** -0.5)
    qpos = PAST_LEN + jnp.arange(t)[:, None]
    kpos = PAST_LEN - WINDOW + jnp.arange(WINDOW + t)[None, :]
    valid = (kpos <= qpos) & (kpos > qpos - WINDOW)
    scores = jnp.where(valid, scores, NEG_INF)
    probs = sink_softmax(scores, sinks.astype(jnp.float32).reshape(SWA_KV_HEADS, SWA_GROUP)[:, :, None, None])
    o = jnp.einsum('nkgts,nskd->ntkgd', probs.astype(vall.dtype), vall)
    return o.reshape(n, t, SWA_HEADS, SWA_HEAD_DIM), kall[:, -WINDOW:], vall[:, -WINDOW:]


def spatial_gate(zu, zv, w_s, b_s):
    l = zv.shape[-3]
    w = jnp.tril(w_s[:, :l, :l])
    mixed = jnp.einsum('gij,...jgd->...igd', w.astype(zv.dtype), zv) + b_s[:, :l].T[:, :, None]
    return zu * mixed


def ab_mixer_prompt(h, pos, w_in, w_out, sinks, ln_g, ln_b, w_s, b_s):
    q, k, v, zu, zv = ab_project(h, pos, w_in, ln_g, ln_b)
    b, s = h.shape[:2]
    attn = swa_banded(q, k, v, sinks).reshape(b, s, SWA_Q_W)
    nc = s // GMLP_CHUNK
    shp = (b, nc, GMLP_CHUNK, GMLP_GROUPS, GMLP_GROUP_DIM)
    gate = spatial_gate(zu.reshape(shp), zv.reshape(shp), w_s, b_s).reshape(b, s, GMLP_WIDTH)
    y = jnp.concatenate([attn, gate], axis=-1) @ w_out
    return y, k[:, -WINDOW:], v[:, -WINDOW:]


def ab_mixer_sample(h, pos, buf_k, buf_v, w_in, w_out, sinks, ln_g, ln_b, w_s, b_s):
    q, k, v, zu, zv = ab_project(h, pos, w_in, ln_g, ln_b)
    n, t = h.shape[:2]
    attn, new_k, new_v = swa_with_buffer(q, k, v, buf_k, buf_v, sinks)
    shp = (n, t, GMLP_GROUPS, GMLP_GROUP_DIM)
    gate = spatial_gate(zu.reshape(shp), zv.reshape(shp), w_s, b_s).reshape(n, t, GMLP_WIDTH)
    y = jnp.concatenate([attn.reshape(n, t, SWA_Q_W), gate], axis=-1) @ w_out
    return y, new_k, new_v, zv


def ret_log_decay():
    return jnp.log1p(-jnp.exp2(-5.0 - jnp.arange(RET_HEADS, dtype=jnp.float32)))


def ret_project(h, pos, w_in):
    z = h @ w_in
    nk = RET_HEADS * RET_KEY_DIM
    nv = RET_HEADS * RET_VAL_DIM
    q, k, v, g = jnp.split(z, [nk, 2 * nk, 2 * nk + nv], axis=-1)
    lead = h.shape[:-1]
    q = rope(q.reshape(*lead, RET_HEADS, RET_KEY_DIM), pos, RET_KEY_DIM, RET_THETA)
    k = rope(k.reshape(*lead, RET_HEADS, RET_KEY_DIM), pos, RET_KEY_DIM, RET_THETA) * (RET_KEY_DIM ** -0.5)
    v = v.reshape(*lead, RET_HEADS, RET_VAL_DIM)
    return q, k, v, g


def retention_chunk(s0, q, k, v):
    l = q.shape[1]
    lg = ret_log_decay()
    idx = jnp.arange(l, dtype=jnp.float32)
    diff = idx[:, None] - idx[None, :]
    dmat = jnp.where(diff >= 0, jnp.exp(lg[:, None, None] * jnp.maximum(diff, 0.0)), 0.0)
    q_dec = jnp.exp(lg[None, :] * (idx[:, None] + 1.0))
    k_dec = jnp.exp(lg[None, :] * (l - 1.0 - idx[:, None]))
    qf, kf, vf = q.astype(jnp.float32), k.astype(jnp.float32), v.astype(jnp.float32)
    att = jnp.einsum('nihd,njhd->nhij', qf, kf) * dmat
    o = (jnp.einsum('nhij,njhe->nihe', att, vf)
         + jnp.einsum('nihd,nhde->nihe', qf * q_dec[:, :, None], s0))
    s1 = (s0 * jnp.exp(lg * l)[:, None, None]
          + jnp.einsum('njhd,njhe->nhde', kf * k_dec[:, :, None], vf))
    return s1, o


def ret_output(o, g, w_out):
    on = o * lax.rsqrt(jnp.mean(o * o, axis=-1, keepdims=True) + EPS)
    lead = g.shape[:-1]
    y = on.reshape(*lead, RET_HEADS * RET_VAL_DIM).astype(g.dtype) * jax.nn.silu(g)
    return y @ w_out


def ret_mixer_prompt(h, pos, w_in, w_out):
    q, k, v, g = ret_project(h, pos, w_in)
    b, s = h.shape[:2]
    nc = s // RET_CHUNK
    def to_chunks(a):
        return a.reshape(b, nc, RET_CHUNK, *a.shape[2:]).swapaxes(0, 1)
    s0 = jnp.zeros((b, RET_HEADS, RET_KEY_DIM, RET_VAL_DIM), jnp.float32)
    s_final, o = lax.scan(lambda st, xs: retention_chunk(st, *xs), s0,
                          (to_chunks(q), to_chunks(k), to_chunks(v)))
    o = o.swapaxes(0, 1).reshape(b, s, RET_HEADS, RET_VAL_DIM)
    return ret_output(o, g, w_out), s_final


def ret_mixer_sample(h, pos, state, w_in, w_out):
    q, k, v, g = ret_project(h, pos, w_in)
    s1, o = retention_chunk(state.astype(jnp.float32), q, k, v)
    return ret_output(o, g, w_out), s1


def mem_kv(mem, g, w_k, w_v):
    m = rms_norm(mem, g)
    lead = mem.shape[:-1]
    return ((m @ w_k).reshape(*lead, MEM_HEADS, MEM_HEAD_DIM),
            (m @ w_v).reshape(*lead, MEM_HEADS, MEM_HEAD_DIM))


def mem_attend(h, w_q, w_o, k, v):
    lead = h.shape[:-1]
    q = (h @ w_q).reshape(*lead, MEM_HEADS, MEM_HEAD_DIM)
    s = jnp.einsum('nshd,nmhd->nhsm', q, k.astype(q.dtype),
                   preferred_element_type=jnp.float32) * (MEM_HEAD_DIM ** -0.5)
    p = jax.nn.softmax(s, axis=-1).astype(q.dtype)
    o = jnp.einsum('nhsm,nmhd->nshd', p, v.astype(q.dtype)).reshape(*lead, MEM_W)
    return o @ w_o


def sq_relu_mlp(h, w_up, w_down):
    a = jax.nn.relu(h @ w_up)
    return (a * a) @ w_down


def setup_inputs(seed: int = 0) -> dict:
    key = jax.random.key(seed)
    keys = iter(jax.random.split(key, 48))

    def nrm(shape, scale):
        return jax.random.normal(next(keys), shape, jnp.float32) * scale

    def gain(shape):
        return 1.0 + nrm(shape, 0.05)

    d = D_MODEL
    return {
        'x_prompt': nrm((BATCH, SEQ, d), 1.0),
        'x_sample': nrm((DEC_BATCH, DEC_SEQ, d), 1.0),
        'cache_swa_k': nrm((N_EVEN, DEC_BATCH, WINDOW, SWA_KV_HEADS, SWA_HEAD_DIM), 1.0),
        'cache_swa_v': nrm((N_EVEN, DEC_BATCH, WINDOW, SWA_KV_HEADS, SWA_HEAD_DIM), 1.0),
        'state_ret': nrm((N_ODD, DEC_BATCH, RET_HEADS, RET_KEY_DIM, RET_VAL_DIM), 0.1),
        'cache_mem_k': nrm((DEPTH, DEC_BATCH, MEM_LEN, MEM_HEADS, MEM_HEAD_DIM), 1.0),
        'cache_mem_v': nrm((DEPTH, DEC_BATCH, MEM_LEN, MEM_HEADS, MEM_HEAD_DIM), 1.0),
        'mem_prompt': nrm((BATCH, MEM_LEN, d), 1.0),
        'norm_mix_pre': gain((DEPTH, d)),
        'norm_mix_post': gain((DEPTH, d)),
        'norm_mem': gain((DEPTH, d)),
        'norm_x_pre': gain((DEPTH, d)),
        'norm_x_post': gain((DEPTH, d)),
        'norm_ffn_pre': gain((DEPTH, d)),
        'norm_ffn_post': gain((DEPTH, d)),
        'w_ab_in': nrm((N_EVEN, d, AB_IN), d ** -0.5),
        'w_ab_out': nrm((N_EVEN, AB_OUT, d), AB_OUT ** -0.5),
        'swa_sinks': nrm((N_EVEN, SWA_HEADS), 0.5),
        'gmlp_ln_g': gain((N_EVEN, GMLP_WIDTH)),
        'gmlp_ln_b': nrm((N_EVEN, GMLP_WIDTH), 0.02),
        'gmlp_w_s': nrm((N_EVEN, GMLP_GROUPS, GMLP_CHUNK, GMLP_CHUNK), GMLP_CHUNK ** -0.5),
        'gmlp_b_s': gain((N_EVEN, GMLP_GROUPS, GMLP_CHUNK)),
        'w_ret_in': nrm((N_ODD, d, RET_IN), d ** -0.5),
        'w_ret_out': nrm((N_ODD, RET_HEADS * RET_VAL_DIM, d), (RET_HEADS * RET_VAL_DIM) ** -0.5),
        'w_mem_q': nrm((DEPTH, d, MEM_W), d ** -0.5),
        'w_mem_k': nrm((DEPTH, d, MEM_W), d ** -0.5),
        'w_mem_v': nrm((DEPTH, d, MEM_W), d ** -0.5),
        'w_mem_o': nrm((DEPTH, MEM_W, d), MEM_W ** -0.5),
        'w_ffn_up': nrm((DEPTH, d, D_FF), d ** -0.5),
        'w_ffn_down': nrm((DEPTH, D_FF, d), D_FF ** -0.5),
    }


def reference(x_prompt, x_sample, cache_swa_k, cache_swa_v, state_ret, cache_mem_k, cache_mem_v,
              mem_prompt, norm_mix_pre, norm_mix_post, norm_mem, norm_x_pre, norm_x_post,
              norm_ffn_pre, norm_ffn_post, w_ab_in, w_ab_out, swa_sinks, gmlp_ln_g, gmlp_ln_b,
              gmlp_w_s, gmlp_b_s, w_ret_in, w_ret_out, w_mem_q, w_mem_k, w_mem_v, w_mem_o,
              w_ffn_up, w_ffn_down):
    pos_p = jnp.arange(x_prompt.shape[1], dtype=jnp.int32)
    pos_s = PAST_LEN + jnp.arange(x_sample.shape[1], dtype=jnp.int32)
    xp, xs = x_prompt, x_sample
    swa_k_p, swa_v_p, swa_k_s, swa_v_s, gmlp_v_s = [], [], [], [], []
    ret_p, ret_s, mem_k_p, mem_v_p = [], [], [], []
    for layer in range(DEPTH):
        j = layer // 2
        hp = rms_norm(xp, norm_mix_pre[layer])
        hs = rms_norm(xs, norm_mix_pre[layer])
        if layer % 2 == 0:
            yp, kp_new, vp_new = ab_mixer_prompt(hp, pos_p, w_ab_in[j], w_ab_out[j], swa_sinks[j],
                                                 gmlp_ln_g[j], gmlp_ln_b[j], gmlp_w_s[j], gmlp_b_s[j])
            ys, ks_new, vs_new, gv_new = ab_mixer_sample(hs, pos_s, cache_swa_k[j], cache_swa_v[j],
                                                         w_ab_in[j], w_ab_out[j], swa_sinks[j],
                                                         gmlp_ln_g[j], gmlp_ln_b[j], gmlp_w_s[j], gmlp_b_s[j])
            swa_k_p.append(kp_new)
            swa_v_p.append(vp_new)
            swa_k_s.append(ks_new)
            swa_v_s.append(vs_new)
            gmlp_v_s.append(gv_new)
        else:
            yp, sp_new = ret_mixer_prompt(hp, pos_p, w_ret_in[j], w_ret_out[j])
            ys, ss_new = ret_mixer_sample(hs, pos_s, state_ret[j], w_ret_in[j], w_ret_out[j])
            ret_p.append(sp_new.astype(state_ret.dtype))
            ret_s.append(ss_new.astype(state_ret.dtype))
        xp = xp + rms_norm(yp, norm_mix_post[layer])
        xs = xs + rms_norm(ys, norm_mix_post[layer])
        mk, mv = mem_kv(mem_prompt, norm_mem[layer], w_mem_k[layer], w_mem_v[layer])
        mem_k_p.append(mk)
        mem_v_p.append(mv)
        xp = xp + rms_norm(mem_attend(rms_norm(xp, norm_x_pre[layer]), w_mem_q[layer], w_mem_o[layer], mk, mv),
                           norm_x_post[layer])
        xs = xs + rms_norm(mem_attend(rms_norm(xs, norm_x_pre[layer]), w_mem_q[layer], w_mem_o[layer],
                                      cache_mem_k[layer], cache_mem_v[layer]), norm_x_post[layer])
        xp = xp + rms_norm(sq_relu_mlp(rms_norm(xp, norm_ffn_pre[layer]), w_ffn_up[layer], w_ffn_down[layer]),
                           norm_ffn_post[layer])
        xs = xs + rms_norm(sq_relu_mlp(rms_norm(xs, norm_ffn_pre[layer]), w_ffn_up[layer], w_ffn_down[layer]),
                           norm_ffn_post[layer])
    return (xp, xs, jnp.stack(swa_k_p), jnp.stack(swa_v_p), jnp.stack(swa_k_s), jnp.stack(swa_v_s),
            jnp.stack(gmlp_v_s), jnp.stack(ret_p), jnp.stack(ret_s), jnp.stack(mem_k_p), jnp.stack(mem_v_p))
```

```python
import functools
import math

import jax
import jax.numpy as jnp
from jax import lax
from jax.experimental import pallas as pl
from jax.experimental.pallas import tpu as pltpu

F32 = jnp.float32
BF16 = jnp.bfloat16

D_MODEL = 2048
BATCH = 2
SEQ = 4096
DEPTH = 2
DEC_BATCH = 128
DEC_SEQ = 8
PAST_LEN = 8192

SWA_HEADS = 16
SWA_KV_HEADS = 4
SWA_HEAD_DIM = 64
SWA_GROUP = SWA_HEADS // SWA_KV_HEADS
WINDOW = 128
ROPE_THETA = 500000.0
ROPE_DIM = SWA_HEAD_DIM // 4
ROPE_HALF = ROPE_DIM // 2

GMLP_GROUPS = 4
GMLP_WIDTH = 1024
GMLP_GROUP_DIM = GMLP_WIDTH // GMLP_GROUPS
GMLP_CHUNK = 128

SWA_Q_W = SWA_HEADS * SWA_HEAD_DIM
SWA_KV_W = SWA_KV_HEADS * SWA_HEAD_DIM
AB_IN = SWA_Q_W + 2 * SWA_KV_W + 2 * GMLP_WIDTH
AB_OUT = SWA_Q_W + GMLP_WIDTH

RET_HEADS = 8
RET_KEY_DIM = D_MODEL // RET_HEADS
RET_VAL_DIM = 2 * RET_KEY_DIM
RET_CHUNK = 128
RET_THETA = 10000.0
RET_QK_W = RET_HEADS * RET_KEY_DIM
RET_V_W = RET_HEADS * RET_VAL_DIM
RET_IN = 2 * RET_QK_W + 2 * RET_V_W

MEM_LEN = 256
MEM_HEADS = 4
MEM_HEAD_DIM = 128
MEM_W = MEM_HEADS * MEM_HEAD_DIM

D_FF = 4 * D_MODEL
EPS = 1e-6
NEG_INF = -1e30

MP = BATCH * SEQ
MS = DEC_BATCH * DEC_SEQ
M = MP + MS
LANES = 128
MIB = 1024 * 1024

TM_IN = 1024
TAB_ROWS = SEQ + MS
TM_OUT = 512


def _cparams(sem, vmem_mib):
    return pltpu.CompilerParams(dimension_semantics=sem, vmem_limit_bytes=vmem_mib * MIB)


def _rms(x, g):
    return x * lax.rsqrt(jnp.mean(x * x, axis=-1, keepdims=True) + EPS) * g


def _residual(y, x, g_post, g_next):
    xn = x + _rms(y, g_post)
    return xn, _rms(xn, g_next).astype(BF16)


def _rope_table_kernel(sc_ref, s1_ref, s2_ref, rc_ref, rs_ref):
    tr = sc_ref.shape[0]
    r = lax.broadcasted_iota(jnp.int32, (tr, LANES), 0) + pl.program_id(0) * tr
    pos = jnp.where(r < SEQ, r, PAST_LEN + ((r - SEQ) & (DEC_SEQ - 1))).astype(F32)
    lane = lax.broadcasted_iota(jnp.int32, (tr, LANES), 1)
    d = lane & (SWA_HEAD_DIM - 1)
    inv = jnp.exp((d & (ROPE_HALF - 1)).astype(F32) * (-math.log(ROPE_THETA) / ROPE_HALF))
    ang = pos * inv
    c, s = jnp.cos(ang), jnp.sin(ang)
    sc_ref[...] = jnp.where(d < ROPE_DIM, c, 1.0)
    s1_ref[...] = jnp.where(d < ROPE_HALF, -s, 0.0)
    s2_ref[...] = jnp.where((d >= ROPE_HALF) & (d < ROPE_DIM), s, 0.0)
    inv_r = jnp.exp(lane.astype(F32) * (-math.log(RET_THETA) / LANES))
    ang_r = pos * inv_r
    rc_ref[...] = jnp.cos(ang_r)
    rs_ref[...] = jnp.sin(ang_r)


def _rope_tables():
    tr = 512
    spec = pl.BlockSpec((tr, LANES), lambda i: (i, 0))
    shp = jax.ShapeDtypeStruct((TAB_ROWS, LANES), F32)
    return pl.pallas_call(
        _rope_table_kernel, grid=(TAB_ROWS // tr,), out_specs=[spec] * 5, out_shape=[shp] * 5,
        compiler_params=_cparams(("arbitrary",), 16), name="rope_tables")()


def _norm_kernel(x_ref, g_ref, h_ref):
    h_ref[...] = _rms(x_ref[...], g_ref[...]).astype(BF16)


def _prenorm(x, g):
    tm = 1024
    return pl.pallas_call(
        _norm_kernel, grid=(M // tm,),
        in_specs=[pl.BlockSpec((tm, D_MODEL), lambda i: (i, 0)),
                  pl.BlockSpec((1, D_MODEL), lambda i: (0, 0))],
        out_specs=pl.BlockSpec((tm, D_MODEL), lambda i: (i, 0)),
        out_shape=jax.ShapeDtypeStruct((M, D_MODEL), BF16),
        compiler_params=_cparams(("arbitrary",), 40), name="prenorm")(x, g)


def _inproj_kernel(h_ref, w_ref, *refs, n_side, epilogue):
    side, outs, wbf_ref = refs[:n_side], refs[n_side:-1], refs[-1]

    @pl.when(pl.program_id(1) == 0)
    def _():
        wbf_ref[...] = w_ref[...].astype(BF16)

    acc = jnp.dot(h_ref[...], wbf_ref[...], preferred_element_type=F32)
    epilogue(acc, side, outs)


def _tab_index(j, i):
    return (jnp.where(i < MP // TM_IN, i % (SEQ // TM_IN), SEQ // TM_IN), 0)


def _inproj(h, w, col0, ncols, tn, epilogue, side, side_specs, out_shapes, out_specs, name, vmem_mib):
    tm = TM_IN
    k = h.shape[1]
    return pl.pallas_call(
        functools.partial(_inproj_kernel, n_side=len(side), epilogue=epilogue),
        grid=(ncols // tn, M // tm),
        in_specs=[pl.BlockSpec((tm, k), lambda j, i: (i, 0)),
                  pl.BlockSpec((k, tn), lambda j, i: (0, col0 // tn + j))] + side_specs,
        out_specs=out_specs, out_shape=out_shapes,
        scratch_shapes=[pltpu.VMEM((k, tn), BF16)],
        compiler_params=_cparams(("arbitrary", "arbitrary"), vmem_mib), name=name)(h, w, *side)


def _swa_rope(x, c, s1, s2):
    return x * c + pltpu.roll(x, LANES - ROPE_HALF, 1) * s1 + pltpu.roll(x, ROPE_HALF, 1) * s2


def _epi_swa_q(acc, side, outs):
    c, s1, s2 = (t[...] for t in side)
    (q_ref,) = outs
    for cc in range(acc.shape[1] // LANES):
        sl = slice(cc * LANES, (cc + 1) * LANES)
        q_ref[:, sl] = (_swa_rope(acc[:, sl], c, s1, s2) * (SWA_HEAD_DIM ** -0.5)).astype(BF16)


def _epi_swa_kv(acc, side, outs):
    c, s1, s2 = (t[...] for t in side)
    (kv_ref,) = outs
    for cc in range(SWA_KV_W // LANES):
        sl = slice(cc * LANES, (cc + 1) * LANES)
        kv_ref[:, sl] = _swa_rope(acc[:, sl], c, s1, s2)
    kv_ref[:, SWA_KV_W:] = acc[:, SWA_KV_W:]


def _epi_gelu(acc, side, outs):
    outs[0][...] = jax.nn.gelu(acc).astype(BF16)


def _epi_gelu_ln(acc, side, outs):
    g_ref, b_ref = side
    zv_ref, zvs_ref = outs
    a = jax.nn.gelu(acc)
    xc = a - jnp.mean(a, axis=-1, keepdims=True)
    y = xc * lax.rsqrt(jnp.mean(xc * xc, axis=-1, keepdims=True) + EPS) * g_ref[...] + b_ref[...]
    zv_ref[...] = y.astype(BF16)

    @pl.when(pl.program_id(1) == MP // TM_IN)
    def _():
        zvs_ref[...] = y


def _epi_ret(acc, side, outs, tn):
    cos, sin = (t[...] for t in side)
    (z_ref,) = outs
    j = pl.program_id(0)
    qk_tiles = 2 * RET_QK_W // tn

    @pl.when(j < qk_tiles)
    def _():
        scale = jnp.where(j < qk_tiles // 2, 1.0, RET_KEY_DIM ** -0.5)
        for hh in range(tn // RET_KEY_DIM):
            c0 = hh * RET_KEY_DIM
            x1, x2 = acc[:, c0:c0 + LANES], acc[:, c0 + LANES:c0 + 2 * LANES]
            z_ref[:, c0:c0 + LANES] = ((x1 * cos - x2 * sin) * scale).astype(BF16)
            z_ref[:, c0 + LANES:c0 + 2 * LANES] = ((x2 * cos + x1 * sin) * scale).astype(BF16)

    @pl.when((j >= qk_tiles) & (j < qk_tiles + RET_V_W // tn))
    def _():
        z_ref[...] = acc.astype(BF16)

    @pl.when(j >= qk_tiles + RET_V_W // tn)
    def _():
        z_ref[...] = (acc / (1.0 + jnp.exp(-acc))).astype(BF16)


def _ab_in_proj(h, w_in, ln_g, ln_b, tabs):
    sc, s1, s2 = tabs
    tab_spec = pl.BlockSpec((TM_IN, LANES), _tab_index)
    row_out = lambda tn: pl.BlockSpec((TM_IN, tn), lambda j, i: (i, j))
    q = _inproj(h, w_in, 0, SWA_Q_W, 512, _epi_swa_q, [sc, s1, s2], [tab_spec] * 3,
                jax.ShapeDtypeStruct((M, SWA_Q_W), BF16), row_out(512), "ab_in_q", 40)
    kv = _inproj(h, w_in, SWA_Q_W, 2 * SWA_KV_W, 512, _epi_swa_kv, [sc, s1, s2], [tab_spec] * 3,
                 jax.ShapeDtypeStruct((M, 2 * SWA_KV_W), F32), row_out(512), "ab_in_kv", 40)
    u = _inproj(h, w_in, SWA_Q_W + 2 * SWA_KV_W, GMLP_WIDTH, 512, _epi_gelu, [], [],
                jax.ShapeDtypeStruct((M, GMLP_WIDTH), BF16), row_out(512), "ab_in_u", 40)
    vec_spec = pl.BlockSpec((1, GMLP_WIDTH), lambda j, i: (0, 0))
    zv, zv_s = _inproj(
        h, w_in[:, AB_IN - GMLP_WIDTH:], 0, GMLP_WIDTH, GMLP_WIDTH, _epi_gelu_ln,
        [ln_g, ln_b], [vec_spec] * 2,
        [jax.ShapeDtypeStruct((M, GMLP_WIDTH), BF16), jax.ShapeDtypeStruct((MS, GMLP_WIDTH), F32)],
        [row_out(GMLP_WIDTH), pl.BlockSpec((MS, GMLP_WIDTH), lambda j, i: (0, 0))], "ab_in_zv", 52)
    return q, kv, u, zv, zv_s


def _ret_in_proj(h, w_in, tabs):
    rc, rs = tabs
    tn = 1024
    tab_spec = pl.BlockSpec((TM_IN, LANES), _tab_index)
    return _inproj(h, w_in, 0, RET_IN, tn, functools.partial(_epi_ret, tn=tn), [rc, rs], [tab_spec] * 2,
                   jax.ShapeDtypeStruct((M, RET_IN), BF16),
                   pl.BlockSpec((TM_IN, tn), lambda j, i: (i, j)), "ret_in", 52)


AB_ROWS = 512


def _sink_softmax_pv(s, sink, v):
    m = jnp.maximum(jnp.max(s, axis=-1, keepdims=True), sink)
    p = jnp.exp(s - m)
    den = jnp.sum(p, axis=-1, keepdims=True) + jnp.exp(sink - m)
    return p, den


def _ab_prompt_kernel(sink_ref, q_ref, kv_ref, kvp_ref, u_ref, zv_ref, ws_ref, bs_ref, cat_ref,
                      kc_ref, vc_ref):
    r = pl.program_id(1)
    kc_ref[:WINDOW, :] = kvp_ref[:, :SWA_KV_W].astype(BF16)
    kc_ref[WINDOW:, :] = kv_ref[:, :SWA_KV_W].astype(BF16)
    vc_ref[:WINDOW, :] = kvp_ref[:, SWA_KV_W:].astype(BF16)
    vc_ref[WINDOW:, :] = kv_ref[:, SWA_KV_W:].astype(BF16)

    rows = SWA_GROUP * WINDOW
    t = lax.broadcasted_iota(jnp.int32, (rows, 2 * WINDOW), 0) & (WINDOW - 1)
    sk = lax.broadcasted_iota(jnp.int32, (rows, 2 * WINDOW), 1)
    band = (sk <= t + WINDOW) & (sk > t)
    tril = (lax.broadcasted_iota(jnp.int32, (GMLP_CHUNK, GMLP_CHUNK), 0)
            >= lax.broadcasted_iota(jnp.int32, (GMLP_CHUNK, GMLP_CHUNK), 1))

    for sb in range(AB_ROWS // WINDOW):
        rs = slice(sb * WINDOW, (sb + 1) * WINDOW)
        valid = band & ((sk >= WINDOW) | (r * (AB_ROWS // WINDOW) + sb > 0))
        qb = q_ref[rs, :]
        for kh in range(SWA_KV_HEADS):
            heads = [kh * SWA_GROUP + g for g in range(SWA_GROUP)]
            qs = jnp.concatenate([qb[:, hd * SWA_HEAD_DIM:(hd + 1) * SWA_HEAD_DIM] for hd in heads], axis=0)
            cs = slice(kh * SWA_HEAD_DIM, (kh + 1) * SWA_HEAD_DIM)
            kk = kc_ref[sb * WINDOW:sb * WINDOW + 2 * WINDOW, cs]
            vv = vc_ref[sb * WINDOW:sb * WINDOW + 2 * WINDOW, cs]
            s = lax.dot_general(qs, kk, (((1,), (1,)), ((), ())), preferred_element_type=F32)
            s = jnp.where(valid, s, NEG_INF)
            sink = jnp.concatenate([jnp.full((WINDOW, 1), sink_ref[hd], F32) for hd in heads], axis=0)
            p, den = _sink_softmax_pv(s, sink, vv)
            o = jnp.dot(p.astype(BF16), vv, preferred_element_type=F32) / den
            ob = jnp.concatenate([o[g * WINDOW:(g + 1) * WINDOW] for g in range(SWA_GROUP)], axis=1)
            cat_ref[rs, kh * SWA_GROUP * SWA_HEAD_DIM:(kh + 1) * SWA_GROUP * SWA_HEAD_DIM] = ob.astype(BF16)
        zb = zv_ref[rs, :]
        ub = u_ref[rs, :]
        for g in range(GMLP_GROUPS):
            gs = slice(g * GMLP_GROUP_DIM, (g + 1) * GMLP_GROUP_DIM)
            w = jnp.where(tril, ws_ref[g], 0.0).astype(BF16)
            mixed = jnp.dot(w, zb[:, gs], preferred_element_type=F32) + bs_ref[g]
            cat_ref[rs, SWA_Q_W + g * GMLP_GROUP_DIM:SWA_Q_W + (g + 1) * GMLP_GROUP_DIM] = (
                ub[:, gs].astype(F32) * mixed).astype(BF16)


def _ab_mix_prompt(q, kv, u, zv, sinks, w_s, b_s):
    nr = SEQ // AB_ROWS
    sub = AB_ROWS // WINDOW
    row = lambda b, r: (b * nr + r, 0)
    return pl.pallas_call(
        _ab_prompt_kernel, grid=(BATCH, nr),
        in_specs=[pl.BlockSpec(memory_space=pltpu.SMEM),
                  pl.BlockSpec((AB_ROWS, SWA_Q_W), row),
                  pl.BlockSpec((AB_ROWS, 2 * SWA_KV_W), row),
                  pl.BlockSpec((WINDOW, 2 * SWA_KV_W), lambda b, r: (jnp.maximum((b * nr + r) * sub - 1, 0), 0)),
                  pl.BlockSpec((AB_ROWS, GMLP_WIDTH), row),
                  pl.BlockSpec((AB_ROWS, GMLP_WIDTH), row),
                  pl.BlockSpec((GMLP_GROUPS, GMLP_CHUNK, GMLP_CHUNK), lambda b, r: (0, 0, 0)),
                  pl.BlockSpec((GMLP_GROUPS, GMLP_CHUNK, 1), lambda b, r: (0, 0, 0))],
        out_specs=pl.BlockSpec((AB_ROWS, AB_OUT), row),
        out_shape=jax.ShapeDtypeStruct((M, AB_OUT), BF16),
        scratch_shapes=[pltpu.VMEM((WINDOW + AB_ROWS, SWA_KV_W), BF16)] * 2,
        compiler_params=_cparams(("arbitrary", "arbitrary"), 40), name="ab_mix_prompt",
    )(sinks, q, kv, kv, u, zv, w_s, b_s.reshape(GMLP_GROUPS, GMLP_CHUNK, 1))


AB_NB = 16
AB_KEYS = 2 * WINDOW


def _ab_sample_kernel(sink_ref, q_ref, kv_ref, u_ref, zv_ref, ck_ref, cv_ref, ws_ref, bs_ref, cat_in_ref,
                      cat_ref, nk_ref, nv_ref):
    del cat_in_ref
    nb, t = AB_NB, DEC_SEQ
    kv3 = kv_ref[...].reshape(nb, t, 2 * SWA_KV_W)
    kn, vn = kv3[:, :, :SWA_KV_W], kv3[:, :, SWA_KV_W:]
    ck, cv = ck_ref[...], cv_ref[...]
    nk_ref[:, :WINDOW - t, :] = ck[:, t:, :]
    nk_ref[:, WINDOW - t:, :] = kn
    nv_ref[:, :WINDOW - t, :] = cv[:, t:, :]
    nv_ref[:, WINDOW - t:, :] = vn
    pad = jnp.zeros((nb, AB_KEYS - WINDOW - t, SWA_KV_W), F32)
    kall = jnp.concatenate([ck, kn, pad], axis=1).astype(BF16)
    vall = jnp.concatenate([cv, vn, pad], axis=1).astype(BF16)

    rows = SWA_GROUP * t
    tq = lax.broadcasted_iota(jnp.int32, (rows, AB_KEYS), 0) & (t - 1)
    sk = lax.broadcasted_iota(jnp.int32, (rows, AB_KEYS), 1)
    valid = ((sk < WINDOW) & (sk > tq)) | ((sk >= WINDOW) & (sk - WINDOW <= tq))

    q3 = q_ref[...].astype(F32).reshape(nb, t, SWA_Q_W)
    pieces = []
    for kh in range(SWA_KV_HEADS):
        heads = [kh * SWA_GROUP + g for g in range(SWA_GROUP)]
        qs = jnp.concatenate([q3[:, :, hd * SWA_HEAD_DIM:(hd + 1) * SWA_HEAD_DIM] for hd in heads],
                             axis=1).astype(BF16)
        cs = slice(kh * SWA_HEAD_DIM, (kh + 1) * SWA_HEAD_DIM)
        s = jnp.einsum('nqd,nkd->nqk', qs, kall[:, :, cs], preferred_element_type=F32)
        s = jnp.where(valid[None], s, NEG_INF)
        sink = jnp.concatenate([jnp.full((t, 1), sink_ref[hd], F32) for hd in heads], axis=0)[None]
        p, den = _sink_softmax_pv(s, sink, None)
        o = jnp.einsum('nqk,nkd->nqd', p.astype(BF16), vall[:, :, cs], preferred_element_type=F32) / den
        pieces += [o[:, g * t:(g + 1) * t, :] for g in range(SWA_GROUP)]
    attn = jnp.concatenate(pieces, axis=2).reshape(nb * t, SWA_Q_W)
    cat_ref[:, :SWA_Q_W] = attn.astype(BF16)

    zv3 = zv_ref[...].reshape(nb, t, GMLP_WIDTH)
    u3 = u_ref[...].astype(F32).reshape(nb, t, GMLP_WIDTH)
    irow = lax.broadcasted_iota(jnp.int32, (t, GMLP_GROUP_DIM), 0)
    for g in range(GMLP_GROUPS):
        gs = slice(g * GMLP_GROUP_DIM, (g + 1) * GMLP_GROUP_DIM)
        zg = zv3[:, :, gs]
        mixed = jnp.zeros((nb, t, GMLP_GROUP_DIM), F32) + bs_ref[g][None]
        for j in range(t):
            wcol = jnp.where(irow >= j, ws_ref[g, j], 0.0)
            mixed = mixed + zg[:, j:j + 1, :] * wcol[None]
        gate = (u3[:, :, gs] * mixed).reshape(nb * t, GMLP_GROUP_DIM)
        cat_ref[:, SWA_Q_W + g * GMLP_GROUP_DIM:SWA_Q_W + (g + 1) * GMLP_GROUP_DIM] = gate.astype(BF16)


def _ab_mix_sample(cat, q, kv, u, zv_s, cache_k, cache_v, sinks, w_s, b_s):
    nb, t = AB_NB, DEC_SEQ
    rows = nb * t
    off = MP // rows
    srow = lambda i: (off + i, 0)
    wt = jnp.broadcast_to(jnp.swapaxes(w_s[:, :t, :t], 1, 2)[..., None], (GMLP_GROUPS, t, t, GMLP_GROUP_DIM))
    bt = jnp.broadcast_to(b_s[:, :t, None], (GMLP_GROUPS, t, GMLP_GROUP_DIM))
    cspec = pl.BlockSpec((nb, WINDOW, SWA_KV_W), lambda i: (i, 0, 0))
    cshape = jax.ShapeDtypeStruct((DEC_BATCH, WINDOW, SWA_KV_W), F32)
    return pl.pallas_call(
        _ab_sample_kernel, grid=(DEC_BATCH // nb,),
        in_specs=[pl.BlockSpec(memory_space=pltpu.SMEM),
                  pl.BlockSpec((rows, SWA_Q_W), srow),
                  pl.BlockSpec((rows, 2 * SWA_KV_W), srow),
                  pl.BlockSpec((rows, GMLP_WIDTH), srow),
                  pl.BlockSpec((rows, GMLP_WIDTH), lambda i: (i, 0)),
                  cspec, cspec,
                  pl.BlockSpec((GMLP_GROUPS, t, t, GMLP_GROUP_DIM), lambda i: (0, 0, 0, 0)),
                  pl.BlockSpec((GMLP_GROUPS, t, GMLP_GROUP_DIM), lambda i: (0, 0, 0)),
                  pl.BlockSpec(memory_space=pl.ANY)],
        out_specs=[pl.BlockSpec((rows, AB_OUT), srow), cspec, cspec],
        out_shape=[jax.ShapeDtypeStruct((M, AB_OUT), BF16), cshape, cshape],
        input_output_aliases={9: 0},
        compiler_params=_cparams(("arbitrary",), 48), name="ab_mix_sample",
    )(sinks, q, kv, u, zv_s, cache_k, cache_v, wt, bt, cat)


RET_ROWS = 1024


def _ret_prompt_kernel(lg_ref, q_ref, k_ref, v_ref, g_ref, y_ref, sfin_ref, s_ref):
    lg = lg_ref[pl.program_id(1)]
    c = pl.program_id(2)
    L = RET_CHUNK

    @pl.when(c == 0)
    def _():
        s_ref[...] = jnp.zeros_like(s_ref)

    diff = (lax.broadcasted_iota(jnp.int32, (L, L), 0) - lax.broadcasted_iota(jnp.int32, (L, L), 1)).astype(F32)
    dmat = jnp.where(diff >= 0, jnp.exp(lg * jnp.maximum(diff, 0.0)), 0.0)
    idx = lax.broadcasted_iota(jnp.int32, (L, 1), 0).astype(F32)
    q_dec = jnp.exp(lg * (idx + 1.0))
    k_dec = jnp.exp(lg * (L - 1.0 - idx))
    s_dec = jnp.exp(lg * jnp.full((1, 1), float(L), F32))

    def chunk(ci, carry):
        rs = pl.ds(pl.multiple_of(ci * L, L), L)
        q, k, v = q_ref[rs, :], k_ref[rs, :], v_ref[rs, :]
        att = lax.dot_general(q, k, (((1,), (1,)), ((), ())), preferred_element_type=F32) * dmat
        s0 = s_ref[...]
        o = (jnp.dot(att.astype(BF16), v, preferred_element_type=F32)
             + jnp.dot(q, s0.astype(BF16), preferred_element_type=F32) * q_dec)
        kd = (k.astype(F32) * k_dec).astype(BF16)
        s_ref[...] = s0 * s_dec + lax.dot_general(kd, v, (((0,), (0,)), ((), ())), preferred_element_type=F32)
        on = o * lax.rsqrt(jnp.mean(o * o, axis=-1, keepdims=True) + EPS)
        y_ref[rs, :] = (on * g_ref[rs, :].astype(F32)).astype(BF16)
        return carry

    lax.fori_loop(0, RET_ROWS // L, chunk, 0)

    @pl.when(c == pl.num_programs(2) - 1)
    def _():
        sfin_ref[...] = s_ref[...]


def _ret_prompt(z, lg):
    nc = SEQ // RET_ROWS
    kb = RET_QK_W // RET_KEY_DIM
    vb = 2 * RET_QK_W // RET_VAL_DIM
    gb = vb + RET_HEADS
    return pl.pallas_call(
        _ret_prompt_kernel, grid=(BATCH, RET_HEADS, nc),
        in_specs=[pl.BlockSpec(memory_space=pltpu.SMEM),
                  pl.BlockSpec((RET_ROWS, RET_KEY_DIM), lambda b, h, c: (b * nc + c, h)),
                  pl.BlockSpec((RET_ROWS, RET_KEY_DIM), lambda b, h, c: (b * nc + c, kb + h)),
                  pl.BlockSpec((RET_ROWS, RET_VAL_DIM), lambda b, h, c: (b * nc + c, vb + h)),
                  pl.BlockSpec((RET_ROWS, RET_VAL_DIM), lambda b, h, c: (b * nc + c, gb + h))],
        out_specs=[pl.BlockSpec((RET_ROWS, RET_VAL_DIM), lambda b, h, c: (b * nc + c, h)),
                   pl.BlockSpec((None, None, RET_KEY_DIM, RET_VAL_DIM), lambda b, h, c: (b, h, 0, 0))],
        out_shape=[jax.ShapeDtypeStruct((M, RET_V_W), BF16),
                   jax.ShapeDtypeStruct((BATCH, RET_HEADS, RET_KEY_DIM, RET_VAL_DIM), F32)],
        scratch_shapes=[pltpu.VMEM((RET_KEY_DIM, RET_VAL_DIM), F32)],
        compiler_params=_cparams(("arbitrary", "arbitrary", "arbitrary"), 32), name="ret_prompt",
    )(lg, z, z, z, z)


RET_NB = 16


def _ret_sample_kernel(lg_ref, q_ref, k_ref, v_ref, g_ref, s0_ref, y_in_ref, y_ref, s1_ref):
    del y_in_ref
    lg = lg_ref[pl.program_id(1)]
    nb, t = RET_NB, DEC_SEQ
    rows = nb * t
    q, k, v = q_ref[...], k_ref[...], v_ref[...]
    ri = lax.broadcasted_iota(jnp.int32, (rows, rows), 0)
    ci = lax.broadcasted_iota(jnp.int32, (rows, rows), 1)
    d = ((ri & (t - 1)) - (ci & (t - 1))).astype(F32)
    same = (ri // t) == (ci // t)
    dmat = jnp.where(same & (d >= 0), jnp.exp(lg * jnp.maximum(d, 0.0)), 0.0)
    att = lax.dot_general(q, k, (((1,), (1,)), ((), ())), preferred_element_type=F32) * dmat
    o_intra = jnp.dot(att.astype(BF16), v, preferred_element_type=F32)

    rcol = lax.broadcasted_iota(jnp.int32, (rows, 1), 0)
    tt = (rcol & (t - 1)).astype(F32)
    q_dec = jnp.exp(lg * (tt + 1.0))
    k_dec = jnp.exp(lg * (t - 1.0 - tt))
    s_dec = jnp.exp(lg * jnp.full((1, 1), float(t), F32))
    kd = (k.astype(F32) * k_dec).astype(BF16)
    qf = q.astype(F32)
    vf = v.astype(F32)
    o_rows = []
    for n in range(nb):
        s0 = s0_ref[n]
        qn = qf[n * t:(n + 1) * t].astype(BF16)
        o_rows.append(jnp.dot(qn, s0.astype(BF16), preferred_element_type=F32))
        vn = jnp.where(rcol // t == n, vf, 0.0).astype(BF16)
        s1_ref[n] = s0 * s_dec + lax.dot_general(kd, vn, (((0,), (0,)), ((), ())), preferred_element_type=F32)
    o = o_intra + jnp.concatenate(o_rows, axis=0) * q_dec
    on = o * lax.rsqrt(jnp.mean(o * o, axis=-1, keepdims=True) + EPS)
    y_ref[...] = (on * g_ref[...].astype(F32)).astype(BF16)


def _ret_sample(y, z, state, lg):
    nb = RET_NB
    rows = nb * DEC_SEQ
    off = MP // rows
    kb = RET_QK_W // RET_KEY_DIM
    vb = 2 * RET_QK_W // RET_VAL_DIM
    gb = vb + RET_HEADS
    sspec = pl.BlockSpec((nb, None, RET_KEY_DIM, RET_VAL_DIM), lambda n, h: (n, h, 0, 0))
    return pl.pallas_call(
        _ret_sample_kernel, grid=(DEC_BATCH // nb, RET_HEADS),
        in_specs=[pl.BlockSpec(memory_space=pltpu.SMEM),
                  pl.BlockSpec((rows, RET_KEY_DIM), lambda n, h: (off + n, h)),
                  pl.BlockSpec((rows, RET_KEY_DIM), lambda n, h: (off + n, kb + h)),
                  pl.BlockSpec((rows, RET_VAL_DIM), lambda n, h: (off + n, vb + h)),
                  pl.BlockSpec((rows, RET_VAL_DIM), lambda n, h: (off + n, gb + h)),
                  sspec,
                  pl.BlockSpec(memory_space=pl.ANY)],
        out_specs=[pl.BlockSpec((rows, RET_VAL_DIM), lambda n, h: (off + n, h)), sspec],
        out_shape=[jax.ShapeDtypeStruct((M, RET_V_W), BF16),
                   jax.ShapeDtypeStruct((DEC_BATCH, RET_HEADS, RET_KEY_DIM, RET_VAL_DIM), F32)],
        input_output_aliases={6: 0},
        compiler_params=_cparams(("arbitrary", "arbitrary"), 52), name="ret_sample",
    )(lg, z, z, z, z, state, y)


def _accumulate(acc_ref, part, k):
    @pl.when(k == 0)
    def _():
        acc_ref[...] = part

    @pl.when(k > 0)
    def _():
        acc_ref[...] += part


def _outproj_kernel(a_ref, w_ref, x_ref, gp_ref, gn_ref, xn_ref, hn_ref):
    k = pl.program_id(1)
    _accumulate(xn_ref, jnp.dot(a_ref[...], w_ref[...], preferred_element_type=F32), k)

    @pl.when(k == pl.num_programs(1) - 1)
    def _():
        xn, hn = _residual(xn_ref[...], x_ref[...], gp_ref[...], gn_ref[...])
        xn_ref[...] = xn
        hn_ref[...] = hn


def _out_proj(a, w_bf, x, g_post, g_next, name):
    tm, tk = TM_OUT, 1024
    kdim = a.shape[1]
    row = pl.BlockSpec((tm, D_MODEL), lambda i, k: (i, 0))
    vec = pl.BlockSpec((1, D_MODEL), lambda i, k: (0, 0))
    return pl.pallas_call(
        _outproj_kernel, grid=(M // tm, kdim // tk),
        in_specs=[pl.BlockSpec((tm, tk), lambda i, k: (i, k)),
                  pl.BlockSpec((tk, D_MODEL), lambda i, k: (k, 0)), row, vec, vec],
        out_specs=[row, row],
        out_shape=[jax.ShapeDtypeStruct((M, D_MODEL), F32), jax.ShapeDtypeStruct((M, D_MODEL), BF16)],
        compiler_params=_cparams(("arbitrary", "arbitrary"), 48), name=name,
    )(a, w_bf, x, g_post, g_next)


def _ffn_kernel(h_ref, wu_ref, wd_ref, x_ref, gp_ref, gn_ref, xn_ref, hn_ref):
    c = pl.program_id(1)
    a = jnp.maximum(jnp.dot(h_ref[...], wu_ref[...], preferred_element_type=F32), 0.0)
    _accumulate(xn_ref, jnp.dot((a * a).astype(BF16), wd_ref[...], preferred_element_type=F32), c)

    @pl.when(c == pl.num_programs(1) - 1)
    def _():
        xn, hn = _residual(xn_ref[...], x_ref[...], gp_ref[...], gn_ref[...])
        xn_ref[...] = xn
        hn_ref[...] = hn


def _ffn(h, wu_bf, wd_bf, x, g_post, g_next):
    tm, tc = TM_OUT, 1024
    row = pl.BlockSpec((tm, D_MODEL), lambda i, c: (i, 0))
    vec = pl.BlockSpec((1, D_MODEL), lambda i, c: (0, 0))
    return pl.pallas_call(
        _ffn_kernel, grid=(M // tm, D_FF // tc),
        in_specs=[row, pl.BlockSpec((D_MODEL, tc), lambda i, c: (0, c)),
                  pl.BlockSpec((tc, D_MODEL), lambda i, c: (c, 0)), row, vec, vec],
        out_specs=[row, row],
        out_shape=[jax.ShapeDtypeStruct((M, D_MODEL), F32), jax.ShapeDtypeStruct((M, D_MODEL), BF16)],
        compiler_params=_cparams(("arbitrary", "arbitrary"), 56), name="ffn",
    )(h, wu_bf, wd_bf, x, g_post, g_next)


def _mem_kv_kernel(m_ref, g_ref, wk_ref, wv_ref, k_ref, v_ref):
    m = _rms(m_ref[...], g_ref[0]).astype(BF16)
    k_ref[0] = jnp.dot(m, wk_ref[0].astype(BF16), preferred_element_type=F32)
    v_ref[0] = jnp.dot(m, wv_ref[0].astype(BF16), preferred_element_type=F32)


def _mem_kv(mem, g, w_k, w_v):
    rows = BATCH * MEM_LEN
    wspec = pl.BlockSpec((1, D_MODEL, MEM_W), lambda l: (l, 0, 0))
    ospec = pl.BlockSpec((1, rows, MEM_W), lambda l: (l, 0, 0))
    oshape = jax.ShapeDtypeStruct((DEPTH, rows, MEM_W), F32)
    return pl.pallas_call(
        _mem_kv_kernel, grid=(DEPTH,),
        in_specs=[pl.BlockSpec((rows, D_MODEL), lambda l: (0, 0)),
                  pl.BlockSpec((1, 1, D_MODEL), lambda l: (l, 0, 0)), wspec, wspec],
        out_specs=[ospec, ospec], out_shape=[oshape, oshape],
        compiler_params=_cparams(("arbitrary",), 48), name="mem_kv",
    )(mem.reshape(rows, D_MODEL), g.reshape(DEPTH, 1, D_MODEL), w_k, w_v)


def _mem_prompt_kernel(h_ref, wq_ref, wo_ref, mk_ref, mv_ref, x_ref, gp_ref, gn_ref, xn_ref, hn_ref):
    q = jnp.dot(h_ref[...], wq_ref[...], preferred_element_type=F32).astype(BF16)
    mk, mv = mk_ref[...].astype(BF16), mv_ref[...].astype(BF16)
    outs = []
    for hh in range(MEM_HEADS):
        cs = slice(hh * MEM_HEAD_DIM, (hh + 1) * MEM_HEAD_DIM)
        s = lax.dot_general(q[:, cs], mk[:, cs], (((1,), (1,)), ((), ())),
                            preferred_element_type=F32) * (MEM_HEAD_DIM ** -0.5)
        p = jnp.exp(s - jnp.max(s, axis=-1, keepdims=True))
        den = jnp.sum(p, axis=-1, keepdims=True)
        outs.append(jnp.dot(p.astype(BF16), mv[:, cs], preferred_element_type=F32) / den)
    o = jnp.concatenate(outs, axis=1).astype(BF16)
    y = jnp.dot(o, wo_ref[...], preferred_element_type=F32)
    xn, hn = _residual(y, x_ref[...], gp_ref[...], gn_ref[...])
    xn_ref[...] = xn
    hn_ref[...] = hn


def _mem_prompt(h, wq_bf, wo_bf, mk, mv, x, g_post, g_next):
    tm = TM_OUT
    per_b = SEQ // tm
    row = pl.BlockSpec((tm, D_MODEL), lambda i: (i, 0))
    vec = pl.BlockSpec((1, D_MODEL), lambda i: (0, 0))
    kvspec = pl.BlockSpec((MEM_LEN, MEM_W), lambda i: (i // per_b, 0))
    return pl.pallas_call(
        _mem_prompt_kernel, grid=(MP // tm,),
        in_specs=[row, pl.BlockSpec((D_MODEL, MEM_W), lambda i: (0, 0)),
                  pl.BlockSpec((MEM_W, D_MODEL), lambda i: (0, 0)), kvspec, kvspec, row, vec, vec],
        out_specs=[row, row],
        out_shape=[jax.ShapeDtypeStruct((M, D_MODEL), F32), jax.ShapeDtypeStruct((M, D_MODEL), BF16)],
        input_output_aliases={5: 0},
        compiler_params=_cparams(("arbitrary",), 40), name="mem_prompt",
    )(h, wq_bf, wo_bf, mk, mv, x, g_post, g_next)


MEM_NB = 8


def _mem_sample_kernel(h_ref, wq_ref, wo_ref, ck_ref, cv_ref, x_ref, gp_ref, gn_ref, hn_in_ref,
                       xn_ref, hn_ref):
    del hn_in_ref
    nb, t = MEM_NB, DEC_SEQ
    q = jnp.dot(h_ref[...], wq_ref[...], preferred_element_type=F32)
    outs = []
    for hh in range(MEM_HEADS):
        cs = slice(hh * MEM_HEAD_DIM, (hh + 1) * MEM_HEAD_DIM)
        q3 = q[:, cs].reshape(nb, t, MEM_HEAD_DIM).astype(BF16)
        s = jnp.einsum('ntd,nmd->ntm', q3, ck_ref[:, :, cs].astype(BF16),
                       preferred_element_type=F32) * (MEM_HEAD_DIM ** -0.5)
        p = jnp.exp(s - jnp.max(s, axis=-1, keepdims=True))
        den = jnp.sum(p, axis=-1, keepdims=True)
        o3 = jnp.einsum('ntm,nmd->ntd', p.astype(BF16), cv_ref[:, :, cs].astype(BF16),
                        preferred_element_type=F32) / den
        outs.append(o3.reshape(nb * t, MEM_HEAD_DIM))
    o = jnp.concatenate(outs, axis=1).astype(BF16)
    y = jnp.dot(o, wo_ref[...], preferred_element_type=F32)
    xn, hn = _residual(y, x_ref[...], gp_ref[...], gn_ref[...])
    xn_ref[...] = xn
    hn_ref[...] = hn


def _mem_sample(xn, hn, h, wq_bf, wo_bf, cache_k, cache_v, layer, g_post, g_next):
    nb = MEM_NB
    rows = nb * DEC_SEQ
    off = MP // rows
    row = pl.BlockSpec((rows, D_MODEL), lambda i: (off + i, 0))
    vec = pl.BlockSpec((1, D_MODEL), lambda i: (0, 0))
    cspec = pl.BlockSpec((None, nb, MEM_LEN, MEM_W), lambda i: (layer, i, 0, 0))
    anyspec = pl.BlockSpec(memory_space=pl.ANY)
    return pl.pallas_call(
        _mem_sample_kernel, grid=(DEC_BATCH // nb,),
        in_specs=[row, pl.BlockSpec((D_MODEL, MEM_W), lambda i: (0, 0)),
                  pl.BlockSpec((MEM_W, D_MODEL), lambda i: (0, 0)), cspec, cspec, row, vec, vec, anyspec],
        out_specs=[row, row],
        out_shape=[jax.ShapeDtypeStruct((M, D_MODEL), F32), jax.ShapeDtypeStruct((M, D_MODEL), BF16)],
        input_output_aliases={5: 0, 8: 1},
        compiler_params=_cparams(("arbitrary",), 40), name="mem_sample",
    )(h, wq_bf, wo_bf, cache_k, cache_v, xn, g_post, g_next, hn)


def kernel(x_prompt, x_sample, cache_swa_k, cache_swa_v, state_ret, cache_mem_k, cache_mem_v, mem_prompt, norm_mix_pre, norm_mix_post, norm_mem, norm_x_pre, norm_x_post, norm_ffn_pre, norm_ffn_post, w_ab_in, w_ab_out, swa_sinks, gmlp_ln_g, gmlp_ln_b, gmlp_w_s, gmlp_b_s, w_ret_in, w_ret_out, w_mem_q, w_mem_k, w_mem_v, w_mem_o, w_ffn_up, w_ffn_down):
    vec = lambda g, layer: g[layer].reshape(1, D_MODEL)
    x = jnp.concatenate([x_prompt.reshape(MP, D_MODEL), x_sample.reshape(MS, D_MODEL)], axis=0)
    swa_c, swa_s1, swa_s2, ret_c, ret_s = _rope_tables()
    mem_k, mem_v = _mem_kv(mem_prompt, norm_mem, w_mem_k, w_mem_v)
    ck_mem = cache_mem_k.reshape(DEPTH, DEC_BATCH, MEM_LEN, MEM_W)
    cv_mem = cache_mem_v.reshape(DEPTH, DEC_BATCH, MEM_LEN, MEM_W)
    lg = jnp.log1p(-jnp.exp2(-5.0 - jnp.arange(RET_HEADS, dtype=F32)))

    h = _prenorm(x, vec(norm_mix_pre, 0))
    outs = {}
    for layer in range(DEPTH):
        j = layer // 2
        if layer % 2 == 0:
            q, kv, u, zv, zv_s = _ab_in_proj(h, w_ab_in[j], gmlp_ln_g[j].reshape(1, GMLP_WIDTH),
                                             gmlp_ln_b[j].reshape(1, GMLP_WIDTH), (swa_c, swa_s1, swa_s2))
            cat = _ab_mix_prompt(q, kv, u, zv, swa_sinks[j], gmlp_w_s[j], gmlp_b_s[j])
            cat, nk_s, nv_s = _ab_mix_sample(
                cat, q, kv, u, zv_s, cache_swa_k[j].reshape(DEC_BATCH, WINDOW, SWA_KV_W),
                cache_swa_v[j].reshape(DEC_BATCH, WINDOW, SWA_KV_W), swa_sinks[j], gmlp_w_s[j], gmlp_b_s[j])
            kv_p = kv[:MP].reshape(BATCH, SEQ, 2, SWA_KV_HEADS, SWA_HEAD_DIM)[:, SEQ - WINDOW:]
            outs.setdefault('swa_k_p', []).append(kv_p[:, :, 0])
            outs.setdefault('swa_v_p', []).append(kv_p[:, :, 1])
            outs.setdefault('swa_k_s', []).append(nk_s.reshape(DEC_BATCH, WINDOW, SWA_KV_HEADS, SWA_HEAD_DIM))
            outs.setdefault('swa_v_s', []).append(nv_s.reshape(DEC_BATCH, WINDOW, SWA_KV_HEADS, SWA_HEAD_DIM))
            outs.setdefault('gmlp_v_s', []).append(zv_s.reshape(DEC_BATCH, DEC_SEQ, GMLP_WIDTH))
            mixed, w_out, mix_name = cat, w_ab_out[j], "ab_out"
        else:
            z = _ret_in_proj(h, w_ret_in[j], (ret_c, ret_s))
            y, s_p = _ret_prompt(z, lg)
            y, s_s = _ret_sample(y, z, state_ret[j], lg)
            outs.setdefault('ret_p', []).append(s_p)
            outs.setdefault('ret_s', []).append(s_s)
            mixed, w_out, mix_name = y, w_ret_out[j], "ret_out"
        x, h = _out_proj(mixed, w_out.astype(BF16), x, vec(norm_mix_post, layer), vec(norm_x_pre, layer),
                         mix_name)
        wq_bf, wo_bf = w_mem_q[layer].astype(BF16), w_mem_o[layer].astype(BF16)
        g_post = vec(norm_x_post, layer)
        g_next = vec(norm_ffn_pre, layer)
        xn, hn = _mem_prompt(h, wq_bf, wo_bf, mem_k[layer], mem_v[layer], x, g_post, g_next)
        x, h = _mem_sample(xn, hn, h, wq_bf, wo_bf, ck_mem, cv_mem, layer, g_post, g_next)
        g_next = vec(norm_mix_pre, layer + 1) if layer + 1 < DEPTH else vec(norm_ffn_post, layer)
        x, h = _ffn(h, w_ffn_up[layer].astype(BF16), w_ffn_down[layer].astype(BF16), x,
                    vec(norm_ffn_post, layer), g_next)

    stack = lambda name: jnp.stack(outs[name])
    mem_shape = (DEPTH, BATCH, MEM_LEN, MEM_HEADS, MEM_HEAD_DIM)
    return (x[:MP].reshape(BATCH, SEQ, D_MODEL), x[MP:].reshape(DEC_BATCH, DEC_SEQ, D_MODEL),
            stack('swa_k_p'), stack('swa_v_p'), stack('swa_k_s'), stack('swa_v_s'), stack('gmlp_v_s'),
            stack('ret_p'), stack('ret_s'), mem_k.reshape(mem_shape), mem_v.reshape(mem_shape))
```

```python
import functools
import math

import jax
import jax.numpy as jnp
from jax import lax
from jax.experimental import pallas as pl
from jax.experimental.pallas import tpu as pltpu

F32 = jnp.float32
BF16 = jnp.bfloat16

D_MODEL = 2048
BATCH = 2
SEQ = 4096
DEPTH = 2
DEC_BATCH = 128
DEC_SEQ = 8
PAST_LEN = 8192

SWA_HEADS = 16
SWA_KV_HEADS = 4
SWA_HEAD_DIM = 64
SWA_GROUP = SWA_HEADS // SWA_KV_HEADS
WINDOW = 128
ROPE_THETA = 500000.0
ROPE_DIM = SWA_HEAD_DIM // 4
ROPE_HALF = ROPE_DIM // 2

GMLP_GROUPS = 4
GMLP_WIDTH = 1024
GMLP_GROUP_DIM = GMLP_WIDTH // GMLP_GROUPS
GMLP_CHUNK = 128

SWA_Q_W = SWA_HEADS * SWA_HEAD_DIM
SWA_KV_W = SWA_KV_HEADS * SWA_HEAD_DIM
AB_IN = SWA_Q_W + 2 * SWA_KV_W + 2 * GMLP_WIDTH
AB_OUT = SWA_Q_W + GMLP_WIDTH

RET_HEADS = 8
RET_KEY_DIM = D_MODEL // RET_HEADS
RET_VAL_DIM = 2 * RET_KEY_DIM
RET_CHUNK = 128
RET_THETA = 10000.0
RET_QK_W = RET_HEADS * RET_KEY_DIM
RET_V_W = RET_HEADS * RET_VAL_DIM
RET_IN = 2 * RET_QK_W + 2 * RET_V_W

MEM_LEN = 256
MEM_HEADS = 4
MEM_HEAD_DIM = 128
MEM_W = MEM_HEADS * MEM_HEAD_DIM

D_FF = 4 * D_MODEL
EPS = 1e-6
NEG_INF = -1e30

MP = BATCH * SEQ
MS = DEC_BATCH * DEC_SEQ
M = MP + MS
LANES = 128
MIB = 1024 * 1024

TM_IN = 1024
TAB_ROWS = SEQ + MS
TM_OUT = 512


def _cparams(sem, vmem_mib):
    return pltpu.CompilerParams(dimension_semantics=sem, vmem_limit_bytes=vmem_mib * MIB)


def _rms(x, g):
    return x * lax.rsqrt(jnp.mean(x * x, axis=-1, keepdims=True) + EPS) * g


def _residual(y, x, g_post, g_next):
    xn = x + _rms(y, g_post)
    return xn, _rms(xn, g_next).astype(BF16)


def _rope_table_kernel(sc_ref, s1_ref, s2_ref, rc_ref, rs_ref):
    tr = sc_ref.shape[0]
    r = lax.broadcasted_iota(jnp.int32, (tr, LANES), 0) + pl.program_id(0) * tr
    pos = jnp.where(r < SEQ, r, PAST_LEN + ((r - SEQ) & (DEC_SEQ - 1))).astype(F32)
    lane = lax.broadcasted_iota(jnp.int32, (tr, LANES), 1)
    d = lane & (SWA_HEAD_DIM - 1)
    inv = jnp.exp((d & (ROPE_HALF - 1)).astype(F32) * (-math.log(ROPE_THETA) / ROPE_HALF))
    ang = pos * inv
    c, s = jnp.cos(ang), jnp.sin(ang)
    sc_ref[...] = jnp.where(d < ROPE_DIM, c, 1.0)
    s1_ref[...] = jnp.where(d < ROPE_HALF, -s, 0.0)
    s2_ref[...] = jnp.where((d >= ROPE_HALF) & (d < ROPE_DIM), s, 0.0)
    inv_r = jnp.exp(lane.astype(F32) * (-math.log(RET_THETA) / LANES))
    ang_r = pos * inv_r
    rc_ref[...] = jnp.cos(ang_r)
    rs_ref[...] = jnp.sin(ang_r)


def _rope_tables():
    tr = 512
    spec = pl.BlockSpec((tr, LANES), lambda i: (i, 0))
    shp = jax.ShapeDtypeStruct((TAB_ROWS, LANES), F32)
    return pl.pallas_call(
        _rope_table_kernel, grid=(TAB_ROWS // tr,), out_specs=[spec] * 5, out_shape=[shp] * 5,
        compiler_params=_cparams(("arbitrary",), 16), name="rope_tables")()


def _norm_kernel(xp_ref, xs_ref, g_ref, x_ref, h_ref):
    def emit(src_ref):
        x = src_ref[...]
        x_ref[...] = x
        h_ref[...] = _rms(x, g_ref[...]).astype(BF16)

    is_prompt = pl.program_id(0) < MP // x_ref.shape[0]
    pl.when(is_prompt)(lambda: emit(xp_ref))
    pl.when(jnp.logical_not(is_prompt))(lambda: emit(xs_ref))


def _stack_and_prenorm(x_prompt, x_sample, g):
    tm = TM_OUT
    row = pl.BlockSpec((tm, D_MODEL), lambda i: (i, 0))
    return pl.pallas_call(
        _norm_kernel, grid=(M // tm,),
        in_specs=[pl.BlockSpec((tm, D_MODEL), lambda i: (jnp.minimum(i, MP // tm - 1), 0)),
                  pl.BlockSpec((tm, D_MODEL), lambda i: (jnp.maximum(i - MP // tm, 0), 0)),
                  pl.BlockSpec((1, D_MODEL), lambda i: (0, 0))],
        out_specs=[row, row],
        out_shape=[jax.ShapeDtypeStruct((M, D_MODEL), F32), jax.ShapeDtypeStruct((M, D_MODEL), BF16)],
        compiler_params=_cparams(("arbitrary",), 40), name="prenorm",
    )(x_prompt.reshape(MP, D_MODEL), x_sample.reshape(MS, D_MODEL), g)


W_SUB = 512


def _inproj_kernel(h_ref, *refs, n_sub, n_side, epilogue):
    w_refs, side, outs, wbf_ref = refs[:n_sub], refs[n_sub:n_sub + n_side], refs[n_sub + n_side:-1], refs[-1]

    @pl.when(pl.program_id(1) == 0)
    def _():
        for s, w_ref in enumerate(w_refs):
            wbf_ref[:, s * W_SUB:(s + 1) * W_SUB] = w_ref[...].astype(BF16)

    acc = jnp.dot(h_ref[...], wbf_ref[...], preferred_element_type=F32)
    epilogue(acc, side, outs)


def _tab_index(j, i):
    return (jnp.where(i < MP // TM_IN, i % (SEQ // TM_IN), SEQ // TM_IN), 0)


def _inproj(h, w, col0, ncols, tn, epilogue, side, side_specs, out_shapes, out_specs, name, vmem_mib):
    tm = TM_IN
    k = h.shape[1]
    n_sub = tn // W_SUB
    w_specs = [pl.BlockSpec((k, W_SUB), lambda j, i, s=s: (0, col0 // W_SUB + j * n_sub + s))
               for s in range(n_sub)]
    return pl.pallas_call(
        functools.partial(_inproj_kernel, n_sub=n_sub, n_side=len(side), epilogue=epilogue),
        grid=(ncols // tn, M // tm),
        in_specs=[pl.BlockSpec((tm, k), lambda j, i: (i, 0))] + w_specs + side_specs,
        out_specs=out_specs, out_shape=out_shapes,
        scratch_shapes=[pltpu.VMEM((k, tn), BF16)],
        compiler_params=_cparams(("arbitrary", "arbitrary"), vmem_mib), name=name)(h, *([w] * n_sub), *side)


def _swa_rope(x, c, s1, s2):
    return x * c + pltpu.roll(x, LANES - ROPE_HALF, 1) * s1 + pltpu.roll(x, ROPE_HALF, 1) * s2


def _epi_swa_q(acc, side, outs):
    c, s1, s2 = (t[...] for t in side)
    (q_ref,) = outs
    for cc in range(acc.shape[1] // LANES):
        sl = slice(cc * LANES, (cc + 1) * LANES)
        q_ref[:, sl] = (_swa_rope(acc[:, sl], c, s1, s2) * (SWA_HEAD_DIM ** -0.5)).astype(BF16)


def _epi_swa_kv(acc, side, outs):
    c, s1, s2 = (t[...] for t in side)
    (kv_ref,) = outs
    for cc in range(SWA_KV_W // LANES):
        sl = slice(cc * LANES, (cc + 1) * LANES)
        kv_ref[:, sl] = _swa_rope(acc[:, sl], c, s1, s2)
    kv_ref[:, SWA_KV_W:] = acc[:, SWA_KV_W:]


def _epi_gelu(acc, side, outs):
    outs[0][...] = jax.nn.gelu(acc).astype(BF16)


def _epi_gelu_ln(acc, side, outs):
    g_ref, b_ref = side
    zv_ref, zvs_ref = outs
    a = jax.nn.gelu(acc)
    xc = a - jnp.mean(a, axis=-1, keepdims=True)
    y = xc * lax.rsqrt(jnp.mean(xc * xc, axis=-1, keepdims=True) + EPS) * g_ref[...] + b_ref[...]
    zv_ref[...] = y.astype(BF16)

    @pl.when(pl.program_id(1) == MP // TM_IN)
    def _():
        zvs_ref[...] = y


def _epi_ret(acc, side, outs, tn):
    cos, sin = (t[...] for t in side)
    (z_ref,) = outs
    j = pl.program_id(0)
    qk_tiles = 2 * RET_QK_W // tn

    @pl.when(j < qk_tiles)
    def _():
        scale = jnp.where(j < qk_tiles // 2, 1.0, RET_KEY_DIM ** -0.5)
        for hh in range(tn // RET_KEY_DIM):
            c0 = hh * RET_KEY_DIM
            x1, x2 = acc[:, c0:c0 + LANES], acc[:, c0 + LANES:c0 + 2 * LANES]
            z_ref[:, c0:c0 + LANES] = ((x1 * cos - x2 * sin) * scale).astype(BF16)
            z_ref[:, c0 + LANES:c0 + 2 * LANES] = ((x2 * cos + x1 * sin) * scale).astype(BF16)

    @pl.when((j >= qk_tiles) & (j < qk_tiles + RET_V_W // tn))
    def _():
        z_ref[...] = acc.astype(BF16)

    @pl.when(j >= qk_tiles + RET_V_W // tn)
    def _():
        z_ref[...] = (acc / (1.0 + jnp.exp(-acc))).astype(BF16)


def _ab_in_proj(h, w_in, ln_g, ln_b, tabs):
    sc, s1, s2 = tabs
    tab_spec = pl.BlockSpec((TM_IN, LANES), _tab_index)
    row_out = lambda tn: pl.BlockSpec((TM_IN, tn), lambda j, i: (i, j))
    q = _inproj(h, w_in, 0, SWA_Q_W, SWA_Q_W, _epi_swa_q, [sc, s1, s2], [tab_spec] * 3,
                jax.ShapeDtypeStruct((M, SWA_Q_W), BF16), row_out(SWA_Q_W), "ab_in_q", 52)
    kv = _inproj(h, w_in, SWA_Q_W, 2 * SWA_KV_W, 512, _epi_swa_kv, [sc, s1, s2], [tab_spec] * 3,
                 jax.ShapeDtypeStruct((M, 2 * SWA_KV_W), F32), row_out(512), "ab_in_kv", 40)
    u = _inproj(h, w_in, SWA_Q_W + 2 * SWA_KV_W, GMLP_WIDTH, GMLP_WIDTH, _epi_gelu, [], [],
                jax.ShapeDtypeStruct((M, GMLP_WIDTH), BF16), row_out(GMLP_WIDTH), "ab_in_u", 52)
    vec_spec = pl.BlockSpec((1, GMLP_WIDTH), lambda j, i: (0, 0))
    zv, zv_s = _inproj(
        h, w_in, AB_IN - GMLP_WIDTH, GMLP_WIDTH, GMLP_WIDTH, _epi_gelu_ln,
        [ln_g, ln_b], [vec_spec] * 2,
        [jax.ShapeDtypeStruct((M, GMLP_WIDTH), BF16), jax.ShapeDtypeStruct((MS, GMLP_WIDTH), F32)],
        [row_out(GMLP_WIDTH), pl.BlockSpec((MS, GMLP_WIDTH), lambda j, i: (0, 0))], "ab_in_zv", 52)
    return q, kv, u, zv, zv_s


def _ret_in_proj(h, w_in, tabs):
    rc, rs = tabs
    tn = 1024
    tab_spec = pl.BlockSpec((TM_IN, LANES), _tab_index)
    return _inproj(h, w_in, 0, RET_IN, tn, functools.partial(_epi_ret, tn=tn), [rc, rs], [tab_spec] * 2,
                   jax.ShapeDtypeStruct((M, RET_IN), BF16),
                   pl.BlockSpec((TM_IN, tn), lambda j, i: (i, j)), "ret_in", 52)


AB_ROWS = 512


def _sink_softmax(s, sink):
    m = jnp.maximum(jnp.max(s, axis=-1, keepdims=True), sink)
    p = jnp.exp(s - m)
    den = jnp.sum(p, axis=-1, keepdims=True) + jnp.exp(sink - m)
    return p, den


def _ab_prompt_kernel(sink_ref, q_ref, kv_ref, kvp_ref, u_ref, zv_ref, ws_ref, bs_ref, cat_ref,
                      kd_ref, vd_ref):
    r = pl.program_id(1)
    low = lax.broadcasted_iota(jnp.int32, (1, LANES), 1) < SWA_HEAD_DIM

    def both_halves(x, half):
        sw = pltpu.roll(x, SWA_HEAD_DIM, 1)
        return (jnp.where(low, x, sw) if half == 0 else jnp.where(low, sw, x)).astype(BF16)

    for kh in range(SWA_KV_HEADS):
        kcol = slice((kh // 2) * LANES, (kh // 2 + 1) * LANES)
        vcol = slice(SWA_KV_W + (kh // 2) * LANES, SWA_KV_W + (kh // 2 + 1) * LANES)
        kd_ref[kh, :WINDOW, :] = both_halves(kvp_ref[:, kcol], kh % 2)
        kd_ref[kh, WINDOW:, :] = both_halves(kv_ref[:, kcol], kh % 2)
        vd_ref[kh, :WINDOW, :] = both_halves(kvp_ref[:, vcol], kh % 2)
        vd_ref[kh, WINDOW:, :] = both_halves(kv_ref[:, vcol], kh % 2)

    t = lax.broadcasted_iota(jnp.int32, (WINDOW, 2 * WINDOW), 0)
    sk = lax.broadcasted_iota(jnp.int32, (WINDOW, 2 * WINDOW), 1)
    band = (sk <= t + WINDOW) & (sk > t)
    sink_slot = sk == t
    tril = (lax.broadcasted_iota(jnp.int32, (GMLP_CHUNK, GMLP_CHUNK), 0)
            >= lax.broadcasted_iota(jnp.int32, (GMLP_CHUNK, GMLP_CHUNK), 1))

    for sb in range(AB_ROWS // WINDOW):
        rs = slice(sb * WINDOW, (sb + 1) * WINDOW)
        ks = slice(sb * WINDOW, sb * WINDOW + 2 * WINDOW)
        valid = band & ((sk >= WINDOW) | (r * (AB_ROWS // WINDOW) + sb > 0))
        for kh in range(SWA_KV_HEADS):
            qm = []
            for g in range(SWA_GROUP):
                hd = kh * SWA_GROUP + g
                qb = q_ref[rs, (hd // 2) * LANES:(hd // 2 + 1) * LANES]
                qm.append(jnp.where(low if hd % 2 == 0 else jnp.logical_not(low), qb, jnp.zeros_like(qb)))
            s = lax.dot_general(jnp.concatenate(qm, axis=0), kd_ref[kh, ks, :], (((1,), (1,)), ((), ())),
                                preferred_element_type=F32)
            pv, den = [], []
            for g in range(SWA_GROUP):
                sg = s[g * WINDOW:(g + 1) * WINDOW]
                sg = jnp.where(valid, sg, jnp.where(sink_slot, sink_ref[kh * SWA_GROUP + g], NEG_INF))
                p = jnp.exp(sg - jnp.max(sg, axis=-1, keepdims=True))
                den.append(jnp.sum(p, axis=-1, keepdims=True))
                pv.append(jnp.where(sink_slot, 0.0, p).astype(BF16))
            o = jnp.dot(jnp.concatenate(pv, axis=0), vd_ref[kh, ks, :], preferred_element_type=F32)
            for pr in range(SWA_GROUP // 2):
                lo = o[(2 * pr) * WINDOW:(2 * pr + 1) * WINDOW] / den[2 * pr]
                hi = o[(2 * pr + 1) * WINDOW:(2 * pr + 2) * WINDOW] / den[2 * pr + 1]
                c0 = (kh * (SWA_GROUP // 2) + pr) * LANES
                cat_ref[rs, c0:c0 + LANES] = jnp.where(low, lo, hi).astype(BF16)
        zb = zv_ref[rs, :]
        ub = u_ref[rs, :]
        for g in range(GMLP_GROUPS):
            gs = slice(g * GMLP_GROUP_DIM, (g + 1) * GMLP_GROUP_DIM)
            w = jnp.where(tril, ws_ref[g], 0.0).astype(BF16)
            mixed = jnp.dot(w, zb[:, gs], preferred_element_type=F32) + bs_ref[g]
            cat_ref[rs, SWA_Q_W + g * GMLP_GROUP_DIM:SWA_Q_W + (g + 1) * GMLP_GROUP_DIM] = (
                ub[:, gs].astype(F32) * mixed).astype(BF16)


def _ab_mix_prompt(q, kv, u, zv, sinks, w_s, b_s):
    nr = SEQ // AB_ROWS
    sub = AB_ROWS // WINDOW
    row = lambda b, r: (b * nr + r, 0)
    return pl.pallas_call(
        _ab_prompt_kernel, grid=(BATCH, nr),
        in_specs=[pl.BlockSpec(memory_space=pltpu.SMEM),
                  pl.BlockSpec((AB_ROWS, SWA_Q_W), row),
                  pl.BlockSpec((AB_ROWS, 2 * SWA_KV_W), row),
                  pl.BlockSpec((WINDOW, 2 * SWA_KV_W), lambda b, r: (jnp.maximum((b * nr + r) * sub - 1, 0), 0)),
                  pl.BlockSpec((AB_ROWS, GMLP_WIDTH), row),
                  pl.BlockSpec((AB_ROWS, GMLP_WIDTH), row),
                  pl.BlockSpec((GMLP_GROUPS, GMLP_CHUNK, GMLP_CHUNK), lambda b, r: (0, 0, 0)),
                  pl.BlockSpec((GMLP_GROUPS, GMLP_CHUNK, 1), lambda b, r: (0, 0, 0))],
        out_specs=pl.BlockSpec((AB_ROWS, AB_OUT), row),
        out_shape=jax.ShapeDtypeStruct((M, AB_OUT), BF16),
        scratch_shapes=[pltpu.VMEM((SWA_KV_HEADS, WINDOW + AB_ROWS, LANES), BF16)] * 2,
        compiler_params=_cparams(("arbitrary", "arbitrary"), 40), name="ab_mix_prompt",
    )(sinks, q, kv, kv, u, zv, w_s, b_s.reshape(GMLP_GROUPS, GMLP_CHUNK, 1))


AB_NB = 16
AB_KEYS = 2 * WINDOW


def _ab_sample_kernel(sink_ref, q_ref, kv_ref, u_ref, zv_ref, ck_ref, cv_ref, ws_ref, bs_ref, cat_in_ref,
                      cat_ref, nk_ref, nv_ref):
    del cat_in_ref
    nb, t = AB_NB, DEC_SEQ
    kv3 = kv_ref[...].reshape(nb, t, 2 * SWA_KV_W)
    kn, vn = kv3[:, :, :SWA_KV_W], kv3[:, :, SWA_KV_W:]
    ck, cv = ck_ref[...], cv_ref[...]
    nk_ref[:, :WINDOW - t, :] = ck[:, t:, :]
    nk_ref[:, WINDOW - t:, :] = kn
    nv_ref[:, :WINDOW - t, :] = cv[:, t:, :]
    nv_ref[:, WINDOW - t:, :] = vn
    pad = jnp.zeros((nb, AB_KEYS - WINDOW - t, SWA_KV_W), F32)
    kall = jnp.concatenate([ck, kn, pad], axis=1).astype(BF16)
    vall = jnp.concatenate([cv, vn, pad], axis=1).astype(BF16)

    rows = SWA_GROUP * t
    tq = lax.broadcasted_iota(jnp.int32, (rows, AB_KEYS), 0) & (t - 1)
    sk = lax.broadcasted_iota(jnp.int32, (rows, AB_KEYS), 1)
    valid = ((sk < WINDOW) & (sk > tq)) | ((sk >= WINDOW) & (sk - WINDOW <= tq))

    q3 = q_ref[...].astype(F32).reshape(nb, t, SWA_Q_W)
    pieces = []
    for kh in range(SWA_KV_HEADS):
        heads = [kh * SWA_GROUP + g for g in range(SWA_GROUP)]
        qs = jnp.concatenate([q3[:, :, hd * SWA_HEAD_DIM:(hd + 1) * SWA_HEAD_DIM] for hd in heads],
                             axis=1).astype(BF16)
        cs = slice(kh * SWA_HEAD_DIM, (kh + 1) * SWA_HEAD_DIM)
        s = jnp.einsum('nqd,nkd->nqk', qs, kall[:, :, cs], preferred_element_type=F32)
        s = jnp.where(valid[None], s, NEG_INF)
        sink = jnp.concatenate([jnp.full((t, 1), sink_ref[hd], F32) for hd in heads], axis=0)[None]
        p, den = _sink_softmax(s, sink)
        o = jnp.einsum('nqk,nkd->nqd', p.astype(BF16), vall[:, :, cs], preferred_element_type=F32) / den
        pieces += [o[:, g * t:(g + 1) * t, :] for g in range(SWA_GROUP)]
    attn = jnp.concatenate(pieces, axis=2).reshape(nb * t, SWA_Q_W)
    cat_ref[:, :SWA_Q_W] = attn.astype(BF16)

    zv3 = zv_ref[...].reshape(nb, t, GMLP_WIDTH)
    u3 = u_ref[...].astype(F32).reshape(nb, t, GMLP_WIDTH)
    irow = lax.broadcasted_iota(jnp.int32, (t, GMLP_GROUP_DIM), 0)
    for g in range(GMLP_GROUPS):
        gs = slice(g * GMLP_GROUP_DIM, (g + 1) * GMLP_GROUP_DIM)
        zg = zv3[:, :, gs]
        mixed = jnp.zeros((nb, t, GMLP_GROUP_DIM), F32) + bs_ref[g][None]
        for j in range(t):
            wcol = jnp.where(irow >= j, ws_ref[g, j], 0.0)
            mixed = mixed + zg[:, j:j + 1, :] * wcol[None]
        gate = (u3[:, :, gs] * mixed).reshape(nb * t, GMLP_GROUP_DIM)
        cat_ref[:, SWA_Q_W + g * GMLP_GROUP_DIM:SWA_Q_W + (g + 1) * GMLP_GROUP_DIM] = gate.astype(BF16)


def _ab_mix_sample(cat, q, kv, u, zv_s, cache_k, cache_v, sinks, w_s, b_s):
    nb, t = AB_NB, DEC_SEQ
    rows = nb * t
    off = MP // rows
    srow = lambda i: (off + i, 0)
    wt = jnp.broadcast_to(jnp.swapaxes(w_s[:, :t, :t], 1, 2)[..., None], (GMLP_GROUPS, t, t, GMLP_GROUP_DIM))
    bt = jnp.broadcast_to(b_s[:, :t, None], (GMLP_GROUPS, t, GMLP_GROUP_DIM))
    cspec = pl.BlockSpec((nb, WINDOW, SWA_KV_W), lambda i: (i, 0, 0))
    cshape = jax.ShapeDtypeStruct((DEC_BATCH, WINDOW, SWA_KV_W), F32)
    return pl.pallas_call(
        _ab_sample_kernel, grid=(DEC_BATCH // nb,),
        in_specs=[pl.BlockSpec(memory_space=pltpu.SMEM),
                  pl.BlockSpec((rows, SWA_Q_W), srow),
                  pl.BlockSpec((rows, 2 * SWA_KV_W), srow),
                  pl.BlockSpec((rows, GMLP_WIDTH), srow),
                  pl.BlockSpec((rows, GMLP_WIDTH), lambda i: (i, 0)),
                  cspec, cspec,
                  pl.BlockSpec((GMLP_GROUPS, t, t, GMLP_GROUP_DIM), lambda i: (0, 0, 0, 0)),
                  pl.BlockSpec((GMLP_GROUPS, t, GMLP_GROUP_DIM), lambda i: (0, 0, 0)),
                  pl.BlockSpec(memory_space=pl.ANY)],
        out_specs=[pl.BlockSpec((rows, AB_OUT), srow), cspec, cspec],
        out_shape=[jax.ShapeDtypeStruct((M, AB_OUT), BF16), cshape, cshape],
        input_output_aliases={9: 0},
        compiler_params=_cparams(("arbitrary",), 48), name="ab_mix_sample",
    )(sinks, q, kv, u, zv_s, cache_k, cache_v, wt, bt, cat)


RET_ROWS = 1024


def _ret_prompt_kernel(lg_ref, q_ref, k_ref, v_ref, g_ref, y_ref, sfin_ref, s_ref):
    lg = lg_ref[pl.program_id(1)]
    c = pl.program_id(2)
    L = RET_CHUNK

    @pl.when(c == 0)
    def _():
        s_ref[...] = jnp.zeros_like(s_ref)

    diff = (lax.broadcasted_iota(jnp.int32, (L, L), 0) - lax.broadcasted_iota(jnp.int32, (L, L), 1)).astype(F32)
    dmat = jnp.where(diff >= 0, jnp.exp(lg * jnp.maximum(diff, 0.0)), 0.0)
    idx = lax.broadcasted_iota(jnp.int32, (L, 1), 0).astype(F32)
    q_dec = jnp.exp(lg * (idx + 1.0))
    k_dec = jnp.exp(lg * (L - 1.0 - idx))
    s_dec = jnp.exp(lg * jnp.full((1, 1), float(L), F32))

    def chunk(ci, carry):
        rs = pl.ds(pl.multiple_of(ci * L, L), L)
        q, k, v = q_ref[rs, :], k_ref[rs, :], v_ref[rs, :]
        att = lax.dot_general(q, k, (((1,), (1,)), ((), ())), preferred_element_type=F32) * dmat
        s0 = s_ref[...]
        o = (jnp.dot(att.astype(BF16), v, preferred_element_type=F32)
             + jnp.dot(q, s0.astype(BF16), preferred_element_type=F32) * q_dec)
        kd = (k.astype(F32) * k_dec).astype(BF16)
        s_ref[...] = s0 * s_dec + lax.dot_general(kd, v, (((0,), (0,)), ((), ())), preferred_element_type=F32)
        on = o * lax.rsqrt(jnp.mean(o * o, axis=-1, keepdims=True) + EPS)
        y_ref[rs, :] = (on * g_ref[rs, :].astype(F32)).astype(BF16)
        return carry

    lax.fori_loop(0, RET_ROWS // L, chunk, 0)

    @pl.when(c == pl.num_programs(2) - 1)
    def _():
        sfin_ref[...] = s_ref[...]


def _ret_prompt(z, lg):
    nc = SEQ // RET_ROWS
    kb = RET_QK_W // RET_KEY_DIM
    vb = 2 * RET_QK_W // RET_VAL_DIM
    gb = vb + RET_HEADS
    return pl.pallas_call(
        _ret_prompt_kernel, grid=(BATCH, RET_HEADS, nc),
        in_specs=[pl.BlockSpec(memory_space=pltpu.SMEM),
                  pl.BlockSpec((RET_ROWS, RET_KEY_DIM), lambda b, h, c: (b * nc + c, h)),
                  pl.BlockSpec((RET_ROWS, RET_KEY_DIM), lambda b, h, c: (b * nc + c, kb + h)),
                  pl.BlockSpec((RET_ROWS, RET_VAL_DIM), lambda b, h, c: (b * nc + c, vb + h)),
                  pl.BlockSpec((RET_ROWS, RET_VAL_DIM), lambda b, h, c: (b * nc + c, gb + h))],
        out_specs=[pl.BlockSpec((RET_ROWS, RET_VAL_DIM), lambda b, h, c: (b * nc + c, h)),
                   pl.BlockSpec((None, None, RET_KEY_DIM, RET_VAL_DIM), lambda b, h, c: (b, h, 0, 0))],
        out_shape=[jax.ShapeDtypeStruct((M, RET_V_W), BF16),
                   jax.ShapeDtypeStruct((BATCH, RET_HEADS, RET_KEY_DIM, RET_VAL_DIM), F32)],
        scratch_shapes=[pltpu.VMEM((RET_KEY_DIM, RET_VAL_DIM), F32)],
        compiler_params=_cparams(("arbitrary", "arbitrary", "arbitrary"), 32), name="ret_prompt",
    )(lg, z, z, z, z)


RET_NB = 16


def _ret_sample_kernel(lg_ref, q_ref, k_ref, v_ref, g_ref, s0_ref, y_in_ref, y_ref, s1_ref):
    del y_in_ref
    lg = lg_ref[pl.program_id(1)]
    nb, t = RET_NB, DEC_SEQ
    rows = nb * t
    q, k, v = q_ref[...], k_ref[...], v_ref[...]
    ri = lax.broadcasted_iota(jnp.int32, (rows, rows), 0)
    ci = lax.broadcasted_iota(jnp.int32, (rows, rows), 1)
    d = ((ri & (t - 1)) - (ci & (t - 1))).astype(F32)
    same = (ri // t) == (ci // t)
    dmat = jnp.where(same & (d >= 0), jnp.exp(lg * jnp.maximum(d, 0.0)), 0.0)
    att = lax.dot_general(q, k, (((1,), (1,)), ((), ())), preferred_element_type=F32) * dmat
    o_intra = jnp.dot(att.astype(BF16), v, preferred_element_type=F32)

    rcol = lax.broadcasted_iota(jnp.int32, (rows, 1), 0)
    tt = (rcol & (t - 1)).astype(F32)
    q_dec = jnp.exp(lg * (tt + 1.0))
    k_dec = jnp.exp(lg * (t - 1.0 - tt))
    s_dec = jnp.exp(lg * jnp.full((1, 1), float(t), F32))
    kd = (k.astype(F32) * k_dec).astype(BF16)
    qf = q.astype(F32)
    vf = v.astype(F32)
    o_rows = []
    for n in range(nb):
        s0 = s0_ref[n]
        qn = qf[n * t:(n + 1) * t].astype(BF16)
        o_rows.append(jnp.dot(qn, s0.astype(BF16), preferred_element_type=F32))
        vn = jnp.where(rcol // t == n, vf, 0.0).astype(BF16)
        s1_ref[n] = s0 * s_dec + lax.dot_general(kd, vn, (((0,), (0,)), ((), ())), preferred_element_type=F32)
    o = o_intra + jnp.concatenate(o_rows, axis=0) * q_dec
    on = o * lax.rsqrt(jnp.mean(o * o, axis=-1, keepdims=True) + EPS)
    y_ref[...] = (on * g_ref[...].astype(F32)).astype(BF16)


def _ret_sample(y, z, state, lg):
    nb = RET_NB
    rows = nb * DEC_SEQ
    off = MP // rows
    kb = RET_QK_W // RET_KEY_DIM
    vb = 2 * RET_QK_W // RET_VAL_DIM
    gb = vb + RET_HEADS
    sspec = pl.BlockSpec((nb, None, RET_KEY_DIM, RET_VAL_DIM), lambda n, h: (n, h, 0, 0))
    return pl.pallas_call(
        _ret_sample_kernel, grid=(DEC_BATCH // nb, RET_HEADS),
        in_specs=[pl.BlockSpec(memory_space=pltpu.SMEM),
                  pl.BlockSpec((rows, RET_KEY_DIM), lambda n, h: (off + n, h)),
                  pl.BlockSpec((rows, RET_KEY_DIM), lambda n, h: (off + n, kb + h)),
                  pl.BlockSpec((rows, RET_VAL_DIM), lambda n, h: (off + n, vb + h)),
                  pl.BlockSpec((rows, RET_VAL_DIM), lambda n, h: (off + n, gb + h)),
                  sspec,
                  pl.BlockSpec(memory_space=pl.ANY)],
        out_specs=[pl.BlockSpec((rows, RET_VAL_DIM), lambda n, h: (off + n, h)), sspec],
        out_shape=[jax.ShapeDtypeStruct((M, RET_V_W), BF16),
                   jax.ShapeDtypeStruct((DEC_BATCH, RET_HEADS, RET_KEY_DIM, RET_VAL_DIM), F32)],
        input_output_aliases={6: 0},
        compiler_params=_cparams(("arbitrary", "arbitrary"), 52), name="ret_sample",
    )(lg, z, z, z, z, state, y)


def _accumulate(acc_ref, part, k):
    @pl.when(k == 0)
    def _():
        acc_ref[...] = part

    @pl.when(k > 0)
    def _():
        acc_ref[...] += part


def _outproj_kernel(a_ref, w_ref, x_ref, gp_ref, gn_ref, xn_ref, hn_ref):
    k = pl.program_id(1)
    _accumulate(xn_ref, jnp.dot(a_ref[...], w_ref[...], preferred_element_type=F32), k)

    @pl.when(k == pl.num_programs(1) - 1)
    def _():
        xn, hn = _residual(xn_ref[...], x_ref[...], gp_ref[...], gn_ref[...])
        xn_ref[...] = xn
        hn_ref[...] = hn


def _out_proj(a, w_bf, x, g_post, g_next, name):
    tm, tk = TM_OUT, 1024
    kdim = a.shape[1]
    row = pl.BlockSpec((tm, D_MODEL), lambda i, k: (i, 0))
    vec = pl.BlockSpec((1, D_MODEL), lambda i, k: (0, 0))
    return pl.pallas_call(
        _outproj_kernel, grid=(M // tm, kdim // tk),
        in_specs=[pl.BlockSpec((tm, tk), lambda i, k: (i, k)),
                  pl.BlockSpec((tk, D_MODEL), lambda i, k: (k, 0)), row, vec, vec],
        out_specs=[row, row],
        out_shape=[jax.ShapeDtypeStruct((M, D_MODEL), F32), jax.ShapeDtypeStruct((M, D_MODEL), BF16)],
        compiler_params=_cparams(("arbitrary", "arbitrary"), 48), name=name,
    )(a, w_bf, x, g_post, g_next)


def _ffn_up_down(h_ref, wu_ref, wd_ref):
    a = jnp.maximum(jnp.dot(h_ref[...], wu_ref[...], preferred_element_type=F32), 0.0)
    return jnp.dot((a * a).astype(BF16), wd_ref[...], preferred_element_type=F32)


def _ffn_kernel(h_ref, wu_ref, wd_ref, x_ref, gp_ref, gn_ref, xn_ref, hn_ref):
    c = pl.program_id(1)
    _accumulate(xn_ref, _ffn_up_down(h_ref, wu_ref, wd_ref), c)

    @pl.when(c == pl.num_programs(1) - 1)
    def _():
        xn, hn = _residual(xn_ref[...], x_ref[...], gp_ref[...], gn_ref[...])
        xn_ref[...] = xn
        hn_ref[...] = hn


def _ffn_last_kernel(h_ref, wu_ref, wd_ref, x_ref, gp_ref, yp_ref, ys_ref, acc_ref):
    i, c = pl.program_id(0), pl.program_id(1)
    _accumulate(acc_ref, _ffn_up_down(h_ref, wu_ref, wd_ref), c)
    last = c == pl.num_programs(1) - 1
    is_prompt = i < MP // acc_ref.shape[0]

    @pl.when(last & is_prompt)
    def _():
        yp_ref[...] = x_ref[...] + _rms(acc_ref[...], gp_ref[...])

    @pl.when(last & jnp.logical_not(is_prompt))
    def _():
        ys_ref[...] = x_ref[...] + _rms(acc_ref[...], gp_ref[...])


def _ffn(h, wu_bf, wd_bf, x, g_post, g_next):
    tm, tc = TM_OUT, 1024
    row = pl.BlockSpec((tm, D_MODEL), lambda i, c: (i, 0))
    vec = pl.BlockSpec((1, D_MODEL), lambda i, c: (0, 0))
    w_specs = [pl.BlockSpec((D_MODEL, tc), lambda i, c: (0, c)), pl.BlockSpec((tc, D_MODEL), lambda i, c: (c, 0))]
    grid = (M // tm, D_FF // tc)
    if g_next is not None:
        return pl.pallas_call(
            _ffn_kernel, grid=grid, in_specs=[row] + w_specs + [row, vec, vec], out_specs=[row, row],
            out_shape=[jax.ShapeDtypeStruct((M, D_MODEL), F32), jax.ShapeDtypeStruct((M, D_MODEL), BF16)],
            compiler_params=_cparams(("arbitrary", "arbitrary"), 56), name="ffn",
        )(h, wu_bf, wd_bf, x, g_post, g_next)
    np_t = MP // tm
    return pl.pallas_call(
        _ffn_last_kernel, grid=grid, in_specs=[row] + w_specs + [row, vec],
        out_specs=[pl.BlockSpec((tm, D_MODEL), lambda i, c: (jnp.minimum(i, np_t - 1), 0)),
                   pl.BlockSpec((tm, D_MODEL), lambda i, c: (jnp.maximum(i - np_t, 0), 0))],
        out_shape=[jax.ShapeDtypeStruct((MP, D_MODEL), F32), jax.ShapeDtypeStruct((MS, D_MODEL), F32)],
        scratch_shapes=[pltpu.VMEM((tm, D_MODEL), F32)],
        compiler_params=_cparams(("arbitrary", "arbitrary"), 56), name="ffn_last",
    )(h, wu_bf, wd_bf, x, g_post)


def _mem_kv_kernel(m_ref, g_ref, wk_ref, wv_ref, k_ref, v_ref):
    m = _rms(m_ref[...], g_ref[0]).astype(BF16)
    k_ref[0] = jnp.dot(m, wk_ref[0].astype(BF16), preferred_element_type=F32)
    v_ref[0] = jnp.dot(m, wv_ref[0].astype(BF16), preferred_element_type=F32)


def _mem_kv(mem, g, w_k, w_v):
    rows = BATCH * MEM_LEN
    wspec = pl.BlockSpec((1, D_MODEL, MEM_W), lambda l: (l, 0, 0))
    ospec = pl.BlockSpec((1, rows, MEM_W), lambda l: (l, 0, 0))
    oshape = jax.ShapeDtypeStruct((DEPTH, rows, MEM_W), F32)
    return pl.pallas_call(
        _mem_kv_kernel, grid=(DEPTH,),
        in_specs=[pl.BlockSpec((rows, D_MODEL), lambda l: (0, 0)),
                  pl.BlockSpec((1, 1, D_MODEL), lambda l: (l, 0, 0)), wspec, wspec],
        out_specs=[ospec, ospec], out_shape=[oshape, oshape],
        compiler_params=_cparams(("arbitrary",), 48), name="mem_kv",
    )(mem.reshape(rows, D_MODEL), g.reshape(DEPTH, 1, D_MODEL), w_k, w_v)


def _mem_prompt_kernel(h_ref, wq_ref, wo_ref, mk_ref, mv_ref, x_ref, gp_ref, gn_ref, xn_ref, hn_ref):
    q = jnp.dot(h_ref[...], wq_ref[...], preferred_element_type=F32).astype(BF16)
    mk, mv = mk_ref[...].astype(BF16), mv_ref[...].astype(BF16)
    outs = []
    for hh in range(MEM_HEADS):
        cs = slice(hh * MEM_HEAD_DIM, (hh + 1) * MEM_HEAD_DIM)
        s = lax.dot_general(q[:, cs], mk[:, cs], (((1,), (1,)), ((), ())),
                            preferred_element_type=F32) * (MEM_HEAD_DIM ** -0.5)
        p = jnp.exp(s - jnp.max(s, axis=-1, keepdims=True))
        den = jnp.sum(p, axis=-1, keepdims=True)
        outs.append(jnp.dot(p.astype(BF16), mv[:, cs], preferred_element_type=F32) / den)
    o = jnp.concatenate(outs, axis=1).astype(BF16)
    y = jnp.dot(o, wo_ref[...], preferred_element_type=F32)
    xn, hn = _residual(y, x_ref[...], gp_ref[...], gn_ref[...])
    xn_ref[...] = xn
    hn_ref[...] = hn


def _mem_prompt(h, wq_bf, wo_bf, mk, mv, x, g_post, g_next):
    tm = TM_OUT
    per_b = SEQ // tm
    row = pl.BlockSpec((tm, D_MODEL), lambda i: (i, 0))
    vec = pl.BlockSpec((1, D_MODEL), lambda i: (0, 0))
    kvspec = pl.BlockSpec((MEM_LEN, MEM_W), lambda i: (i // per_b, 0))
    return pl.pallas_call(
        _mem_prompt_kernel, grid=(MP // tm,),
        in_specs=[row, pl.BlockSpec((D_MODEL, MEM_W), lambda i: (0, 0)),
                  pl.BlockSpec((MEM_W, D_MODEL), lambda i: (0, 0)), kvspec, kvspec, row, vec, vec],
        out_specs=[row, row],
        out_shape=[jax.ShapeDtypeStruct((M, D_MODEL), F32), jax.ShapeDtypeStruct((M, D_MODEL), BF16)],
        input_output_aliases={5: 0},
        compiler_params=_cparams(("arbitrary",), 40), name="mem_prompt",
    )(h, wq_bf, wo_bf, mk, mv, x, g_post, g_next)


MEM_NB = 8


def _mem_sample_kernel(h_ref, wq_ref, wo_ref, ck_ref, cv_ref, x_ref, gp_ref, gn_ref, hn_in_ref,
                       xn_ref, hn_ref):
    del hn_in_ref
    nb, t = MEM_NB, DEC_SEQ
    q = jnp.dot(h_ref[...], wq_ref[...], preferred_element_type=F32)
    qs = jnp.concatenate([q[:, hh * MEM_HEAD_DIM:(hh + 1) * MEM_HEAD_DIM].reshape(nb, t, MEM_HEAD_DIM)
                          for hh in range(MEM_HEADS)], axis=1).astype(BF16)
    s = jnp.einsum('nqd,nkd->nqk', qs, ck_ref[...].astype(BF16),
                   preferred_element_type=F32) * (MEM_HEAD_DIM ** -0.5)
    shape = (MEM_HEADS * t, MEM_LEN * MEM_HEADS)
    own = ((lax.broadcasted_iota(jnp.int32, shape, 1) & (MEM_HEADS - 1))
           == lax.broadcasted_iota(jnp.int32, shape, 0) // t)
    s = jnp.where(own[None], s, NEG_INF)
    p = jnp.exp(s - jnp.max(s, axis=-1, keepdims=True))
    den = jnp.sum(p, axis=-1, keepdims=True)
    o3 = jnp.einsum('nqk,nkd->nqd', p.astype(BF16), cv_ref[...].astype(BF16),
                    preferred_element_type=F32) / den
    o = jnp.concatenate([o3[:, hh * t:(hh + 1) * t, :].reshape(nb * t, MEM_HEAD_DIM)
                         for hh in range(MEM_HEADS)], axis=1).astype(BF16)
    y = jnp.dot(o, wo_ref[...], preferred_element_type=F32)
    xn, hn = _residual(y, x_ref[...], gp_ref[...], gn_ref[...])
    xn_ref[...] = xn
    hn_ref[...] = hn


def _mem_sample(xn, hn, h, wq_bf, wo_bf, cache_k, cache_v, layer, g_post, g_next):
    nb = MEM_NB
    rows = nb * DEC_SEQ
    off = MP // rows
    row = pl.BlockSpec((rows, D_MODEL), lambda i: (off + i, 0))
    vec = pl.BlockSpec((1, D_MODEL), lambda i: (0, 0))
    cspec = pl.BlockSpec((None, nb, MEM_LEN * MEM_HEADS, MEM_HEAD_DIM), lambda i: (layer, i, 0, 0))
    anyspec = pl.BlockSpec(memory_space=pl.ANY)
    return pl.pallas_call(
        _mem_sample_kernel, grid=(DEC_BATCH // nb,),
        in_specs=[row, pl.BlockSpec((D_MODEL, MEM_W), lambda i: (0, 0)),
                  pl.BlockSpec((MEM_W, D_MODEL), lambda i: (0, 0)), cspec, cspec, row, vec, vec, anyspec],
        out_specs=[row, row],
        out_shape=[jax.ShapeDtypeStruct((M, D_MODEL), F32), jax.ShapeDtypeStruct((M, D_MODEL), BF16)],
        input_output_aliases={5: 0, 8: 1},
        compiler_params=_cparams(("arbitrary",), 40), name="mem_sample",
    )(h, wq_bf, wo_bf, cache_k, cache_v, xn, g_post, g_next, hn)


def kernel(x_prompt, x_sample, cache_swa_k, cache_swa_v, state_ret, cache_mem_k, cache_mem_v, mem_prompt, norm_mix_pre, norm_mix_post, norm_mem, norm_x_pre, norm_x_post, norm_ffn_pre, norm_ffn_post, w_ab_in, w_ab_out, swa_sinks, gmlp_ln_g, gmlp_ln_b, gmlp_w_s, gmlp_b_s, w_ret_in, w_ret_out, w_mem_q, w_mem_k, w_mem_v, w_mem_o, w_ffn_up, w_ffn_down):
    vec = lambda g, layer: g[layer].reshape(1, D_MODEL)
    swa_c, swa_s1, swa_s2, ret_c, ret_s = _rope_tables()
    mem_k, mem_v = _mem_kv(mem_prompt, norm_mem, w_mem_k, w_mem_v)
    ck_mem = cache_mem_k.reshape(DEPTH, DEC_BATCH, MEM_LEN * MEM_HEADS, MEM_HEAD_DIM)
    cv_mem = cache_mem_v.reshape(DEPTH, DEC_BATCH, MEM_LEN * MEM_HEADS, MEM_HEAD_DIM)
    lg = jnp.log1p(-jnp.exp2(-5.0 - jnp.arange(RET_HEADS, dtype=F32)))

    x, h = _stack_and_prenorm(x_prompt, x_sample, vec(norm_mix_pre, 0))
    outs = {}
    for layer in range(DEPTH):
        j = layer // 2
        if layer % 2 == 0:
            q, kv, u, zv, zv_s = _ab_in_proj(h, w_ab_in[j], gmlp_ln_g[j].reshape(1, GMLP_WIDTH),
                                             gmlp_ln_b[j].reshape(1, GMLP_WIDTH), (swa_c, swa_s1, swa_s2))
            cat = _ab_mix_prompt(q, kv, u, zv, swa_sinks[j], gmlp_w_s[j], gmlp_b_s[j])
            cat, nk_s, nv_s = _ab_mix_sample(
                cat, q, kv, u, zv_s, cache_swa_k[j].reshape(DEC_BATCH, WINDOW, SWA_KV_W),
                cache_swa_v[j].reshape(DEC_BATCH, WINDOW, SWA_KV_W), swa_sinks[j], gmlp_w_s[j], gmlp_b_s[j])
            kv_p = kv[:MP].reshape(BATCH, SEQ, 2, SWA_KV_HEADS, SWA_HEAD_DIM)[:, SEQ - WINDOW:]
            outs.setdefault('swa_k_p', []).append(kv_p[:, :, 0])
            outs.setdefault('swa_v_p', []).append(kv_p[:, :, 1])
            outs.setdefault('swa_k_s', []).append(nk_s.reshape(DEC_BATCH, WINDOW, SWA_KV_HEADS, SWA_HEAD_DIM))
            outs.setdefault('swa_v_s', []).append(nv_s.reshape(DEC_BATCH, WINDOW, SWA_KV_HEADS, SWA_HEAD_DIM))
            outs.setdefault('gmlp_v_s', []).append(zv_s.reshape(DEC_BATCH, DEC_SEQ, GMLP_WIDTH))
            mixed, w_out, mix_name = cat, w_ab_out[j], "ab_out"
        else:
            z = _ret_in_proj(h, w_ret_in[j], (ret_c, ret_s))
            y, s_p = _ret_prompt(z, lg)
            y, s_s = _ret_sample(y, z, state_ret[j], lg)
            outs.setdefault('ret_p', []).append(s_p)
            outs.setdefault('ret_s', []).append(s_s)
            mixed, w_out, mix_name = y, w_ret_out[j], "ret_out"
        x, h = _out_proj(mixed, w_out.astype(BF16), x, vec(norm_mix_post, layer), vec(norm_x_pre, layer),
                         mix_name)
        wq_bf, wo_bf = w_mem_q[layer].astype(BF16), w_mem_o[layer].astype(BF16)
        g_post = vec(norm_x_post, layer)
        g_next = vec(norm_ffn_pre, layer)
        xn, hn = _mem_prompt(h, wq_bf, wo_bf, mem_k[layer], mem_v[layer], x, g_post, g_next)
        x, h = _mem_sample(xn, hn, h, wq_bf, wo_bf, ck_mem, cv_mem, layer, g_post, g_next)
        g_next = vec(norm_mix_pre, layer + 1) if layer + 1 < DEPTH else None
        x, h = _ffn(h, w_ffn_up[layer].astype(BF16), w_ffn_down[layer].astype(BF16), x,
                    vec(norm_ffn_post, layer), g_next)

    y_prompt, y_sample = x, h
    stack = lambda name: jnp.stack(outs[name])
    mem_shape = (DEPTH, BATCH, MEM_LEN, MEM_HEADS, MEM_HEAD_DIM)
    return (y_prompt.reshape(BATCH, SEQ, D_MODEL), y_sample.reshape(DEC_BATCH, DEC_SEQ, D_MODEL),
            stack('swa_k_p'), stack('swa_v_p'), stack('swa_k_s'), stack('swa_v_s'), stack('gmlp_v_s'),
            stack('ret_p'), stack('ret_s'), mem_k.reshape(mem_shape), mem_v.reshape(mem_shape))
```

```python
import functools
import math

import jax
import jax.numpy as jnp
from jax import lax
from jax.experimental import pallas as pl
from jax.experimental.pallas import tpu as pltpu

F32 = jnp.float32
BF16 = jnp.bfloat16

D_MODEL = 2048
BATCH = 2
SEQ = 4096
DEPTH = 2
DEC_BATCH = 128
DEC_SEQ = 8
PAST_LEN = 8192

SWA_HEADS = 16
SWA_KV_HEADS = 4
SWA_HEAD_DIM = 64
SWA_GROUP = SWA_HEADS // SWA_KV_HEADS
WINDOW = 128
ROPE_THETA = 500000.0
ROPE_DIM = SWA_HEAD_DIM // 4
ROPE_HALF = ROPE_DIM // 2

GMLP_GROUPS = 4
GMLP_WIDTH = 1024
GMLP_GROUP_DIM = GMLP_WIDTH // GMLP_GROUPS
GMLP_CHUNK = 128

SWA_Q_W = SWA_HEADS * SWA_HEAD_DIM
SWA_KV_W = SWA_KV_HEADS * SWA_HEAD_DIM
AB_IN = SWA_Q_W + 2 * SWA_KV_W + 2 * GMLP_WIDTH
AB_OUT = SWA_Q_W + GMLP_WIDTH

RET_HEADS = 8
RET_KEY_DIM = D_MODEL // RET_HEADS
RET_VAL_DIM = 2 * RET_KEY_DIM
RET_CHUNK = 128
RET_THETA = 10000.0
RET_QK_W = RET_HEADS * RET_KEY_DIM
RET_V_W = RET_HEADS * RET_VAL_DIM
RET_IN = 2 * RET_QK_W + 2 * RET_V_W

MEM_LEN = 256
MEM_HEADS = 4
MEM_HEAD_DIM = 128
MEM_W = MEM_HEADS * MEM_HEAD_DIM

D_FF = 4 * D_MODEL
EPS = 1e-6
NEG_INF = -1e30

MP = BATCH * SEQ
MS = DEC_BATCH * DEC_SEQ
M = MP + MS
LANES = 128
MIB = 1024 * 1024

TM_IN = 1024
TAB_ROWS = SEQ + MS
TM_OUT = 512


def _cparams(sem, vmem_mib):
    return pltpu.CompilerParams(dimension_semantics=sem, vmem_limit_bytes=vmem_mib * MIB)


def _rms(x, g):
    return x * lax.rsqrt(jnp.mean(x * x, axis=-1, keepdims=True) + EPS) * g


def _residual(y, x, g_post, g_next):
    xn = x + _rms(y, g_post)
    return xn, _rms(xn, g_next).astype(BF16)


def _rope_table_kernel(sc_ref, s1_ref, s2_ref, rc_ref, rs_ref):
    tr = sc_ref.shape[0]
    r = lax.broadcasted_iota(jnp.int32, (tr, LANES), 0) + pl.program_id(0) * tr
    pos = jnp.where(r < SEQ, r, PAST_LEN + ((r - SEQ) & (DEC_SEQ - 1))).astype(F32)
    lane = lax.broadcasted_iota(jnp.int32, (tr, LANES), 1)
    d = lane & (SWA_HEAD_DIM - 1)
    inv = jnp.exp((d & (ROPE_HALF - 1)).astype(F32) * (-math.log(ROPE_THETA) / ROPE_HALF))
    ang = pos * inv
    c, s = jnp.cos(ang), jnp.sin(ang)
    sc_ref[...] = jnp.where(d < ROPE_DIM, c, 1.0)
    s1_ref[...] = jnp.where(d < ROPE_HALF, -s, 0.0)
    s2_ref[...] = jnp.where((d >= ROPE_HALF) & (d < ROPE_DIM), s, 0.0)
    inv_r = jnp.exp(lane.astype(F32) * (-math.log(RET_THETA) / LANES))
    ang_r = pos * inv_r
    rc_ref[...] = jnp.cos(ang_r)
    rs_ref[...] = jnp.sin(ang_r)


def _rope_tables():
    tr = 512
    spec = pl.BlockSpec((tr, LANES), lambda i: (i, 0))
    shp = jax.ShapeDtypeStruct((TAB_ROWS, LANES), F32)
    return pl.pallas_call(
        _rope_table_kernel, grid=(TAB_ROWS // tr,), out_specs=[spec] * 5, out_shape=[shp] * 5,
        compiler_params=_cparams(("arbitrary",), 16), name="rope_tables")()


def _norm_kernel(xp_ref, xs_ref, g_ref, x_ref, h_ref):
    def emit(src_ref):
        x = src_ref[...]
        x_ref[...] = x
        h_ref[...] = _rms(x, g_ref[...]).astype(BF16)

    is_prompt = pl.program_id(0) < MP // x_ref.shape[0]
    pl.when(is_prompt)(lambda: emit(xp_ref))
    pl.when(jnp.logical_not(is_prompt))(lambda: emit(xs_ref))


def _stack_and_prenorm(x_prompt, x_sample, g):
    tm = TM_OUT
    row = pl.BlockSpec((tm, D_MODEL), lambda i: (i, 0))
    return pl.pallas_call(
        _norm_kernel, grid=(M // tm,),
        in_specs=[pl.BlockSpec((tm, D_MODEL), lambda i: (jnp.minimum(i, MP // tm - 1), 0)),
                  pl.BlockSpec((tm, D_MODEL), lambda i: (jnp.maximum(i - MP // tm, 0), 0)),
                  pl.BlockSpec((1, D_MODEL), lambda i: (0, 0))],
        out_specs=[row, row],
        out_shape=[jax.ShapeDtypeStruct((M, D_MODEL), F32), jax.ShapeDtypeStruct((M, D_MODEL), BF16)],
        compiler_params=_cparams(("arbitrary",), 40), name="prenorm",
    )(x_prompt.reshape(MP, D_MODEL), x_sample.reshape(MS, D_MODEL), g)


W_SUB = 512


def _inproj_kernel(h_ref, *refs, n_sub, n_side, epilogue):
    w_refs, side, outs, wbf_ref = refs[:n_sub], refs[n_sub:n_sub + n_side], refs[n_sub + n_side:-1], refs[-1]

    @pl.when(pl.program_id(1) == 0)
    def _():
        for s, w_ref in enumerate(w_refs):
            wbf_ref[:, s * W_SUB:(s + 1) * W_SUB] = w_ref[...].astype(BF16)

    acc = jnp.dot(h_ref[...], wbf_ref[...], preferred_element_type=F32)
    epilogue(acc, side, outs)


def _tab_index(j, i):
    return (jnp.where(i < MP // TM_IN, i % (SEQ // TM_IN), SEQ // TM_IN), 0)


def _inproj(h, w, col0, ncols, tn, epilogue, side, side_specs, out_shapes, out_specs, name, vmem_mib):
    tm = TM_IN
    k = h.shape[1]
    n_sub = tn // W_SUB
    w_specs = [pl.BlockSpec((k, W_SUB), lambda j, i, s=s: (0, col0 // W_SUB + j * n_sub + s))
               for s in range(n_sub)]
    return pl.pallas_call(
        functools.partial(_inproj_kernel, n_sub=n_sub, n_side=len(side), epilogue=epilogue),
        grid=(ncols // tn, M // tm),
        in_specs=[pl.BlockSpec((tm, k), lambda j, i: (i, 0))] + w_specs + side_specs,
        out_specs=out_specs, out_shape=out_shapes,
        scratch_shapes=[pltpu.VMEM((k, tn), BF16)],
        compiler_params=_cparams(("arbitrary", "arbitrary"), vmem_mib), name=name)(h, *([w] * n_sub), *side)


def _swa_rope(x, c, s1, s2):
    return x * c + pltpu.roll(x, LANES - ROPE_HALF, 1) * s1 + pltpu.roll(x, ROPE_HALF, 1) * s2


def _epi_swa_q(acc, side, outs):
    c, s1, s2 = (t[...] for t in side)
    (q_ref,) = outs
    for cc in range(acc.shape[1] // LANES):
        sl = slice(cc * LANES, (cc + 1) * LANES)
        q_ref[:, sl] = (_swa_rope(acc[:, sl], c, s1, s2) * (SWA_HEAD_DIM ** -0.5)).astype(BF16)


def _epi_swa_kv(acc, side, outs):
    c, s1, s2 = (t[...] for t in side)
    (kv_ref,) = outs
    for cc in range(SWA_KV_W // LANES):
        sl = slice(cc * LANES, (cc + 1) * LANES)
        kv_ref[:, sl] = _swa_rope(acc[:, sl], c, s1, s2)
    kv_ref[:, SWA_KV_W:] = acc[:, SWA_KV_W:]


def _epi_gelu(acc, side, outs):
    outs[0][...] = jax.nn.gelu(acc).astype(BF16)


def _epi_gelu_ln(acc, side, outs):
    g_ref, b_ref = side
    zv_ref, zvs_ref = outs
    a = jax.nn.gelu(acc)
    xc = a - jnp.mean(a, axis=-1, keepdims=True)
    y = xc * lax.rsqrt(jnp.mean(xc * xc, axis=-1, keepdims=True) + EPS) * g_ref[...] + b_ref[...]
    zv_ref[...] = y.astype(BF16)

    @pl.when(pl.program_id(1) == MP // TM_IN)
    def _():
        zvs_ref[...] = y


def _epi_ret(acc, side, outs, tn):
    cos, sin = (t[...] for t in side)
    (z_ref,) = outs
    j = pl.program_id(0)
    qk_tiles = 2 * RET_QK_W // tn

    @pl.when(j < qk_tiles)
    def _():
        scale = jnp.where(j < qk_tiles // 2, 1.0, RET_KEY_DIM ** -0.5)
        for hh in range(tn // RET_KEY_DIM):
            c0 = hh * RET_KEY_DIM
            x1, x2 = acc[:, c0:c0 + LANES], acc[:, c0 + LANES:c0 + 2 * LANES]
            z_ref[:, c0:c0 + LANES] = ((x1 * cos - x2 * sin) * scale).astype(BF16)
            z_ref[:, c0 + LANES:c0 + 2 * LANES] = ((x2 * cos + x1 * sin) * scale).astype(BF16)

    @pl.when((j >= qk_tiles) & (j < qk_tiles + RET_V_W // tn))
    def _():
        z_ref[...] = acc.astype(BF16)

    @pl.when(j >= qk_tiles + RET_V_W // tn)
    def _():
        z_ref[...] = (acc / (1.0 + jnp.exp(-acc))).astype(BF16)


def _ab_in_proj(h, w_in, ln_g, ln_b, tabs):
    sc, s1, s2 = tabs
    tab_spec = pl.BlockSpec((TM_IN, LANES), _tab_index)
    row_out = lambda tn: pl.BlockSpec((TM_IN, tn), lambda j, i: (i, j))
    q = _inproj(h, w_in, 0, SWA_Q_W, SWA_Q_W, _epi_swa_q, [sc, s1, s2], [tab_spec] * 3,
                jax.ShapeDtypeStruct((M, SWA_Q_W), BF16), row_out(SWA_Q_W), "ab_in_q", 52)
    kv = _inproj(h, w_in, SWA_Q_W, 2 * SWA_KV_W, 512, _epi_swa_kv, [sc, s1, s2], [tab_spec] * 3,
                 jax.ShapeDtypeStruct((M, 2 * SWA_KV_W), F32), row_out(512), "ab_in_kv", 40)
    u = _inproj(h, w_in, SWA_Q_W + 2 * SWA_KV_W, GMLP_WIDTH, GMLP_WIDTH, _epi_gelu, [], [],
                jax.ShapeDtypeStruct((M, GMLP_WIDTH), BF16), row_out(GMLP_WIDTH), "ab_in_u", 52)
    vec_spec = pl.BlockSpec((1, GMLP_WIDTH), lambda j, i: (0, 0))
    zv, zv_s = _inproj(
        h, w_in, AB_IN - GMLP_WIDTH, GMLP_WIDTH, GMLP_WIDTH, _epi_gelu_ln,
        [ln_g, ln_b], [vec_spec] * 2,
        [jax.ShapeDtypeStruct((M, GMLP_WIDTH), BF16), jax.ShapeDtypeStruct((MS, GMLP_WIDTH), F32)],
        [row_out(GMLP_WIDTH), pl.BlockSpec((MS, GMLP_WIDTH), lambda j, i: (0, 0))], "ab_in_zv", 52)
    return q, kv, u, zv, zv_s


def _ret_in_proj(h, w_in, tabs):
    rc, rs = tabs
    tn = 1024
    tab_spec = pl.BlockSpec((TM_IN, LANES), _tab_index)
    return _inproj(h, w_in, 0, RET_IN, tn, functools.partial(_epi_ret, tn=tn), [rc, rs], [tab_spec] * 2,
                   jax.ShapeDtypeStruct((M, RET_IN), BF16),
                   pl.BlockSpec((TM_IN, tn), lambda j, i: (i, j)), "ret_in", 52)


AB_ROWS = 512


def _sink_softmax(s, sink):
    m = jnp.maximum(jnp.max(s, axis=-1, keepdims=True), sink)
    p = jnp.exp(s - m)
    den = jnp.sum(p, axis=-1, keepdims=True) + jnp.exp(sink - m)
    return p, den


def _ab_prompt_kernel(sink_ref, q_ref, kv_ref, kvp_ref, u_ref, zv_ref, ws_ref, bs_ref, cat_ref,
                      kd_ref, vd_ref):
    r = pl.program_id(1)
    low = lax.broadcasted_iota(jnp.int32, (1, LANES), 1) < SWA_HEAD_DIM

    def both_halves(x, half):
        sw = pltpu.roll(x, SWA_HEAD_DIM, 1)
        return (jnp.where(low, x, sw) if half == 0 else jnp.where(low, sw, x)).astype(BF16)

    for kh in range(SWA_KV_HEADS):
        kcol = slice((kh // 2) * LANES, (kh // 2 + 1) * LANES)
        vcol = slice(SWA_KV_W + (kh // 2) * LANES, SWA_KV_W + (kh // 2 + 1) * LANES)
        kd_ref[kh, :WINDOW, :] = both_halves(kvp_ref[:, kcol], kh % 2)
        kd_ref[kh, WINDOW:, :] = both_halves(kv_ref[:, kcol], kh % 2)
        vd_ref[kh, :WINDOW, :] = both_halves(kvp_ref[:, vcol], kh % 2)
        vd_ref[kh, WINDOW:, :] = both_halves(kv_ref[:, vcol], kh % 2)

    t = lax.broadcasted_iota(jnp.int32, (WINDOW, 2 * WINDOW), 0)
    sk = lax.broadcasted_iota(jnp.int32, (WINDOW, 2 * WINDOW), 1)
    band = (sk <= t + WINDOW) & (sk > t)
    sink_slot = sk == t
    tril = (lax.broadcasted_iota(jnp.int32, (GMLP_CHUNK, GMLP_CHUNK), 0)
            >= lax.broadcasted_iota(jnp.int32, (GMLP_CHUNK, GMLP_CHUNK), 1))

    for sb in range(AB_ROWS // WINDOW):
        rs = slice(sb * WINDOW, (sb + 1) * WINDOW)
        ks = slice(sb * WINDOW, sb * WINDOW + 2 * WINDOW)
        valid = band & ((sk >= WINDOW) | (r * (AB_ROWS // WINDOW) + sb > 0))
        for kh in range(SWA_KV_HEADS):
            qm = []
            for g in range(SWA_GROUP):
                hd = kh * SWA_GROUP + g
                qb = q_ref[rs, (hd // 2) * LANES:(hd // 2 + 1) * LANES]
                qm.append(jnp.where(low if hd % 2 == 0 else jnp.logical_not(low), qb, jnp.zeros_like(qb)))
            s = lax.dot_general(jnp.concatenate(qm, axis=0), kd_ref[kh, ks, :], (((1,), (1,)), ((), ())),
                                preferred_element_type=F32)
            pv, den = [], []
            for g in range(SWA_GROUP):
                sg = s[g * WINDOW:(g + 1) * WINDOW]
                sg = jnp.where(valid, sg, jnp.where(sink_slot, sink_ref[kh * SWA_GROUP + g], NEG_INF))
                p = jnp.exp(sg - jnp.max(sg, axis=-1, keepdims=True))
                den.append(jnp.sum(p, axis=-1, keepdims=True))
                pv.append(jnp.where(sink_slot, 0.0, p).astype(BF16))
            o = jnp.dot(jnp.concatenate(pv, axis=0), vd_ref[kh, ks, :], preferred_element_type=F32)
            for pr in range(SWA_GROUP // 2):
                lo = o[(2 * pr) * WINDOW:(2 * pr + 1) * WINDOW] / den[2 * pr]
                hi = o[(2 * pr + 1) * WINDOW:(2 * pr + 2) * WINDOW] / den[2 * pr + 1]
                c0 = (kh * (SWA_GROUP // 2) + pr) * LANES
                cat_ref[rs, c0:c0 + LANES] = jnp.where(low, lo, hi).astype(BF16)
        zb = zv_ref[rs, :]
        ub = u_ref[rs, :]
        for g in range(GMLP_GROUPS):
            gs = slice(g * GMLP_GROUP_DIM, (g + 1) * GMLP_GROUP_DIM)
            w = jnp.where(tril, ws_ref[g], 0.0).astype(BF16)
            mixed = jnp.dot(w, zb[:, gs], preferred_element_type=F32) + bs_ref[g]
            cat_ref[rs, SWA_Q_W + g * GMLP_GROUP_DIM:SWA_Q_W + (g + 1) * GMLP_GROUP_DIM] = (
                ub[:, gs].astype(F32) * mixed).astype(BF16)


def _ab_mix_prompt(q, kv, u, zv, sinks, w_s, b_s):
    nr = SEQ // AB_ROWS
    sub = AB_ROWS // WINDOW
    row = lambda b, r: (b * nr + r, 0)
    return pl.pallas_call(
        _ab_prompt_kernel, grid=(BATCH, nr),
        in_specs=[pl.BlockSpec(memory_space=pltpu.SMEM),
                  pl.BlockSpec((AB_ROWS, SWA_Q_W), row),
                  pl.BlockSpec((AB_ROWS, 2 * SWA_KV_W), row),
                  pl.BlockSpec((WINDOW, 2 * SWA_KV_W), lambda b, r: (jnp.maximum((b * nr + r) * sub - 1, 0), 0)),
                  pl.BlockSpec((AB_ROWS, GMLP_WIDTH), row),
                  pl.BlockSpec((AB_ROWS, GMLP_WIDTH), row),
                  pl.BlockSpec((GMLP_GROUPS, GMLP_CHUNK, GMLP_CHUNK), lambda b, r: (0, 0, 0)),
                  pl.BlockSpec((GMLP_GROUPS, GMLP_CHUNK, 1), lambda b, r: (0, 0, 0))],
        out_specs=pl.BlockSpec((AB_ROWS, AB_OUT), row),
        out_shape=jax.ShapeDtypeStruct((M, AB_OUT), BF16),
        scratch_shapes=[pltpu.VMEM((SWA_KV_HEADS, WINDOW + AB_ROWS, LANES), BF16)] * 2,
        compiler_params=_cparams(("arbitrary", "arbitrary"), 40), name="ab_mix_prompt",
    )(sinks, q, kv, kv, u, zv, w_s, b_s.reshape(GMLP_GROUPS, GMLP_CHUNK, 1))


AB_NB = 16
AB_KEYS = 2 * WINDOW


def _ab_sample_kernel(sink_ref, q_ref, kv_ref, u_ref, zv_ref, ck_ref, cv_ref, ws_ref, bs_ref, cat_in_ref,
                      cat_ref, nk_ref, nv_ref):
    del cat_in_ref
    nb, t = AB_NB, DEC_SEQ
    kv3 = kv_ref[...].reshape(nb, t, 2 * SWA_KV_W)
    kn, vn = kv3[:, :, :SWA_KV_W], kv3[:, :, SWA_KV_W:]
    ck, cv = ck_ref[...], cv_ref[...]
    nk_ref[:, :WINDOW - t, :] = ck[:, t:, :]
    nk_ref[:, WINDOW - t:, :] = kn
    nv_ref[:, :WINDOW - t, :] = cv[:, t:, :]
    nv_ref[:, WINDOW - t:, :] = vn
    pad = jnp.zeros((nb, AB_KEYS - WINDOW - t, SWA_KV_W), F32)
    kall = jnp.concatenate([ck, kn, pad], axis=1).astype(BF16)
    vall = jnp.concatenate([cv, vn, pad], axis=1).astype(BF16)

    rows = SWA_GROUP * t
    tq = lax.broadcasted_iota(jnp.int32, (rows, AB_KEYS), 0) & (t - 1)
    sk = lax.broadcasted_iota(jnp.int32, (rows, AB_KEYS), 1)
    valid = ((sk < WINDOW) & (sk > tq)) | ((sk >= WINDOW) & (sk - WINDOW <= tq))

    q3 = q_ref[...].astype(F32).reshape(nb, t, SWA_Q_W)
    pieces = []
    for kh in range(SWA_KV_HEADS):
        heads = [kh * SWA_GROUP + g for g in range(SWA_GROUP)]
        qs = jnp.concatenate([q3[:, :, hd * SWA_HEAD_DIM:(hd + 1) * SWA_HEAD_DIM] for hd in heads],
                             axis=1).astype(BF16)
        cs = slice(kh * SWA_HEAD_DIM, (kh + 1) * SWA_HEAD_DIM)
        s = jnp.einsum('nqd,nkd->nqk', qs, kall[:, :, cs], preferred_element_type=F32)
        s = jnp.where(valid[None], s, NEG_INF)
        sink = jnp.concatenate([jnp.full((t, 1), sink_ref[hd], F32) for hd in heads], axis=0)[None]
        p, den = _sink_softmax(s, sink)
        o = jnp.einsum('nqk,nkd->nqd', p.astype(BF16), vall[:, :, cs], preferred_element_type=F32) / den
        pieces += [o[:, g * t:(g + 1) * t, :] for g in range(SWA_GROUP)]
    attn = jnp.concatenate(pieces, axis=2).reshape(nb * t, SWA_Q_W)
    cat_ref[:, :SWA_Q_W] = attn.astype(BF16)

    zv3 = zv_ref[...].reshape(nb, t, GMLP_WIDTH)
    u3 = u_ref[...].astype(F32).reshape(nb, t, GMLP_WIDTH)
    irow = lax.broadcasted_iota(jnp.int32, (t, GMLP_GROUP_DIM), 0)
    for g in range(GMLP_GROUPS):
        gs = slice(g * GMLP_GROUP_DIM, (g + 1) * GMLP_GROUP_DIM)
        zg = zv3[:, :, gs]
        mixed = jnp.zeros((nb, t, GMLP_GROUP_DIM), F32) + bs_ref[g][None]
        for j in range(t):
            wcol = jnp.where(irow >= j, ws_ref[g, j], 0.0)
            mixed = mixed + zg[:, j:j + 1, :] * wcol[None]
        gate = (u3[:, :, gs] * mixed).reshape(nb * t, GMLP_GROUP_DIM)
        cat_ref[:, SWA_Q_W + g * GMLP_GROUP_DIM:SWA_Q_W + (g + 1) * GMLP_GROUP_DIM] = gate.astype(BF16)


def _ab_mix_sample(cat, q, kv, u, zv_s, cache_k, cache_v, sinks, w_s, b_s):
    nb, t = AB_NB, DEC_SEQ
    rows = nb * t
    off = MP // rows
    srow = lambda i: (off + i, 0)
    wt = jnp.broadcast_to(jnp.swapaxes(w_s[:, :t, :t], 1, 2)[..., None], (GMLP_GROUPS, t, t, GMLP_GROUP_DIM))
    bt = jnp.broadcast_to(b_s[:, :t, None], (GMLP_GROUPS, t, GMLP_GROUP_DIM))
    cspec = pl.BlockSpec((nb, WINDOW, SWA_KV_W), lambda i: (i, 0, 0))
    cshape = jax.ShapeDtypeStruct((DEC_BATCH, WINDOW, SWA_KV_W), F32)
    return pl.pallas_call(
        _ab_sample_kernel, grid=(DEC_BATCH // nb,),
        in_specs=[pl.BlockSpec(memory_space=pltpu.SMEM),
                  pl.BlockSpec((rows, SWA_Q_W), srow),
                  pl.BlockSpec((rows, 2 * SWA_KV_W), srow),
                  pl.BlockSpec((rows, GMLP_WIDTH), srow),
                  pl.BlockSpec((rows, GMLP_WIDTH), lambda i: (i, 0)),
                  cspec, cspec,
                  pl.BlockSpec((GMLP_GROUPS, t, t, GMLP_GROUP_DIM), lambda i: (0, 0, 0, 0)),
                  pl.BlockSpec((GMLP_GROUPS, t, GMLP_GROUP_DIM), lambda i: (0, 0, 0)),
                  pl.BlockSpec(memory_space=pl.ANY)],
        out_specs=[pl.BlockSpec((rows, AB_OUT), srow), cspec, cspec],
        out_shape=[jax.ShapeDtypeStruct((M, AB_OUT), BF16), cshape, cshape],
        input_output_aliases={9: 0},
        compiler_params=_cparams(("arbitrary",), 48), name="ab_mix_sample",
    )(sinks, q, kv, u, zv_s, cache_k, cache_v, wt, bt, cat)


RET_ROWS = 1024


def _ret_prompt_part(lg, q_ref, k_ref, v_ref, g_ref, y_ref, sfin_ref, s_ref):
    c = pl.program_id(2)
    L = RET_CHUNK

    @pl.when(c == 0)
    def _():
        s_ref[...] = jnp.zeros_like(s_ref)

    diff = (lax.broadcasted_iota(jnp.int32, (L, L), 0) - lax.broadcasted_iota(jnp.int32, (L, L), 1)).astype(F32)
    dmat = jnp.where(diff >= 0, jnp.exp(lg * jnp.maximum(diff, 0.0)), 0.0)
    idx = lax.broadcasted_iota(jnp.int32, (L, 1), 0).astype(F32)
    q_dec = jnp.exp(lg * (idx + 1.0))
    k_dec = jnp.exp(lg * (L - 1.0 - idx))
    s_dec = jnp.exp(lg * jnp.full((1, 1), float(L), F32))

    def chunk(ci, carry):
        rs = pl.ds(pl.multiple_of(ci * L, L), L)
        q, k, v = q_ref[rs, :], k_ref[rs, :], v_ref[rs, :]
        att = lax.dot_general(q, k, (((1,), (1,)), ((), ())), preferred_element_type=F32) * dmat
        s0 = s_ref[...]
        o = (jnp.dot(att.astype(BF16), v, preferred_element_type=F32)
             + jnp.dot(q, s0.astype(BF16), preferred_element_type=F32) * q_dec)
        kd = (k.astype(F32) * k_dec).astype(BF16)
        s_ref[...] = s0 * s_dec + lax.dot_general(kd, v, (((0,), (0,)), ((), ())), preferred_element_type=F32)
        on = o * lax.rsqrt(jnp.mean(o * o, axis=-1, keepdims=True) + EPS)
        y_ref[rs, :] = (on * g_ref[rs, :].astype(F32)).astype(BF16)
        return carry

    lax.fori_loop(0, RET_ROWS // L, chunk, 0, unroll=2)

    @pl.when(c == pl.num_programs(2) - 1)
    def _():
        sfin_ref[...] = s_ref[...]


RET_NB = DEC_BATCH // (BATCH * (SEQ // RET_ROWS))


def _ret_sample_part(lg, q_ref, k_ref, v_ref, g_ref, s0_ref, y_ref, s1_ref):
    nb, t = RET_NB, DEC_SEQ
    rows = nb * t
    q, k, v = q_ref[...], k_ref[...], v_ref[...]
    ri = lax.broadcasted_iota(jnp.int32, (rows, rows), 0)
    ci = lax.broadcasted_iota(jnp.int32, (rows, rows), 1)
    d = ((ri & (t - 1)) - (ci & (t - 1))).astype(F32)
    same = (ri // t) == (ci // t)
    dmat = jnp.where(same & (d >= 0), jnp.exp(lg * jnp.maximum(d, 0.0)), 0.0)
    att = lax.dot_general(q, k, (((1,), (1,)), ((), ())), preferred_element_type=F32) * dmat
    o_intra = jnp.dot(att.astype(BF16), v, preferred_element_type=F32)

    rcol = lax.broadcasted_iota(jnp.int32, (rows, 1), 0)
    tt = (rcol & (t - 1)).astype(F32)
    q_dec = jnp.exp(lg * (tt + 1.0))
    k_dec = jnp.exp(lg * (t - 1.0 - tt))
    s_dec = jnp.exp(lg * jnp.full((1, 1), float(t), F32))
    kd = (k.astype(F32) * k_dec).astype(BF16)
    qf = q.astype(F32)
    vf = v.astype(F32)
    o_rows = []
    for n in range(nb):
        s0 = s0_ref[n]
        qn = qf[n * t:(n + 1) * t].astype(BF16)
        o_rows.append(jnp.dot(qn, s0.astype(BF16), preferred_element_type=F32))
        vn = jnp.where(rcol // t == n, vf, 0.0).astype(BF16)
        s1_ref[n] = s0 * s_dec + lax.dot_general(kd, vn, (((0,), (0,)), ((), ())), preferred_element_type=F32)
    o = o_intra + jnp.concatenate(o_rows, axis=0) * q_dec
    on = o * lax.rsqrt(jnp.mean(o * o, axis=-1, keepdims=True) + EPS)
    y_ref[...] = (on * g_ref[...].astype(F32)).astype(BF16)


def _retention_kernel(lg_ref, q_ref, k_ref, v_ref, g_ref, sq_ref, sk_ref, sv_ref, sg_ref, s0_ref,
                      yp_ref, sfin_ref, ys_ref, s1_ref, s_ref):
    lg = lg_ref[pl.program_id(1)]
    _ret_prompt_part(lg, q_ref, k_ref, v_ref, g_ref, yp_ref, sfin_ref, s_ref)
    _ret_sample_part(lg, sq_ref, sk_ref, sv_ref, sg_ref, s0_ref, ys_ref, s1_ref)


def _retention(z, state, lg):
    nc = SEQ // RET_ROWS
    kb = RET_QK_W // RET_KEY_DIM
    vb = 2 * RET_QK_W // RET_VAL_DIM
    gb = vb + RET_HEADS
    srows = RET_NB * DEC_SEQ
    soff = MP // srows
    prow = lambda b, c: b * nc + c
    head_specs = lambda rows, row: [
        pl.BlockSpec((rows, RET_KEY_DIM), lambda b, h, c: (row(b, c), h)),
        pl.BlockSpec((rows, RET_KEY_DIM), lambda b, h, c: (row(b, c), kb + h)),
        pl.BlockSpec((rows, RET_VAL_DIM), lambda b, h, c: (row(b, c), vb + h)),
        pl.BlockSpec((rows, RET_VAL_DIM), lambda b, h, c: (row(b, c), gb + h))]
    sspec = pl.BlockSpec((RET_NB, None, RET_KEY_DIM, RET_VAL_DIM), lambda b, h, c: (prow(b, c), h, 0, 0))
    return pl.pallas_call(
        _retention_kernel, grid=(BATCH, RET_HEADS, nc),
        in_specs=([pl.BlockSpec(memory_space=pltpu.SMEM)] + head_specs(RET_ROWS, prow)
                  + head_specs(srows, lambda b, c: soff + prow(b, c)) + [sspec]),
        out_specs=[pl.BlockSpec((RET_ROWS, RET_VAL_DIM), lambda b, h, c: (prow(b, c), h)),
                   pl.BlockSpec((None, None, RET_KEY_DIM, RET_VAL_DIM), lambda b, h, c: (b, h, 0, 0)),
                   pl.BlockSpec((srows, RET_VAL_DIM), lambda b, h, c: (prow(b, c), h)),
                   sspec],
        out_shape=[jax.ShapeDtypeStruct((MP, RET_V_W), BF16),
                   jax.ShapeDtypeStruct((BATCH, RET_HEADS, RET_KEY_DIM, RET_VAL_DIM), F32),
                   jax.ShapeDtypeStruct((MS, RET_V_W), BF16),
                   jax.ShapeDtypeStruct((DEC_BATCH, RET_HEADS, RET_KEY_DIM, RET_VAL_DIM), F32)],
        scratch_shapes=[pltpu.VMEM((RET_KEY_DIM, RET_VAL_DIM), F32)],
        compiler_params=_cparams(("arbitrary", "arbitrary", "arbitrary"), 56), name="retention",
    )(lg, z, z, z, z, z, z, z, z, state)


def _accumulate(acc_ref, part, k):
    @pl.when(k == 0)
    def _():
        acc_ref[...] = part

    @pl.when(k > 0)
    def _():
        acc_ref[...] += part


def _outproj_kernel(*refs, n_prompt_tiles):
    a_refs, (w_ref, x_ref, gp_ref, gn_ref, xn_ref, hn_ref) = refs[:-6], refs[-6:]

    def emit(a_ref):
        y = jnp.dot(a_ref[...], w_ref[...], preferred_element_type=F32)
        xn, hn = _residual(y, x_ref[...], gp_ref[...], gn_ref[...])
        xn_ref[...] = xn
        hn_ref[...] = hn

    if len(a_refs) == 1:
        emit(a_refs[0])
    else:
        is_prompt = pl.program_id(0) < n_prompt_tiles
        pl.when(is_prompt)(lambda: emit(a_refs[0]))
        pl.when(jnp.logical_not(is_prompt))(lambda: emit(a_refs[1]))


def _out_proj(a_parts, w_bf, x, g_post, g_next, name):
    kdim = w_bf.shape[0]
    tm = TM_OUT * D_MODEL // kdim
    np_t = MP // tm
    row = pl.BlockSpec((tm, D_MODEL), lambda i: (i, 0))
    vec = pl.BlockSpec((1, D_MODEL), lambda i: (0, 0))
    if len(a_parts) == 1:
        a_specs = [pl.BlockSpec((tm, kdim), lambda i: (i, 0))]
    else:
        a_specs = [pl.BlockSpec((tm, kdim), lambda i: (jnp.minimum(i, np_t - 1), 0)),
                   pl.BlockSpec((tm, kdim), lambda i: (jnp.maximum(i - np_t, 0), 0))]
    return pl.pallas_call(
        functools.partial(_outproj_kernel, n_prompt_tiles=np_t), grid=(M // tm,),
        in_specs=a_specs + [pl.BlockSpec((kdim, D_MODEL), lambda i: (0, 0), pipeline_mode=pl.Buffered(1)),
                            row, vec, vec],
        out_specs=[row, row],
        out_shape=[jax.ShapeDtypeStruct((M, D_MODEL), F32), jax.ShapeDtypeStruct((M, D_MODEL), BF16)],
        compiler_params=_cparams(("arbitrary",), 48), name=name,
    )(*a_parts, w_bf, x, g_post, g_next)


def _ffn_up_down(h_ref, wu_ref, wd_ref):
    a = jnp.maximum(jnp.dot(h_ref[...], wu_ref[...], preferred_element_type=F32), 0.0)
    return jnp.dot((a * a).astype(BF16), wd_ref[...], preferred_element_type=F32)


def _ffn_kernel(h_ref, wu_ref, wd_ref, x_ref, gp_ref, gn_ref, xn_ref, hn_ref):
    c = pl.program_id(1)
    _accumulate(xn_ref, _ffn_up_down(h_ref, wu_ref, wd_ref), c)

    @pl.when(c == pl.num_programs(1) - 1)
    def _():
        xn, hn = _residual(xn_ref[...], x_ref[...], gp_ref[...], gn_ref[...])
        xn_ref[...] = xn
        hn_ref[...] = hn


def _ffn_last_kernel(h_ref, wu_ref, wd_ref, x_ref, gp_ref, yp_ref, ys_ref, acc_ref):
    i, c = pl.program_id(0), pl.program_id(1)
    _accumulate(acc_ref, _ffn_up_down(h_ref, wu_ref, wd_ref), c)
    last = c == pl.num_programs(1) - 1
    is_prompt = i < MP // acc_ref.shape[0]

    @pl.when(last & is_prompt)
    def _():
        yp_ref[...] = x_ref[...] + _rms(acc_ref[...], gp_ref[...])

    @pl.when(last & jnp.logical_not(is_prompt))
    def _():
        ys_ref[...] = x_ref[...] + _rms(acc_ref[...], gp_ref[...])


def _ffn(h, wu_bf, wd_bf, layer, x, g_post, g_next):
    tm, tc = TM_OUT, 1024
    row = pl.BlockSpec((tm, D_MODEL), lambda i, c: (i, 0))
    vec = pl.BlockSpec((1, D_MODEL), lambda i, c: (0, 0))
    w_specs = [pl.BlockSpec((None, D_MODEL, tc), lambda i, c: (layer, 0, c)),
               pl.BlockSpec((None, tc, D_MODEL), lambda i, c: (layer, c, 0))]
    grid = (M // tm, D_FF // tc)
    if g_next is not None:
        return pl.pallas_call(
            _ffn_kernel, grid=grid, in_specs=[row] + w_specs + [row, vec, vec], out_specs=[row, row],
            out_shape=[jax.ShapeDtypeStruct((M, D_MODEL), F32), jax.ShapeDtypeStruct((M, D_MODEL), BF16)],
            compiler_params=_cparams(("arbitrary", "arbitrary"), 56), name="ffn",
        )(h, wu_bf, wd_bf, x, g_post, g_next)
    np_t = MP // tm
    return pl.pallas_call(
        _ffn_last_kernel, grid=grid, in_specs=[row] + w_specs + [row, vec],
        out_specs=[pl.BlockSpec((tm, D_MODEL), lambda i, c: (jnp.minimum(i, np_t - 1), 0)),
                   pl.BlockSpec((tm, D_MODEL), lambda i, c: (jnp.maximum(i - np_t, 0), 0))],
        out_shape=[jax.ShapeDtypeStruct((MP, D_MODEL), F32), jax.ShapeDtypeStruct((MS, D_MODEL), F32)],
        scratch_shapes=[pltpu.VMEM((tm, D_MODEL), F32)],
        compiler_params=_cparams(("arbitrary", "arbitrary"), 56), name="ffn_last",
    )(h, wu_bf, wd_bf, x, g_post)


def _mem_kv_kernel(m_ref, g_ref, wk_ref, wv_ref, k_ref, v_ref):
    m = _rms(m_ref[...], g_ref[0]).astype(BF16)
    k_ref[0] = jnp.dot(m, wk_ref[0].astype(BF16), preferred_element_type=F32)
    v_ref[0] = jnp.dot(m, wv_ref[0].astype(BF16), preferred_element_type=F32)


def _mem_kv(mem, g, w_k, w_v):
    rows = BATCH * MEM_LEN
    wspec = pl.BlockSpec((1, D_MODEL, MEM_W), lambda l: (l, 0, 0))
    ospec = pl.BlockSpec((1, rows, MEM_W), lambda l: (l, 0, 0))
    oshape = jax.ShapeDtypeStruct((DEPTH, rows, MEM_W), F32)
    return pl.pallas_call(
        _mem_kv_kernel, grid=(DEPTH,),
        in_specs=[pl.BlockSpec((rows, D_MODEL), lambda l: (0, 0)),
                  pl.BlockSpec((1, 1, D_MODEL), lambda l: (l, 0, 0)), wspec, wspec],
        out_specs=[ospec, ospec], out_shape=[oshape, oshape],
        compiler_params=_cparams(("arbitrary",), 48), name="mem_kv",
    )(mem.reshape(rows, D_MODEL), g.reshape(DEPTH, 1, D_MODEL), w_k, w_v)


def _mem_prompt_kernel(h_ref, wq_ref, wo_ref, mk_ref, mv_ref, x_ref, gp_ref, gn_ref, xn_ref, hn_ref):
    q = jnp.dot(h_ref[...], wq_ref[...], preferred_element_type=F32).astype(BF16)
    mk, mv = mk_ref[...].astype(BF16), mv_ref[...].astype(BF16)
    outs = []
    for hh in range(MEM_HEADS):
        cs = slice(hh * MEM_HEAD_DIM, (hh + 1) * MEM_HEAD_DIM)
        s = lax.dot_general(q[:, cs], mk[:, cs], (((1,), (1,)), ((), ())),
                            preferred_element_type=F32) * (MEM_HEAD_DIM ** -0.5)
        p = jnp.exp(s - jnp.max(s, axis=-1, keepdims=True))
        den = jnp.sum(p, axis=-1, keepdims=True)
        outs.append(jnp.dot(p.astype(BF16), mv[:, cs], preferred_element_type=F32) / den)
    o = jnp.concatenate(outs, axis=1).astype(BF16)
    y = jnp.dot(o, wo_ref[...], preferred_element_type=F32)
    xn, hn = _residual(y, x_ref[...], gp_ref[...], gn_ref[...])
    xn_ref[...] = xn
    hn_ref[...] = hn


def _mem_prompt(h, wq_bf, wo_bf, mk, mv, x, g_post, g_next):
    tm = TM_OUT
    per_b = SEQ // tm
    row = pl.BlockSpec((tm, D_MODEL), lambda i: (i, 0))
    vec = pl.BlockSpec((1, D_MODEL), lambda i: (0, 0))
    kvspec = pl.BlockSpec((MEM_LEN, MEM_W), lambda i: (i // per_b, 0))
    return pl.pallas_call(
        _mem_prompt_kernel, grid=(MP // tm,),
        in_specs=[row, pl.BlockSpec((D_MODEL, MEM_W), lambda i: (0, 0)),
                  pl.BlockSpec((MEM_W, D_MODEL), lambda i: (0, 0)), kvspec, kvspec, row, vec, vec],
        out_specs=[row, row],
        out_shape=[jax.ShapeDtypeStruct((M, D_MODEL), F32), jax.ShapeDtypeStruct((M, D_MODEL), BF16)],
        input_output_aliases={5: 0},
        compiler_params=_cparams(("arbitrary",), 40), name="mem_prompt",
    )(h, wq_bf, wo_bf, mk, mv, x, g_post, g_next)


MEM_NB = 8


def _mem_sample_kernel(h_ref, wq_ref, wo_ref, ck_ref, cv_ref, x_ref, gp_ref, gn_ref, hn_in_ref,
                       xn_ref, hn_ref):
    del hn_in_ref
    nb, t = MEM_NB, DEC_SEQ
    q = jnp.dot(h_ref[...], wq_ref[...], preferred_element_type=F32)
    qs = jnp.concatenate([q[:, hh * MEM_HEAD_DIM:(hh + 1) * MEM_HEAD_DIM].reshape(nb, t, MEM_HEAD_DIM)
                          for hh in range(MEM_HEADS)], axis=1).astype(BF16)
    s = jnp.einsum('nqd,nkd->nqk', qs, ck_ref[...].astype(BF16),
                   preferred_element_type=F32) * (MEM_HEAD_DIM ** -0.5)
    shape = (MEM_HEADS * t, MEM_LEN * MEM_HEADS)
    own = ((lax.broadcasted_iota(jnp.int32, shape, 1) & (MEM_HEADS - 1))
           == lax.broadcasted_iota(jnp.int32, shape, 0) // t)
    s = jnp.where(own[None], s, NEG_INF)
    p = jnp.exp(s - jnp.max(s, axis=-1, keepdims=True))
    den = jnp.sum(p, axis=-1, keepdims=True)
    o3 = jnp.einsum('nqk,nkd->nqd', p.astype(BF16), cv_ref[...].astype(BF16),
                    preferred_element_type=F32) / den
    o = jnp.concatenate([o3[:, hh * t:(hh + 1) * t, :].reshape(nb * t, MEM_HEAD_DIM)
                         for hh in range(MEM_HEADS)], axis=1).astype(BF16)
    y = jnp.dot(o, wo_ref[...], preferred_element_type=F32)
    xn, hn = _residual(y, x_ref[...], gp_ref[...], gn_ref[...])
    xn_ref[...] = xn
    hn_ref[...] = hn


def _mem_sample(xn, hn, h, wq_bf, wo_bf, cache_k, cache_v, layer, g_post, g_next):
    nb = MEM_NB
    rows = nb * DEC_SEQ
    off = MP // rows
    row = pl.BlockSpec((rows, D_MODEL), lambda i: (off + i, 0))
    vec = pl.BlockSpec((1, D_MODEL), lambda i: (0, 0))
    cspec = pl.BlockSpec((None, nb, MEM_LEN * MEM_HEADS, MEM_HEAD_DIM), lambda i: (layer, i, 0, 0))
    anyspec = pl.BlockSpec(memory_space=pl.ANY)
    return pl.pallas_call(
        _mem_sample_kernel, grid=(DEC_BATCH // nb,),
        in_specs=[row, pl.BlockSpec((D_MODEL, MEM_W), lambda i: (0, 0)),
                  pl.BlockSpec((MEM_W, D_MODEL), lambda i: (0, 0)), cspec, cspec, row, vec, vec, anyspec],
        out_specs=[row, row],
        out_shape=[jax.ShapeDtypeStruct((M, D_MODEL), F32), jax.ShapeDtypeStruct((M, D_MODEL), BF16)],
        input_output_aliases={5: 0, 8: 1},
        compiler_params=_cparams(("arbitrary",), 40), name="mem_sample",
    )(h, wq_bf, wo_bf, cache_k, cache_v, xn, g_post, g_next, hn)


def kernel(x_prompt, x_sample, cache_swa_k, cache_swa_v, state_ret, cache_mem_k, cache_mem_v, mem_prompt, norm_mix_pre, norm_mix_post, norm_mem, norm_x_pre, norm_x_post, norm_ffn_pre, norm_ffn_post, w_ab_in, w_ab_out, swa_sinks, gmlp_ln_g, gmlp_ln_b, gmlp_w_s, gmlp_b_s, w_ret_in, w_ret_out, w_mem_q, w_mem_k, w_mem_v, w_mem_o, w_ffn_up, w_ffn_down):
    vec = lambda g, layer: g[layer].reshape(1, D_MODEL)
    swa_c, swa_s1, swa_s2, ret_c, ret_s = _rope_tables()
    mem_k, mem_v = _mem_kv(mem_prompt, norm_mem, w_mem_k, w_mem_v)
    ck_mem = cache_mem_k.reshape(DEPTH, DEC_BATCH, MEM_LEN * MEM_HEADS, MEM_HEAD_DIM)
    cv_mem = cache_mem_v.reshape(DEPTH, DEC_BATCH, MEM_LEN * MEM_HEADS, MEM_HEAD_DIM)
    lg = jnp.log1p(-jnp.exp2(-5.0 - jnp.arange(RET_HEADS, dtype=F32)))
    wu_bf, wd_bf = w_ffn_up.astype(BF16), w_ffn_down.astype(BF16)

    x, h = _stack_and_prenorm(x_prompt, x_sample, vec(norm_mix_pre, 0))
    outs = {}
    for layer in range(DEPTH):
        j = layer // 2
        if layer % 2 == 0:
            q, kv, u, zv, zv_s = _ab_in_proj(h, w_ab_in[j], gmlp_ln_g[j].reshape(1, GMLP_WIDTH),
                                             gmlp_ln_b[j].reshape(1, GMLP_WIDTH), (swa_c, swa_s1, swa_s2))
            cat = _ab_mix_prompt(q, kv, u, zv, swa_sinks[j], gmlp_w_s[j], gmlp_b_s[j])
            cat, nk_s, nv_s = _ab_mix_sample(
                cat, q, kv, u, zv_s, cache_swa_k[j].reshape(DEC_BATCH, WINDOW, SWA_KV_W),
                cache_swa_v[j].reshape(DEC_BATCH, WINDOW, SWA_KV_W), swa_sinks[j], gmlp_w_s[j], gmlp_b_s[j])
            kv_p = jnp.stack([kv[(b + 1) * SEQ - WINDOW:(b + 1) * SEQ] for b in range(BATCH)])
            kv_p = kv_p.reshape(BATCH, WINDOW, 2, SWA_KV_HEADS, SWA_HEAD_DIM)
            outs.setdefault('swa_k_p', []).append(kv_p[:, :, 0])
            outs.setdefault('swa_v_p', []).append(kv_p[:, :, 1])
            outs.setdefault('swa_k_s', []).append(nk_s.reshape(DEC_BATCH, WINDOW, SWA_KV_HEADS, SWA_HEAD_DIM))
            outs.setdefault('swa_v_s', []).append(nv_s.reshape(DEC_BATCH, WINDOW, SWA_KV_HEADS, SWA_HEAD_DIM))
            outs.setdefault('gmlp_v_s', []).append(zv_s.reshape(DEC_BATCH, DEC_SEQ, GMLP_WIDTH))
            mixed, w_out, mix_name = (cat,), w_ab_out[j], "ab_out"
        else:
            z = _ret_in_proj(h, w_ret_in[j], (ret_c, ret_s))
            y_p, s_p, y_s, s_s = _retention(z, state_ret[j], lg)
            outs.setdefault('ret_p', []).append(s_p)
            outs.setdefault('ret_s', []).append(s_s)
            mixed, w_out, mix_name = (y_p, y_s), w_ret_out[j], "ret_out"
        x, h = _out_proj(mixed, w_out.astype(BF16), x, vec(norm_mix_post, layer), vec(norm_x_pre, layer),
                         mix_name)
        wq_bf, wo_bf = w_mem_q[layer].astype(BF16), w_mem_o[layer].astype(BF16)
        g_post = vec(norm_x_post, layer)
        g_next = vec(norm_ffn_pre, layer)
        xn, hn = _mem_prompt(h, wq_bf, wo_bf, mem_k[layer], mem_v[layer], x, g_post, g_next)
        x, h = _mem_sample(xn, hn, h, wq_bf, wo_bf, ck_mem, cv_mem, layer, g_post, g_next)
        g_next = vec(norm_mix_pre, layer + 1) if layer + 1 < DEPTH else None
        x, h = _ffn(h, wu_bf, wd_bf, layer, x, vec(norm_ffn_post, layer), g_next)

    y_prompt, y_sample = x, h
    stack = lambda name: jnp.stack(outs[name])
    mem_shape = (DEPTH, BATCH, MEM_LEN, MEM_HEADS, MEM_HEAD_DIM)
    return (y_prompt.reshape(BATCH, SEQ, D_MODEL), y_sample.reshape(DEC_BATCH, DEC_SEQ, D_MODEL),
            stack('swa_k_p'), stack('swa_v_p'), stack('swa_k_s'), stack('swa_v_s'), stack('gmlp_v_s'),
            stack('ret_p'), stack('ret_s'), mem_k.reshape(mem_shape), mem_v.reshape(mem_shape))
```

```python
import functools
import math

import jax
import jax.numpy as jnp
from jax import lax
from jax.experimental import pallas as pl
from jax.experimental.pallas import tpu as pltpu

F32 = jnp.float32
BF16 = jnp.bfloat16

D_MODEL = 2048
BATCH = 2
SEQ = 4096
DEPTH = 2
DEC_BATCH = 128
DEC_SEQ = 8
PAST_LEN = 8192

SWA_HEADS = 16
SWA_KV_HEADS = 4
SWA_HEAD_DIM = 64
SWA_GROUP = SWA_HEADS // SWA_KV_HEADS
WINDOW = 128
ROPE_THETA = 500000.0
ROPE_DIM = SWA_HEAD_DIM // 4
ROPE_HALF = ROPE_DIM // 2

GMLP_GROUPS = 4
GMLP_WIDTH = 1024
GMLP_GROUP_DIM = GMLP_WIDTH // GMLP_GROUPS
GMLP_CHUNK = 128

SWA_Q_W = SWA_HEADS * SWA_HEAD_DIM
SWA_KV_W = SWA_KV_HEADS * SWA_HEAD_DIM
AB_IN = SWA_Q_W + 2 * SWA_KV_W + 2 * GMLP_WIDTH
AB_OUT = SWA_Q_W + GMLP_WIDTH

RET_HEADS = 8
RET_KEY_DIM = D_MODEL // RET_HEADS
RET_VAL_DIM = 2 * RET_KEY_DIM
RET_CHUNK = 128
RET_THETA = 10000.0
RET_QK_W = RET_HEADS * RET_KEY_DIM
RET_V_W = RET_HEADS * RET_VAL_DIM
RET_IN = 2 * RET_QK_W + 2 * RET_V_W

MEM_LEN = 256
MEM_HEADS = 4
MEM_HEAD_DIM = 128
MEM_W = MEM_HEADS * MEM_HEAD_DIM

D_FF = 4 * D_MODEL
EPS = 1e-6
NEG_INF = -1e30

MP = BATCH * SEQ
MS = DEC_BATCH * DEC_SEQ
M = MP + MS
LANES = 128
MIB = 1024 * 1024

TM_IN = 1024
TAB_ROWS = SEQ + MS
TM_OUT = 512


def _cparams(sem, vmem_mib):
    return pltpu.CompilerParams(dimension_semantics=sem, vmem_limit_bytes=vmem_mib * MIB)


def _rms(x, g):
    return x * lax.rsqrt(jnp.mean(x * x, axis=-1, keepdims=True) + EPS) * g


def _residual(y, x, g_post, g_next):
    xn = x + _rms(y, g_post)
    return xn, _rms(xn, g_next).astype(BF16)


def _rope_table_kernel(sc_ref, s1_ref, s2_ref, rc_ref, rs_ref):
    tr = sc_ref.shape[0]
    r = lax.broadcasted_iota(jnp.int32, (tr, LANES), 0) + pl.program_id(0) * tr
    pos = jnp.where(r < SEQ, r, PAST_LEN + ((r - SEQ) & (DEC_SEQ - 1))).astype(F32)
    lane = lax.broadcasted_iota(jnp.int32, (tr, LANES), 1)
    d = lane & (SWA_HEAD_DIM - 1)
    inv = jnp.exp((d & (ROPE_HALF - 1)).astype(F32) * (-math.log(ROPE_THETA) / ROPE_HALF))
    ang = pos * inv
    c, s = jnp.cos(ang), jnp.sin(ang)
    sc_ref[...] = jnp.where(d < ROPE_DIM, c, 1.0)
    s1_ref[...] = jnp.where(d < ROPE_HALF, -s, 0.0)
    s2_ref[...] = jnp.where((d >= ROPE_HALF) & (d < ROPE_DIM), s, 0.0)
    inv_r = jnp.exp(lane.astype(F32) * (-math.log(RET_THETA) / LANES))
    ang_r = pos * inv_r
    rc_ref[...] = jnp.cos(ang_r)
    rs_ref[...] = jnp.sin(ang_r)


def _rope_tables():
    tr = 512
    spec = pl.BlockSpec((tr, LANES), lambda i: (i, 0))
    shp = jax.ShapeDtypeStruct((TAB_ROWS, LANES), F32)
    return pl.pallas_call(
        _rope_table_kernel, grid=(TAB_ROWS // tr,), out_specs=[spec] * 5, out_shape=[shp] * 5,
        compiler_params=_cparams(("arbitrary",), 16), name="rope_tables")()


def _norm_kernel(xp_ref, xs_ref, g_ref, x_ref, h_ref):
    def emit(src_ref):
        x = src_ref[...]
        x_ref[...] = x
        h_ref[...] = _rms(x, g_ref[...]).astype(BF16)

    is_prompt = pl.program_id(0) < MP // x_ref.shape[0]
    pl.when(is_prompt)(lambda: emit(xp_ref))
    pl.when(jnp.logical_not(is_prompt))(lambda: emit(xs_ref))


def _stack_and_prenorm(x_prompt, x_sample, g):
    tm = TM_OUT
    row = pl.BlockSpec((tm, D_MODEL), lambda i: (i, 0))
    return pl.pallas_call(
        _norm_kernel, grid=(M // tm,),
        in_specs=[pl.BlockSpec((tm, D_MODEL), lambda i: (jnp.minimum(i, MP // tm - 1), 0)),
                  pl.BlockSpec((tm, D_MODEL), lambda i: (jnp.maximum(i - MP // tm, 0), 0)),
                  pl.BlockSpec((1, D_MODEL), lambda i: (0, 0))],
        out_specs=[row, row],
        out_shape=[jax.ShapeDtypeStruct((M, D_MODEL), F32), jax.ShapeDtypeStruct((M, D_MODEL), BF16)],
        compiler_params=_cparams(("arbitrary",), 40), name="prenorm",
    )(x_prompt.reshape(MP, D_MODEL), x_sample.reshape(MS, D_MODEL), g)


W_SUB = 512


def _inproj_kernel(h_ref, *refs, n_sub, n_side, epilogue):
    w_refs, side, outs, wbf_ref = refs[:n_sub], refs[n_sub:n_sub + n_side], refs[n_sub + n_side:-1], refs[-1]

    @pl.when(pl.program_id(1) == 0)
    def _():
        for s, w_ref in enumerate(w_refs):
            wbf_ref[:, s * W_SUB:(s + 1) * W_SUB] = w_ref[...].astype(BF16)

    acc = jnp.dot(h_ref[...], wbf_ref[...], preferred_element_type=F32)
    epilogue(acc, side, outs)


def _tab_index(j, i):
    return (jnp.where(i < MP // TM_IN, i % (SEQ // TM_IN), SEQ // TM_IN), 0)


def _inproj(h, w, col0, ncols, tn, epilogue, side, side_specs, out_shapes, out_specs, name, vmem_mib):
    tm = TM_IN
    k = h.shape[1]
    n_sub = tn // W_SUB
    w_specs = [pl.BlockSpec((k, W_SUB), lambda j, i, s=s: (0, col0 // W_SUB + j * n_sub + s))
               for s in range(n_sub)]
    return pl.pallas_call(
        functools.partial(_inproj_kernel, n_sub=n_sub, n_side=len(side), epilogue=epilogue),
        grid=(ncols // tn, M // tm),
        in_specs=[pl.BlockSpec((tm, k), lambda j, i: (i, 0))] + w_specs + side_specs,
        out_specs=out_specs, out_shape=out_shapes,
        scratch_shapes=[pltpu.VMEM((k, tn), BF16)],
        compiler_params=_cparams(("arbitrary", "arbitrary"), vmem_mib), name=name)(h, *([w] * n_sub), *side)


def _swa_rope(x, c, s1, s2):
    return x * c + pltpu.roll(x, LANES - ROPE_HALF, 1) * s1 + pltpu.roll(x, ROPE_HALF, 1) * s2


def _epi_swa_q(acc, side, outs):
    c, s1, s2 = (t[...] for t in side)
    (q_ref,) = outs
    for cc in range(acc.shape[1] // LANES):
        sl = slice(cc * LANES, (cc + 1) * LANES)
        q_ref[:, sl] = (_swa_rope(acc[:, sl], c, s1, s2) * (SWA_HEAD_DIM ** -0.5)).astype(BF16)


def _epi_swa_kv(acc, side, outs):
    c, s1, s2 = (t[...] for t in side)
    (kv_ref,) = outs
    for cc in range(SWA_KV_W // LANES):
        sl = slice(cc * LANES, (cc + 1) * LANES)
        kv_ref[:, sl] = _swa_rope(acc[:, sl], c, s1, s2)
    kv_ref[:, SWA_KV_W:] = acc[:, SWA_KV_W:]


def _epi_gelu(acc, side, outs):
    outs[0][...] = jax.nn.gelu(acc).astype(BF16)


def _epi_gelu_ln(acc, side, outs):
    g_ref, b_ref = side
    zv_ref, zvs_ref = outs
    a = jax.nn.gelu(acc)
    xc = a - jnp.mean(a, axis=-1, keepdims=True)
    y = xc * lax.rsqrt(jnp.mean(xc * xc, axis=-1, keepdims=True) + EPS) * g_ref[...] + b_ref[...]
    zv_ref[...] = y.astype(BF16)

    @pl.when(pl.program_id(1) == MP // TM_IN)
    def _():
        zvs_ref[...] = y


def _epi_ret_qk(acc, side, outs):
    cos, sin = (t[...] for t in side)
    (z_ref,) = outs
    tn = acc.shape[1]
    scale = jnp.where(pl.program_id(0) < RET_QK_W // tn, 1.0, RET_KEY_DIM ** -0.5)
    for hh in range(tn // RET_KEY_DIM):
        c0 = hh * RET_KEY_DIM
        x1, x2 = acc[:, c0:c0 + LANES], acc[:, c0 + LANES:c0 + 2 * LANES]
        z_ref[:, c0:c0 + LANES] = ((x1 * cos - x2 * sin) * scale).astype(BF16)
        z_ref[:, c0 + LANES:c0 + 2 * LANES] = ((x2 * cos + x1 * sin) * scale).astype(BF16)


def _epi_cast(acc, side, outs):
    outs[0][...] = acc.astype(BF16)


def _epi_silu(acc, side, outs):
    outs[0][...] = (acc / (1.0 + jnp.exp(-acc))).astype(BF16)


def _ab_in_proj(h, w_in, ln_g, ln_b, tabs):
    sc, s1, s2 = tabs
    tab_spec = pl.BlockSpec((TM_IN, LANES), _tab_index)
    row_out = lambda tn: pl.BlockSpec((TM_IN, tn), lambda j, i: (i, j))
    q = _inproj(h, w_in, 0, SWA_Q_W, SWA_Q_W, _epi_swa_q, [sc, s1, s2], [tab_spec] * 3,
                jax.ShapeDtypeStruct((M, SWA_Q_W), BF16), row_out(SWA_Q_W), "ab_in_q", 52)
    kv = _inproj(h, w_in, SWA_Q_W, 2 * SWA_KV_W, 512, _epi_swa_kv, [sc, s1, s2], [tab_spec] * 3,
                 jax.ShapeDtypeStruct((M, 2 * SWA_KV_W), F32), row_out(512), "ab_in_kv", 40)
    u = _inproj(h, w_in, SWA_Q_W + 2 * SWA_KV_W, GMLP_WIDTH, GMLP_WIDTH, _epi_gelu, [], [],
                jax.ShapeDtypeStruct((M, GMLP_WIDTH), BF16), row_out(GMLP_WIDTH), "ab_in_u", 52)
    vec_spec = pl.BlockSpec((1, GMLP_WIDTH), lambda j, i: (0, 0))
    zv, zv_s = _inproj(
        h, w_in, AB_IN - GMLP_WIDTH, GMLP_WIDTH, GMLP_WIDTH, _epi_gelu_ln,
        [ln_g, ln_b], [vec_spec] * 2,
        [jax.ShapeDtypeStruct((M, GMLP_WIDTH), BF16), jax.ShapeDtypeStruct((MS, GMLP_WIDTH), F32)],
        [row_out(GMLP_WIDTH), pl.BlockSpec((MS, GMLP_WIDTH), lambda j, i: (0, 0))], "ab_in_zv", 52)
    return q, kv, u, zv, zv_s


def _ret_in_proj(h, w_in, tabs):
    rc, rs = tabs
    tn = 1024
    tab_spec = pl.BlockSpec((TM_IN, LANES), _tab_index)
    out_spec = pl.BlockSpec((TM_IN, tn), lambda j, i: (i, j))
    seg = lambda col0, ncols, epi, side, specs, name: _inproj(
        h, w_in, col0, ncols, tn, epi, side, specs, jax.ShapeDtypeStruct((M, ncols), BF16), out_spec, name, 52)
    return (seg(0, 2 * RET_QK_W, _epi_ret_qk, [rc, rs], [tab_spec] * 2, "ret_in_qk"),
            seg(2 * RET_QK_W, RET_V_W, _epi_cast, [], [], "ret_in_v"),
            seg(2 * RET_QK_W + RET_V_W, RET_V_W, _epi_silu, [], [], "ret_in_g"))


AB_ROWS = 512


def _sink_softmax(s, sink):
    m = jnp.maximum(jnp.max(s, axis=-1, keepdims=True), sink)
    p = jnp.exp(s - m)
    den = jnp.sum(p, axis=-1, keepdims=True) + jnp.exp(sink - m)
    return p, den


def _ab_prompt_kernel(sink_ref, q_ref, kv_ref, kvp_ref, u_ref, zv_ref, ws_ref, bs_ref, cat_ref,
                      kd_ref, vd_ref):
    r = pl.program_id(1)
    low = lax.broadcasted_iota(jnp.int32, (1, LANES), 1) < SWA_HEAD_DIM

    def both_halves(x, half):
        sw = pltpu.roll(x, SWA_HEAD_DIM, 1)
        return (jnp.where(low, x, sw) if half == 0 else jnp.where(low, sw, x)).astype(BF16)

    for kh in range(SWA_KV_HEADS):
        kcol = slice((kh // 2) * LANES, (kh // 2 + 1) * LANES)
        vcol = slice(SWA_KV_W + (kh // 2) * LANES, SWA_KV_W + (kh // 2 + 1) * LANES)
        kd_ref[kh, :WINDOW, :] = both_halves(kvp_ref[:, kcol], kh % 2)
        kd_ref[kh, WINDOW:, :] = both_halves(kv_ref[:, kcol], kh % 2)
        vd_ref[kh, :WINDOW, :] = both_halves(kvp_ref[:, vcol], kh % 2)
        vd_ref[kh, WINDOW:, :] = both_halves(kv_ref[:, vcol], kh % 2)

    t = lax.broadcasted_iota(jnp.int32, (WINDOW, 2 * WINDOW), 0)
    sk = lax.broadcasted_iota(jnp.int32, (WINDOW, 2 * WINDOW), 1)
    band = (sk <= t + WINDOW) & (sk > t)
    sink_slot = sk == t
    tril = (lax.broadcasted_iota(jnp.int32, (GMLP_CHUNK, GMLP_CHUNK), 0)
            >= lax.broadcasted_iota(jnp.int32, (GMLP_CHUNK, GMLP_CHUNK), 1))

    for sb in range(AB_ROWS // WINDOW):
        rs = slice(sb * WINDOW, (sb + 1) * WINDOW)
        ks = slice(sb * WINDOW, sb * WINDOW + 2 * WINDOW)
        valid = band & ((sk >= WINDOW) | (r * (AB_ROWS // WINDOW) + sb > 0))
        for kh in range(SWA_KV_HEADS):
            qm = []
            for g in range(SWA_GROUP):
                hd = kh * SWA_GROUP + g
                qb = q_ref[rs, (hd // 2) * LANES:(hd // 2 + 1) * LANES]
                qm.append(jnp.where(low if hd % 2 == 0 else jnp.logical_not(low), qb, jnp.zeros_like(qb)))
            s = lax.dot_general(jnp.concatenate(qm, axis=0), kd_ref[kh, ks, :], (((1,), (1,)), ((), ())),
                                preferred_element_type=F32)
            pv, den = [], []
            for g in range(SWA_GROUP):
                sg = s[g * WINDOW:(g + 1) * WINDOW]
                sg = jnp.where(valid, sg, jnp.where(sink_slot, sink_ref[kh * SWA_GROUP + g], NEG_INF))
                p = jnp.exp(sg - jnp.max(sg, axis=-1, keepdims=True))
                den.append(jnp.sum(p, axis=-1, keepdims=True))
                pv.append(jnp.where(sink_slot, 0.0, p).astype(BF16))
            o = jnp.dot(jnp.concatenate(pv, axis=0), vd_ref[kh, ks, :], preferred_element_type=F32)
            for pr in range(SWA_GROUP // 2):
                lo = o[(2 * pr) * WINDOW:(2 * pr + 1) * WINDOW] / den[2 * pr]
                hi = o[(2 * pr + 1) * WINDOW:(2 * pr + 2) * WINDOW] / den[2 * pr + 1]
                c0 = (kh * (SWA_GROUP // 2) + pr) * LANES
                cat_ref[rs, c0:c0 + LANES] = jnp.where(low, lo, hi).astype(BF16)
        zb = zv_ref[rs, :]
        ub = u_ref[rs, :]
        for g in range(GMLP_GROUPS):
            gs = slice(g * GMLP_GROUP_DIM, (g + 1) * GMLP_GROUP_DIM)
            w = jnp.where(tril, ws_ref[g], 0.0).astype(BF16)
            mixed = jnp.dot(w, zb[:, gs], preferred_element_type=F32) + bs_ref[g]
            cat_ref[rs, SWA_Q_W + g * GMLP_GROUP_DIM:SWA_Q_W + (g + 1) * GMLP_GROUP_DIM] = (
                ub[:, gs].astype(F32) * mixed).astype(BF16)


def _ab_mix_prompt(q, kv, u, zv, sinks, w_s, b_s):
    nr = SEQ // AB_ROWS
    sub = AB_ROWS // WINDOW
    row = lambda b, r: (b * nr + r, 0)
    return pl.pallas_call(
        _ab_prompt_kernel, grid=(BATCH, nr),
        in_specs=[pl.BlockSpec(memory_space=pltpu.SMEM),
                  pl.BlockSpec((AB_ROWS, SWA_Q_W), row),
                  pl.BlockSpec((AB_ROWS, 2 * SWA_KV_W), row),
                  pl.BlockSpec((WINDOW, 2 * SWA_KV_W), lambda b, r: (jnp.maximum((b * nr + r) * sub - 1, 0), 0)),
                  pl.BlockSpec((AB_ROWS, GMLP_WIDTH), row),
                  pl.BlockSpec((AB_ROWS, GMLP_WIDTH), row),
                  pl.BlockSpec((GMLP_GROUPS, GMLP_CHUNK, GMLP_CHUNK), lambda b, r: (0, 0, 0)),
                  pl.BlockSpec((GMLP_GROUPS, GMLP_CHUNK, 1), lambda b, r: (0, 0, 0))],
        out_specs=pl.BlockSpec((AB_ROWS, AB_OUT), row),
        out_shape=jax.ShapeDtypeStruct((M, AB_OUT), BF16),
        scratch_shapes=[pltpu.VMEM((SWA_KV_HEADS, WINDOW + AB_ROWS, LANES), BF16)] * 2,
        compiler_params=_cparams(("arbitrary", "arbitrary"), 40), name="ab_mix_prompt",
    )(sinks, q, kv, kv, u, zv, w_s, b_s.reshape(GMLP_GROUPS, GMLP_CHUNK, 1))


AB_NB = 16
AB_KEYS = 2 * WINDOW


def _ab_sample_kernel(sink_ref, q_ref, kv_ref, u_ref, zv_ref, ck_ref, cv_ref, ws_ref, bs_ref, cat_in_ref,
                      cat_ref, nk_ref, nv_ref):
    del cat_in_ref
    nb, t = AB_NB, DEC_SEQ
    kv3 = kv_ref[...].reshape(nb, t, 2 * SWA_KV_W)
    kn, vn = kv3[:, :, :SWA_KV_W], kv3[:, :, SWA_KV_W:]
    ck, cv = ck_ref[...], cv_ref[...]
    nk_ref[:, :WINDOW - t, :] = ck[:, t:, :]
    nk_ref[:, WINDOW - t:, :] = kn
    nv_ref[:, :WINDOW - t, :] = cv[:, t:, :]
    nv_ref[:, WINDOW - t:, :] = vn
    pad = jnp.zeros((nb, AB_KEYS - WINDOW - t, SWA_KV_W), F32)
    kall = jnp.concatenate([ck, kn, pad], axis=1).astype(BF16)
    vall = jnp.concatenate([cv, vn, pad], axis=1).astype(BF16)

    rows = SWA_GROUP * t
    tq = lax.broadcasted_iota(jnp.int32, (rows, AB_KEYS), 0) & (t - 1)
    sk = lax.broadcasted_iota(jnp.int32, (rows, AB_KEYS), 1)
    valid = ((sk < WINDOW) & (sk > tq)) | ((sk >= WINDOW) & (sk - WINDOW <= tq))

    q3 = q_ref[...].astype(F32).reshape(nb, t, SWA_Q_W)
    pieces = []
    for kh in range(SWA_KV_HEADS):
        heads = [kh * SWA_GROUP + g for g in range(SWA_GROUP)]
        qs = jnp.concatenate([q3[:, :, hd * SWA_HEAD_DIM:(hd + 1) * SWA_HEAD_DIM] for hd in heads],
                             axis=1).astype(BF16)
        cs = slice(kh * SWA_HEAD_DIM, (kh + 1) * SWA_HEAD_DIM)
        s = jnp.einsum('nqd,nkd->nqk', qs, kall[:, :, cs], preferred_element_type=F32)
        s = jnp.where(valid[None], s, NEG_INF)
        sink = jnp.concatenate([jnp.full((t, 1), sink_ref[hd], F32) for hd in heads], axis=0)[None]
        p, den = _sink_softmax(s, sink)
        o = jnp.einsum('nqk,nkd->nqd', p.astype(BF16), vall[:, :, cs], preferred_element_type=F32) / den
        pieces += [o[:, g * t:(g + 1) * t, :] for g in range(SWA_GROUP)]
    attn = jnp.concatenate(pieces, axis=2).reshape(nb * t, SWA_Q_W)
    cat_ref[:, :SWA_Q_W] = attn.astype(BF16)

    zv3 = zv_ref[...].reshape(nb, t, GMLP_WIDTH)
    u3 = u_ref[...].astype(F32).reshape(nb, t, GMLP_WIDTH)
    irow = lax.broadcasted_iota(jnp.int32, (t, GMLP_GROUP_DIM), 0)
    for g in range(GMLP_GROUPS):
        gs = slice(g * GMLP_GROUP_DIM, (g + 1) * GMLP_GROUP_DIM)
        zg = zv3[:, :, gs]
        mixed = jnp.zeros((nb, t, GMLP_GROUP_DIM), F32) + bs_ref[g][None]
        for j in range(t):
            wcol = jnp.where(irow >= j, ws_ref[g, j], 0.0)
            mixed = mixed + zg[:, j:j + 1, :] * wcol[None]
        gate = (u3[:, :, gs] * mixed).reshape(nb * t, GMLP_GROUP_DIM)
        cat_ref[:, SWA_Q_W + g * GMLP_GROUP_DIM:SWA_Q_W + (g + 1) * GMLP_GROUP_DIM] = gate.astype(BF16)


def _ab_mix_sample(cat, q, kv, u, zv_s, cache_k, cache_v, sinks, w_s, b_s):
    nb, t = AB_NB, DEC_SEQ
    rows = nb * t
    off = MP // rows
    srow = lambda i: (off + i, 0)
    wt = jnp.broadcast_to(jnp.swapaxes(w_s[:, :t, :t], 1, 2)[..., None], (GMLP_GROUPS, t, t, GMLP_GROUP_DIM))
    bt = jnp.broadcast_to(b_s[:, :t, None], (GMLP_GROUPS, t, GMLP_GROUP_DIM))
    cspec = pl.BlockSpec((nb, WINDOW, SWA_KV_W), lambda i: (i, 0, 0))
    cshape = jax.ShapeDtypeStruct((DEC_BATCH, WINDOW, SWA_KV_W), F32)
    return pl.pallas_call(
        _ab_sample_kernel, grid=(DEC_BATCH // nb,),
        in_specs=[pl.BlockSpec(memory_space=pltpu.SMEM),
                  pl.BlockSpec((rows, SWA_Q_W), srow),
                  pl.BlockSpec((rows, 2 * SWA_KV_W), srow),
                  pl.BlockSpec((rows, GMLP_WIDTH), srow),
                  pl.BlockSpec((rows, GMLP_WIDTH), lambda i: (i, 0)),
                  cspec, cspec,
                  pl.BlockSpec((GMLP_GROUPS, t, t, GMLP_GROUP_DIM), lambda i: (0, 0, 0, 0)),
                  pl.BlockSpec((GMLP_GROUPS, t, GMLP_GROUP_DIM), lambda i: (0, 0, 0)),
                  pl.BlockSpec(memory_space=pl.ANY)],
        out_specs=[pl.BlockSpec((rows, AB_OUT), srow), cspec, cspec],
        out_shape=[jax.ShapeDtypeStruct((M, AB_OUT), BF16), cshape, cshape],
        input_output_aliases={9: 0},
        compiler_params=_cparams(("arbitrary",), 48), name="ab_mix_sample",
    )(sinks, q, kv, u, zv_s, cache_k, cache_v, wt, bt, cat)


RET_ROWS = 1024
RET_BLOCK = 256


def _ret_prompt_part(lg, q_ref, k_ref, v_ref, g_ref, y_ref, sfin_ref, s_ref):
    c = pl.program_id(2)
    L = RET_BLOCK

    @pl.when(c == 0)
    def _():
        s_ref[...] = jnp.zeros_like(s_ref)

    diff = (lax.broadcasted_iota(jnp.int32, (L, L), 0) - lax.broadcasted_iota(jnp.int32, (L, L), 1)).astype(F32)
    dmat = jnp.where(diff >= 0, jnp.exp(lg * jnp.maximum(diff, 0.0)), 0.0)
    idx = lax.broadcasted_iota(jnp.int32, (L, 1), 0).astype(F32)
    q_dec = jnp.exp(lg * (idx + 1.0))
    k_dec = jnp.exp(lg * (L - 1.0 - idx))
    s_dec = jnp.exp(lg * jnp.full((1, 1), float(L), F32))

    def chunk(ci, carry):
        rs = pl.ds(pl.multiple_of(ci * L, L), L)
        q, k, v = q_ref[rs, :], k_ref[rs, :], v_ref[rs, :]
        att = lax.dot_general(q, k, (((1,), (1,)), ((), ())), preferred_element_type=F32) * dmat
        s0 = s_ref[...]
        o = (jnp.dot(att.astype(BF16), v, preferred_element_type=F32)
             + jnp.dot(q, s0.astype(BF16), preferred_element_type=F32) * q_dec)
        kd = (k.astype(F32) * k_dec).astype(BF16)
        s_ref[...] = s0 * s_dec + lax.dot_general(kd, v, (((0,), (0,)), ((), ())), preferred_element_type=F32)
        on = o * lax.rsqrt(jnp.mean(o * o, axis=-1, keepdims=True) + EPS)
        y_ref[rs, :] = (on * g_ref[rs, :].astype(F32)).astype(BF16)
        return carry

    lax.fori_loop(0, RET_ROWS // L, chunk, 0, unroll=2)

    @pl.when(c == pl.num_programs(2) - 1)
    def _():
        sfin_ref[...] = s_ref[...]


RET_NB = DEC_BATCH // (BATCH * (SEQ // RET_ROWS))


def _ret_sample_part(lg, q_ref, k_ref, v_ref, g_ref, s0_ref, y_ref, s1_ref):
    nb, t = RET_NB, DEC_SEQ
    rows = nb * t
    q, k, v = q_ref[...], k_ref[...], v_ref[...]
    ri = lax.broadcasted_iota(jnp.int32, (rows, rows), 0)
    ci = lax.broadcasted_iota(jnp.int32, (rows, rows), 1)
    d = ((ri & (t - 1)) - (ci & (t - 1))).astype(F32)
    same = (ri // t) == (ci // t)
    dmat = jnp.where(same & (d >= 0), jnp.exp(lg * jnp.maximum(d, 0.0)), 0.0)
    att = lax.dot_general(q, k, (((1,), (1,)), ((), ())), preferred_element_type=F32) * dmat
    o_intra = jnp.dot(att.astype(BF16), v, preferred_element_type=F32)

    rcol = lax.broadcasted_iota(jnp.int32, (rows, 1), 0)
    tt = (rcol & (t - 1)).astype(F32)
    q_dec = jnp.exp(lg * (tt + 1.0))
    k_dec = jnp.exp(lg * (t - 1.0 - tt))
    s_dec = jnp.exp(lg * jnp.full((1, 1), float(t), F32))
    kd = (k.astype(F32) * k_dec).astype(BF16)
    qf = q.astype(F32)
    vf = v.astype(F32)
    o_rows = []
    for n in range(nb):
        s0 = s0_ref[n]
        qn = qf[n * t:(n + 1) * t].astype(BF16)
        o_rows.append(jnp.dot(qn, s0.astype(BF16), preferred_element_type=F32))
        vn = jnp.where(rcol // t == n, vf, 0.0).astype(BF16)
        s1_ref[n] = s0 * s_dec + lax.dot_general(kd, vn, (((0,), (0,)), ((), ())), preferred_element_type=F32)
    o = o_intra + jnp.concatenate(o_rows, axis=0) * q_dec
    on = o * lax.rsqrt(jnp.mean(o * o, axis=-1, keepdims=True) + EPS)
    y_ref[...] = (on * g_ref[...].astype(F32)).astype(BF16)


def _retention_kernel(lg_ref, q_ref, k_ref, v_ref, g_ref, sq_ref, sk_ref, sv_ref, sg_ref, s0_ref,
                      yp_ref, sfin_ref, ys_ref, s1_ref, s_ref):
    lg = lg_ref[pl.program_id(1)]
    _ret_prompt_part(lg, q_ref, k_ref, v_ref, g_ref, yp_ref, sfin_ref, s_ref)
    _ret_sample_part(lg, sq_ref, sk_ref, sv_ref, sg_ref, s0_ref, ys_ref, s1_ref)


def _retention(zqk, zv, zg, state, lg):
    nc = SEQ // RET_ROWS
    srows = RET_NB * DEC_SEQ
    soff = MP // srows
    prow = lambda b, c: b * nc + c
    head_specs = lambda rows, row: [
        pl.BlockSpec((rows, RET_KEY_DIM), lambda b, h, c: (row(b, c), h)),
        pl.BlockSpec((rows, RET_KEY_DIM), lambda b, h, c: (row(b, c), RET_HEADS + h)),
        pl.BlockSpec((rows, RET_VAL_DIM), lambda b, h, c: (row(b, c), h)),
        pl.BlockSpec((rows, RET_VAL_DIM), lambda b, h, c: (row(b, c), h))]
    sspec = pl.BlockSpec((RET_NB, None, RET_KEY_DIM, RET_VAL_DIM), lambda b, h, c: (prow(b, c), h, 0, 0))
    return pl.pallas_call(
        _retention_kernel, grid=(BATCH, RET_HEADS, nc),
        in_specs=([pl.BlockSpec(memory_space=pltpu.SMEM)] + head_specs(RET_ROWS, prow)
                  + head_specs(srows, lambda b, c: soff + prow(b, c)) + [sspec]),
        out_specs=[pl.BlockSpec((RET_ROWS, RET_VAL_DIM), lambda b, h, c: (prow(b, c), h)),
                   pl.BlockSpec((None, None, RET_KEY_DIM, RET_VAL_DIM), lambda b, h, c: (b, h, 0, 0)),
                   pl.BlockSpec((srows, RET_VAL_DIM), lambda b, h, c: (prow(b, c), h)),
                   sspec],
        out_shape=[jax.ShapeDtypeStruct((MP, RET_V_W), BF16),
                   jax.ShapeDtypeStruct((BATCH, RET_HEADS, RET_KEY_DIM, RET_VAL_DIM), F32),
                   jax.ShapeDtypeStruct((MS, RET_V_W), BF16),
                   jax.ShapeDtypeStruct((DEC_BATCH, RET_HEADS, RET_KEY_DIM, RET_VAL_DIM), F32)],
        scratch_shapes=[pltpu.VMEM((RET_KEY_DIM, RET_VAL_DIM), F32)],
        compiler_params=_cparams(("arbitrary", "arbitrary", "arbitrary"), 56), name="retention",
    )(lg, zqk, zqk, zv, zg, zqk, zqk, zv, zg, state)


def _zero_on_first(acc_ref, k):
    @pl.when(k == 0)
    def _():
        acc_ref[...] = jnp.zeros_like(acc_ref)


def _outproj_kernel(*refs, n_prompt_tiles):
    a_refs, (w_ref, x_ref, gp_ref, gn_ref, xn_ref, hn_ref) = refs[:-6], refs[-6:]

    def emit(a_ref):
        y = jnp.dot(a_ref[...], w_ref[...], preferred_element_type=F32)
        xn, hn = _residual(y, x_ref[...], gp_ref[...], gn_ref[...])
        xn_ref[...] = xn
        hn_ref[...] = hn

    if len(a_refs) == 1:
        emit(a_refs[0])
    else:
        is_prompt = pl.program_id(0) < n_prompt_tiles
        pl.when(is_prompt)(lambda: emit(a_refs[0]))
        pl.when(jnp.logical_not(is_prompt))(lambda: emit(a_refs[1]))


def _out_proj(a_parts, w_bf, x, g_post, g_next, name):
    kdim = w_bf.shape[0]
    tm = TM_OUT * D_MODEL // kdim
    np_t = MP // tm
    row = pl.BlockSpec((tm, D_MODEL), lambda i: (i, 0))
    vec = pl.BlockSpec((1, D_MODEL), lambda i: (0, 0))
    if len(a_parts) == 1:
        a_specs = [pl.BlockSpec((tm, kdim), lambda i: (i, 0))]
    else:
        a_specs = [pl.BlockSpec((tm, kdim), lambda i: (jnp.minimum(i, np_t - 1), 0)),
                   pl.BlockSpec((tm, kdim), lambda i: (jnp.maximum(i - np_t, 0), 0))]
    return pl.pallas_call(
        functools.partial(_outproj_kernel, n_prompt_tiles=np_t), grid=(M // tm,),
        in_specs=a_specs + [pl.BlockSpec((kdim, D_MODEL), lambda i: (0, 0), pipeline_mode=pl.Buffered(1)),
                            row, vec, vec],
        out_specs=[row, row],
        out_shape=[jax.ShapeDtypeStruct((M, D_MODEL), F32), jax.ShapeDtypeStruct((M, D_MODEL), BF16)],
        compiler_params=_cparams(("arbitrary",), 48), name=name,
    )(*a_parts, w_bf, x, g_post, g_next)


def _ffn_up_down(h_ref, wu_ref, wd_ref):
    a = jnp.maximum(jnp.dot(h_ref[...], wu_ref[...], preferred_element_type=F32), 0.0)
    return jnp.dot((a * a).astype(BF16), wd_ref[...], preferred_element_type=F32)


def _ffn_kernel(h_ref, wu_ref, wd_ref, x_ref, gp_ref, gn_ref, xn_ref, hn_ref):
    c = pl.program_id(1)
    _zero_on_first(xn_ref, c)
    xn_ref[...] += _ffn_up_down(h_ref, wu_ref, wd_ref)

    @pl.when(c == pl.num_programs(1) - 1)
    def _():
        xn, hn = _residual(xn_ref[...], x_ref[...], gp_ref[...], gn_ref[...])
        xn_ref[...] = xn
        hn_ref[...] = hn


def _ffn_last_kernel(h_ref, wu_ref, wd_ref, x_ref, gp_ref, yp_ref, ys_ref, acc_ref):
    i, c = pl.program_id(0), pl.program_id(1)
    _zero_on_first(acc_ref, c)
    acc_ref[...] += _ffn_up_down(h_ref, wu_ref, wd_ref)
    last = c == pl.num_programs(1) - 1
    is_prompt = i < MP // acc_ref.shape[0]

    @pl.when(last & is_prompt)
    def _():
        yp_ref[...] = x_ref[...] + _rms(acc_ref[...], gp_ref[...])

    @pl.when(last & jnp.logical_not(is_prompt))
    def _():
        ys_ref[...] = x_ref[...] + _rms(acc_ref[...], gp_ref[...])


def _ffn(h, wu_bf, wd_bf, layer, x, g_post, g_next):
    tm, tc = TM_OUT, 1024
    row = pl.BlockSpec((tm, D_MODEL), lambda i, c: (i, 0))
    vec = pl.BlockSpec((1, D_MODEL), lambda i, c: (0, 0))
    w_specs = [pl.BlockSpec((None, D_MODEL, tc), lambda i, c: (layer, 0, c)),
               pl.BlockSpec((None, tc, D_MODEL), lambda i, c: (layer, c, 0))]
    grid = (M // tm, D_FF // tc)
    if g_next is not None:
        return pl.pallas_call(
            _ffn_kernel, grid=grid, in_specs=[row] + w_specs + [row, vec, vec], out_specs=[row, row],
            out_shape=[jax.ShapeDtypeStruct((M, D_MODEL), F32), jax.ShapeDtypeStruct((M, D_MODEL), BF16)],
            compiler_params=_cparams(("arbitrary", "arbitrary"), 56), name="ffn",
        )(h, wu_bf, wd_bf, x, g_post, g_next)
    np_t = MP // tm
    return pl.pallas_call(
        _ffn_last_kernel, grid=grid, in_specs=[row] + w_specs + [row, vec],
        out_specs=[pl.BlockSpec((tm, D_MODEL), lambda i, c: (jnp.minimum(i, np_t - 1), 0)),
                   pl.BlockSpec((tm, D_MODEL), lambda i, c: (jnp.maximum(i - np_t, 0), 0))],
        out_shape=[jax.ShapeDtypeStruct((MP, D_MODEL), F32), jax.ShapeDtypeStruct((MS, D_MODEL), F32)],
        scratch_shapes=[pltpu.VMEM((tm, D_MODEL), F32)],
        compiler_params=_cparams(("arbitrary", "arbitrary"), 56), name="ffn_last",
    )(h, wu_bf, wd_bf, x, g_post)


def _mem_kv_kernel(m_ref, g_ref, wk_ref, wv_ref, k_ref, v_ref):
    m = _rms(m_ref[...], g_ref[0]).astype(BF16)
    k_ref[0] = jnp.dot(m, wk_ref[0].astype(BF16), preferred_element_type=F32)
    v_ref[0] = jnp.dot(m, wv_ref[0].astype(BF16), preferred_element_type=F32)


def _mem_kv(mem, g, w_k, w_v):
    rows = BATCH * MEM_LEN
    wspec = pl.BlockSpec((1, D_MODEL, MEM_W), lambda l: (l, 0, 0))
    ospec = pl.BlockSpec((1, rows, MEM_W), lambda l: (l, 0, 0))
    oshape = jax.ShapeDtypeStruct((DEPTH, rows, MEM_W), F32)
    return pl.pallas_call(
        _mem_kv_kernel, grid=(DEPTH,),
        in_specs=[pl.BlockSpec((rows, D_MODEL), lambda l: (0, 0)),
                  pl.BlockSpec((1, 1, D_MODEL), lambda l: (l, 0, 0)), wspec, wspec],
        out_specs=[ospec, ospec], out_shape=[oshape, oshape],
        compiler_params=_cparams(("arbitrary",), 48), name="mem_kv",
    )(mem.reshape(rows, D_MODEL), g.reshape(DEPTH, 1, D_MODEL), w_k, w_v)


def _mem_prompt_kernel(h_ref, wq_ref, wo_ref, mk_ref, mv_ref, x_ref, gp_ref, gn_ref, xn_ref, hn_ref):
    q = jnp.dot(h_ref[...], wq_ref[...], preferred_element_type=F32).astype(BF16)
    mk, mv = mk_ref[...].astype(BF16), mv_ref[...].astype(BF16)
    outs = []
    for hh in range(MEM_HEADS):
        cs = slice(hh * MEM_HEAD_DIM, (hh + 1) * MEM_HEAD_DIM)
        s = lax.dot_general(q[:, cs], mk[:, cs], (((1,), (1,)), ((), ())),
                            preferred_element_type=F32) * (MEM_HEAD_DIM ** -0.5)
        p = jnp.exp(s - jnp.max(s, axis=-1, keepdims=True))
        den = jnp.sum(p, axis=-1, keepdims=True)
        outs.append(jnp.dot(p.astype(BF16), mv[:, cs], preferred_element_type=F32) / den)
    o = jnp.concatenate(outs, axis=1).astype(BF16)
    y = jnp.dot(o, wo_ref[...], preferred_element_type=F32)
    xn, hn = _residual(y, x_ref[...], gp_ref[...], gn_ref[...])
    xn_ref[...] = xn
    hn_ref[...] = hn


def _mem_prompt(h, wq_bf, wo_bf, mk, mv, x, g_post, g_next):
    tm = TM_OUT
    per_b = SEQ // tm
    row = pl.BlockSpec((tm, D_MODEL), lambda i: (i, 0))
    vec = pl.BlockSpec((1, D_MODEL), lambda i: (0, 0))
    kvspec = pl.BlockSpec((MEM_LEN, MEM_W), lambda i: (i // per_b, 0))
    return pl.pallas_call(
        _mem_prompt_kernel, grid=(MP // tm,),
        in_specs=[row, pl.BlockSpec((D_MODEL, MEM_W), lambda i: (0, 0)),
                  pl.BlockSpec((MEM_W, D_MODEL), lambda i: (0, 0)), kvspec, kvspec, row, vec, vec],
        out_specs=[row, row],
        out_shape=[jax.ShapeDtypeStruct((M, D_MODEL), F32), jax.ShapeDtypeStruct((M, D_MODEL), BF16)],
        input_output_aliases={5: 0},
        compiler_params=_cparams(("arbitrary",), 40), name="mem_prompt",
    )(h, wq_bf, wo_bf, mk, mv, x, g_post, g_next)


MEM_NB = 8


def _mem_sample_kernel(h_ref, wq_ref, wo_ref, ck_ref, cv_ref, x_ref, gp_ref, gn_ref, hn_in_ref,
                       xn_ref, hn_ref):
    del hn_in_ref
    nb, t = MEM_NB, DEC_SEQ
    q = jnp.dot(h_ref[...], wq_ref[...], preferred_element_type=F32)
    qs = jnp.concatenate([q[:, hh * MEM_HEAD_DIM:(hh + 1) * MEM_HEAD_DIM].reshape(nb, t, MEM_HEAD_DIM)
                          for hh in range(MEM_HEADS)], axis=1).astype(BF16)
    s = jnp.einsum('nqd,nkd->nqk', qs, ck_ref[...].astype(BF16),
                   preferred_element_type=F32) * (MEM_HEAD_DIM ** -0.5)
    shape = (MEM_HEADS * t, MEM_LEN * MEM_HEADS)
    own = ((lax.broadcasted_iota(jnp.int32, shape, 1) & (MEM_HEADS - 1))
           == lax.broadcasted_iota(jnp.int32, shape, 0) // t)
    s = jnp.where(own[None], s, NEG_INF)
    p = jnp.exp(s - jnp.max(s, axis=-1, keepdims=True))
    den = jnp.sum(p, axis=-1, keepdims=True)
    o3 = jnp.einsum('nqk,nkd->nqd', p.astype(BF16), cv_ref[...].astype(BF16),
                    preferred_element_type=F32) / den
    o = jnp.concatenate([o3[:, hh * t:(hh + 1) * t, :].reshape(nb * t, MEM_HEAD_DIM)
                         for hh in range(MEM_HEADS)], axis=1).astype(BF16)
    y = jnp.dot(o, wo_ref[...], preferred_element_type=F32)
    xn, hn = _residual(y, x_ref[...], gp_ref[...], gn_ref[...])
    xn_ref[...] = xn
    hn_ref[...] = hn


def _mem_sample(xn, hn, h, wq_bf, wo_bf, cache_k, cache_v, layer, g_post, g_next):
    nb = MEM_NB
    rows = nb * DEC_SEQ
    off = MP // rows
    row = pl.BlockSpec((rows, D_MODEL), lambda i: (off + i, 0))
    vec = pl.BlockSpec((1, D_MODEL), lambda i: (0, 0))
    cspec = pl.BlockSpec((None, nb, MEM_LEN * MEM_HEADS, MEM_HEAD_DIM), lambda i: (layer, i, 0, 0))
    anyspec = pl.BlockSpec(memory_space=pl.ANY)
    return pl.pallas_call(
        _mem_sample_kernel, grid=(DEC_BATCH // nb,),
        in_specs=[row, pl.BlockSpec((D_MODEL, MEM_W), lambda i: (0, 0)),
                  pl.BlockSpec((MEM_W, D_MODEL), lambda i: (0, 0)), cspec, cspec, row, vec, vec, anyspec],
        out_specs=[row, row],
        out_shape=[jax.ShapeDtypeStruct((M, D_MODEL), F32), jax.ShapeDtypeStruct((M, D_MODEL), BF16)],
        input_output_aliases={5: 0, 8: 1},
        compiler_params=_cparams(("arbitrary",), 40), name="mem_sample",
    )(h, wq_bf, wo_bf, cache_k, cache_v, xn, g_post, g_next, hn)


def kernel(x_prompt, x_sample, cache_swa_k, cache_swa_v, state_ret, cache_mem_k, cache_mem_v, mem_prompt, norm_mix_pre, norm_mix_post, norm_mem, norm_x_pre, norm_x_post, norm_ffn_pre, norm_ffn_post, w_ab_in, w_ab_out, swa_sinks, gmlp_ln_g, gmlp_ln_b, gmlp_w_s, gmlp_b_s, w_ret_in, w_ret_out, w_mem_q, w_mem_k, w_mem_v, w_mem_o, w_ffn_up, w_ffn_down):
    vec = lambda g, layer: g[layer].reshape(1, D_MODEL)
    swa_c, swa_s1, swa_s2, ret_c, ret_s = _rope_tables()
    mem_k, mem_v = _mem_kv(mem_prompt, norm_mem, w_mem_k, w_mem_v)
    ck_mem = cache_mem_k.reshape(DEPTH, DEC_BATCH, MEM_LEN * MEM_HEADS, MEM_HEAD_DIM)
    cv_mem = cache_mem_v.reshape(DEPTH, DEC_BATCH, MEM_LEN * MEM_HEADS, MEM_HEAD_DIM)
    lg = jnp.log1p(-jnp.exp2(-5.0 - jnp.arange(RET_HEADS, dtype=F32)))
    wu_bf, wd_bf = w_ffn_up.astype(BF16), w_ffn_down.astype(BF16)

    x, h = _stack_and_prenorm(x_prompt, x_sample, vec(norm_mix_pre, 0))
    outs = {}
    for layer in range(DEPTH):
        j = layer // 2
        if layer % 2 == 0:
            q, kv, u, zv, zv_s = _ab_in_proj(h, w_ab_in[j], gmlp_ln_g[j].reshape(1, GMLP_WIDTH),
                                             gmlp_ln_b[j].reshape(1, GMLP_WIDTH), (swa_c, swa_s1, swa_s2))
            cat = _ab_mix_prompt(q, kv, u, zv, swa_sinks[j], gmlp_w_s[j], gmlp_b_s[j])
            cat, nk_s, nv_s = _ab_mix_sample(
                cat, q, kv, u, zv_s, cache_swa_k[j].reshape(DEC_BATCH, WINDOW, SWA_KV_W),
                cache_swa_v[j].reshape(DEC_BATCH, WINDOW, SWA_KV_W), swa_sinks[j], gmlp_w_s[j], gmlp_b_s[j])
            kv_p = jnp.stack([kv[(b + 1) * SEQ - WINDOW:(b + 1) * SEQ] for b in range(BATCH)])
            kv_p = kv_p.reshape(BATCH, WINDOW, 2, SWA_KV_HEADS, SWA_HEAD_DIM)
            outs.setdefault('swa_k_p', []).append(kv_p[:, :, 0])
            outs.setdefault('swa_v_p', []).append(kv_p[:, :, 1])
            outs.setdefault('swa_k_s', []).append(nk_s.reshape(DEC_BATCH, WINDOW, SWA_KV_HEADS, SWA_HEAD_DIM))
            outs.setdefault('swa_v_s', []).append(nv_s.reshape(DEC_BATCH, WINDOW, SWA_KV_HEADS, SWA_HEAD_DIM))
            outs.setdefault('gmlp_v_s', []).append(zv_s.reshape(DEC_BATCH, DEC_SEQ, GMLP_WIDTH))
            mixed, w_out, mix_name = (cat,), w_ab_out[j], "ab_out"
        else:
            zqk, zv, zg = _ret_in_proj(h, w_ret_in[j], (ret_c, ret_s))
            y_p, s_p, y_s, s_s = _retention(zqk, zv, zg, state_ret[j], lg)
            outs.setdefault('ret_p', []).append(s_p)
            outs.setdefault('ret_s', []).append(s_s)
            mixed, w_out, mix_name = (y_p, y_s), w_ret_out[j], "ret_out"
        x, h = _out_proj(mixed, w_out.astype(BF16), x, vec(norm_mix_post, layer), vec(norm_x_pre, layer),
                         mix_name)
        wq_bf, wo_bf = w_mem_q[layer].astype(BF16), w_mem_o[layer].astype(BF16)
        g_post = vec(norm_x_post, layer)
        g_next = vec(norm_ffn_pre, layer)
        xn, hn = _mem_prompt(h, wq_bf, wo_bf, mem_k[layer], mem_v[layer], x, g_post, g_next)
        x, h = _mem_sample(xn, hn, h, wq_bf, wo_bf, ck_mem, cv_mem, layer, g_post, g_next)
        g_next = vec(norm_mix_pre, layer + 1) if layer + 1 < DEPTH else None
        x, h = _ffn(h, wu_bf, wd_bf, layer, x, vec(norm_ffn_post, layer), g_next)

    y_prompt, y_sample = x, h
    stack = lambda name: jnp.stack(outs[name])
    mem_shape = (DEPTH, BATCH, MEM_LEN, MEM_HEADS, MEM_HEAD_DIM)
    return (y_prompt.reshape(BATCH, SEQ, D_MODEL), y_sample.reshape(DEC_BATCH, DEC_SEQ, D_MODEL),
            stack('swa_k_p'), stack('swa_v_p'), stack('swa_k_s'), stack('swa_v_s'), stack('gmlp_v_s'),
            stack('ret_p'), stack('ret_s'), mem_k.reshape(mem_shape), mem_v.reshape(mem_shape))
```

```python
import functools
import math

import jax
import jax.numpy as jnp
from jax import lax
from jax.experimental import pallas as pl
from jax.experimental.pallas import tpu as pltpu

F32 = jnp.float32
BF16 = jnp.bfloat16

D_MODEL = 2048
BATCH = 2
SEQ = 4096
DEPTH = 2
DEC_BATCH = 128
DEC_SEQ = 8
PAST_LEN = 8192

SWA_HEADS = 16
SWA_KV_HEADS = 4
SWA_HEAD_DIM = 64
SWA_GROUP = SWA_HEADS // SWA_KV_HEADS
WINDOW = 128
ROPE_THETA = 500000.0
ROPE_DIM = SWA_HEAD_DIM // 4
ROPE_HALF = ROPE_DIM // 2

GMLP_GROUPS = 4
GMLP_WIDTH = 1024
GMLP_GROUP_DIM = GMLP_WIDTH // GMLP_GROUPS
GMLP_CHUNK = 128

SWA_Q_W = SWA_HEADS * SWA_HEAD_DIM
SWA_KV_W = SWA_KV_HEADS * SWA_HEAD_DIM
AB_IN = SWA_Q_W + 2 * SWA_KV_W + 2 * GMLP_WIDTH
AB_OUT = SWA_Q_W + GMLP_WIDTH

RET_HEADS = 8
RET_KEY_DIM = D_MODEL // RET_HEADS
RET_VAL_DIM = 2 * RET_KEY_DIM
RET_CHUNK = 128
RET_THETA = 10000.0
RET_QK_W = RET_HEADS * RET_KEY_DIM
RET_V_W = RET_HEADS * RET_VAL_DIM
RET_IN = 2 * RET_QK_W + 2 * RET_V_W

MEM_LEN = 256
MEM_HEADS = 4
MEM_HEAD_DIM = 128
MEM_W = MEM_HEADS * MEM_HEAD_DIM

D_FF = 4 * D_MODEL
EPS = 1e-6
NEG_INF = -1e30

MP = BATCH * SEQ
MS = DEC_BATCH * DEC_SEQ
M = MP + MS
LANES = 128
MIB = 1024 * 1024

TM_IN = 1024
TAB_ROWS = SEQ + MS
TM_OUT = 512


def _cparams(sem, vmem_mib):
    return pltpu.CompilerParams(dimension_semantics=sem, vmem_limit_bytes=vmem_mib * MIB)


def _rms(x, g):
    return x * lax.rsqrt(jnp.mean(x * x, axis=-1, keepdims=True) + EPS) * g


def _residual(y, x, g_post, g_next):
    xn = x + _rms(y, g_post)
    return xn, _rms(xn, g_next).astype(BF16)


def _rope_table_kernel(sc_ref, s1_ref, s2_ref, rc_ref, rs_ref):
    tr = sc_ref.shape[0]
    r = lax.broadcasted_iota(jnp.int32, (tr, LANES), 0) + pl.program_id(0) * tr
    pos = jnp.where(r < SEQ, r, PAST_LEN + ((r - SEQ) & (DEC_SEQ - 1))).astype(F32)
    lane = lax.broadcasted_iota(jnp.int32, (tr, LANES), 1)
    d = lane & (SWA_HEAD_DIM - 1)
    inv = jnp.exp((d & (ROPE_HALF - 1)).astype(F32) * (-math.log(ROPE_THETA) / ROPE_HALF))
    ang = pos * inv
    c, s = jnp.cos(ang), jnp.sin(ang)
    sc_ref[...] = jnp.where(d < ROPE_DIM, c, 1.0)
    s1_ref[...] = jnp.where(d < ROPE_HALF, -s, 0.0)
    s2_ref[...] = jnp.where((d >= ROPE_HALF) & (d < ROPE_DIM), s, 0.0)
    inv_r = jnp.exp(lane.astype(F32) * (-math.log(RET_THETA) / LANES))
    ang_r = pos * inv_r
    rc_ref[...] = jnp.cos(ang_r)
    rs_ref[...] = jnp.sin(ang_r)


def _rope_tables():
    tr = 512
    spec = pl.BlockSpec((tr, LANES), lambda i: (i, 0))
    shp = jax.ShapeDtypeStruct((TAB_ROWS, LANES), F32)
    return pl.pallas_call(
        _rope_table_kernel, grid=(TAB_ROWS // tr,), out_specs=[spec] * 5, out_shape=[shp] * 5,
        compiler_params=_cparams(("arbitrary",), 16), name="rope_tables")()


def _norm_kernel(xp_ref, xs_ref, g_ref, x_ref, h_ref):
    def emit(src_ref):
        x = src_ref[...]
        x_ref[...] = x
        h_ref[...] = _rms(x, g_ref[...]).astype(BF16)

    is_prompt = pl.program_id(0) < MP // x_ref.shape[0]
    pl.when(is_prompt)(lambda: emit(xp_ref))
    pl.when(jnp.logical_not(is_prompt))(lambda: emit(xs_ref))


def _stack_and_prenorm(x_prompt, x_sample, g):
    tm = TM_OUT
    row = pl.BlockSpec((tm, D_MODEL), lambda i: (i, 0))
    return pl.pallas_call(
        _norm_kernel, grid=(M // tm,),
        in_specs=[pl.BlockSpec((tm, D_MODEL), lambda i: (jnp.minimum(i, MP // tm - 1), 0)),
                  pl.BlockSpec((tm, D_MODEL), lambda i: (jnp.maximum(i - MP // tm, 0), 0)),
                  pl.BlockSpec((1, D_MODEL), lambda i: (0, 0))],
        out_specs=[row, row],
        out_shape=[jax.ShapeDtypeStruct((M, D_MODEL), F32), jax.ShapeDtypeStruct((M, D_MODEL), BF16)],
        compiler_params=_cparams(("arbitrary",), 40), name="prenorm",
    )(x_prompt.reshape(MP, D_MODEL), x_sample.reshape(MS, D_MODEL), g)


W_SUB = 512


def _inproj_kernel(h_ref, *refs, n_sub, n_side, epilogue):
    w_refs, side, outs, wbf_ref = refs[:n_sub], refs[n_sub:n_sub + n_side], refs[n_sub + n_side:-1], refs[-1]

    @pl.when(pl.program_id(1) == 0)
    def _():
        for s, w_ref in enumerate(w_refs):
            wbf_ref[:, s * W_SUB:(s + 1) * W_SUB] = w_ref[...].astype(BF16)

    acc = jnp.dot(h_ref[...], wbf_ref[...], preferred_element_type=F32)
    epilogue(acc, side, outs)


def _tab_index(j, i):
    return (jnp.where(i < MP // TM_IN, i % (SEQ // TM_IN), SEQ // TM_IN), 0)


def _inproj(h, w, col0, ncols, tn, epilogue, side, side_specs, out_shapes, out_specs, name, vmem_mib):
    tm = TM_IN
    k = h.shape[1]
    n_sub = tn // W_SUB
    w_specs = [pl.BlockSpec((k, W_SUB), lambda j, i, s=s: (0, col0 // W_SUB + j * n_sub + s))
               for s in range(n_sub)]
    return pl.pallas_call(
        functools.partial(_inproj_kernel, n_sub=n_sub, n_side=len(side), epilogue=epilogue),
        grid=(ncols // tn, M // tm),
        in_specs=[pl.BlockSpec((tm, k), lambda j, i: (i, 0))] + w_specs + side_specs,
        out_specs=out_specs, out_shape=out_shapes,
        scratch_shapes=[pltpu.VMEM((k, tn), BF16)],
        compiler_params=_cparams(("arbitrary", "arbitrary"), vmem_mib), name=name)(h, *([w] * n_sub), *side)


def _swa_rope(x, c, s1, s2):
    return x * c + pltpu.roll(x, LANES - ROPE_HALF, 1) * s1 + pltpu.roll(x, ROPE_HALF, 1) * s2


def _epi_swa_q(acc, side, outs):
    c, s1, s2 = (t[...] for t in side)
    (q_ref,) = outs
    for cc in range(acc.shape[1] // LANES):
        sl = slice(cc * LANES, (cc + 1) * LANES)
        q_ref[:, sl] = (_swa_rope(acc[:, sl], c, s1, s2) * (SWA_HEAD_DIM ** -0.5)).astype(BF16)


def _epi_swa_kv(acc, side, outs):
    c, s1, s2 = (t[...] for t in side)
    (kv_ref,) = outs
    for cc in range(SWA_KV_W // LANES):
        sl = slice(cc * LANES, (cc + 1) * LANES)
        kv_ref[:, sl] = _swa_rope(acc[:, sl], c, s1, s2)
    kv_ref[:, SWA_KV_W:] = acc[:, SWA_KV_W:]


def _epi_gelu(acc, side, outs):
    outs[0][...] = jax.nn.gelu(acc).astype(BF16)


def _epi_gelu_ln(acc, side, outs):
    g_ref, b_ref = side
    zv_ref, zvs_ref = outs
    a = jax.nn.gelu(acc)
    xc = a - jnp.mean(a, axis=-1, keepdims=True)
    y = xc * lax.rsqrt(jnp.mean(xc * xc, axis=-1, keepdims=True) + EPS) * g_ref[...] + b_ref[...]
    zv_ref[...] = y.astype(BF16)

    @pl.when(pl.program_id(1) == MP // TM_IN)
    def _():
        zvs_ref[...] = y


def _epi_ret_qk(acc, side, outs):
    cos, sin = (t[...] for t in side)
    (z_ref,) = outs
    tn = acc.shape[1]
    scale = jnp.where(pl.program_id(0) < RET_QK_W // tn, 1.0, RET_KEY_DIM ** -0.5)
    for hh in range(tn // RET_KEY_DIM):
        c0 = hh * RET_KEY_DIM
        x1, x2 = acc[:, c0:c0 + LANES], acc[:, c0 + LANES:c0 + 2 * LANES]
        z_ref[:, c0:c0 + LANES] = ((x1 * cos - x2 * sin) * scale).astype(BF16)
        z_ref[:, c0 + LANES:c0 + 2 * LANES] = ((x2 * cos + x1 * sin) * scale).astype(BF16)


def _epi_cast(acc, side, outs):
    outs[0][...] = acc.astype(BF16)


def _epi_silu(acc, side, outs):
    outs[0][...] = (acc / (1.0 + jnp.exp(-acc))).astype(BF16)


def _ab_in_proj(h, w_in, ln_g, ln_b, tabs):
    sc, s1, s2 = tabs
    tab_spec = pl.BlockSpec((TM_IN, LANES), _tab_index)
    row_out = lambda tn: pl.BlockSpec((TM_IN, tn), lambda j, i: (i, j))
    q = _inproj(h, w_in, 0, SWA_Q_W, SWA_Q_W, _epi_swa_q, [sc, s1, s2], [tab_spec] * 3,
                jax.ShapeDtypeStruct((M, SWA_Q_W), BF16), row_out(SWA_Q_W), "ab_in_q", 52)
    kv = _inproj(h, w_in, SWA_Q_W, 2 * SWA_KV_W, 512, _epi_swa_kv, [sc, s1, s2], [tab_spec] * 3,
                 jax.ShapeDtypeStruct((M, 2 * SWA_KV_W), F32), row_out(512), "ab_in_kv", 40)
    u = _inproj(h, w_in, SWA_Q_W + 2 * SWA_KV_W, GMLP_WIDTH, GMLP_WIDTH, _epi_gelu, [], [],
                jax.ShapeDtypeStruct((M, GMLP_WIDTH), BF16), row_out(GMLP_WIDTH), "ab_in_u", 52)
    vec_spec = pl.BlockSpec((1, GMLP_WIDTH), lambda j, i: (0, 0))
    zv, zv_s = _inproj(
        h, w_in, AB_IN - GMLP_WIDTH, GMLP_WIDTH, GMLP_WIDTH, _epi_gelu_ln,
        [ln_g, ln_b], [vec_spec] * 2,
        [jax.ShapeDtypeStruct((M, GMLP_WIDTH), BF16), jax.ShapeDtypeStruct((MS, GMLP_WIDTH), F32)],
        [row_out(GMLP_WIDTH), pl.BlockSpec((MS, GMLP_WIDTH), lambda j, i: (0, 0))], "ab_in_zv", 52)
    return q, kv, u, zv, zv_s


def _ret_in_proj(h, w_in, tabs):
    rc, rs = tabs
    tn = 1024
    tab_spec = pl.BlockSpec((TM_IN, LANES), _tab_index)
    out_spec = pl.BlockSpec((TM_IN, tn), lambda j, i: (i, j))
    seg = lambda col0, ncols, epi, side, specs, name: _inproj(
        h, w_in, col0, ncols, tn, epi, side, specs, jax.ShapeDtypeStruct((M, ncols), BF16), out_spec, name, 52)
    return (seg(0, 2 * RET_QK_W, _epi_ret_qk, [rc, rs], [tab_spec] * 2, "ret_in_qk"),
            seg(2 * RET_QK_W, RET_V_W, _epi_cast, [], [], "ret_in_v"),
            seg(2 * RET_QK_W + RET_V_W, RET_V_W, _epi_silu, [], [], "ret_in_g"))


AB_ROWS = 512


def _sink_softmax(s, sink):
    m = jnp.maximum(jnp.max(s, axis=-1, keepdims=True), sink)
    p = jnp.exp(s - m)
    den = jnp.sum(p, axis=-1, keepdims=True) + jnp.exp(sink - m)
    return p, den


def _ab_prompt_kernel(sink_ref, q_ref, kv_ref, kvp_ref, u_ref, zv_ref, ws_ref, bs_ref, cat_ref,
                      kd_ref, vd_ref):
    r = pl.program_id(1)
    low = lax.broadcasted_iota(jnp.int32, (1, LANES), 1) < SWA_HEAD_DIM

    def both_halves(x, half):
        sw = pltpu.roll(x, SWA_HEAD_DIM, 1)
        return (jnp.where(low, x, sw) if half == 0 else jnp.where(low, sw, x)).astype(BF16)

    for kh in range(SWA_KV_HEADS):
        kcol = slice((kh // 2) * LANES, (kh // 2 + 1) * LANES)
        vcol = slice(SWA_KV_W + (kh // 2) * LANES, SWA_KV_W + (kh // 2 + 1) * LANES)
        kd_ref[kh, :WINDOW, :] = both_halves(kvp_ref[:, kcol], kh % 2)
        kd_ref[kh, WINDOW:, :] = both_halves(kv_ref[:, kcol], kh % 2)
        vd_ref[kh, :WINDOW, :] = both_halves(kvp_ref[:, vcol], kh % 2)
        vd_ref[kh, WINDOW:, :] = both_halves(kv_ref[:, vcol], kh % 2)

    t = lax.broadcasted_iota(jnp.int32, (WINDOW, 2 * WINDOW), 0)
    sk = lax.broadcasted_iota(jnp.int32, (WINDOW, 2 * WINDOW), 1)
    band = (sk <= t + WINDOW) & (sk > t)
    sink_slot = sk == t
    tril = (lax.broadcasted_iota(jnp.int32, (GMLP_CHUNK, GMLP_CHUNK), 0)
            >= lax.broadcasted_iota(jnp.int32, (GMLP_CHUNK, GMLP_CHUNK), 1))

    for sb in range(AB_ROWS // WINDOW):
        rs = slice(sb * WINDOW, (sb + 1) * WINDOW)
        ks = slice(sb * WINDOW, sb * WINDOW + 2 * WINDOW)
        valid = band & ((sk >= WINDOW) | (r * (AB_ROWS // WINDOW) + sb > 0))
        for kh in range(SWA_KV_HEADS):
            qm = []
            for g in range(SWA_GROUP):
                hd = kh * SWA_GROUP + g
                qb = q_ref[rs, (hd // 2) * LANES:(hd // 2 + 1) * LANES]
                qm.append(jnp.where(low if hd % 2 == 0 else jnp.logical_not(low), qb, jnp.zeros_like(qb)))
            s = lax.dot_general(jnp.concatenate(qm, axis=0), kd_ref[kh, ks, :], (((1,), (1,)), ((), ())),
                                preferred_element_type=F32)
            pv, den = [], []
            for g in range(SWA_GROUP):
                sg = s[g * WINDOW:(g + 1) * WINDOW]
                sg = jnp.where(valid, sg, jnp.where(sink_slot, sink_ref[kh * SWA_GROUP + g], NEG_INF))
                p = jnp.exp(sg - jnp.max(sg, axis=-1, keepdims=True))
                den.append(jnp.sum(p, axis=-1, keepdims=True))
                pv.append(jnp.where(sink_slot, 0.0, p).astype(BF16))
            o = jnp.dot(jnp.concatenate(pv, axis=0), vd_ref[kh, ks, :], preferred_element_type=F32)
            for pr in range(SWA_GROUP // 2):
                lo = o[(2 * pr) * WINDOW:(2 * pr + 1) * WINDOW] / den[2 * pr]
                hi = o[(2 * pr + 1) * WINDOW:(2 * pr + 2) * WINDOW] / den[2 * pr + 1]
                c0 = (kh * (SWA_GROUP // 2) + pr) * LANES
                cat_ref[rs, c0:c0 + LANES] = jnp.where(low, lo, hi).astype(BF16)
        zb = zv_ref[rs, :]
        ub = u_ref[rs, :]
        for g in range(GMLP_GROUPS):
            gs = slice(g * GMLP_GROUP_DIM, (g + 1) * GMLP_GROUP_DIM)
            w = jnp.where(tril, ws_ref[g], 0.0).astype(BF16)
            mixed = jnp.dot(w, zb[:, gs], preferred_element_type=F32) + bs_ref[g]
            cat_ref[rs, SWA_Q_W + g * GMLP_GROUP_DIM:SWA_Q_W + (g + 1) * GMLP_GROUP_DIM] = (
                ub[:, gs].astype(F32) * mixed).astype(BF16)


def _ab_mix_prompt(q, kv, u, zv, sinks, w_s, b_s):
    nr = SEQ // AB_ROWS
    sub = AB_ROWS // WINDOW
    row = lambda b, r: (b * nr + r, 0)
    return pl.pallas_call(
        _ab_prompt_kernel, grid=(BATCH, nr),
        in_specs=[pl.BlockSpec(memory_space=pltpu.SMEM),
                  pl.BlockSpec((AB_ROWS, SWA_Q_W), row),
                  pl.BlockSpec((AB_ROWS, 2 * SWA_KV_W), row),
                  pl.BlockSpec((WINDOW, 2 * SWA_KV_W), lambda b, r: (jnp.maximum((b * nr + r) * sub - 1, 0), 0)),
                  pl.BlockSpec((AB_ROWS, GMLP_WIDTH), row),
                  pl.BlockSpec((AB_ROWS, GMLP_WIDTH), row),
                  pl.BlockSpec((GMLP_GROUPS, GMLP_CHUNK, GMLP_CHUNK), lambda b, r: (0, 0, 0)),
                  pl.BlockSpec((GMLP_GROUPS, GMLP_CHUNK, 1), lambda b, r: (0, 0, 0))],
        out_specs=pl.BlockSpec((AB_ROWS, AB_OUT), row),
        out_shape=jax.ShapeDtypeStruct((M, AB_OUT), BF16),
        scratch_shapes=[pltpu.VMEM((SWA_KV_HEADS, WINDOW + AB_ROWS, LANES), BF16)] * 2,
        compiler_params=_cparams(("arbitrary", "arbitrary"), 40), name="ab_mix_prompt",
    )(sinks, q, kv, kv, u, zv, w_s, b_s.reshape(GMLP_GROUPS, GMLP_CHUNK, 1))


AB_NB = 16
AB_KEYS = 2 * WINDOW


def _ab_sample_kernel(sink_ref, q_ref, kv_ref, u_ref, zv_ref, ck_ref, cv_ref, ws_ref, bs_ref, cat_in_ref,
                      cat_ref, nk_ref, nv_ref):
    del cat_in_ref
    nb, t = AB_NB, DEC_SEQ
    kv3 = kv_ref[...].reshape(nb, t, 2 * SWA_KV_W)
    kn, vn = kv3[:, :, :SWA_KV_W], kv3[:, :, SWA_KV_W:]
    ck, cv = ck_ref[...], cv_ref[...]
    nk_ref[:, :WINDOW - t, :] = ck[:, t:, :]
    nk_ref[:, WINDOW - t:, :] = kn
    nv_ref[:, :WINDOW - t, :] = cv[:, t:, :]
    nv_ref[:, WINDOW - t:, :] = vn
    pad = jnp.zeros((nb, AB_KEYS - WINDOW - t, SWA_KV_W), F32)
    kall = jnp.concatenate([ck, kn, pad], axis=1).astype(BF16)
    vall = jnp.concatenate([cv, vn, pad], axis=1).astype(BF16)

    rows = SWA_GROUP * t
    tq = lax.broadcasted_iota(jnp.int32, (rows, AB_KEYS), 0) & (t - 1)
    sk = lax.broadcasted_iota(jnp.int32, (rows, AB_KEYS), 1)
    valid = ((sk < WINDOW) & (sk > tq)) | ((sk >= WINDOW) & (sk - WINDOW <= tq))

    q3 = q_ref[...].astype(F32).reshape(nb, t, SWA_Q_W)
    pieces = []
    for kh in range(SWA_KV_HEADS):
        heads = [kh * SWA_GROUP + g for g in range(SWA_GROUP)]
        qs = jnp.concatenate([q3[:, :, hd * SWA_HEAD_DIM:(hd + 1) * SWA_HEAD_DIM] for hd in heads],
                             axis=1).astype(BF16)
        cs = slice(kh * SWA_HEAD_DIM, (kh + 1) * SWA_HEAD_DIM)
        s = jnp.einsum('nqd,nkd->nqk', qs, kall[:, :, cs], preferred_element_type=F32)
        s = jnp.where(valid[None], s, NEG_INF)
        sink = jnp.concatenate([jnp.full((t, 1), sink_ref[hd], F32) for hd in heads], axis=0)[None]
        p, den = _sink_softmax(s, sink)
        o = jnp.einsum('nqk,nkd->nqd', p.astype(BF16), vall[:, :, cs], preferred_element_type=F32) / den
        pieces += [o[:, g * t:(g + 1) * t, :] for g in range(SWA_GROUP)]
    attn = jnp.concatenate(pieces, axis=2).reshape(nb * t, SWA_Q_W)
    cat_ref[:, :SWA_Q_W] = attn.astype(BF16)

    zv3 = zv_ref[...].reshape(nb, t, GMLP_WIDTH)
    u3 = u_ref[...].astype(F32).reshape(nb, t, GMLP_WIDTH)
    irow = lax.broadcasted_iota(jnp.int32, (t, GMLP_GROUP_DIM), 0)
    for g in range(GMLP_GROUPS):
        gs = slice(g * GMLP_GROUP_DIM, (g + 1) * GMLP_GROUP_DIM)
        zg = zv3[:, :, gs]
        mixed = jnp.zeros((nb, t, GMLP_GROUP_DIM), F32) + bs_ref[g][None]
        for j in range(t):
            wcol = jnp.where(irow >= j, ws_ref[g, j], 0.0)
            mixed = mixed + zg[:, j:j + 1, :] * wcol[None]
        gate = (u3[:, :, gs] * mixed).reshape(nb * t, GMLP_GROUP_DIM)
        cat_ref[:, SWA_Q_W + g * GMLP_GROUP_DIM:SWA_Q_W + (g + 1) * GMLP_GROUP_DIM] = gate.astype(BF16)


def _ab_mix_sample(cat, q, kv, u, zv_s, cache_k, cache_v, sinks, w_s, b_s):
    nb, t = AB_NB, DEC_SEQ
    rows = nb * t
    off = MP // rows
    srow = lambda i: (off + i, 0)
    wt = jnp.broadcast_to(jnp.swapaxes(w_s[:, :t, :t], 1, 2)[..., None], (GMLP_GROUPS, t, t, GMLP_GROUP_DIM))
    bt = jnp.broadcast_to(b_s[:, :t, None], (GMLP_GROUPS, t, GMLP_GROUP_DIM))
    cspec = pl.BlockSpec((nb, WINDOW, SWA_KV_W), lambda i: (i, 0, 0))
    cshape = jax.ShapeDtypeStruct((DEC_BATCH, WINDOW, SWA_KV_W), F32)
    return pl.pallas_call(
        _ab_sample_kernel, grid=(DEC_BATCH // nb,),
        in_specs=[pl.BlockSpec(memory_space=pltpu.SMEM),
                  pl.BlockSpec((rows, SWA_Q_W), srow),
                  pl.BlockSpec((rows, 2 * SWA_KV_W), srow),
                  pl.BlockSpec((rows, GMLP_WIDTH), srow),
                  pl.BlockSpec((rows, GMLP_WIDTH), lambda i: (i, 0)),
                  cspec, cspec,
                  pl.BlockSpec((GMLP_GROUPS, t, t, GMLP_GROUP_DIM), lambda i: (0, 0, 0, 0)),
                  pl.BlockSpec((GMLP_GROUPS, t, GMLP_GROUP_DIM), lambda i: (0, 0, 0)),
                  pl.BlockSpec(memory_space=pl.ANY)],
        out_specs=[pl.BlockSpec((rows, AB_OUT), srow), cspec, cspec],
        out_shape=[jax.ShapeDtypeStruct((M, AB_OUT), BF16), cshape, cshape],
        input_output_aliases={9: 0},
        compiler_params=_cparams(("arbitrary",), 48), name="ab_mix_sample",
    )(sinks, q, kv, u, zv_s, cache_k, cache_v, wt, bt, cat)


RET_ROWS = 1024
RET_BLOCK = 256


def _ret_prompt_part(lg, q_ref, k_ref, v_ref, g_ref, y_ref, sfin_ref, s_ref):
    c = pl.program_id(2)
    L = RET_BLOCK

    @pl.when(c == 0)
    def _():
        s_ref[...] = jnp.zeros_like(s_ref)

    diff = (lax.broadcasted_iota(jnp.int32, (L, L), 0) - lax.broadcasted_iota(jnp.int32, (L, L), 1)).astype(F32)
    dmat = jnp.where(diff >= 0, jnp.exp(lg * jnp.maximum(diff, 0.0)), 0.0)
    idx = lax.broadcasted_iota(jnp.int32, (L, 1), 0).astype(F32)
    q_dec = jnp.exp(lg * (idx + 1.0))
    k_dec = jnp.exp(lg * (L - 1.0 - idx))
    s_dec = jnp.exp(lg * jnp.full((1, 1), float(L), F32))

    def chunk(ci, carry):
        rs = pl.ds(pl.multiple_of(ci * L, L), L)
        q, k, v = q_ref[rs, :], k_ref[rs, :], v_ref[rs, :]
        att = lax.dot_general(q, k, (((1,), (1,)), ((), ())), preferred_element_type=F32) * dmat
        s0 = s_ref[...]
        o = (jnp.dot(att.astype(BF16), v, preferred_element_type=F32)
             + jnp.dot(q, s0.astype(BF16), preferred_element_type=F32) * q_dec)
        kd = (k.astype(F32) * k_dec).astype(BF16)
        s_ref[...] = s0 * s_dec + lax.dot_general(kd, v, (((0,), (0,)), ((), ())), preferred_element_type=F32)
        on = o * lax.rsqrt(jnp.mean(o * o, axis=-1, keepdims=True) + EPS)
        y_ref[rs, :] = (on * g_ref[rs, :].astype(F32)).astype(BF16)
        return carry

    lax.fori_loop(0, RET_ROWS // L, chunk, 0, unroll=2)

    @pl.when(c == pl.num_programs(2) - 1)
    def _():
        sfin_ref[...] = s_ref[...]


RET_NB = DEC_BATCH // (BATCH * (SEQ // RET_ROWS))


def _ret_sample_part(lg, q_ref, k_ref, v_ref, g_ref, s0_ref, y_ref, s1_ref):
    nb, t = RET_NB, DEC_SEQ
    rows = nb * t
    q, k, v = q_ref[...], k_ref[...], v_ref[...]
    ri = lax.broadcasted_iota(jnp.int32, (rows, rows), 0)
    ci = lax.broadcasted_iota(jnp.int32, (rows, rows), 1)
    d = ((ri & (t - 1)) - (ci & (t - 1))).astype(F32)
    same = (ri // t) == (ci // t)
    dmat = jnp.where(same & (d >= 0), jnp.exp(lg * jnp.maximum(d, 0.0)), 0.0)
    att = lax.dot_general(q, k, (((1,), (1,)), ((), ())), preferred_element_type=F32) * dmat
    o_intra = jnp.dot(att.astype(BF16), v, preferred_element_type=F32)

    rcol = lax.broadcasted_iota(jnp.int32, (rows, 1), 0)
    tt = (rcol & (t - 1)).astype(F32)
    q_dec = jnp.exp(lg * (tt + 1.0))
    k_dec = jnp.exp(lg * (t - 1.0 - tt))
    s_dec = jnp.exp(lg * jnp.full((1, 1), float(t), F32))
    kd = (k.astype(F32) * k_dec).astype(BF16)
    qf = q.astype(F32)
    vf = v.astype(F32)
    o_rows = []
    for n in range(nb):
        s0 = s0_ref[n]
        qn = qf[n * t:(n + 1) * t].astype(BF16)
        o_rows.append(jnp.dot(qn, s0.astype(BF16), preferred_element_type=F32))
        vn = jnp.where(rcol // t == n, vf, 0.0).astype(BF16)
        s1_ref[n] = s0 * s_dec + lax.dot_general(kd, vn, (((0,), (0,)), ((), ())), preferred_element_type=F32)
    o = o_intra + jnp.concatenate(o_rows, axis=0) * q_dec
    on = o * lax.rsqrt(jnp.mean(o * o, axis=-1, keepdims=True) + EPS)
    y_ref[...] = (on * g_ref[...].astype(F32)).astype(BF16)


def _retention_kernel(lg_ref, q_ref, k_ref, v_ref, g_ref, sq_ref, sk_ref, sv_ref, sg_ref, s0_ref,
                      yp_ref, sfin_ref, ys_ref, s1_ref, s_ref):
    lg = lg_ref[pl.program_id(1)]
    _ret_prompt_part(lg, q_ref, k_ref, v_ref, g_ref, yp_ref, sfin_ref, s_ref)
    _ret_sample_part(lg, sq_ref, sk_ref, sv_ref, sg_ref, s0_ref, ys_ref, s1_ref)


def _retention(zqk, zv, zg, state, lg):
    nc = SEQ // RET_ROWS
    srows = RET_NB * DEC_SEQ
    soff = MP // srows
    prow = lambda b, c: b * nc + c
    head_specs = lambda rows, row: [
        pl.BlockSpec((rows, RET_KEY_DIM), lambda b, h, c: (row(b, c), h)),
        pl.BlockSpec((rows, RET_KEY_DIM), lambda b, h, c: (row(b, c), RET_HEADS + h)),
        pl.BlockSpec((rows, RET_VAL_DIM), lambda b, h, c: (row(b, c), h)),
        pl.BlockSpec((rows, RET_VAL_DIM), lambda b, h, c: (row(b, c), h))]
    sspec = pl.BlockSpec((RET_NB, None, RET_KEY_DIM, RET_VAL_DIM), lambda b, h, c: (prow(b, c), h, 0, 0))
    return pl.pallas_call(
        _retention_kernel, grid=(BATCH, RET_HEADS, nc),
        in_specs=([pl.BlockSpec(memory_space=pltpu.SMEM)] + head_specs(RET_ROWS, prow)
                  + head_specs(srows, lambda b, c: soff + prow(b, c)) + [sspec]),
        out_specs=[pl.BlockSpec((RET_ROWS, RET_VAL_DIM), lambda b, h, c: (prow(b, c), h)),
                   pl.BlockSpec((None, None, RET_KEY_DIM, RET_VAL_DIM), lambda b, h, c: (b, h, 0, 0)),
                   pl.BlockSpec((srows, RET_VAL_DIM), lambda b, h, c: (prow(b, c), h)),
                   sspec],
        out_shape=[jax.ShapeDtypeStruct((MP, RET_V_W), BF16),
                   jax.ShapeDtypeStruct((BATCH, RET_HEADS, RET_KEY_DIM, RET_VAL_DIM), F32),
                   jax.ShapeDtypeStruct((MS, RET_V_W), BF16),
                   jax.ShapeDtypeStruct((DEC_BATCH, RET_HEADS, RET_KEY_DIM, RET_VAL_DIM), F32)],
        scratch_shapes=[pltpu.VMEM((RET_KEY_DIM, RET_VAL_DIM), F32)],
        compiler_params=_cparams(("arbitrary", "arbitrary", "arbitrary"), 56), name="retention",
    )(lg, zqk, zqk, zv, zg, zqk, zqk, zv, zg, state)


def _zero_on_first(acc_ref, k):
    @pl.when(k == 0)
    def _():
        acc_ref[...] = jnp.zeros_like(acc_ref)


def _outproj_kernel(*refs, n_prompt_tiles):
    a_refs, (w_ref, x_ref, gp_ref, gn_ref, xn_ref, hn_ref) = refs[:-6], refs[-6:]

    def emit(a_ref):
        y = jnp.dot(a_ref[...], w_ref[...], preferred_element_type=F32)
        xn, hn = _residual(y, x_ref[...], gp_ref[...], gn_ref[...])
        xn_ref[...] = xn
        hn_ref[...] = hn

    if len(a_refs) == 1:
        emit(a_refs[0])
    else:
        is_prompt = pl.program_id(0) < n_prompt_tiles
        pl.when(is_prompt)(lambda: emit(a_refs[0]))
        pl.when(jnp.logical_not(is_prompt))(lambda: emit(a_refs[1]))


def _out_proj(a_parts, w_bf, x, g_post, g_next, name):
    kdim = w_bf.shape[0]
    tm = TM_OUT * D_MODEL // kdim
    np_t = MP // tm
    row = pl.BlockSpec((tm, D_MODEL), lambda i: (i, 0))
    vec = pl.BlockSpec((1, D_MODEL), lambda i: (0, 0))
    if len(a_parts) == 1:
        a_specs = [pl.BlockSpec((tm, kdim), lambda i: (i, 0))]
    else:
        a_specs = [pl.BlockSpec((tm, kdim), lambda i: (jnp.minimum(i, np_t - 1), 0)),
                   pl.BlockSpec((tm, kdim), lambda i: (jnp.maximum(i - np_t, 0), 0))]
    return pl.pallas_call(
        functools.partial(_outproj_kernel, n_prompt_tiles=np_t), grid=(M // tm,),
        in_specs=a_specs + [pl.BlockSpec((kdim, D_MODEL), lambda i: (0, 0), pipeline_mode=pl.Buffered(1)),
                            row, vec, vec],
        out_specs=[row, row],
        out_shape=[jax.ShapeDtypeStruct((M, D_MODEL), F32), jax.ShapeDtypeStruct((M, D_MODEL), BF16)],
        compiler_params=_cparams(("arbitrary",), 48), name=name,
    )(*a_parts, w_bf, x, g_post, g_next)


def _ffn_up_down(h_ref, wu_ref, wd_ref):
    a = jnp.maximum(jnp.dot(h_ref[...], wu_ref[...], preferred_element_type=F32), 0.0)
    return jnp.dot((a * a).astype(BF16), wd_ref[...], preferred_element_type=F32)


def _ffn_kernel(h_ref, wu_ref, wd_ref, x_ref, gp_ref, gn_ref, xn_ref, hn_ref):
    c = pl.program_id(1)
    _zero_on_first(xn_ref, c)
    xn_ref[...] += _ffn_up_down(h_ref, wu_ref, wd_ref)

    @pl.when(c == pl.num_programs(1) - 1)
    def _():
        xn, hn = _residual(xn_ref[...], x_ref[...], gp_ref[...], gn_ref[...])
        xn_ref[...] = xn
        hn_ref[...] = hn


def _ffn_last_kernel(h_ref, wu_ref, wd_ref, x_ref, gp_ref, yp_ref, ys_ref, acc_ref):
    i, c = pl.program_id(0), pl.program_id(1)
    _zero_on_first(acc_ref, c)
    acc_ref[...] += _ffn_up_down(h_ref, wu_ref, wd_ref)
    last = c == pl.num_programs(1) - 1
    is_prompt = i < MP // acc_ref.shape[0]

    @pl.when(last & is_prompt)
    def _():
        yp_ref[...] = x_ref[...] + _rms(acc_ref[...], gp_ref[...])

    @pl.when(last & jnp.logical_not(is_prompt))
    def _():
        ys_ref[...] = x_ref[...] + _rms(acc_ref[...], gp_ref[...])


def _ffn(h, wu_bf, wd_bf, layer, x, g_post, g_next):
    tm, tc = TM_OUT, 1024
    row = pl.BlockSpec((tm, D_MODEL), lambda i, c: (i, 0))
    vec = pl.BlockSpec((1, D_MODEL), lambda i, c: (0, 0))
    w_specs = [pl.BlockSpec((None, D_MODEL, tc), lambda i, c: (layer, 0, c)),
               pl.BlockSpec((None, tc, D_MODEL), lambda i, c: (layer, c, 0))]
    grid = (M // tm, D_FF // tc)
    if g_next is not None:
        return pl.pallas_call(
            _ffn_kernel, grid=grid, in_specs=[row] + w_specs + [row, vec, vec], out_specs=[row, row],
            out_shape=[jax.ShapeDtypeStruct((M, D_MODEL), F32), jax.ShapeDtypeStruct((M, D_MODEL), BF16)],
            compiler_params=_cparams(("arbitrary", "arbitrary"), 56), name="ffn",
        )(h, wu_bf, wd_bf, x, g_post, g_next)
    np_t = MP // tm
    return pl.pallas_call(
        _ffn_last_kernel, grid=grid, in_specs=[row] + w_specs + [row, vec],
        out_specs=[pl.BlockSpec((tm, D_MODEL), lambda i, c: (jnp.minimum(i, np_t - 1), 0)),
                   pl.BlockSpec((tm, D_MODEL), lambda i, c: (jnp.maximum(i - np_t, 0), 0))],
        out_shape=[jax.ShapeDtypeStruct((MP, D_MODEL), F32), jax.ShapeDtypeStruct((MS, D_MODEL), F32)],
        scratch_shapes=[pltpu.VMEM((tm, D_MODEL), F32)],
        compiler_params=_cparams(("arbitrary", "arbitrary"), 56), name="ffn_last",
    )(h, wu_bf, wd_bf, x, g_post)


def _mem_kv_kernel(m_ref, g_ref, wk_ref, wv_ref, k_ref, v_ref):
    m = _rms(m_ref[...], g_ref[0]).astype(BF16)
    k_ref[0] = jnp.dot(m, wk_ref[0].astype(BF16), preferred_element_type=F32)
    v_ref[0] = jnp.dot(m, wv_ref[0].astype(BF16), preferred_element_type=F32)


def _mem_kv(mem, g, w_k, w_v):
    rows = BATCH * MEM_LEN
    wspec = pl.BlockSpec((1, D_MODEL, MEM_W), lambda l: (l, 0, 0))
    ospec = pl.BlockSpec((1, rows, MEM_W), lambda l: (l, 0, 0))
    oshape = jax.ShapeDtypeStruct((DEPTH, rows, MEM_W), F32)
    return pl.pallas_call(
        _mem_kv_kernel, grid=(DEPTH,),
        in_specs=[pl.BlockSpec((rows, D_MODEL), lambda l: (0, 0)),
                  pl.BlockSpec((1, 1, D_MODEL), lambda l: (l, 0, 0)), wspec, wspec],
        out_specs=[ospec, ospec], out_shape=[oshape, oshape],
        compiler_params=_cparams(("arbitrary",), 48), name="mem_kv",
    )(mem.reshape(rows, D_MODEL), g.reshape(DEPTH, 1, D_MODEL), w_k, w_v)


def _mem_attend_prompt(q, mk_ref, mv_ref):
    q = q.astype(BF16)
    mk, mv = mk_ref[...].astype(BF16), mv_ref[...].astype(BF16)
    outs = []
    for hh in range(MEM_HEADS):
        cs = slice(hh * MEM_HEAD_DIM, (hh + 1) * MEM_HEAD_DIM)
        s = lax.dot_general(q[:, cs], mk[:, cs], (((1,), (1,)), ((), ())),
                            preferred_element_type=F32) * (MEM_HEAD_DIM ** -0.5)
        p = jnp.exp(s - jnp.max(s, axis=-1, keepdims=True))
        den = jnp.sum(p, axis=-1, keepdims=True)
        outs.append(jnp.dot(p.astype(BF16), mv[:, cs], preferred_element_type=F32) / den)
    return jnp.concatenate(outs, axis=1)


def _mem_attend_sample(q, ck_ref, cv_ref):
    nb, t = ck_ref.shape[0], DEC_SEQ
    qs = jnp.concatenate([q[:, hh * MEM_HEAD_DIM:(hh + 1) * MEM_HEAD_DIM].reshape(nb, t, MEM_HEAD_DIM)
                          for hh in range(MEM_HEADS)], axis=1).astype(BF16)
    s = jnp.einsum('nqd,nkd->nqk', qs, ck_ref[...].astype(BF16),
                   preferred_element_type=F32) * (MEM_HEAD_DIM ** -0.5)
    shape = (MEM_HEADS * t, MEM_LEN * MEM_HEADS)
    own = ((lax.broadcasted_iota(jnp.int32, shape, 1) & (MEM_HEADS - 1))
           == lax.broadcasted_iota(jnp.int32, shape, 0) // t)
    s = jnp.where(own[None], s, NEG_INF)
    p = jnp.exp(s - jnp.max(s, axis=-1, keepdims=True))
    den = jnp.sum(p, axis=-1, keepdims=True)
    o3 = jnp.einsum('nqk,nkd->nqd', p.astype(BF16), cv_ref[...].astype(BF16),
                    preferred_element_type=F32) / den
    return jnp.concatenate([o3[:, hh * t:(hh + 1) * t, :].reshape(nb * t, MEM_HEAD_DIM)
                            for hh in range(MEM_HEADS)], axis=1)


def _mem_kernel(hp_ref, hs_ref, wq_ref, wo_ref, mk_ref, mv_ref, ck_ref, cv_ref, x_ref, gp_ref, gn_ref,
                xn_ref, hn_ref, os_ref):
    i = pl.program_id(0)
    tm = x_ref.shape[0]
    n_prompt = MP // tm
    srows = hs_ref.shape[0]
    project = lambda h_ref: jnp.dot(h_ref[...], wq_ref[...], preferred_element_type=F32)

    def finish(o):
        y = jnp.dot(o.astype(BF16), wo_ref[...], preferred_element_type=F32)
        xn, hn = _residual(y, x_ref[...], gp_ref[...], gn_ref[...])
        xn_ref[...] = xn
        hn_ref[...] = hn

    @pl.when(i < n_prompt)
    def _():
        os_ref[pl.ds(pl.multiple_of(i * srows, srows), srows), :] = _mem_attend_sample(
            project(hs_ref), ck_ref, cv_ref)
        finish(_mem_attend_prompt(project(hp_ref), mk_ref, mv_ref))

    @pl.when(i >= n_prompt)
    def _():
        finish(os_ref[pl.ds(pl.multiple_of((i - n_prompt) * tm, tm), tm), :])


MEM_TM = 256
MEM_NB = DEC_BATCH // (MP // MEM_TM)


def _mem_attention(h, x, wq_bf, wo_bf, mk, mv, cache_k, cache_v, layer, g_post, g_next):
    tm, nb = MEM_TM, MEM_NB
    srows = nb * DEC_SEQ
    soff = MP // srows
    per_b = SEQ // tm
    last_p = MP // tm - 1
    pidx = lambda i: jnp.minimum(i, last_p)
    row = pl.BlockSpec((tm, D_MODEL), lambda i: (i, 0))
    vec = pl.BlockSpec((1, D_MODEL), lambda i: (0, 0))
    kvspec = pl.BlockSpec((None, MEM_LEN, MEM_W), lambda i: (layer * BATCH + pidx(i) // per_b, 0, 0))
    cspec = pl.BlockSpec((None, nb, MEM_LEN * MEM_HEADS, MEM_HEAD_DIM), lambda i: (layer, pidx(i), 0, 0))
    resident = lambda shape: pl.BlockSpec(shape, lambda i: (0, 0), pipeline_mode=pl.Buffered(1))
    return pl.pallas_call(
        _mem_kernel, grid=(M // tm,),
        in_specs=[pl.BlockSpec((tm, D_MODEL), lambda i: (pidx(i), 0)),
                  pl.BlockSpec((srows, D_MODEL), lambda i: (soff + pidx(i), 0)),
                  resident((D_MODEL, MEM_W)), resident((MEM_W, D_MODEL)), kvspec, kvspec,
                  cspec, cspec, row, vec, vec],
        out_specs=[row, row],
        out_shape=[jax.ShapeDtypeStruct((M, D_MODEL), F32), jax.ShapeDtypeStruct((M, D_MODEL), BF16)],
        scratch_shapes=[pltpu.VMEM((MS, MEM_W), F32)],
        compiler_params=_cparams(("arbitrary",), 48), name="mem_attention",
    )(h, h, wq_bf, wo_bf, mk, mv, cache_k, cache_v, x, g_post, g_next)


def kernel(x_prompt, x_sample, cache_swa_k, cache_swa_v, state_ret, cache_mem_k, cache_mem_v, mem_prompt, norm_mix_pre, norm_mix_post, norm_mem, norm_x_pre, norm_x_post, norm_ffn_pre, norm_ffn_post, w_ab_in, w_ab_out, swa_sinks, gmlp_ln_g, gmlp_ln_b, gmlp_w_s, gmlp_b_s, w_ret_in, w_ret_out, w_mem_q, w_mem_k, w_mem_v, w_mem_o, w_ffn_up, w_ffn_down):
    vec = lambda g, layer: g[layer].reshape(1, D_MODEL)
    swa_c, swa_s1, swa_s2, ret_c, ret_s = _rope_tables()
    mem_k, mem_v = _mem_kv(mem_prompt, norm_mem, w_mem_k, w_mem_v)
    mem_k_b = mem_k.reshape(DEPTH * BATCH, MEM_LEN, MEM_W)
    mem_v_b = mem_v.reshape(DEPTH * BATCH, MEM_LEN, MEM_W)
    ck_mem = cache_mem_k.reshape(DEPTH, DEC_BATCH, MEM_LEN * MEM_HEADS, MEM_HEAD_DIM)
    cv_mem = cache_mem_v.reshape(DEPTH, DEC_BATCH, MEM_LEN * MEM_HEADS, MEM_HEAD_DIM)
    lg = jnp.log1p(-jnp.exp2(-5.0 - jnp.arange(RET_HEADS, dtype=F32)))
    wu_bf, wd_bf = w_ffn_up.astype(BF16), w_ffn_down.astype(BF16)

    x, h = _stack_and_prenorm(x_prompt, x_sample, vec(norm_mix_pre, 0))
    outs = {}
    for layer in range(DEPTH):
        j = layer // 2
        if layer % 2 == 0:
            q, kv, u, zv, zv_s = _ab_in_proj(h, w_ab_in[j], gmlp_ln_g[j].reshape(1, GMLP_WIDTH),
                                             gmlp_ln_b[j].reshape(1, GMLP_WIDTH), (swa_c, swa_s1, swa_s2))
            cat = _ab_mix_prompt(q, kv, u, zv, swa_sinks[j], gmlp_w_s[j], gmlp_b_s[j])
            cat, nk_s, nv_s = _ab_mix_sample(
                cat, q, kv, u, zv_s, cache_swa_k[j].reshape(DEC_BATCH, WINDOW, SWA_KV_W),
                cache_swa_v[j].reshape(DEC_BATCH, WINDOW, SWA_KV_W), swa_sinks[j], gmlp_w_s[j], gmlp_b_s[j])
            kv_p = jnp.stack([kv[(b + 1) * SEQ - WINDOW:(b + 1) * SEQ] for b in range(BATCH)])
            kv_p = kv_p.reshape(BATCH, WINDOW, 2, SWA_KV_HEADS, SWA_HEAD_DIM)
            outs.setdefault('swa_k_p', []).append(kv_p[:, :, 0])
            outs.setdefault('swa_v_p', []).append(kv_p[:, :, 1])
            outs.setdefault('swa_k_s', []).append(nk_s.reshape(DEC_BATCH, WINDOW, SWA_KV_HEADS, SWA_HEAD_DIM))
            outs.setdefault('swa_v_s', []).append(nv_s.reshape(DEC_BATCH, WINDOW, SWA_KV_HEADS, SWA_HEAD_DIM))
            outs.setdefault('gmlp_v_s', []).append(zv_s.reshape(DEC_BATCH, DEC_SEQ, GMLP_WIDTH))
            mixed, w_out, mix_name = (cat,), w_ab_out[j], "ab_out"
        else:
            zqk, zv, zg = _ret_in_proj(h, w_ret_in[j], (ret_c, ret_s))
            y_p, s_p, y_s, s_s = _retention(zqk, zv, zg, state_ret[j], lg)
            outs.setdefault('ret_p', []).append(s_p)
            outs.setdefault('ret_s', []).append(s_s)
            mixed, w_out, mix_name = (y_p, y_s), w_ret_out[j], "ret_out"
        x, h = _out_proj(mixed, w_out.astype(BF16), x, vec(norm_mix_post, layer), vec(norm_x_pre, layer),
                         mix_name)
        wq_bf, wo_bf = w_mem_q[layer].astype(BF16), w_mem_o[layer].astype(BF16)
        g_post = vec(norm_x_post, layer)
        g_next = vec(norm_ffn_pre, layer)
        x, h = _mem_attention(h, x, wq_bf, wo_bf, mem_k_b, mem_v_b, ck_mem, cv_mem, layer, g_post, g_next)
        g_next = vec(norm_mix_pre, layer + 1) if layer + 1 < DEPTH else None
        x, h = _ffn(h, wu_bf, wd_bf, layer, x, vec(norm_ffn_post, layer), g_next)

    y_prompt, y_sample = x, h
    stack = lambda name: jnp.stack(outs[name])
    mem_shape = (DEPTH, BATCH, MEM_LEN, MEM_HEADS, MEM_HEAD_DIM)
    return (y_prompt.reshape(BATCH, SEQ, D_MODEL), y_sample.reshape(DEC_BATCH, DEC_SEQ, D_MODEL),
            stack('swa_k_p'), stack('swa_v_p'), stack('swa_k_s'), stack('swa_v_s'), stack('gmlp_v_s'),
            stack('ret_p'), stack('ret_s'), mem_k.reshape(mem_shape), mem_v.reshape(mem_shape))
```

```python
import functools
import math

import jax
import jax.numpy as jnp
from jax import lax
from jax.experimental import pallas as pl
from jax.experimental.pallas import tpu as pltpu

F32 = jnp.float32
BF16 = jnp.bfloat16

D_MODEL = 2048
BATCH = 2
SEQ = 4096
DEPTH = 2
DEC_BATCH = 128
DEC_SEQ = 8
PAST_LEN = 8192

SWA_HEADS = 16
SWA_KV_HEADS = 4
SWA_HEAD_DIM = 64
SWA_GROUP = SWA_HEADS // SWA_KV_HEADS
WINDOW = 128
ROPE_THETA = 500000.0
ROPE_DIM = SWA_HEAD_DIM // 4
ROPE_HALF = ROPE_DIM // 2

GMLP_GROUPS = 4
GMLP_WIDTH = 1024
GMLP_GROUP_DIM = GMLP_WIDTH // GMLP_GROUPS
GMLP_CHUNK = 128

SWA_Q_W = SWA_HEADS * SWA_HEAD_DIM
SWA_KV_W = SWA_KV_HEADS * SWA_HEAD_DIM
AB_IN = SWA_Q_W + 2 * SWA_KV_W + 2 * GMLP_WIDTH
AB_OUT = SWA_Q_W + GMLP_WIDTH

RET_HEADS = 8
RET_KEY_DIM = D_MODEL // RET_HEADS
RET_VAL_DIM = 2 * RET_KEY_DIM
RET_CHUNK = 128
RET_THETA = 10000.0
RET_QK_W = RET_HEADS * RET_KEY_DIM
RET_V_W = RET_HEADS * RET_VAL_DIM
RET_IN = 2 * RET_QK_W + 2 * RET_V_W

MEM_LEN = 256
MEM_HEADS = 4
MEM_HEAD_DIM = 128
MEM_W = MEM_HEADS * MEM_HEAD_DIM

D_FF = 4 * D_MODEL
EPS = 1e-6
NEG_INF = -1e30

MP = BATCH * SEQ
MS = DEC_BATCH * DEC_SEQ
M = MP + MS
LANES = 128
MIB = 1024 * 1024

TM_IN = 1024
TAB_ROWS = SEQ + MS
TM_OUT = 512


def _cparams(sem, vmem_mib):
    return pltpu.CompilerParams(dimension_semantics=sem, vmem_limit_bytes=vmem_mib * MIB)


def _rms(x, g):
    return x * lax.rsqrt(jnp.mean(x * x, axis=-1, keepdims=True) + EPS) * g


def _residual(y, x, g_post, g_next):
    xn = x + _rms(y, g_post)
    return xn, _rms(xn, g_next).astype(BF16)


def _rope_table_kernel(sc_ref, s1_ref, s2_ref, rc_ref, rs_ref):
    tr = sc_ref.shape[0]
    r = lax.broadcasted_iota(jnp.int32, (tr, LANES), 0) + pl.program_id(0) * tr
    pos = jnp.where(r < SEQ, r, PAST_LEN + ((r - SEQ) & (DEC_SEQ - 1))).astype(F32)
    lane = lax.broadcasted_iota(jnp.int32, (tr, LANES), 1)
    d = lane & (SWA_HEAD_DIM - 1)
    inv = jnp.exp((d & (ROPE_HALF - 1)).astype(F32) * (-math.log(ROPE_THETA) / ROPE_HALF))
    ang = pos * inv
    c, s = jnp.cos(ang), jnp.sin(ang)
    sc_ref[...] = jnp.where(d < ROPE_DIM, c, 1.0)
    s1_ref[...] = jnp.where(d < ROPE_HALF, -s, 0.0)
    s2_ref[...] = jnp.where((d >= ROPE_HALF) & (d < ROPE_DIM), s, 0.0)
    inv_r = jnp.exp(lane.astype(F32) * (-math.log(RET_THETA) / LANES))
    ang_r = pos * inv_r
    rc_ref[...] = jnp.cos(ang_r)
    rs_ref[...] = jnp.sin(ang_r)


def _rope_tables():
    tr = 512
    spec = pl.BlockSpec((tr, LANES), lambda i: (i, 0))
    shp = jax.ShapeDtypeStruct((TAB_ROWS, LANES), F32)
    return pl.pallas_call(
        _rope_table_kernel, grid=(TAB_ROWS // tr,), out_specs=[spec] * 5, out_shape=[shp] * 5,
        compiler_params=_cparams(("arbitrary",), 16), name="rope_tables")()


def _norm_kernel(xp_ref, xs_ref, g_ref, x_ref, h_ref):
    def emit(src_ref):
        x = src_ref[...]
        x_ref[...] = x
        h_ref[...] = _rms(x, g_ref[...]).astype(BF16)

    is_prompt = pl.program_id(0) < MP // x_ref.shape[0]
    pl.when(is_prompt)(lambda: emit(xp_ref))
    pl.when(jnp.logical_not(is_prompt))(lambda: emit(xs_ref))


def _stack_and_prenorm(x_prompt, x_sample, g):
    tm = TM_OUT
    row = pl.BlockSpec((tm, D_MODEL), lambda i: (i, 0))
    return pl.pallas_call(
        _norm_kernel, grid=(M // tm,),
        in_specs=[pl.BlockSpec((tm, D_MODEL), lambda i: (jnp.minimum(i, MP // tm - 1), 0)),
                  pl.BlockSpec((tm, D_MODEL), lambda i: (jnp.maximum(i - MP // tm, 0), 0)),
                  pl.BlockSpec((1, D_MODEL), lambda i: (0, 0))],
        out_specs=[row, row],
        out_shape=[jax.ShapeDtypeStruct((M, D_MODEL), F32), jax.ShapeDtypeStruct((M, D_MODEL), BF16)],
        compiler_params=_cparams(("arbitrary",), 40), name="prenorm",
    )(x_prompt.reshape(MP, D_MODEL), x_sample.reshape(MS, D_MODEL), g)


W_SUB = 512


def _inproj_kernel(h_ref, *refs, n_sub, n_side, n_cast, epilogue):
    w_refs, refs = refs[:n_sub], refs[n_sub:]
    side, refs = refs[:n_side], refs[n_side:]
    cast_src, refs = refs[:n_cast], refs[n_cast:]
    wbf_ref = refs[-1]
    outs, cast_dst = refs[:len(refs) - 1 - n_cast], refs[len(refs) - 1 - n_cast:-1]

    @pl.when(pl.program_id(1) == 0)
    def _():
        for s, w_ref in enumerate(w_refs):
            wbf_ref[:, s * W_SUB:(s + 1) * W_SUB] = w_ref[...].astype(BF16)

    acc = jnp.dot(h_ref[...], wbf_ref[...], preferred_element_type=F32)
    epilogue(acc, side, outs)
    for src, dst in zip(cast_src, cast_dst):
        dst[...] = src[...].astype(BF16)


def _tab_index(j, i):
    return (jnp.where(i < MP // TM_IN, i % (SEQ // TM_IN), SEQ // TM_IN), 0)


CAST_STEPS = MP // TM_IN


def _cast_job(w, layer, nrb, ncb, block_of):
    rows, cols = w.shape[1:]
    blk = (rows // nrb, cols // ncb)
    index = lambda j, i: block_of(j, jnp.minimum(i, CAST_STEPS - 1))
    return (w, pl.BlockSpec((None,) + blk, lambda j, i: (layer,) + tuple(index(j, i))),
            jax.ShapeDtypeStruct((rows, cols), BF16), pl.BlockSpec(blk, index))


def _inproj(h, w, col0, ncols, tn, epilogue, side, side_specs, out_shapes, out_specs, name, vmem_mib,
            casts=()):
    tm = TM_IN
    k = h.shape[1]
    n_sub = tn // W_SUB
    w_specs = [pl.BlockSpec((k, W_SUB), lambda j, i, s=s: (0, col0 // W_SUB + j * n_sub + s))
               for s in range(n_sub)]
    as_list = lambda v: list(v) if isinstance(v, (list, tuple)) else [v]
    outs = pl.pallas_call(
        functools.partial(_inproj_kernel, n_sub=n_sub, n_side=len(side), n_cast=len(casts), epilogue=epilogue),
        grid=(ncols // tn, M // tm),
        in_specs=([pl.BlockSpec((tm, k), lambda j, i: (i, 0))] + w_specs + side_specs
                  + [c[1] for c in casts]),
        out_specs=as_list(out_specs) + [c[3] for c in casts],
        out_shape=as_list(out_shapes) + [c[2] for c in casts],
        scratch_shapes=[pltpu.VMEM((k, tn), BF16)],
        compiler_params=_cparams(("arbitrary", "arbitrary"), vmem_mib), name=name,
    )(h, *([w] * n_sub), *side, *[c[0] for c in casts])
    n_main = len(outs) - len(casts)
    main = outs[0] if n_main == 1 else tuple(outs[:n_main])
    return (main, tuple(outs[n_main:])) if casts else main


def _swa_rope(x, c, s1, s2):
    return x * c + pltpu.roll(x, LANES - ROPE_HALF, 1) * s1 + pltpu.roll(x, ROPE_HALF, 1) * s2


def _epi_swa_q(acc, side, outs):
    c, s1, s2 = (t[...] for t in side)
    (q_ref,) = outs
    for cc in range(acc.shape[1] // LANES):
        sl = slice(cc * LANES, (cc + 1) * LANES)
        q_ref[:, sl] = (_swa_rope(acc[:, sl], c, s1, s2) * (SWA_HEAD_DIM ** -0.5)).astype(BF16)


def _epi_swa_kv(acc, side, outs):
    c, s1, s2 = (t[...] for t in side)
    (kv_ref,) = outs
    for cc in range(SWA_KV_W // LANES):
        sl = slice(cc * LANES, (cc + 1) * LANES)
        kv_ref[:, sl] = _swa_rope(acc[:, sl], c, s1, s2)
    kv_ref[:, SWA_KV_W:] = acc[:, SWA_KV_W:]


def _epi_gelu(acc, side, outs):
    outs[0][...] = jax.nn.gelu(acc).astype(BF16)


def _epi_gelu_ln(acc, side, outs):
    g_ref, b_ref = side
    zv_ref, zvs_ref = outs
    a = jax.nn.gelu(acc)
    xc = a - jnp.mean(a, axis=-1, keepdims=True)
    y = xc * lax.rsqrt(jnp.mean(xc * xc, axis=-1, keepdims=True) + EPS) * g_ref[...] + b_ref[...]
    zv_ref[...] = y.astype(BF16)

    @pl.when(pl.program_id(1) == MP // TM_IN)
    def _():
        zvs_ref[...] = y


def _epi_ret_qk(acc, side, outs):
    cos, sin = (t[...] for t in side)
    (z_ref,) = outs
    tn = acc.shape[1]
    scale = jnp.where(pl.program_id(0) < RET_QK_W // tn, 1.0, RET_KEY_DIM ** -0.5)
    for hh in range(tn // RET_KEY_DIM):
        c0 = hh * RET_KEY_DIM
        x1, x2 = acc[:, c0:c0 + LANES], acc[:, c0 + LANES:c0 + 2 * LANES]
        z_ref[:, c0:c0 + LANES] = ((x1 * cos - x2 * sin) * scale).astype(BF16)
        z_ref[:, c0 + LANES:c0 + 2 * LANES] = ((x2 * cos + x1 * sin) * scale).astype(BF16)


def _epi_cast(acc, side, outs):
    outs[0][...] = acc.astype(BF16)


def _epi_silu(acc, side, outs):
    outs[0][...] = (acc / (1.0 + jnp.exp(-acc))).astype(BF16)


def _ab_in_proj(h, w_in, ln_g, ln_b, tabs, w_up, w_down, w_out, w_mem_q, w_mem_o, layer, j):
    sc, s1, s2 = tabs
    tab_spec = pl.BlockSpec((TM_IN, LANES), _tab_index)
    row_out = lambda tn: pl.BlockSpec((TM_IN, tn), lambda j, i: (i, j))
    half = W_SUB
    steps = CAST_STEPS
    q, (wu_bf,) = _inproj(
        h, w_in, 0, SWA_Q_W, half, _epi_swa_q, [sc, s1, s2], [tab_spec] * 3,
        jax.ShapeDtypeStruct((M, SWA_Q_W), BF16), row_out(half), "ab_in_q", 52,
        casts=[_cast_job(w_up, layer, 1, 2 * steps, lambda j, i: (0, j * steps + i))])
    kv, (wout_bf,) = _inproj(
        h, w_in, SWA_Q_W, 2 * SWA_KV_W, half, _epi_swa_kv, [sc, s1, s2], [tab_spec] * 3,
        jax.ShapeDtypeStruct((M, 2 * SWA_KV_W), F32), row_out(half), "ab_in_kv", 40,
        casts=[_cast_job(w_out, j, steps, 1, lambda j, i: (i, 0))])
    u, (wd_bf,) = _inproj(
        h, w_in, SWA_Q_W + 2 * SWA_KV_W, GMLP_WIDTH, half, _epi_gelu, [], [],
        jax.ShapeDtypeStruct((M, GMLP_WIDTH), BF16), row_out(half), "ab_in_u", 52,
        casts=[_cast_job(w_down, layer, 2 * steps, 1, lambda j, i: (j * steps + i, 0))])
    vec_spec = pl.BlockSpec((1, GMLP_WIDTH), lambda j, i: (0, 0))
    stacked = lambda w: w.reshape(1, w.shape[0] * w.shape[1], w.shape[2])
    (zv, zv_s), (wq_bf, wo_bf) = _inproj(
        h, w_in, AB_IN - GMLP_WIDTH, GMLP_WIDTH, GMLP_WIDTH, _epi_gelu_ln,
        [ln_g, ln_b], [vec_spec] * 2,
        [jax.ShapeDtypeStruct((M, GMLP_WIDTH), BF16), jax.ShapeDtypeStruct((MS, GMLP_WIDTH), F32)],
        [row_out(GMLP_WIDTH), pl.BlockSpec((MS, GMLP_WIDTH), lambda j, i: (0, 0))], "ab_in_zv", 52,
        casts=[_cast_job(stacked(w_mem_q), 0, steps, 1, lambda j, i: (i, 0)),
               _cast_job(stacked(w_mem_o), 0, steps, 1, lambda j, i: (i, 0))])
    return (q, kv, u, zv, zv_s), (wu_bf, wd_bf, wout_bf, wq_bf, wo_bf)


def _ret_in_proj(h, w_in, tabs, w_up, w_down, w_out, layer, j):
    rc, rs = tabs
    tn = 1024
    tab_spec = pl.BlockSpec((TM_IN, LANES), _tab_index)
    out_spec = pl.BlockSpec((TM_IN, tn), lambda j, i: (i, j))
    by_tile = lambda j, i: (j, i)
    seg = lambda col0, ncols, epi, side, specs, name, w, lead: _inproj(
        h, w_in, col0, ncols, tn, epi, side, specs, jax.ShapeDtypeStruct((M, ncols), BF16), out_spec, name, 56,
        casts=[_cast_job(w, lead, ncols // tn, CAST_STEPS, by_tile)])
    zqk, (wu_bf,) = seg(0, 2 * RET_QK_W, _epi_ret_qk, [rc, rs], [tab_spec] * 2, "ret_in_qk", w_up, layer)
    zv, (wd_bf,) = seg(2 * RET_QK_W, RET_V_W, _epi_cast, [], [], "ret_in_v", w_down, layer)
    zg, (wout_bf,) = seg(2 * RET_QK_W + RET_V_W, RET_V_W, _epi_silu, [], [], "ret_in_g", w_out, j)
    return (zqk, zv, zg), (wu_bf, wd_bf, wout_bf)


AB_ROWS = 512


def _sink_softmax(s, sink):
    m = jnp.maximum(jnp.max(s, axis=-1, keepdims=True), sink)
    p = jnp.exp(s - m)
    den = jnp.sum(p, axis=-1, keepdims=True) + jnp.exp(sink - m)
    return p, den


def _ab_prompt_kernel(sink_ref, q_ref, kv_ref, kvp_ref, u_ref, zv_ref, ws_ref, bs_ref, cat_ref,
                      kd_ref, vd_ref):
    r = pl.program_id(1)
    low = lax.broadcasted_iota(jnp.int32, (1, LANES), 1) < SWA_HEAD_DIM

    def both_halves(x, half):
        sw = pltpu.roll(x, SWA_HEAD_DIM, 1)
        return (jnp.where(low, x, sw) if half == 0 else jnp.where(low, sw, x)).astype(BF16)

    for kh in range(SWA_KV_HEADS):
        kcol = slice((kh // 2) * LANES, (kh // 2 + 1) * LANES)
        vcol = slice(SWA_KV_W + (kh // 2) * LANES, SWA_KV_W + (kh // 2 + 1) * LANES)
        kd_ref[kh, :WINDOW, :] = both_halves(kvp_ref[:, kcol], kh % 2)
        kd_ref[kh, WINDOW:, :] = both_halves(kv_ref[:, kcol], kh % 2)
        vd_ref[kh, :WINDOW, :] = both_halves(kvp_ref[:, vcol], kh % 2)
        vd_ref[kh, WINDOW:, :] = both_halves(kv_ref[:, vcol], kh % 2)

    t = lax.broadcasted_iota(jnp.int32, (WINDOW, 2 * WINDOW), 0)
    sk = lax.broadcasted_iota(jnp.int32, (WINDOW, 2 * WINDOW), 1)
    band = (sk <= t + WINDOW) & (sk > t)
    sink_slot = sk == t
    tril = (lax.broadcasted_iota(jnp.int32, (GMLP_CHUNK, GMLP_CHUNK), 0)
            >= lax.broadcasted_iota(jnp.int32, (GMLP_CHUNK, GMLP_CHUNK), 1))

    for sb in range(AB_ROWS // WINDOW):
        rs = slice(sb * WINDOW, (sb + 1) * WINDOW)
        ks = slice(sb * WINDOW, sb * WINDOW + 2 * WINDOW)
        valid = band & ((sk >= WINDOW) | (r * (AB_ROWS // WINDOW) + sb > 0))
        for kh in range(SWA_KV_HEADS):
            qm = []
            for g in range(SWA_GROUP):
                hd = kh * SWA_GROUP + g
                qb = q_ref[rs, (hd // 2) * LANES:(hd // 2 + 1) * LANES]
                qm.append(jnp.where(low if hd % 2 == 0 else jnp.logical_not(low), qb, jnp.zeros_like(qb)))
            s = lax.dot_general(jnp.concatenate(qm, axis=0), kd_ref[kh, ks, :], (((1,), (1,)), ((), ())),
                                preferred_element_type=F32)
            pv, den = [], []
            for g in range(SWA_GROUP):
                sg = s[g * WINDOW:(g + 1) * WINDOW]
                sg = jnp.where(valid, sg, jnp.where(sink_slot, sink_ref[kh * SWA_GROUP + g], NEG_INF))
                p = jnp.exp(sg - jnp.max(sg, axis=-1, keepdims=True))
                den.append(jnp.sum(p, axis=-1, keepdims=True))
                pv.append(jnp.where(sink_slot, 0.0, p).astype(BF16))
            o = jnp.dot(jnp.concatenate(pv, axis=0), vd_ref[kh, ks, :], preferred_element_type=F32)
            for pr in range(SWA_GROUP // 2):
                lo = o[(2 * pr) * WINDOW:(2 * pr + 1) * WINDOW] / den[2 * pr]
                hi = o[(2 * pr + 1) * WINDOW:(2 * pr + 2) * WINDOW] / den[2 * pr + 1]
                c0 = (kh * (SWA_GROUP // 2) + pr) * LANES
                cat_ref[rs, c0:c0 + LANES] = jnp.where(low, lo, hi).astype(BF16)
        zb = zv_ref[rs, :]
        ub = u_ref[rs, :]
        for g in range(GMLP_GROUPS):
            gs = slice(g * GMLP_GROUP_DIM, (g + 1) * GMLP_GROUP_DIM)
            w = jnp.where(tril, ws_ref[g], 0.0).astype(BF16)
            mixed = jnp.dot(w, zb[:, gs], preferred_element_type=F32) + bs_ref[g]
            cat_ref[rs, SWA_Q_W + g * GMLP_GROUP_DIM:SWA_Q_W + (g + 1) * GMLP_GROUP_DIM] = (
                ub[:, gs].astype(F32) * mixed).astype(BF16)


def _ab_mix_prompt(q, kv, u, zv, sinks, w_s, b_s):
    nr = SEQ // AB_ROWS
    sub = AB_ROWS // WINDOW
    row = lambda b, r: (b * nr + r, 0)
    return pl.pallas_call(
        _ab_prompt_kernel, grid=(BATCH, nr),
        in_specs=[pl.BlockSpec(memory_space=pltpu.SMEM),
                  pl.BlockSpec((AB_ROWS, SWA_Q_W), row),
                  pl.BlockSpec((AB_ROWS, 2 * SWA_KV_W), row),
                  pl.BlockSpec((WINDOW, 2 * SWA_KV_W), lambda b, r: (jnp.maximum((b * nr + r) * sub - 1, 0), 0)),
                  pl.BlockSpec((AB_ROWS, GMLP_WIDTH), row),
                  pl.BlockSpec((AB_ROWS, GMLP_WIDTH), row),
                  pl.BlockSpec((GMLP_GROUPS, GMLP_CHUNK, GMLP_CHUNK), lambda b, r: (0, 0, 0)),
                  pl.BlockSpec((GMLP_GROUPS, GMLP_CHUNK, 1), lambda b, r: (0, 0, 0))],
        out_specs=pl.BlockSpec((AB_ROWS, AB_OUT), row),
        out_shape=jax.ShapeDtypeStruct((M, AB_OUT), BF16),
        scratch_shapes=[pltpu.VMEM((SWA_KV_HEADS, WINDOW + AB_ROWS, LANES), BF16)] * 2,
        compiler_params=_cparams(("arbitrary", "arbitrary"), 40), name="ab_mix_prompt",
    )(sinks, q, kv, kv, u, zv, w_s, b_s.reshape(GMLP_GROUPS, GMLP_CHUNK, 1))


AB_NB = 16
AB_KEYS = 2 * WINDOW


def _ab_sample_kernel(sink_ref, q_ref, kv_ref, u_ref, zv_ref, ck_ref, cv_ref, ws_ref, bs_ref, cat_in_ref,
                      cat_ref, nk_ref, nv_ref):
    del cat_in_ref
    nb, t = AB_NB, DEC_SEQ
    kv3 = kv_ref[...].reshape(nb, t, 2 * SWA_KV_W)
    kn, vn = kv3[:, :, :SWA_KV_W], kv3[:, :, SWA_KV_W:]
    ck, cv = ck_ref[...], cv_ref[...]
    nk_ref[:, :WINDOW - t, :] = ck[:, t:, :]
    nk_ref[:, WINDOW - t:, :] = kn
    nv_ref[:, :WINDOW - t, :] = cv[:, t:, :]
    nv_ref[:, WINDOW - t:, :] = vn
    pad = jnp.zeros((nb, AB_KEYS - WINDOW - t, SWA_KV_W), F32)
    kall = jnp.concatenate([ck, kn, pad], axis=1).astype(BF16)
    vall = jnp.concatenate([cv, vn, pad], axis=1).astype(BF16)

    rows = SWA_GROUP * t
    tq = lax.broadcasted_iota(jnp.int32, (rows, AB_KEYS), 0) & (t - 1)
    sk = lax.broadcasted_iota(jnp.int32, (rows, AB_KEYS), 1)
    valid = ((sk < WINDOW) & (sk > tq)) | ((sk >= WINDOW) & (sk - WINDOW <= tq))

    q3 = q_ref[...].astype(F32).reshape(nb, t, SWA_Q_W)
    pieces = []
    for kh in range(SWA_KV_HEADS):
        heads = [kh * SWA_GROUP + g for g in range(SWA_GROUP)]
        qs = jnp.concatenate([q3[:, :, hd * SWA_HEAD_DIM:(hd + 1) * SWA_HEAD_DIM] for hd in heads],
                             axis=1).astype(BF16)
        cs = slice(kh * SWA_HEAD_DIM, (kh + 1) * SWA_HEAD_DIM)
        s = jnp.einsum('nqd,nkd->nqk', qs, kall[:, :, cs], preferred_element_type=F32)
        s = jnp.where(valid[None], s, NEG_INF)
        sink = jnp.concatenate([jnp.full((t, 1), sink_ref[hd], F32) for hd in heads], axis=0)[None]
        p, den = _sink_softmax(s, sink)
        o = jnp.einsum('nqk,nkd->nqd', p.astype(BF16), vall[:, :, cs], preferred_element_type=F32) / den
        pieces += [o[:, g * t:(g + 1) * t, :] for g in range(SWA_GROUP)]
    attn = jnp.concatenate(pieces, axis=2).reshape(nb * t, SWA_Q_W)
    cat_ref[:, :SWA_Q_W] = attn.astype(BF16)

    zv3 = zv_ref[...].reshape(nb, t, GMLP_WIDTH)
    u3 = u_ref[...].astype(F32).reshape(nb, t, GMLP_WIDTH)
    irow = lax.broadcasted_iota(jnp.int32, (t, GMLP_GROUP_DIM), 0)
    for g in range(GMLP_GROUPS):
        gs = slice(g * GMLP_GROUP_DIM, (g + 1) * GMLP_GROUP_DIM)
        zg = zv3[:, :, gs]
        mixed = jnp.zeros((nb, t, GMLP_GROUP_DIM), F32) + bs_ref[g][None]
        for j in range(t):
            wcol = jnp.where(irow >= j, ws_ref[g, j], 0.0)
            mixed = mixed + zg[:, j:j + 1, :] * wcol[None]
        gate = (u3[:, :, gs] * mixed).reshape(nb * t, GMLP_GROUP_DIM)
        cat_ref[:, SWA_Q_W + g * GMLP_GROUP_DIM:SWA_Q_W + (g + 1) * GMLP_GROUP_DIM] = gate.astype(BF16)


def _ab_mix_sample(cat, q, kv, u, zv_s, cache_k, cache_v, sinks, w_s, b_s):
    nb, t = AB_NB, DEC_SEQ
    rows = nb * t
    off = MP // rows
    srow = lambda i: (off + i, 0)
    wt = jnp.broadcast_to(jnp.swapaxes(w_s[:, :t, :t], 1, 2)[..., None], (GMLP_GROUPS, t, t, GMLP_GROUP_DIM))
    bt = jnp.broadcast_to(b_s[:, :t, None], (GMLP_GROUPS, t, GMLP_GROUP_DIM))
    cspec = pl.BlockSpec((nb, WINDOW, SWA_KV_W), lambda i: (i, 0, 0))
    cshape = jax.ShapeDtypeStruct((DEC_BATCH, WINDOW, SWA_KV_W), F32)
    return pl.pallas_call(
        _ab_sample_kernel, grid=(DEC_BATCH // nb,),
        in_specs=[pl.BlockSpec(memory_space=pltpu.SMEM),
                  pl.BlockSpec((rows, SWA_Q_W), srow),
                  pl.BlockSpec((rows, 2 * SWA_KV_W), srow),
                  pl.BlockSpec((rows, GMLP_WIDTH), srow),
                  pl.BlockSpec((rows, GMLP_WIDTH), lambda i: (i, 0)),
                  cspec, cspec,
                  pl.BlockSpec((GMLP_GROUPS, t, t, GMLP_GROUP_DIM), lambda i: (0, 0, 0, 0)),
                  pl.BlockSpec((GMLP_GROUPS, t, GMLP_GROUP_DIM), lambda i: (0, 0, 0)),
                  pl.BlockSpec(memory_space=pl.ANY)],
        out_specs=[pl.BlockSpec((rows, AB_OUT), srow), cspec, cspec],
        out_shape=[jax.ShapeDtypeStruct((M, AB_OUT), BF16), cshape, cshape],
        input_output_aliases={9: 0},
        compiler_params=_cparams(("arbitrary",), 48), name="ab_mix_sample",
    )(sinks, q, kv, u, zv_s, cache_k, cache_v, wt, bt, cat)


RET_ROWS = 1024
RET_BLOCK = 256


def _ret_prompt_part(lg, q_ref, k_ref, v_ref, g_ref, y_ref, sfin_ref, s_ref):
    c = pl.program_id(2)
    L = RET_BLOCK

    @pl.when(c == 0)
    def _():
        s_ref[...] = jnp.zeros_like(s_ref)

    diff = (lax.broadcasted_iota(jnp.int32, (L, L), 0) - lax.broadcasted_iota(jnp.int32, (L, L), 1)).astype(F32)
    dmat = jnp.where(diff >= 0, jnp.exp(lg * jnp.maximum(diff, 0.0)), 0.0)
    idx = lax.broadcasted_iota(jnp.int32, (L, 1), 0).astype(F32)
    q_dec = jnp.exp(lg * (idx + 1.0))
    k_dec = jnp.exp(lg * (L - 1.0 - idx))
    s_dec = jnp.exp(lg * jnp.full((1, 1), float(L), F32))

    def chunk(ci, carry):
        rs = pl.ds(pl.multiple_of(ci * L, L), L)
        q, k, v = q_ref[rs, :], k_ref[rs, :], v_ref[rs, :]
        att = lax.dot_general(q, k, (((1,), (1,)), ((), ())), preferred_element_type=F32) * dmat
        s0 = s_ref[...]
        o = (jnp.dot(att.astype(BF16), v, preferred_element_type=F32)
             + jnp.dot(q, s0.astype(BF16), preferred_element_type=F32) * q_dec)
        kd = (k.astype(F32) * k_dec).astype(BF16)
        s_ref[...] = s0 * s_dec + lax.dot_general(kd, v, (((0,), (0,)), ((), ())), preferred_element_type=F32)
        on = o * lax.rsqrt(jnp.mean(o * o, axis=-1, keepdims=True) + EPS)
        y_ref[rs, :] = (on * g_ref[rs, :].astype(F32)).astype(BF16)
        return carry

    lax.fori_loop(0, RET_ROWS // L, chunk, 0, unroll=2)

    @pl.when(c == pl.num_programs(2) - 1)
    def _():
        sfin_ref[...] = s_ref[...]


RET_NB = DEC_BATCH // (BATCH * (SEQ // RET_ROWS))


def _ret_sample_part(lg, q_ref, k_ref, v_ref, g_ref, s0_ref, y_ref, s1_ref):
    nb, t = RET_NB, DEC_SEQ
    rows = nb * t
    q, k, v = q_ref[...], k_ref[...], v_ref[...]
    ri = lax.broadcasted_iota(jnp.int32, (rows, rows), 0)
    ci = lax.broadcasted_iota(jnp.int32, (rows, rows), 1)
    d = ((ri & (t - 1)) - (ci & (t - 1))).astype(F32)
    same = (ri // t) == (ci // t)
    dmat = jnp.where(same & (d >= 0), jnp.exp(lg * jnp.maximum(d, 0.0)), 0.0)
    att = lax.dot_general(q, k, (((1,), (1,)), ((), ())), preferred_element_type=F32) * dmat
    o_intra = jnp.dot(att.astype(BF16), v, preferred_element_type=F32)

    rcol = lax.broadcasted_iota(jnp.int32, (rows, 1), 0)
    tt = (rcol & (t - 1)).astype(F32)
    q_dec = jnp.exp(lg * (tt + 1.0))
    k_dec = jnp.exp(lg * (t - 1.0 - tt))
    s_dec = jnp.exp(lg * jnp.full((1, 1), float(t), F32))
    kd = (k.astype(F32) * k_dec).astype(BF16)
    qf = q.astype(F32)
    vf = v.astype(F32)
    o_rows = []
    for n in range(nb):
        s0 = s0_ref[n]
        qn = qf[n * t:(n + 1) * t].astype(BF16)
        o_rows.append(jnp.dot(qn, s0.astype(BF16), preferred_element_type=F32))
        vn = jnp.where(rcol // t == n, vf, 0.0).astype(BF16)
        s1_ref[n] = s0 * s_dec + lax.dot_general(kd, vn, (((0,), (0,)), ((), ())), preferred_element_type=F32)
    o = o_intra + jnp.concatenate(o_rows, axis=0) * q_dec
    on = o * lax.rsqrt(jnp.mean(o * o, axis=-1, keepdims=True) + EPS)
    y_ref[...] = (on * g_ref[...].astype(F32)).astype(BF16)


def _retention_kernel(lg_ref, q_ref, k_ref, v_ref, g_ref, sq_ref, sk_ref, sv_ref, sg_ref, s0_ref,
                      yp_ref, sfin_ref, ys_ref, s1_ref, s_ref):
    lg = lg_ref[pl.program_id(1)]
    _ret_prompt_part(lg, q_ref, k_ref, v_ref, g_ref, yp_ref, sfin_ref, s_ref)
    _ret_sample_part(lg, sq_ref, sk_ref, sv_ref, sg_ref, s0_ref, ys_ref, s1_ref)


def _retention(zqk, zv, zg, state, lg):
    nc = SEQ // RET_ROWS
    srows = RET_NB * DEC_SEQ
    soff = MP // srows
    prow = lambda b, c: b * nc + c
    head_specs = lambda rows, row: [
        pl.BlockSpec((rows, RET_KEY_DIM), lambda b, h, c: (row(b, c), h)),
        pl.BlockSpec((rows, RET_KEY_DIM), lambda b, h, c: (row(b, c), RET_HEADS + h)),
        pl.BlockSpec((rows, RET_VAL_DIM), lambda b, h, c: (row(b, c), h)),
        pl.BlockSpec((rows, RET_VAL_DIM), lambda b, h, c: (row(b, c), h))]
    sspec = pl.BlockSpec((RET_NB, None, RET_KEY_DIM, RET_VAL_DIM), lambda b, h, c: (prow(b, c), h, 0, 0))
    return pl.pallas_call(
        _retention_kernel, grid=(BATCH, RET_HEADS, nc),
        in_specs=([pl.BlockSpec(memory_space=pltpu.SMEM)] + head_specs(RET_ROWS, prow)
                  + head_specs(srows, lambda b, c: soff + prow(b, c)) + [sspec]),
        out_specs=[pl.BlockSpec((RET_ROWS, RET_VAL_DIM), lambda b, h, c: (prow(b, c), h)),
                   pl.BlockSpec((None, None, RET_KEY_DIM, RET_VAL_DIM), lambda b, h, c: (b, h, 0, 0)),
                   pl.BlockSpec((srows, RET_VAL_DIM), lambda b, h, c: (prow(b, c), h)),
                   sspec],
        out_shape=[jax.ShapeDtypeStruct((MP, RET_V_W), BF16),
                   jax.ShapeDtypeStruct((BATCH, RET_HEADS, RET_KEY_DIM, RET_VAL_DIM), F32),
                   jax.ShapeDtypeStruct((MS, RET_V_W), BF16),
                   jax.ShapeDtypeStruct((DEC_BATCH, RET_HEADS, RET_KEY_DIM, RET_VAL_DIM), F32)],
        scratch_shapes=[pltpu.VMEM((RET_KEY_DIM, RET_VAL_DIM), F32)],
        compiler_params=_cparams(("arbitrary", "arbitrary", "arbitrary"), 56), name="retention",
    )(lg, zqk, zqk, zv, zg, zqk, zqk, zv, zg, state)


def _zero_on_first(acc_ref, k):
    @pl.when(k == 0)
    def _():
        acc_ref[...] = jnp.zeros_like(acc_ref)


def _outproj_kernel(*refs, n_prompt_tiles):
    a_refs, (w_ref, x_ref, gp_ref, gn_ref, xn_ref, hn_ref) = refs[:-6], refs[-6:]

    def emit(a_ref):
        y = jnp.dot(a_ref[...], w_ref[...], preferred_element_type=F32)
        xn, hn = _residual(y, x_ref[...], gp_ref[...], gn_ref[...])
        xn_ref[...] = xn
        hn_ref[...] = hn

    if len(a_refs) == 1:
        emit(a_refs[0])
    else:
        is_prompt = pl.program_id(0) < n_prompt_tiles
        pl.when(is_prompt)(lambda: emit(a_refs[0]))
        pl.when(jnp.logical_not(is_prompt))(lambda: emit(a_refs[1]))


def _out_proj(a_parts, w_bf, x, g_post, g_next, name):
    kdim = w_bf.shape[0]
    tm = TM_OUT * D_MODEL // kdim
    np_t = MP // tm
    row = pl.BlockSpec((tm, D_MODEL), lambda i: (i, 0))
    vec = pl.BlockSpec((1, D_MODEL), lambda i: (0, 0))
    if len(a_parts) == 1:
        a_specs = [pl.BlockSpec((tm, kdim), lambda i: (i, 0))]
    else:
        a_specs = [pl.BlockSpec((tm, kdim), lambda i: (jnp.minimum(i, np_t - 1), 0)),
                   pl.BlockSpec((tm, kdim), lambda i: (jnp.maximum(i - np_t, 0), 0))]
    return pl.pallas_call(
        functools.partial(_outproj_kernel, n_prompt_tiles=np_t), grid=(M // tm,),
        in_specs=a_specs + [pl.BlockSpec((kdim, D_MODEL), lambda i: (0, 0), pipeline_mode=pl.Buffered(1)),
                            row, vec, vec],
        out_specs=[row, row],
        out_shape=[jax.ShapeDtypeStruct((M, D_MODEL), F32), jax.ShapeDtypeStruct((M, D_MODEL), BF16)],
        compiler_params=_cparams(("arbitrary",), 48), name=name,
    )(*a_parts, w_bf, x, g_post, g_next)


def _ffn_up_down(h_ref, wu_ref, wd_ref):
    a = jnp.maximum(jnp.dot(h_ref[...], wu_ref[...], preferred_element_type=F32), 0.0)
    return jnp.dot((a * a).astype(BF16), wd_ref[...], preferred_element_type=F32)


def _ffn_kernel(h_ref, wu_ref, wd_ref, x_ref, gp_ref, gn_ref, xn_ref, hn_ref):
    c = pl.program_id(1)
    _zero_on_first(xn_ref, c)
    xn_ref[...] += _ffn_up_down(h_ref, wu_ref, wd_ref)

    @pl.when(c == pl.num_programs(1) - 1)
    def _():
        xn, hn = _residual(xn_ref[...], x_ref[...], gp_ref[...], gn_ref[...])
        xn_ref[...] = xn
        hn_ref[...] = hn


def _ffn_last_kernel(h_ref, wu_ref, wd_ref, x_ref, gp_ref, yp_ref, ys_ref, acc_ref):
    i, c = pl.program_id(0), pl.program_id(1)
    _zero_on_first(acc_ref, c)
    acc_ref[...] += _ffn_up_down(h_ref, wu_ref, wd_ref)
    last = c == pl.num_programs(1) - 1
    is_prompt = i < MP // acc_ref.shape[0]

    @pl.when(last & is_prompt)
    def _():
        yp_ref[...] = x_ref[...] + _rms(acc_ref[...], gp_ref[...])

    @pl.when(last & jnp.logical_not(is_prompt))
    def _():
        ys_ref[...] = x_ref[...] + _rms(acc_ref[...], gp_ref[...])


def _ffn(h, wu_bf, wd_bf, x, g_post, g_next):
    tm, tc = TM_OUT, 1024
    row = pl.BlockSpec((tm, D_MODEL), lambda i, c: (i, 0))
    vec = pl.BlockSpec((1, D_MODEL), lambda i, c: (0, 0))
    w_specs = [pl.BlockSpec((D_MODEL, tc), lambda i, c: (0, c)), pl.BlockSpec((tc, D_MODEL), lambda i, c: (c, 0))]
    grid = (M // tm, D_FF // tc)
    if g_next is not None:
        return pl.pallas_call(
            _ffn_kernel, grid=grid, in_specs=[row] + w_specs + [row, vec, vec], out_specs=[row, row],
            out_shape=[jax.ShapeDtypeStruct((M, D_MODEL), F32), jax.ShapeDtypeStruct((M, D_MODEL), BF16)],
            compiler_params=_cparams(("arbitrary", "arbitrary"), 56), name="ffn",
        )(h, wu_bf, wd_bf, x, g_post, g_next)
    np_t = MP // tm
    return pl.pallas_call(
        _ffn_last_kernel, grid=grid, in_specs=[row] + w_specs + [row, vec],
        out_specs=[pl.BlockSpec((tm, D_MODEL), lambda i, c: (jnp.minimum(i, np_t - 1), 0)),
                   pl.BlockSpec((tm, D_MODEL), lambda i, c: (jnp.maximum(i - np_t, 0), 0))],
        out_shape=[jax.ShapeDtypeStruct((MP, D_MODEL), F32), jax.ShapeDtypeStruct((MS, D_MODEL), F32)],
        scratch_shapes=[pltpu.VMEM((tm, D_MODEL), F32)],
        compiler_params=_cparams(("arbitrary", "arbitrary"), 56), name="ffn_last",
    )(h, wu_bf, wd_bf, x, g_post)


def _mem_kv_kernel(m_ref, g_ref, wk_ref, wv_ref, k_ref, v_ref):
    m = _rms(m_ref[...], g_ref[0]).astype(BF16)
    k_ref[0] = jnp.dot(m, wk_ref[0].astype(BF16), preferred_element_type=F32)
    v_ref[0] = jnp.dot(m, wv_ref[0].astype(BF16), preferred_element_type=F32)


def _mem_kv(mem, g, w_k, w_v):
    rows = BATCH * MEM_LEN
    wspec = pl.BlockSpec((1, D_MODEL, MEM_W), lambda l: (l, 0, 0))
    ospec = pl.BlockSpec((1, rows, MEM_W), lambda l: (l, 0, 0))
    oshape = jax.ShapeDtypeStruct((DEPTH, rows, MEM_W), F32)
    return pl.pallas_call(
        _mem_kv_kernel, grid=(DEPTH,),
        in_specs=[pl.BlockSpec((rows, D_MODEL), lambda l: (0, 0)),
                  pl.BlockSpec((1, 1, D_MODEL), lambda l: (l, 0, 0)), wspec, wspec],
        out_specs=[ospec, ospec], out_shape=[oshape, oshape],
        compiler_params=_cparams(("arbitrary",), 48), name="mem_kv",
    )(mem.reshape(rows, D_MODEL), g.reshape(DEPTH, 1, D_MODEL), w_k, w_v)


def _mem_attend_prompt(q, mk_ref, mv_ref):
    q = q.astype(BF16)
    mk, mv = mk_ref[...].astype(BF16), mv_ref[...].astype(BF16)
    outs = []
    for hh in range(MEM_HEADS):
        cs = slice(hh * MEM_HEAD_DIM, (hh + 1) * MEM_HEAD_DIM)
        s = lax.dot_general(q[:, cs], mk[:, cs], (((1,), (1,)), ((), ())),
                            preferred_element_type=F32) * (MEM_HEAD_DIM ** -0.5)
        p = jnp.exp(s - jnp.max(s, axis=-1, keepdims=True))
        den = jnp.sum(p, axis=-1, keepdims=True)
        outs.append(jnp.dot(p.astype(BF16), mv[:, cs], preferred_element_type=F32) / den)
    return jnp.concatenate(outs, axis=1)


def _mem_attend_sample(q, ck_ref, cv_ref):
    nb, t = ck_ref.shape[0], DEC_SEQ
    qs = jnp.concatenate([q[:, hh * MEM_HEAD_DIM:(hh + 1) * MEM_HEAD_DIM].reshape(nb, t, MEM_HEAD_DIM)
                          for hh in range(MEM_HEADS)], axis=1).astype(BF16)
    s = jnp.einsum('nqd,nkd->nqk', qs, ck_ref[...].astype(BF16),
                   preferred_element_type=F32) * (MEM_HEAD_DIM ** -0.5)
    shape = (MEM_HEADS * t, MEM_LEN * MEM_HEADS)
    own = ((lax.broadcasted_iota(jnp.int32, shape, 1) & (MEM_HEADS - 1))
           == lax.broadcasted_iota(jnp.int32, shape, 0) // t)
    s = jnp.where(own[None], s, NEG_INF)
    p = jnp.exp(s - jnp.max(s, axis=-1, keepdims=True))
    den = jnp.sum(p, axis=-1, keepdims=True)
    o3 = jnp.einsum('nqk,nkd->nqd', p.astype(BF16), cv_ref[...].astype(BF16),
                    preferred_element_type=F32) / den
    return jnp.concatenate([o3[:, hh * t:(hh + 1) * t, :].reshape(nb * t, MEM_HEAD_DIM)
                            for hh in range(MEM_HEADS)], axis=1)


def _mem_kernel(hp_ref, hs_ref, wq_ref, wo_ref, mk_ref, mv_ref, ck_ref, cv_ref, x_ref, gp_ref, gn_ref,
                xn_ref, hn_ref, os_ref):
    i = pl.program_id(0)
    tm = x_ref.shape[0]
    n_prompt = MP // tm
    srows = hs_ref.shape[0]
    project = lambda h_ref: jnp.dot(h_ref[...], wq_ref[...], preferred_element_type=F32)

    def finish(o):
        y = jnp.dot(o.astype(BF16), wo_ref[...], preferred_element_type=F32)
        xn, hn = _residual(y, x_ref[...], gp_ref[...], gn_ref[...])
        xn_ref[...] = xn
        hn_ref[...] = hn

    @pl.when(i < n_prompt)
    def _():
        os_ref[pl.ds(pl.multiple_of(i * srows, srows), srows), :] = _mem_attend_sample(
            project(hs_ref), ck_ref, cv_ref)
        finish(_mem_attend_prompt(project(hp_ref), mk_ref, mv_ref))

    @pl.when(i >= n_prompt)
    def _():
        finish(os_ref[pl.ds(pl.multiple_of((i - n_prompt) * tm, tm), tm), :])


MEM_TM = 256
MEM_NB = DEC_BATCH // (MP // MEM_TM)


def _mem_attention(h, x, wq_bf, wo_bf, mk, mv, cache_k, cache_v, layer, g_post, g_next):
    tm, nb = MEM_TM, MEM_NB
    srows = nb * DEC_SEQ
    soff = MP // srows
    per_b = SEQ // tm
    last_p = MP // tm - 1
    pidx = lambda i: jnp.minimum(i, last_p)
    row = pl.BlockSpec((tm, D_MODEL), lambda i: (i, 0))
    vec = pl.BlockSpec((1, D_MODEL), lambda i: (0, 0))
    kvspec = pl.BlockSpec((None, MEM_LEN, MEM_W), lambda i: (layer * BATCH + pidx(i) // per_b, 0, 0))
    cspec = pl.BlockSpec((None, nb, MEM_LEN * MEM_HEADS, MEM_HEAD_DIM), lambda i: (layer, pidx(i), 0, 0))
    resident = lambda shape: pl.BlockSpec(shape, lambda i: (layer, 0), pipeline_mode=pl.Buffered(1))
    return pl.pallas_call(
        _mem_kernel, grid=(M // tm,),
        in_specs=[pl.BlockSpec((tm, D_MODEL), lambda i: (pidx(i), 0)),
                  pl.BlockSpec((srows, D_MODEL), lambda i: (soff + pidx(i), 0)),
                  resident((D_MODEL, MEM_W)), resident((MEM_W, D_MODEL)), kvspec, kvspec,
                  cspec, cspec, row, vec, vec],
        out_specs=[row, row],
        out_shape=[jax.ShapeDtypeStruct((M, D_MODEL), F32), jax.ShapeDtypeStruct((M, D_MODEL), BF16)],
        scratch_shapes=[pltpu.VMEM((MS, MEM_W), F32)],
        compiler_params=_cparams(("arbitrary",), 48), name="mem_attention",
    )(h, h, wq_bf, wo_bf, mk, mv, cache_k, cache_v, x, g_post, g_next)


def kernel(x_prompt, x_sample, cache_swa_k, cache_swa_v, state_ret, cache_mem_k, cache_mem_v, mem_prompt, norm_mix_pre, norm_mix_post, norm_mem, norm_x_pre, norm_x_post, norm_ffn_pre, norm_ffn_post, w_ab_in, w_ab_out, swa_sinks, gmlp_ln_g, gmlp_ln_b, gmlp_w_s, gmlp_b_s, w_ret_in, w_ret_out, w_mem_q, w_mem_k, w_mem_v, w_mem_o, w_ffn_up, w_ffn_down):
    vec = lambda g, layer: g[layer].reshape(1, D_MODEL)
    swa_c, swa_s1, swa_s2, ret_c, ret_s = _rope_tables()
    mem_k, mem_v = _mem_kv(mem_prompt, norm_mem, w_mem_k, w_mem_v)
    mem_k_b = mem_k.reshape(DEPTH * BATCH, MEM_LEN, MEM_W)
    mem_v_b = mem_v.reshape(DEPTH * BATCH, MEM_LEN, MEM_W)
    ck_mem = cache_mem_k.reshape(DEPTH, DEC_BATCH, MEM_LEN * MEM_HEADS, MEM_HEAD_DIM)
    cv_mem = cache_mem_v.reshape(DEPTH, DEC_BATCH, MEM_LEN * MEM_HEADS, MEM_HEAD_DIM)
    lg = jnp.log1p(-jnp.exp2(-5.0 - jnp.arange(RET_HEADS, dtype=F32)))

    x, h = _stack_and_prenorm(x_prompt, x_sample, vec(norm_mix_pre, 0))
    outs = {}
    for layer in range(DEPTH):
        j = layer // 2
        if layer % 2 == 0:
            (q, kv, u, zv, zv_s), (wu_bf, wd_bf, wout_bf, wq_bf, wo_bf) = _ab_in_proj(
                h, w_ab_in[j], gmlp_ln_g[j].reshape(1, GMLP_WIDTH), gmlp_ln_b[j].reshape(1, GMLP_WIDTH),
                (swa_c, swa_s1, swa_s2), w_ffn_up, w_ffn_down, w_ab_out, w_mem_q, w_mem_o, layer, j)
            cat = _ab_mix_prompt(q, kv, u, zv, swa_sinks[j], gmlp_w_s[j], gmlp_b_s[j])
            cat, nk_s, nv_s = _ab_mix_sample(
                cat, q, kv, u, zv_s, cache_swa_k[j].reshape(DEC_BATCH, WINDOW, SWA_KV_W),
                cache_swa_v[j].reshape(DEC_BATCH, WINDOW, SWA_KV_W), swa_sinks[j], gmlp_w_s[j], gmlp_b_s[j])
            kv_p = jnp.stack([kv[(b + 1) * SEQ - WINDOW:(b + 1) * SEQ] for b in range(BATCH)])
            kv_p = kv_p.reshape(BATCH, WINDOW, 2, SWA_KV_HEADS, SWA_HEAD_DIM)
            outs.setdefault('swa_k_p', []).append(kv_p[:, :, 0])
            outs.setdefault('swa_v_p', []).append(kv_p[:, :, 1])
            outs.setdefault('swa_k_s', []).append(nk_s.reshape(DEC_BATCH, WINDOW, SWA_KV_HEADS, SWA_HEAD_DIM))
            outs.setdefault('swa_v_s', []).append(nv_s.reshape(DEC_BATCH, WINDOW, SWA_KV_HEADS, SWA_HEAD_DIM))
            outs.setdefault('gmlp_v_s', []).append(zv_s.reshape(DEC_BATCH, DEC_SEQ, GMLP_WIDTH))
            mixed, mix_name = (cat,), "ab_out"
        else:
            (zqk, zv, zg), (wu_bf, wd_bf, wout_bf) = _ret_in_proj(
                h, w_ret_in[j], (ret_c, ret_s), w_ffn_up, w_ffn_down, w_ret_out, layer, j)
            y_p, s_p, y_s, s_s = _retention(zqk, zv, zg, state_ret[j], lg)
            outs.setdefault('ret_p', []).append(s_p)
            outs.setdefault('ret_s', []).append(s_s)
            mixed, mix_name = (y_p, y_s), "ret_out"
        x, h = _out_proj(mixed, wout_bf, x, vec(norm_mix_post, layer), vec(norm_x_pre, layer), mix_name)
        g_post = vec(norm_x_post, layer)
        g_next = vec(norm_ffn_pre, layer)
        x, h = _mem_attention(h, x, wq_bf, wo_bf, mem_k_b, mem_v_b, ck_mem, cv_mem, layer, g_post, g_next)
        g_next = vec(norm_mix_pre, layer + 1) if layer + 1 < DEPTH else None
        x, h = _ffn(h, wu_bf, wd_bf, x, vec(norm_ffn_post, layer), g_next)

    y_prompt, y_sample = x, h
    stack = lambda name: jnp.stack(outs[name])
    mem_shape = (DEPTH, BATCH, MEM_LEN, MEM_HEADS, MEM_HEAD_DIM)
    return (y_prompt.reshape(BATCH, SEQ, D_MODEL), y_sample.reshape(DEC_BATCH, DEC_SEQ, D_MODEL),
            stack('swa_k_p'), stack('swa_v_p'), stack('swa_k_s'), stack('swa_v_s'), stack('gmlp_v_s'),
            stack('ret_p'), stack('ret_s'), mem_k.reshape(mem_shape), mem_v.reshape(mem_shape))
```

```python
import functools
import math

import jax
import jax.numpy as jnp
from jax import lax
from jax.experimental import pallas as pl
from jax.experimental.pallas import tpu as pltpu

F32 = jnp.float32
BF16 = jnp.bfloat16

D_MODEL = 2048
BATCH = 2
SEQ = 4096
DEPTH = 2
DEC_BATCH = 128
DEC_SEQ = 8
PAST_LEN = 8192

SWA_HEADS = 16
SWA_KV_HEADS = 4
SWA_HEAD_DIM = 64
SWA_GROUP = SWA_HEADS // SWA_KV_HEADS
WINDOW = 128
ROPE_THETA = 500000.0
ROPE_DIM = SWA_HEAD_DIM // 4
ROPE_HALF = ROPE_DIM // 2

GMLP_GROUPS = 4
GMLP_WIDTH = 1024
GMLP_GROUP_DIM = GMLP_WIDTH // GMLP_GROUPS
GMLP_CHUNK = 128

SWA_Q_W = SWA_HEADS * SWA_HEAD_DIM
SWA_KV_W = SWA_KV_HEADS * SWA_HEAD_DIM
AB_IN = SWA_Q_W + 2 * SWA_KV_W + 2 * GMLP_WIDTH
AB_OUT = SWA_Q_W + GMLP_WIDTH

RET_HEADS = 8
RET_KEY_DIM = D_MODEL // RET_HEADS
RET_VAL_DIM = 2 * RET_KEY_DIM
RET_CHUNK = 128
RET_THETA = 10000.0
RET_QK_W = RET_HEADS * RET_KEY_DIM
RET_V_W = RET_HEADS * RET_VAL_DIM
RET_IN = 2 * RET_QK_W + 2 * RET_V_W

MEM_LEN = 256
MEM_HEADS = 4
MEM_HEAD_DIM = 128
MEM_W = MEM_HEADS * MEM_HEAD_DIM

D_FF = 4 * D_MODEL
EPS = 1e-6
NEG_INF = -1e30

MP = BATCH * SEQ
MS = DEC_BATCH * DEC_SEQ
M = MP + MS
LANES = 128
MIB = 1024 * 1024

TM_IN = 1024
TAB_ROWS = SEQ + MS
TM_OUT = 512


def _cparams(sem, vmem_mib):
    return pltpu.CompilerParams(dimension_semantics=sem, vmem_limit_bytes=vmem_mib * MIB)


def _rms(x, g):
    return x * lax.rsqrt(jnp.mean(x * x, axis=-1, keepdims=True) + EPS) * g


def _residual(y, x, g_post, g_next):
    xn = x + _rms(y, g_post)
    return xn, _rms(xn, g_next).astype(BF16)


def _rope_table_kernel(sc_ref, s1_ref, s2_ref, rc_ref, rs_ref):
    tr = sc_ref.shape[0]
    r = lax.broadcasted_iota(jnp.int32, (tr, LANES), 0) + pl.program_id(0) * tr
    pos = jnp.where(r < SEQ, r, PAST_LEN + ((r - SEQ) & (DEC_SEQ - 1))).astype(F32)
    lane = lax.broadcasted_iota(jnp.int32, (tr, LANES), 1)
    d = lane & (SWA_HEAD_DIM - 1)
    inv = jnp.exp((d & (ROPE_HALF - 1)).astype(F32) * (-math.log(ROPE_THETA) / ROPE_HALF))
    ang = pos * inv
    c, s = jnp.cos(ang), jnp.sin(ang)
    sc_ref[...] = jnp.where(d < ROPE_DIM, c, 1.0)
    s1_ref[...] = jnp.where(d < ROPE_HALF, -s, 0.0)
    s2_ref[...] = jnp.where((d >= ROPE_HALF) & (d < ROPE_DIM), s, 0.0)
    inv_r = jnp.exp(lane.astype(F32) * (-math.log(RET_THETA) / LANES))
    ang_r = pos * inv_r
    rc_ref[...] = jnp.cos(ang_r)
    rs_ref[...] = jnp.sin(ang_r)


def _rope_tables():
    tr = 512
    spec = pl.BlockSpec((tr, LANES), lambda i: (i, 0))
    shp = jax.ShapeDtypeStruct((TAB_ROWS, LANES), F32)
    return pl.pallas_call(
        _rope_table_kernel, grid=(TAB_ROWS // tr,), out_specs=[spec] * 5, out_shape=[shp] * 5,
        compiler_params=_cparams(("arbitrary",), 16), name="rope_tables")()


def _norm_kernel(xp_ref, xs_ref, g_ref, h_ref):
    def emit(src_ref):
        h_ref[...] = _rms(src_ref[...], g_ref[...]).astype(BF16)

    is_prompt = pl.program_id(0) < MP // h_ref.shape[0]
    pl.when(is_prompt)(lambda: emit(xp_ref))
    pl.when(jnp.logical_not(is_prompt))(lambda: emit(xs_ref))


def _stacked_prenorm(x_prompt, x_sample, g):
    tm = TM_OUT
    return pl.pallas_call(
        _norm_kernel, grid=(M // tm,),
        in_specs=[pl.BlockSpec((tm, D_MODEL), lambda i: (jnp.minimum(i, MP // tm - 1), 0)),
                  pl.BlockSpec((tm, D_MODEL), lambda i: (jnp.maximum(i - MP // tm, 0), 0)),
                  pl.BlockSpec((1, D_MODEL), lambda i: (0, 0))],
        out_specs=pl.BlockSpec((tm, D_MODEL), lambda i: (i, 0)),
        out_shape=jax.ShapeDtypeStruct((M, D_MODEL), BF16),
        compiler_params=_cparams(("arbitrary",), 40), name="prenorm",
    )(x_prompt, x_sample, g)


W_SUB = 512


def _inproj_kernel(h_ref, *refs, n_sub, n_side, n_cast, epilogue):
    w_refs, refs = refs[:n_sub], refs[n_sub:]
    side, refs = refs[:n_side], refs[n_side:]
    cast_src, refs = refs[:n_cast], refs[n_cast:]
    wbf_ref = refs[-1]
    outs, cast_dst = refs[:len(refs) - 1 - n_cast], refs[len(refs) - 1 - n_cast:-1]

    @pl.when(pl.program_id(1) == 0)
    def _():
        for s, w_ref in enumerate(w_refs):
            wbf_ref[:, s * W_SUB:(s + 1) * W_SUB] = w_ref[...].astype(BF16)

    acc = jnp.dot(h_ref[...], wbf_ref[...], preferred_element_type=F32)
    epilogue(acc, side, outs)
    for src, dst in zip(cast_src, cast_dst):
        dst[...] = src[...].astype(BF16)


def _tab_index(j, i):
    return (jnp.where(i < MP // TM_IN, i % (SEQ // TM_IN), SEQ // TM_IN), 0)


CAST_STEPS = MP // TM_IN


def _cast_job(w, layer, nrb, ncb, block_of):
    rows, cols = w.shape[1:]
    blk = (rows // nrb, cols // ncb)
    index = lambda j, i: block_of(j, jnp.minimum(i, CAST_STEPS - 1))
    return (w, pl.BlockSpec((None,) + blk, lambda j, i: (layer,) + tuple(index(j, i))),
            jax.ShapeDtypeStruct((rows, cols), BF16), pl.BlockSpec(blk, index))


def _inproj(h, w, col0, ncols, tn, epilogue, side, side_specs, out_shapes, out_specs, name, vmem_mib,
            casts=()):
    tm = TM_IN
    k = h.shape[1]
    n_sub = tn // W_SUB
    w_specs = [pl.BlockSpec((k, W_SUB), lambda j, i, s=s: (0, col0 // W_SUB + j * n_sub + s))
               for s in range(n_sub)]
    as_list = lambda v: list(v) if isinstance(v, (list, tuple)) else [v]
    outs = pl.pallas_call(
        functools.partial(_inproj_kernel, n_sub=n_sub, n_side=len(side), n_cast=len(casts), epilogue=epilogue),
        grid=(ncols // tn, M // tm),
        in_specs=([pl.BlockSpec((tm, k), lambda j, i: (i, 0))] + w_specs + side_specs
                  + [c[1] for c in casts]),
        out_specs=as_list(out_specs) + [c[3] for c in casts],
        out_shape=as_list(out_shapes) + [c[2] for c in casts],
        scratch_shapes=[pltpu.VMEM((k, tn), BF16)],
        compiler_params=_cparams(("arbitrary", "arbitrary"), vmem_mib), name=name,
    )(h, *([w] * n_sub), *side, *[c[0] for c in casts])
    n_main = len(outs) - len(casts)
    main = outs[0] if n_main == 1 else tuple(outs[:n_main])
    return (main, tuple(outs[n_main:])) if casts else main


def _swa_rope(x, c, s1, s2):
    return x * c + pltpu.roll(x, LANES - ROPE_HALF, 1) * s1 + pltpu.roll(x, ROPE_HALF, 1) * s2


def _epi_swa_q(acc, side, outs):
    c, s1, s2 = (t[...] for t in side)
    (q_ref,) = outs
    for cc in range(acc.shape[1] // LANES):
        sl = slice(cc * LANES, (cc + 1) * LANES)
        q_ref[:, sl] = (_swa_rope(acc[:, sl], c, s1, s2) * (SWA_HEAD_DIM ** -0.5)).astype(BF16)


def _epi_swa_kv(acc, side, outs):
    c, s1, s2 = (t[...] for t in side)
    (kv_ref,) = outs
    for cc in range(SWA_KV_W // LANES):
        sl = slice(cc * LANES, (cc + 1) * LANES)
        kv_ref[:, sl] = _swa_rope(acc[:, sl], c, s1, s2)
    kv_ref[:, SWA_KV_W:] = acc[:, SWA_KV_W:]


def _epi_gelu(acc, side, outs):
    outs[0][...] = jax.nn.gelu(acc).astype(BF16)


def _epi_gelu_ln(acc, side, outs):
    g_ref, b_ref = side
    zv_ref, zvs_ref = outs
    a = jax.nn.gelu(acc)
    xc = a - jnp.mean(a, axis=-1, keepdims=True)
    y = xc * lax.rsqrt(jnp.mean(xc * xc, axis=-1, keepdims=True) + EPS) * g_ref[...] + b_ref[...]
    zv_ref[...] = y.astype(BF16)

    @pl.when(pl.program_id(1) == MP // TM_IN)
    def _():
        zvs_ref[...] = y


def _epi_ret_qk(acc, side, outs):
    cos, sin = (t[...] for t in side)
    (z_ref,) = outs
    tn = acc.shape[1]
    scale = jnp.where(pl.program_id(0) < RET_QK_W // tn, 1.0, RET_KEY_DIM ** -0.5)
    for hh in range(tn // RET_KEY_DIM):
        c0 = hh * RET_KEY_DIM
        x1, x2 = acc[:, c0:c0 + LANES], acc[:, c0 + LANES:c0 + 2 * LANES]
        z_ref[:, c0:c0 + LANES] = ((x1 * cos - x2 * sin) * scale).astype(BF16)
        z_ref[:, c0 + LANES:c0 + 2 * LANES] = ((x2 * cos + x1 * sin) * scale).astype(BF16)


def _epi_cast(acc, side, outs):
    outs[0][...] = acc.astype(BF16)


def _epi_silu(acc, side, outs):
    outs[0][...] = (acc / (1.0 + jnp.exp(-acc))).astype(BF16)


def _ab_in_proj(h, w_in, ln_g, ln_b, tabs, w_up, w_down, w_out, w_mem_q, w_mem_o, layer, j):
    sc, s1, s2 = tabs
    tab_spec = pl.BlockSpec((TM_IN, LANES), _tab_index)
    row_out = lambda tn: pl.BlockSpec((TM_IN, tn), lambda j, i: (i, j))
    half = W_SUB
    steps = CAST_STEPS
    q, (wu_bf,) = _inproj(
        h, w_in, 0, SWA_Q_W, half, _epi_swa_q, [sc, s1, s2], [tab_spec] * 3,
        jax.ShapeDtypeStruct((M, SWA_Q_W), BF16), row_out(half), "ab_in_q", 52,
        casts=[_cast_job(w_up, layer, 1, 2 * steps, lambda j, i: (0, j * steps + i))])
    kv, (wout_bf,) = _inproj(
        h, w_in, SWA_Q_W, 2 * SWA_KV_W, half, _epi_swa_kv, [sc, s1, s2], [tab_spec] * 3,
        jax.ShapeDtypeStruct((M, 2 * SWA_KV_W), F32), row_out(half), "ab_in_kv", 40,
        casts=[_cast_job(w_out, j, steps, 1, lambda j, i: (i, 0))])
    u, (wd_bf,) = _inproj(
        h, w_in, SWA_Q_W + 2 * SWA_KV_W, GMLP_WIDTH, half, _epi_gelu, [], [],
        jax.ShapeDtypeStruct((M, GMLP_WIDTH), BF16), row_out(half), "ab_in_u", 52,
        casts=[_cast_job(w_down, layer, 2 * steps, 1, lambda j, i: (j * steps + i, 0))])
    vec_spec = pl.BlockSpec((1, GMLP_WIDTH), lambda j, i: (0, 0))
    stacked = lambda w: w.reshape(1, w.shape[0] * w.shape[1], w.shape[2])
    (zv, zv_s), (wq_bf, wo_bf) = _inproj(
        h, w_in, AB_IN - GMLP_WIDTH, GMLP_WIDTH, GMLP_WIDTH, _epi_gelu_ln,
        [ln_g, ln_b], [vec_spec] * 2,
        [jax.ShapeDtypeStruct((M, GMLP_WIDTH), BF16), jax.ShapeDtypeStruct((MS, GMLP_WIDTH), F32)],
        [row_out(GMLP_WIDTH), pl.BlockSpec((MS, GMLP_WIDTH), lambda j, i: (0, 0))], "ab_in_zv", 52,
        casts=[_cast_job(stacked(w_mem_q), 0, steps, 1, lambda j, i: (i, 0)),
               _cast_job(stacked(w_mem_o), 0, steps, 1, lambda j, i: (i, 0))])
    return (q, kv, u, zv, zv_s), (wu_bf, wd_bf, wout_bf, wq_bf, wo_bf)


def _ret_in_proj(h, w_in, tabs, w_up, w_down, w_out, layer, j):
    rc, rs = tabs
    tn = 1024
    tab_spec = pl.BlockSpec((TM_IN, LANES), _tab_index)
    out_spec = pl.BlockSpec((TM_IN, tn), lambda j, i: (i, j))
    by_tile = lambda j, i: (j, i)
    seg = lambda col0, ncols, epi, side, specs, name, w, lead: _inproj(
        h, w_in, col0, ncols, tn, epi, side, specs, jax.ShapeDtypeStruct((M, ncols), BF16), out_spec, name, 56,
        casts=[_cast_job(w, lead, ncols // tn, CAST_STEPS, by_tile)])
    zqk, (wu_bf,) = seg(0, 2 * RET_QK_W, _epi_ret_qk, [rc, rs], [tab_spec] * 2, "ret_in_qk", w_up, layer)
    zv, (wd_bf,) = seg(2 * RET_QK_W, RET_V_W, _epi_cast, [], [], "ret_in_v", w_down, layer)
    zg, (wout_bf,) = seg(2 * RET_QK_W + RET_V_W, RET_V_W, _epi_silu, [], [], "ret_in_g", w_out, j)
    return (zqk, zv, zg), (wu_bf, wd_bf, wout_bf)


AB_ROWS = 512


def _sink_softmax(s, sink):
    m = jnp.maximum(jnp.max(s, axis=-1, keepdims=True), sink)
    p = jnp.exp(s - m)
    den = jnp.sum(p, axis=-1, keepdims=True) + jnp.exp(sink - m)
    return p, den


def _ab_prompt_kernel(sink_ref, q_ref, kv_ref, kvp_ref, u_ref, zv_ref, ws_ref, bs_ref, cat_ref,
                      kd_ref, vd_ref):
    r = pl.program_id(1)
    low = lax.broadcasted_iota(jnp.int32, (1, LANES), 1) < SWA_HEAD_DIM

    def both_halves(x, half):
        sw = pltpu.roll(x, SWA_HEAD_DIM, 1)
        return (jnp.where(low, x, sw) if half == 0 else jnp.where(low, sw, x)).astype(BF16)

    for kh in range(SWA_KV_HEADS):
        kcol = slice((kh // 2) * LANES, (kh // 2 + 1) * LANES)
        vcol = slice(SWA_KV_W + (kh // 2) * LANES, SWA_KV_W + (kh // 2 + 1) * LANES)
        kd_ref[kh, :WINDOW, :] = both_halves(kvp_ref[:, kcol], kh % 2)
        kd_ref[kh, WINDOW:, :] = both_halves(kv_ref[:, kcol], kh % 2)
        vd_ref[kh, :WINDOW, :] = both_halves(kvp_ref[:, vcol], kh % 2)
        vd_ref[kh, WINDOW:, :] = both_halves(kv_ref[:, vcol], kh % 2)

    t = lax.broadcasted_iota(jnp.int32, (WINDOW, 2 * WINDOW), 0)
    sk = lax.broadcasted_iota(jnp.int32, (WINDOW, 2 * WINDOW), 1)
    band = (sk <= t + WINDOW) & (sk > t)
    sink_slot = sk == t
    tril = (lax.broadcasted_iota(jnp.int32, (GMLP_CHUNK, GMLP_CHUNK), 0)
            >= lax.broadcasted_iota(jnp.int32, (GMLP_CHUNK, GMLP_CHUNK), 1))

    for sb in range(AB_ROWS // WINDOW):
        rs = slice(sb * WINDOW, (sb + 1) * WINDOW)
        ks = slice(sb * WINDOW, sb * WINDOW + 2 * WINDOW)
        valid = band & ((sk >= WINDOW) | (r * (AB_ROWS // WINDOW) + sb > 0))
        for kh in range(SWA_KV_HEADS):
            qm = []
            for g in range(SWA_GROUP):
                hd = kh * SWA_GROUP + g
                qb = q_ref[rs, (hd // 2) * LANES:(hd // 2 + 1) * LANES]
                qm.append(jnp.where(low if hd % 2 == 0 else jnp.logical_not(low), qb, jnp.zeros_like(qb)))
            s = lax.dot_general(jnp.concatenate(qm, axis=0), kd_ref[kh, ks, :], (((1,), (1,)), ((), ())),
                                preferred_element_type=F32)
            pv, den = [], []
            for g in range(SWA_GROUP):
                sg = s[g * WINDOW:(g + 1) * WINDOW]
                sg = jnp.where(valid, sg, jnp.where(sink_slot, sink_ref[kh * SWA_GROUP + g], NEG_INF))
                p = jnp.exp(sg - jnp.max(sg, axis=-1, keepdims=True))
                den.append(jnp.sum(p, axis=-1, keepdims=True))
                pv.append(jnp.where(sink_slot, 0.0, p).astype(BF16))
            o = jnp.dot(jnp.concatenate(pv, axis=0), vd_ref[kh, ks, :], preferred_element_type=F32)
            for pr in range(SWA_GROUP // 2):
                lo = o[(2 * pr) * WINDOW:(2 * pr + 1) * WINDOW] / den[2 * pr]
                hi = o[(2 * pr + 1) * WINDOW:(2 * pr + 2) * WINDOW] / den[2 * pr + 1]
                c0 = (kh * (SWA_GROUP // 2) + pr) * LANES
                cat_ref[rs, c0:c0 + LANES] = jnp.where(low, lo, hi).astype(BF16)
        zb = zv_ref[rs, :]
        ub = u_ref[rs, :]
        for g in range(GMLP_GROUPS):
            gs = slice(g * GMLP_GROUP_DIM, (g + 1) * GMLP_GROUP_DIM)
            w = jnp.where(tril, ws_ref[g], 0.0).astype(BF16)
            mixed = jnp.dot(w, zb[:, gs], preferred_element_type=F32) + bs_ref[g]
            cat_ref[rs, SWA_Q_W + g * GMLP_GROUP_DIM:SWA_Q_W + (g + 1) * GMLP_GROUP_DIM] = (
                ub[:, gs].astype(F32) * mixed).astype(BF16)


def _ab_mix_prompt(q, kv, u, zv, sinks, w_s, b_s):
    nr = SEQ // AB_ROWS
    sub = AB_ROWS // WINDOW
    row = lambda b, r: (b * nr + r, 0)
    return pl.pallas_call(
        _ab_prompt_kernel, grid=(BATCH, nr),
        in_specs=[pl.BlockSpec(memory_space=pltpu.SMEM),
                  pl.BlockSpec((AB_ROWS, SWA_Q_W), row),
                  pl.BlockSpec((AB_ROWS, 2 * SWA_KV_W), row),
                  pl.BlockSpec((WINDOW, 2 * SWA_KV_W), lambda b, r: (jnp.maximum((b * nr + r) * sub - 1, 0), 0)),
                  pl.BlockSpec((AB_ROWS, GMLP_WIDTH), row),
                  pl.BlockSpec((AB_ROWS, GMLP_WIDTH), row),
                  pl.BlockSpec((GMLP_GROUPS, GMLP_CHUNK, GMLP_CHUNK), lambda b, r: (0, 0, 0)),
                  pl.BlockSpec((GMLP_GROUPS, GMLP_CHUNK, 1), lambda b, r: (0, 0, 0))],
        out_specs=pl.BlockSpec((AB_ROWS, AB_OUT), row),
        out_shape=jax.ShapeDtypeStruct((M, AB_OUT), BF16),
        scratch_shapes=[pltpu.VMEM((SWA_KV_HEADS, WINDOW + AB_ROWS, LANES), BF16)] * 2,
        compiler_params=_cparams(("arbitrary", "arbitrary"), 40), name="ab_mix_prompt",
    )(sinks, q, kv, kv, u, zv, w_s, b_s.reshape(GMLP_GROUPS, GMLP_CHUNK, 1))


AB_NB = 16
AB_KEYS = 2 * WINDOW


def _ab_sample_kernel(sink_ref, q_ref, kv_ref, u_ref, zv_ref, ck_ref, cv_ref, ws_ref, bs_ref, cat_in_ref,
                      cat_ref, nk_ref, nv_ref):
    del cat_in_ref
    nb, t = AB_NB, DEC_SEQ
    kv3 = kv_ref[...].reshape(nb, t, 2 * SWA_KV_W)
    kn, vn = kv3[:, :, :SWA_KV_W], kv3[:, :, SWA_KV_W:]
    ck, cv = ck_ref[...], cv_ref[...]
    nk_ref[:, :WINDOW - t, :] = ck[:, t:, :]
    nk_ref[:, WINDOW - t:, :] = kn
    nv_ref[:, :WINDOW - t, :] = cv[:, t:, :]
    nv_ref[:, WINDOW - t:, :] = vn
    pad = jnp.zeros((nb, AB_KEYS - WINDOW - t, SWA_KV_W), F32)
    kall = jnp.concatenate([ck, kn, pad], axis=1).astype(BF16)
    vall = jnp.concatenate([cv, vn, pad], axis=1).astype(BF16)

    rows = SWA_GROUP * t
    tq = lax.broadcasted_iota(jnp.int32, (rows, AB_KEYS), 0) & (t - 1)
    sk = lax.broadcasted_iota(jnp.int32, (rows, AB_KEYS), 1)
    valid = ((sk < WINDOW) & (sk > tq)) | ((sk >= WINDOW) & (sk - WINDOW <= tq))

    q3 = q_ref[...].astype(F32).reshape(nb, t, SWA_Q_W)
    pieces = []
    for kh in range(SWA_KV_HEADS):
        heads = [kh * SWA_GROUP + g for g in range(SWA_GROUP)]
        qs = jnp.concatenate([q3[:, :, hd * SWA_HEAD_DIM:(hd + 1) * SWA_HEAD_DIM] for hd in heads],
                             axis=1).astype(BF16)
        cs = slice(kh * SWA_HEAD_DIM, (kh + 1) * SWA_HEAD_DIM)
        s = jnp.einsum('nqd,nkd->nqk', qs, kall[:, :, cs], preferred_element_type=F32)
        s = jnp.where(valid[None], s, NEG_INF)
        sink = jnp.concatenate([jnp.full((t, 1), sink_ref[hd], F32) for hd in heads], axis=0)[None]
        p, den = _sink_softmax(s, sink)
        o = jnp.einsum('nqk,nkd->nqd', p.astype(BF16), vall[:, :, cs], preferred_element_type=F32) / den
        pieces += [o[:, g * t:(g + 1) * t, :] for g in range(SWA_GROUP)]
    attn = jnp.concatenate(pieces, axis=2).reshape(nb * t, SWA_Q_W)
    cat_ref[:, :SWA_Q_W] = attn.astype(BF16)

    zv3 = zv_ref[...].reshape(nb, t, GMLP_WIDTH)
    u3 = u_ref[...].astype(F32).reshape(nb, t, GMLP_WIDTH)
    irow = lax.broadcasted_iota(jnp.int32, (t, GMLP_GROUP_DIM), 0)
    for g in range(GMLP_GROUPS):
        gs = slice(g * GMLP_GROUP_DIM, (g + 1) * GMLP_GROUP_DIM)
        zg = zv3[:, :, gs]
        mixed = jnp.zeros((nb, t, GMLP_GROUP_DIM), F32) + bs_ref[g][None]
        for j in range(t):
            wcol = jnp.where(irow >= j, ws_ref[g, j], 0.0)
            mixed = mixed + zg[:, j:j + 1, :] * wcol[None]
        gate = (u3[:, :, gs] * mixed).reshape(nb * t, GMLP_GROUP_DIM)
        cat_ref[:, SWA_Q_W + g * GMLP_GROUP_DIM:SWA_Q_W + (g + 1) * GMLP_GROUP_DIM] = gate.astype(BF16)


def _ab_mix_sample(cat, q, kv, u, zv_s, cache_k, cache_v, sinks, w_s, b_s):
    nb, t = AB_NB, DEC_SEQ
    rows = nb * t
    off = MP // rows
    srow = lambda i: (off + i, 0)
    wt = jnp.broadcast_to(jnp.swapaxes(w_s[:, :t, :t], 1, 2)[..., None], (GMLP_GROUPS, t, t, GMLP_GROUP_DIM))
    bt = jnp.broadcast_to(b_s[:, :t, None], (GMLP_GROUPS, t, GMLP_GROUP_DIM))
    cspec = pl.BlockSpec((nb, WINDOW, SWA_KV_W), lambda i: (i, 0, 0))
    cshape = jax.ShapeDtypeStruct((DEC_BATCH, WINDOW, SWA_KV_W), F32)
    return pl.pallas_call(
        _ab_sample_kernel, grid=(DEC_BATCH // nb,),
        in_specs=[pl.BlockSpec(memory_space=pltpu.SMEM),
                  pl.BlockSpec((rows, SWA_Q_W), srow),
                  pl.BlockSpec((rows, 2 * SWA_KV_W), srow),
                  pl.BlockSpec((rows, GMLP_WIDTH), srow),
                  pl.BlockSpec((rows, GMLP_WIDTH), lambda i: (i, 0)),
                  cspec, cspec,
                  pl.BlockSpec((GMLP_GROUPS, t, t, GMLP_GROUP_DIM), lambda i: (0, 0, 0, 0)),
                  pl.BlockSpec((GMLP_GROUPS, t, GMLP_GROUP_DIM), lambda i: (0, 0, 0)),
                  pl.BlockSpec(memory_space=pl.ANY)],
        out_specs=[pl.BlockSpec((rows, AB_OUT), srow), cspec, cspec],
        out_shape=[jax.ShapeDtypeStruct((M, AB_OUT), BF16), cshape, cshape],
        input_output_aliases={9: 0},
        compiler_params=_cparams(("arbitrary",), 48), name="ab_mix_sample",
    )(sinks, q, kv, u, zv_s, cache_k, cache_v, wt, bt, cat)


RET_ROWS = 512
RET_BLOCK = 256


def _ret_prompt_part(lg, first_run, q_ref, k_ref, v_ref, g_ref, s_ref):
    L = RET_BLOCK

    @pl.when(first_run)
    def _():
        s_ref[...] = jnp.zeros_like(s_ref)

    diff = (lax.broadcasted_iota(jnp.int32, (L, L), 0) - lax.broadcasted_iota(jnp.int32, (L, L), 1)).astype(F32)
    dmat = jnp.where(diff >= 0, jnp.exp(lg * jnp.maximum(diff, 0.0)), 0.0)
    idx = lax.broadcasted_iota(jnp.int32, (L, 1), 0).astype(F32)
    q_dec = jnp.exp(lg * (idx + 1.0))
    k_dec = jnp.exp(lg * (L - 1.0 - idx))
    s_dec = jnp.exp(lg * jnp.full((1, 1), float(L), F32))

    ys = []
    for ci in range(RET_ROWS // L):
        rs = slice(ci * L, (ci + 1) * L)
        q, k, v = q_ref[rs, :], k_ref[rs, :], v_ref[rs, :]
        att = lax.dot_general(q, k, (((1,), (1,)), ((), ())), preferred_element_type=F32) * dmat
        s0 = s_ref[...]
        o = (jnp.dot(att.astype(BF16), v, preferred_element_type=F32)
             + jnp.dot(q, s0.astype(BF16), preferred_element_type=F32) * q_dec)
        kd = (k.astype(F32) * k_dec).astype(BF16)
        s_ref[...] = s0 * s_dec + lax.dot_general(kd, v, (((0,), (0,)), ((), ())), preferred_element_type=F32)
        on = o * lax.rsqrt(jnp.mean(o * o, axis=-1, keepdims=True) + EPS)
        ys.append((on * g_ref[rs, :].astype(F32)).astype(BF16))
    return jnp.concatenate(ys, axis=0)


RET_NB = DEC_BATCH // (BATCH * (SEQ // RET_ROWS))


def _ret_sample_part(lg, q_ref, k_ref, v_ref, g_ref, s0_ref, s1_ref):
    nb, t = RET_NB, DEC_SEQ
    rows = nb * t
    q, k, v = q_ref[...], k_ref[...], v_ref[...]
    ri = lax.broadcasted_iota(jnp.int32, (rows, rows), 0)
    ci = lax.broadcasted_iota(jnp.int32, (rows, rows), 1)
    d = ((ri & (t - 1)) - (ci & (t - 1))).astype(F32)
    same = (ri // t) == (ci // t)
    dmat = jnp.where(same & (d >= 0), jnp.exp(lg * jnp.maximum(d, 0.0)), 0.0)
    att = lax.dot_general(q, k, (((1,), (1,)), ((), ())), preferred_element_type=F32) * dmat
    o_intra = jnp.dot(att.astype(BF16), v, preferred_element_type=F32)

    rcol = lax.broadcasted_iota(jnp.int32, (rows, 1), 0)
    tt = (rcol & (t - 1)).astype(F32)
    q_dec = jnp.exp(lg * (tt + 1.0))
    k_dec = jnp.exp(lg * (t - 1.0 - tt))
    s_dec = jnp.exp(lg * jnp.full((1, 1), float(t), F32))
    kd = (k.astype(F32) * k_dec).astype(BF16)
    qf = q.astype(F32)
    vf = v.astype(F32)
    o_rows = []
    for n in range(nb):
        s0 = s0_ref[n]
        qn = qf[n * t:(n + 1) * t].astype(BF16)
        o_rows.append(jnp.dot(qn, s0.astype(BF16), preferred_element_type=F32))
        vn = jnp.where(rcol // t == n, vf, 0.0).astype(BF16)
        s1_ref[n] = s0 * s_dec + lax.dot_general(kd, vn, (((0,), (0,)), ((), ())), preferred_element_type=F32)
    o = o_intra + jnp.concatenate(o_rows, axis=0) * q_dec
    on = o * lax.rsqrt(jnp.mean(o * o, axis=-1, keepdims=True) + EPS)
    return (on * g_ref[...].astype(F32)).astype(BF16)


def _retention_kernel(lg_ref, q_ref, k_ref, v_ref, g_ref, sq_ref, sk_ref, sv_ref, sg_ref, s0_ref,
                      w_ref, xp_ref, xs_ref, gp_ref, gn_ref,
                      xnp_ref, hnp_ref, xns_ref, hns_ref, sfin_ref, s1_ref,
                      accp_ref, accs_ref):
    c, h = pl.program_id(1), pl.program_id(2)
    lg = lg_ref[h]
    yp = _ret_prompt_part(lg, c == 0, q_ref, k_ref, v_ref, g_ref, sfin_ref.at[h])
    ys = _ret_sample_part(lg, sq_ref, sk_ref, sv_ref, sg_ref, s0_ref, s1_ref)

    _zero_on_first(accp_ref, h)
    _zero_on_first(accs_ref, h)
    w = w_ref[...]
    accp_ref[...] += jnp.dot(yp, w, preferred_element_type=F32)
    accs_ref[...] += jnp.dot(ys, w, preferred_element_type=F32)

    @pl.when(h == pl.num_programs(2) - 1)
    def _():
        for acc_ref, x_ref, xn_ref, hn_ref in ((accp_ref, xp_ref, xnp_ref, hnp_ref),
                                               (accs_ref, xs_ref, xns_ref, hns_ref)):
            xn, hn = _residual(acc_ref[...], x_ref[...], gp_ref[...], gn_ref[...])
            xn_ref[...] = xn
            hn_ref[...] = hn


def _retention_and_out(zqk, zv, zg, state, lg, w_out_bf, x_parts, g_post, g_next):
    nc = SEQ // RET_ROWS
    srows = RET_NB * DEC_SEQ
    soff = MP // srows
    x_soff = soff if len(x_parts) == 1 else 0
    prow = lambda b, c: b * nc + c
    head_specs = lambda rows, row: [
        pl.BlockSpec((rows, RET_KEY_DIM), lambda b, c, h: (row(b, c), h)),
        pl.BlockSpec((rows, RET_KEY_DIM), lambda b, c, h: (row(b, c), RET_HEADS + h)),
        pl.BlockSpec((rows, RET_VAL_DIM), lambda b, c, h: (row(b, c), h)),
        pl.BlockSpec((rows, RET_VAL_DIM), lambda b, c, h: (row(b, c), h))]
    sspec = pl.BlockSpec((RET_NB, None, RET_KEY_DIM, RET_VAL_DIM), lambda b, c, h: (prow(b, c), h, 0, 0))
    vec = pl.BlockSpec((1, D_MODEL), lambda b, c, h: (0, 0))
    prow_spec = pl.BlockSpec((RET_ROWS, D_MODEL), lambda b, c, h: (prow(b, c), 0), pipeline_mode=pl.Buffered(1))
    srow_spec = pl.BlockSpec((srows, D_MODEL), lambda b, c, h: (prow(b, c), 0))
    return pl.pallas_call(
        _retention_kernel, grid=(BATCH, nc, RET_HEADS),
        in_specs=([pl.BlockSpec(memory_space=pltpu.SMEM)] + head_specs(RET_ROWS, prow)
                  + head_specs(srows, lambda b, c: soff + prow(b, c))
                  + [sspec, pl.BlockSpec((RET_VAL_DIM, D_MODEL), lambda b, c, h: (h, 0)),
                     pl.BlockSpec((RET_ROWS, D_MODEL), lambda b, c, h: (prow(b, c), 0),
                                  pipeline_mode=pl.Buffered(1)),
                     pl.BlockSpec((srows, D_MODEL), lambda b, c, h: (x_soff + prow(b, c), 0)), vec, vec]),
        out_specs=[prow_spec, prow_spec, srow_spec, srow_spec,
                   pl.BlockSpec((None, RET_HEADS, RET_KEY_DIM, RET_VAL_DIM), lambda b, c, h: (b, 0, 0, 0)),
                   sspec],
        out_shape=[jax.ShapeDtypeStruct((MP, D_MODEL), F32), jax.ShapeDtypeStruct((MP, D_MODEL), BF16),
                   jax.ShapeDtypeStruct((MS, D_MODEL), F32), jax.ShapeDtypeStruct((MS, D_MODEL), BF16),
                   jax.ShapeDtypeStruct((BATCH, RET_HEADS, RET_KEY_DIM, RET_VAL_DIM), F32),
                   jax.ShapeDtypeStruct((DEC_BATCH, RET_HEADS, RET_KEY_DIM, RET_VAL_DIM), F32)],
        scratch_shapes=[pltpu.VMEM((RET_ROWS, D_MODEL), F32), pltpu.VMEM((srows, D_MODEL), F32)],
        compiler_params=_cparams(("arbitrary", "arbitrary", "arbitrary"), 56), name="retention",
    )(lg, zqk, zqk, zv, zg, zqk, zqk, zv, zg, state, w_out_bf, x_parts[0], x_parts[-1], g_post, g_next)


def _zero_on_first(acc_ref, k):
    @pl.when(k == 0)
    def _():
        acc_ref[...] = jnp.zeros_like(acc_ref)


def _outproj_kernel(a_ref, w_ref, xp_ref, xs_ref, gp_ref, gn_ref, xn_ref, hn_ref):
    y = jnp.dot(a_ref[...], w_ref[...], preferred_element_type=F32)
    is_prompt = pl.program_id(0) < MP // xn_ref.shape[0]
    x = jnp.where(is_prompt, xp_ref[...], xs_ref[...])
    xn, hn = _residual(y, x, gp_ref[...], gn_ref[...])
    xn_ref[...] = xn
    hn_ref[...] = hn


def _out_proj(a, w_bf, x_parts, g_post, g_next, name):
    kdim = w_bf.shape[0]
    tm = TM_OUT
    np_t = MP // tm
    x_prompt, x_sample = x_parts[0], x_parts[-1]
    sample_t0 = np_t if len(x_parts) == 1 else 0
    row = pl.BlockSpec((tm, D_MODEL), lambda i: (i, 0))
    vec = pl.BlockSpec((1, D_MODEL), lambda i: (0, 0))
    return pl.pallas_call(
        _outproj_kernel, grid=(M // tm,),
        in_specs=[pl.BlockSpec((tm, kdim), lambda i: (i, 0)),
                  pl.BlockSpec((kdim, D_MODEL), lambda i: (0, 0), pipeline_mode=pl.Buffered(1)),
                  pl.BlockSpec((tm, D_MODEL), lambda i: (jnp.minimum(i, np_t - 1), 0)),
                  pl.BlockSpec((tm, D_MODEL), lambda i: (sample_t0 + jnp.maximum(i - np_t, 0), 0)), vec, vec],
        out_specs=[row, row],
        out_shape=[jax.ShapeDtypeStruct((M, D_MODEL), F32), jax.ShapeDtypeStruct((M, D_MODEL), BF16)],
        compiler_params=_cparams(("arbitrary",), 56), name=name,
    )(a, w_bf, x_prompt, x_sample, g_post, g_next)


def _ffn_up_down(h_ref, wu_ref, wd_ref):
    a = jnp.maximum(jnp.dot(h_ref[...], wu_ref[...], preferred_element_type=F32), 0.0)
    return jnp.dot((a * a).astype(BF16), wd_ref[...], preferred_element_type=F32)


def _ffn_kernel(h_ref, wu_ref, wd_ref, x_ref, gp_ref, gn_ref, xn_ref, hn_ref):
    c = pl.program_id(1)
    _zero_on_first(xn_ref, c)
    xn_ref[...] += _ffn_up_down(h_ref, wu_ref, wd_ref)

    @pl.when(c == pl.num_programs(1) - 1)
    def _():
        xn, hn = _residual(xn_ref[...], x_ref[...], gp_ref[...], gn_ref[...])
        xn_ref[...] = xn
        hn_ref[...] = hn


def _ffn_last_kernel(h_ref, wu_ref, wd_ref, x_ref, gp_ref, yp_ref, ys_ref, acc_ref):
    i, c = pl.program_id(0), pl.program_id(1)
    _zero_on_first(acc_ref, c)
    acc_ref[...] += _ffn_up_down(h_ref, wu_ref, wd_ref)
    last = c == pl.num_programs(1) - 1
    is_prompt = i < MP // acc_ref.shape[0]

    @pl.when(last & is_prompt)
    def _():
        yp_ref[...] = x_ref[...] + _rms(acc_ref[...], gp_ref[...])

    @pl.when(last & jnp.logical_not(is_prompt))
    def _():
        ys_ref[...] = x_ref[...] + _rms(acc_ref[...], gp_ref[...])


def _ffn(h, wu_bf, wd_bf, x, g_post, g_next):
    tm, tc = TM_OUT, 1024
    row = pl.BlockSpec((tm, D_MODEL), lambda i, c: (i, 0))
    vec = pl.BlockSpec((1, D_MODEL), lambda i, c: (0, 0))
    w_specs = [pl.BlockSpec((D_MODEL, tc), lambda i, c: (0, c)), pl.BlockSpec((tc, D_MODEL), lambda i, c: (c, 0))]
    grid = (M // tm, D_FF // tc)
    if g_next is not None:
        return pl.pallas_call(
            _ffn_kernel, grid=grid, in_specs=[row] + w_specs + [row, vec, vec], out_specs=[row, row],
            out_shape=[jax.ShapeDtypeStruct((M, D_MODEL), F32), jax.ShapeDtypeStruct((M, D_MODEL), BF16)],
            compiler_params=_cparams(("arbitrary", "arbitrary"), 56), name="ffn",
        )(h, wu_bf, wd_bf, x, g_post, g_next)
    np_t = MP // tm
    return pl.pallas_call(
        _ffn_last_kernel, grid=grid, in_specs=[row] + w_specs + [row, vec],
        out_specs=[pl.BlockSpec((tm, D_MODEL), lambda i, c: (jnp.minimum(i, np_t - 1), 0)),
                   pl.BlockSpec((tm, D_MODEL), lambda i, c: (jnp.maximum(i - np_t, 0), 0))],
        out_shape=[jax.ShapeDtypeStruct((MP, D_MODEL), F32), jax.ShapeDtypeStruct((MS, D_MODEL), F32)],
        scratch_shapes=[pltpu.VMEM((tm, D_MODEL), F32)],
        compiler_params=_cparams(("arbitrary", "arbitrary"), 56), name="ffn_last",
    )(h, wu_bf, wd_bf, x, g_post)


def _mem_kv_kernel(m_ref, g_ref, wk_ref, wv_ref, k_ref, v_ref):
    m = _rms(m_ref[...], g_ref[0]).astype(BF16)
    k_ref[0] = jnp.dot(m, wk_ref[0].astype(BF16), preferred_element_type=F32)
    v_ref[0] = jnp.dot(m, wv_ref[0].astype(BF16), preferred_element_type=F32)


def _mem_kv(mem, g, w_k, w_v):
    rows = BATCH * MEM_LEN
    wspec = pl.BlockSpec((1, D_MODEL, MEM_W), lambda l: (l, 0, 0))
    ospec = pl.BlockSpec((1, rows, MEM_W), lambda l: (l, 0, 0))
    oshape = jax.ShapeDtypeStruct((DEPTH, rows, MEM_W), F32)
    return pl.pallas_call(
        _mem_kv_kernel, grid=(DEPTH,),
        in_specs=[pl.BlockSpec((rows, D_MODEL), lambda l: (0, 0)),
                  pl.BlockSpec((1, 1, D_MODEL), lambda l: (l, 0, 0)), wspec, wspec],
        out_specs=[ospec, ospec], out_shape=[oshape, oshape],
        compiler_params=_cparams(("arbitrary",), 48), name="mem_kv",
    )(mem.reshape(rows, D_MODEL), g.reshape(DEPTH, 1, D_MODEL), w_k, w_v)


def _mem_attend_prompt(q, mk_ref, mv_ref):
    q = q.astype(BF16)
    mk, mv = mk_ref[...].astype(BF16), mv_ref[...].astype(BF16)
    outs = []
    for hh in range(MEM_HEADS):
        cs = slice(hh * MEM_HEAD_DIM, (hh + 1) * MEM_HEAD_DIM)
        s = lax.dot_general(q[:, cs], mk[:, cs], (((1,), (1,)), ((), ())),
                            preferred_element_type=F32) * (MEM_HEAD_DIM ** -0.5)
        p = jnp.exp(s - jnp.max(s, axis=-1, keepdims=True))
        den = jnp.sum(p, axis=-1, keepdims=True)
        outs.append(jnp.dot(p.astype(BF16), mv[:, cs], preferred_element_type=F32) / den)
    return jnp.concatenate(outs, axis=1)


def _mem_attend_sample(q, ck_ref, cv_ref):
    nb, t = ck_ref.shape[0], DEC_SEQ
    qs = jnp.concatenate([q[:, hh * MEM_HEAD_DIM:(hh + 1) * MEM_HEAD_DIM].reshape(nb, t, MEM_HEAD_DIM)
                          for hh in range(MEM_HEADS)], axis=1).astype(BF16)
    s = jnp.einsum('nqd,nkd->nqk', qs, ck_ref[...].astype(BF16),
                   preferred_element_type=F32) * (MEM_HEAD_DIM ** -0.5)
    shape = (MEM_HEADS * t, MEM_LEN * MEM_HEADS)
    own = ((lax.broadcasted_iota(jnp.int32, shape, 1) & (MEM_HEADS - 1))
           == lax.broadcasted_iota(jnp.int32, shape, 0) // t)
    s = jnp.where(own[None], s, NEG_INF)
    p = jnp.exp(s - jnp.max(s, axis=-1, keepdims=True))
    den = jnp.sum(p, axis=-1, keepdims=True)
    o3 = jnp.einsum('nqk,nkd->nqd', p.astype(BF16), cv_ref[...].astype(BF16),
                    preferred_element_type=F32) / den
    return jnp.concatenate([o3[:, hh * t:(hh + 1) * t, :].reshape(nb * t, MEM_HEAD_DIM)
                            for hh in range(MEM_HEADS)], axis=1)


def _mem_kernel(hp_ref, hs_ref, wq_ref, wo_ref, mk_ref, mv_ref, ck_ref, cv_ref, xp_ref, xs_ref, gp_ref, gn_ref,
                xn_ref, hn_ref, os_ref):
    i = pl.program_id(0)
    tm = xn_ref.shape[0]
    n_prompt = MP // tm
    srows = hs_ref.shape[0]
    project = lambda h_ref: jnp.dot(h_ref[...], wq_ref[...], preferred_element_type=F32)

    def finish(o, x_ref):
        y = jnp.dot(o.astype(BF16), wo_ref[...], preferred_element_type=F32)
        xn, hn = _residual(y, x_ref[...], gp_ref[...], gn_ref[...])
        xn_ref[...] = xn
        hn_ref[...] = hn

    @pl.when(i < n_prompt)
    def _():
        os_ref[pl.ds(pl.multiple_of(i * srows, srows), srows), :] = _mem_attend_sample(
            project(hs_ref), ck_ref, cv_ref)
        finish(_mem_attend_prompt(project(hp_ref), mk_ref, mv_ref), xp_ref)

    @pl.when(i >= n_prompt)
    def _():
        finish(os_ref[pl.ds(pl.multiple_of((i - n_prompt) * tm, tm), tm), :], xs_ref)


MEM_TM = 256
MEM_NB = DEC_BATCH // (MP // MEM_TM)


def _mem_attention(h_parts, x_parts, wq_bf, wo_bf, mk, mv, cache_k, cache_v, layer, g_post, g_next):
    tm, nb = MEM_TM, MEM_NB
    srows = nb * DEC_SEQ
    sample_row0 = MP if len(h_parts) == 1 else 0
    per_b = SEQ // tm
    n_p = MP // tm
    pidx = lambda i: jnp.minimum(i, n_p - 1)
    row = pl.BlockSpec((tm, D_MODEL), lambda i: (i, 0))
    vec = pl.BlockSpec((1, D_MODEL), lambda i: (0, 0))
    kvspec = pl.BlockSpec((None, MEM_LEN, MEM_W), lambda i: (layer * BATCH + pidx(i) // per_b, 0, 0))
    cspec = pl.BlockSpec((None, nb, MEM_LEN * MEM_HEADS, MEM_HEAD_DIM), lambda i: (layer, pidx(i), 0, 0))
    resident = lambda shape: pl.BlockSpec(shape, lambda i: (layer, 0), pipeline_mode=pl.Buffered(1))
    return pl.pallas_call(
        _mem_kernel, grid=(M // tm,),
        in_specs=[pl.BlockSpec((tm, D_MODEL), lambda i: (pidx(i), 0)),
                  pl.BlockSpec((srows, D_MODEL), lambda i: (sample_row0 // srows + pidx(i), 0)),
                  resident((D_MODEL, MEM_W)), resident((MEM_W, D_MODEL)), kvspec, kvspec,
                  cspec, cspec,
                  pl.BlockSpec((tm, D_MODEL), lambda i: (pidx(i), 0)),
                  pl.BlockSpec((tm, D_MODEL), lambda i: (sample_row0 // tm + jnp.maximum(i - n_p, 0), 0)),
                  vec, vec],
        out_specs=[row, row],
        out_shape=[jax.ShapeDtypeStruct((M, D_MODEL), F32), jax.ShapeDtypeStruct((M, D_MODEL), BF16)],
        scratch_shapes=[pltpu.VMEM((MS, MEM_W), F32)],
        compiler_params=_cparams(("arbitrary",), 48), name="mem_attention",
    )(h_parts[0], h_parts[-1], wq_bf, wo_bf, mk, mv, cache_k, cache_v, x_parts[0], x_parts[-1], g_post, g_next)


def kernel(x_prompt, x_sample, cache_swa_k, cache_swa_v, state_ret, cache_mem_k, cache_mem_v, mem_prompt, norm_mix_pre, norm_mix_post, norm_mem, norm_x_pre, norm_x_post, norm_ffn_pre, norm_ffn_post, w_ab_in, w_ab_out, swa_sinks, gmlp_ln_g, gmlp_ln_b, gmlp_w_s, gmlp_b_s, w_ret_in, w_ret_out, w_mem_q, w_mem_k, w_mem_v, w_mem_o, w_ffn_up, w_ffn_down):
    vec = lambda g, layer: g[layer].reshape(1, D_MODEL)
    swa_c, swa_s1, swa_s2, ret_c, ret_s = _rope_tables()
    mem_k, mem_v = _mem_kv(mem_prompt, norm_mem, w_mem_k, w_mem_v)
    mem_k_b = mem_k.reshape(DEPTH * BATCH, MEM_LEN, MEM_W)
    mem_v_b = mem_v.reshape(DEPTH * BATCH, MEM_LEN, MEM_W)
    ck_mem = cache_mem_k.reshape(DEPTH, DEC_BATCH, MEM_LEN * MEM_HEADS, MEM_HEAD_DIM)
    cv_mem = cache_mem_v.reshape(DEPTH, DEC_BATCH, MEM_LEN * MEM_HEADS, MEM_HEAD_DIM)
    lg = jnp.log1p(-jnp.exp2(-5.0 - jnp.arange(RET_HEADS, dtype=F32)))

    x_parts = (x_prompt.reshape(MP, D_MODEL), x_sample.reshape(MS, D_MODEL))
    h = _stacked_prenorm(*x_parts, vec(norm_mix_pre, 0))
    outs = {}
    for layer in range(DEPTH):
        j = layer // 2
        if layer % 2 == 0:
            (q, kv, u, zv, zv_s), (wu_bf, wd_bf, wout_bf, wq_bf, wo_bf) = _ab_in_proj(
                h, w_ab_in[j], gmlp_ln_g[j].reshape(1, GMLP_WIDTH), gmlp_ln_b[j].reshape(1, GMLP_WIDTH),
                (swa_c, swa_s1, swa_s2), w_ffn_up, w_ffn_down, w_ab_out, w_mem_q, w_mem_o, layer, j)
            cat = _ab_mix_prompt(q, kv, u, zv, swa_sinks[j], gmlp_w_s[j], gmlp_b_s[j])
            cat, nk_s, nv_s = _ab_mix_sample(
                cat, q, kv, u, zv_s, cache_swa_k[j].reshape(DEC_BATCH, WINDOW, SWA_KV_W),
                cache_swa_v[j].reshape(DEC_BATCH, WINDOW, SWA_KV_W), swa_sinks[j], gmlp_w_s[j], gmlp_b_s[j])
            kv_p = jnp.stack([kv[(b + 1) * SEQ - WINDOW:(b + 1) * SEQ] for b in range(BATCH)])
            kv_p = kv_p.reshape(BATCH, WINDOW, 2, SWA_KV_HEADS, SWA_HEAD_DIM)
            outs.setdefault('swa_k_p', []).append(kv_p[:, :, 0])
            outs.setdefault('swa_v_p', []).append(kv_p[:, :, 1])
            outs.setdefault('swa_k_s', []).append(nk_s.reshape(DEC_BATCH, WINDOW, SWA_KV_HEADS, SWA_HEAD_DIM))
            outs.setdefault('swa_v_s', []).append(nv_s.reshape(DEC_BATCH, WINDOW, SWA_KV_HEADS, SWA_HEAD_DIM))
            outs.setdefault('gmlp_v_s', []).append(zv_s.reshape(DEC_BATCH, DEC_SEQ, GMLP_WIDTH))
            x, h = _out_proj(cat, wout_bf, x_parts, vec(norm_mix_post, layer), vec(norm_x_pre, layer), "ab_out")
            x_parts, h_parts = (x,), (h,)
        else:
            (zqk, zv, zg), (wu_bf, wd_bf, wout_bf) = _ret_in_proj(
                h, w_ret_in[j], (ret_c, ret_s), w_ffn_up, w_ffn_down, w_ret_out, layer, j)
            x_p, h_p, x_s, h_s, s_p, s_s = _retention_and_out(
                zqk, zv, zg, state_ret[j], lg, wout_bf, x_parts, vec(norm_mix_post, layer),
                vec(norm_x_pre, layer))
            outs.setdefault('ret_p', []).append(s_p)
            outs.setdefault('ret_s', []).append(s_s)
            x_parts, h_parts = (x_p, x_s), (h_p, h_s)
        g_post = vec(norm_x_post, layer)
        g_next = vec(norm_ffn_pre, layer)
        x, h = _mem_attention(h_parts, x_parts, wq_bf, wo_bf, mem_k_b, mem_v_b, ck_mem, cv_mem, layer,
                              g_post, g_next)
        g_next = vec(norm_mix_pre, layer + 1) if layer + 1 < DEPTH else None
        x, h = _ffn(h, wu_bf, wd_bf, x, vec(norm_ffn_post, layer), g_next)
        x_parts = (x,)

    y_prompt, y_sample = x, h
    stack = lambda name: jnp.stack(outs[name])
    mem_shape = (DEPTH, BATCH, MEM_LEN, MEM_HEADS, MEM_HEAD_DIM)
    return (y_prompt.reshape(BATCH, SEQ, D_MODEL), y_sample.reshape(DEC_BATCH, DEC_SEQ, D_MODEL),
            stack('swa_k_p'), stack('swa_v_p'), stack('swa_k_s'), stack('swa_v_s'), stack('gmlp_v_s'),
            stack('ret_p'), stack('ret_s'), mem_k.reshape(mem_shape), mem_v.reshape(mem_shape))
```

```python
import functools
import math

import jax
import jax.numpy as jnp
from jax import lax
from jax.experimental import pallas as pl
from jax.experimental.pallas import tpu as pltpu

F32 = jnp.float32
BF16 = jnp.bfloat16

D_MODEL = 2048
BATCH = 2
SEQ = 4096
DEPTH = 2
DEC_BATCH = 128
DEC_SEQ = 8
PAST_LEN = 8192

SWA_HEADS = 16
SWA_KV_HEADS = 4
SWA_HEAD_DIM = 64
SWA_GROUP = SWA_HEADS // SWA_KV_HEADS
WINDOW = 128
ROPE_THETA = 500000.0
ROPE_DIM = SWA_HEAD_DIM // 4
ROPE_HALF = ROPE_DIM // 2

GMLP_GROUPS = 4
GMLP_WIDTH = 1024
GMLP_GROUP_DIM = GMLP_WIDTH // GMLP_GROUPS
GMLP_CHUNK = 128

SWA_Q_W = SWA_HEADS * SWA_HEAD_DIM
SWA_KV_W = SWA_KV_HEADS * SWA_HEAD_DIM
AB_IN = SWA_Q_W + 2 * SWA_KV_W + 2 * GMLP_WIDTH
AB_OUT = SWA_Q_W + GMLP_WIDTH

RET_HEADS = 8
RET_KEY_DIM = D_MODEL // RET_HEADS
RET_VAL_DIM = 2 * RET_KEY_DIM
RET_CHUNK = 128
RET_THETA = 10000.0
RET_QK_W = RET_HEADS * RET_KEY_DIM
RET_V_W = RET_HEADS * RET_VAL_DIM
RET_IN = 2 * RET_QK_W + 2 * RET_V_W

MEM_LEN = 256
MEM_HEADS = 4
MEM_HEAD_DIM = 128
MEM_W = MEM_HEADS * MEM_HEAD_DIM

D_FF = 4 * D_MODEL
EPS = 1e-6
NEG_INF = -1e30

MP = BATCH * SEQ
MS = DEC_BATCH * DEC_SEQ
M = MP + MS
LANES = 128
MIB = 1024 * 1024

TM_IN = 1024
TAB_ROWS = SEQ + MS
TM_OUT = 512


def _cparams(sem, vmem_mib):
    return pltpu.CompilerParams(dimension_semantics=sem, vmem_limit_bytes=vmem_mib * MIB)


def _rms(x, g):
    return x * lax.rsqrt(jnp.mean(x * x, axis=-1, keepdims=True) + EPS) * g


def _residual(y, x, g_post, g_next):
    xn = x + _rms(y, g_post)
    return xn, _rms(xn, g_next).astype(BF16)


def _rope_table_kernel(sc_ref, s1_ref, s2_ref, rc_ref, rs_ref):
    tr = sc_ref.shape[0]
    r = lax.broadcasted_iota(jnp.int32, (tr, LANES), 0) + pl.program_id(0) * tr
    pos = jnp.where(r < SEQ, r, PAST_LEN + ((r - SEQ) & (DEC_SEQ - 1))).astype(F32)
    lane = lax.broadcasted_iota(jnp.int32, (tr, LANES), 1)
    d = lane & (SWA_HEAD_DIM - 1)
    inv = jnp.exp((d & (ROPE_HALF - 1)).astype(F32) * (-math.log(ROPE_THETA) / ROPE_HALF))
    ang = pos * inv
    c, s = jnp.cos(ang), jnp.sin(ang)
    sc_ref[...] = jnp.where(d < ROPE_DIM, c, 1.0)
    s1_ref[...] = jnp.where(d < ROPE_HALF, -s, 0.0)
    s2_ref[...] = jnp.where((d >= ROPE_HALF) & (d < ROPE_DIM), s, 0.0)
    inv_r = jnp.exp(lane.astype(F32) * (-math.log(RET_THETA) / LANES))
    ang_r = pos * inv_r
    rc_ref[...] = jnp.cos(ang_r)
    rs_ref[...] = jnp.sin(ang_r)


def _rope_tables():
    tr = 512
    spec = pl.BlockSpec((tr, LANES), lambda i: (i, 0))
    shp = jax.ShapeDtypeStruct((TAB_ROWS, LANES), F32)
    return pl.pallas_call(
        _rope_table_kernel, grid=(TAB_ROWS // tr,), out_specs=[spec] * 5, out_shape=[shp] * 5,
        compiler_params=_cparams(("arbitrary",), 16), name="rope_tables")()


def _norm_kernel(xp_ref, xs_ref, g_ref, h_ref):
    def emit(src_ref):
        h_ref[...] = _rms(src_ref[...], g_ref[...]).astype(BF16)

    is_prompt = pl.program_id(0) < MP // h_ref.shape[0]
    pl.when(is_prompt)(lambda: emit(xp_ref))
    pl.when(jnp.logical_not(is_prompt))(lambda: emit(xs_ref))


def _stacked_prenorm(x_prompt, x_sample, g):
    tm = TM_OUT
    return pl.pallas_call(
        _norm_kernel, grid=(M // tm,),
        in_specs=[pl.BlockSpec((tm, D_MODEL), lambda i: (jnp.minimum(i, MP // tm - 1), 0)),
                  pl.BlockSpec((tm, D_MODEL), lambda i: (jnp.maximum(i - MP // tm, 0), 0)),
                  pl.BlockSpec((1, D_MODEL), lambda i: (0, 0))],
        out_specs=pl.BlockSpec((tm, D_MODEL), lambda i: (i, 0)),
        out_shape=jax.ShapeDtypeStruct((M, D_MODEL), BF16),
        compiler_params=_cparams(("arbitrary",), 40), name="prenorm",
    )(x_prompt, x_sample, g)


W_SUB = 512


def _inproj_kernel(h_ref, *refs, n_sub, n_side, n_cast, epilogue):
    w_refs, refs = refs[:n_sub], refs[n_sub:]
    side, refs = refs[:n_side], refs[n_side:]
    cast_src, refs = refs[:n_cast], refs[n_cast:]
    wbf_ref = refs[-1]
    outs, cast_dst = refs[:len(refs) - 1 - n_cast], refs[len(refs) - 1 - n_cast:-1]

    @pl.when(pl.program_id(1) == 0)
    def _():
        for s, w_ref in enumerate(w_refs):
            wbf_ref[:, s * W_SUB:(s + 1) * W_SUB] = w_ref[...].astype(BF16)

    acc = jnp.dot(h_ref[...], wbf_ref[...], preferred_element_type=F32)
    epilogue(acc, side, outs)
    for src, dst in zip(cast_src, cast_dst):
        dst[...] = src[...].astype(BF16)


def _tab_index(j, i):
    return (jnp.where(i < MP // TM_IN, i % (SEQ // TM_IN), SEQ // TM_IN), 0)


CAST_STEPS = MP // TM_IN


def _cast_job(w, layer, nrb, ncb, block_of):
    rows, cols = w.shape[1:]
    blk = (rows // nrb, cols // ncb)
    index = lambda j, i: block_of(j, jnp.minimum(i, CAST_STEPS - 1))
    return (w, pl.BlockSpec((None,) + blk, lambda j, i: (layer,) + tuple(index(j, i))),
            jax.ShapeDtypeStruct((rows, cols), BF16), pl.BlockSpec(blk, index))


def _inproj(h, w, col0, ncols, tn, epilogue, side, side_specs, out_shapes, out_specs, name, vmem_mib,
            casts=()):
    tm = TM_IN
    k = h.shape[1]
    n_sub = tn // W_SUB
    w_specs = [pl.BlockSpec((k, W_SUB), lambda j, i, s=s: (0, col0 // W_SUB + j * n_sub + s))
               for s in range(n_sub)]
    as_list = lambda v: list(v) if isinstance(v, (list, tuple)) else [v]
    outs = pl.pallas_call(
        functools.partial(_inproj_kernel, n_sub=n_sub, n_side=len(side), n_cast=len(casts), epilogue=epilogue),
        grid=(ncols // tn, M // tm),
        in_specs=([pl.BlockSpec((tm, k), lambda j, i: (i, 0))] + w_specs + side_specs
                  + [c[1] for c in casts]),
        out_specs=as_list(out_specs) + [c[3] for c in casts],
        out_shape=as_list(out_shapes) + [c[2] for c in casts],
        scratch_shapes=[pltpu.VMEM((k, tn), BF16)],
        compiler_params=_cparams(("arbitrary", "arbitrary"), vmem_mib), name=name,
    )(h, *([w] * n_sub), *side, *[c[0] for c in casts])
    n_main = len(outs) - len(casts)
    main = outs[0] if n_main == 1 else tuple(outs[:n_main])
    return (main, tuple(outs[n_main:])) if casts else main


def _swa_rope(x, c, s1, s2):
    return x * c + pltpu.roll(x, LANES - ROPE_HALF, 1) * s1 + pltpu.roll(x, ROPE_HALF, 1) * s2


def _epi_swa_q(acc, side, outs):
    c, s1, s2 = (t[...] for t in side)
    (q_ref,) = outs
    for cc in range(acc.shape[1] // LANES):
        sl = slice(cc * LANES, (cc + 1) * LANES)
        q_ref[:, sl] = (_swa_rope(acc[:, sl], c, s1, s2) * (SWA_HEAD_DIM ** -0.5)).astype(BF16)


def _epi_swa_kv(acc, side, outs):
    c, s1, s2 = (t[...] for t in side)
    (kv_ref,) = outs
    for cc in range(SWA_KV_W // LANES):
        sl = slice(cc * LANES, (cc + 1) * LANES)
        kv_ref[:, sl] = _swa_rope(acc[:, sl], c, s1, s2)
    kv_ref[:, SWA_KV_W:] = acc[:, SWA_KV_W:]


def _epi_gelu(acc, side, outs):
    outs[0][...] = jax.nn.gelu(acc).astype(BF16)


def _epi_gelu_ln(acc, side, outs):
    g_ref, b_ref = side
    zv_ref, zvs_ref = outs
    a = jax.nn.gelu(acc)
    xc = a - jnp.mean(a, axis=-1, keepdims=True)
    y = xc * lax.rsqrt(jnp.mean(xc * xc, axis=-1, keepdims=True) + EPS) * g_ref[...] + b_ref[...]
    zv_ref[...] = y.astype(BF16)

    @pl.when(pl.program_id(1) == MP // TM_IN)
    def _():
        zvs_ref[...] = y


def _epi_ret_qk(acc, side, outs):
    cos, sin = (t[...] for t in side)
    (z_ref,) = outs
    tn = acc.shape[1]
    scale = jnp.where(pl.program_id(0) < RET_QK_W // tn, 1.0, RET_KEY_DIM ** -0.5)
    for hh in range(tn // RET_KEY_DIM):
        c0 = hh * RET_KEY_DIM
        x1, x2 = acc[:, c0:c0 + LANES], acc[:, c0 + LANES:c0 + 2 * LANES]
        z_ref[hh, :, :LANES] = ((x1 * cos - x2 * sin) * scale).astype(BF16)
        z_ref[hh, :, LANES:] = ((x2 * cos + x1 * sin) * scale).astype(BF16)


def _per_head(fn):
    def epilogue(acc, side, outs):
        (z_ref,) = outs
        for hh in range(z_ref.shape[0]):
            z_ref[hh] = fn(acc[:, hh * RET_VAL_DIM:(hh + 1) * RET_VAL_DIM]).astype(BF16)
    return epilogue


_epi_cast = _per_head(lambda a: a)
_epi_silu = _per_head(lambda a: a / (1.0 + jnp.exp(-a)))


def _ab_in_proj(h, w_in, ln_g, ln_b, tabs, w_up, w_down, w_out, w_mem_q, w_mem_o, layer, j):
    sc, s1, s2 = tabs
    tab_spec = pl.BlockSpec((TM_IN, LANES), _tab_index)
    row_out = lambda tn: pl.BlockSpec((TM_IN, tn), lambda j, i: (i, j))
    half = W_SUB
    steps = CAST_STEPS
    q, (wu_bf,) = _inproj(
        h, w_in, 0, SWA_Q_W, half, _epi_swa_q, [sc, s1, s2], [tab_spec] * 3,
        jax.ShapeDtypeStruct((M, SWA_Q_W), BF16), row_out(half), "ab_in_q", 52,
        casts=[_cast_job(w_up, layer, 1, 2 * steps, lambda j, i: (0, j * steps + i))])
    kv, (wout_bf,) = _inproj(
        h, w_in, SWA_Q_W, 2 * SWA_KV_W, half, _epi_swa_kv, [sc, s1, s2], [tab_spec] * 3,
        jax.ShapeDtypeStruct((M, 2 * SWA_KV_W), F32), row_out(half), "ab_in_kv", 40,
        casts=[_cast_job(w_out, j, steps, 1, lambda j, i: (i, 0))])
    u, (wd_bf,) = _inproj(
        h, w_in, SWA_Q_W + 2 * SWA_KV_W, GMLP_WIDTH, half, _epi_gelu, [], [],
        jax.ShapeDtypeStruct((M, GMLP_WIDTH), BF16), row_out(half), "ab_in_u", 52,
        casts=[_cast_job(w_down, layer, 2 * steps, 1, lambda j, i: (j * steps + i, 0))])
    vec_spec = pl.BlockSpec((1, GMLP_WIDTH), lambda j, i: (0, 0))
    stacked = lambda w: w.reshape(1, w.shape[0] * w.shape[1], w.shape[2])
    (zv, zv_s), (wq_bf, wo_bf) = _inproj(
        h, w_in, AB_IN - GMLP_WIDTH, GMLP_WIDTH, GMLP_WIDTH, _epi_gelu_ln,
        [ln_g, ln_b], [vec_spec] * 2,
        [jax.ShapeDtypeStruct((M, GMLP_WIDTH), BF16), jax.ShapeDtypeStruct((MS, GMLP_WIDTH), F32)],
        [row_out(GMLP_WIDTH), pl.BlockSpec((MS, GMLP_WIDTH), lambda j, i: (0, 0))], "ab_in_zv", 52,
        casts=[_cast_job(stacked(w_mem_q), 0, steps, 1, lambda j, i: (i, 0)),
               _cast_job(stacked(w_mem_o), 0, steps, 1, lambda j, i: (i, 0))])
    return (q, kv, u, zv, zv_s), (wu_bf, wd_bf, wout_bf, wq_bf, wo_bf)


def _ret_in_proj(h, w_in, tabs, w_up, w_down, w_out, layer, j):
    rc, rs = tabs
    tn = 1024
    tab_spec = pl.BlockSpec((TM_IN, LANES), _tab_index)
    by_tile = lambda j, i: (j, i)
    head_major = jax.ShapeDtypeStruct((RET_HEADS, M, 2 * RET_KEY_DIM), BF16)
    seg = lambda col0, ncols, epi, side, specs, out_spec, name, w, lead: _inproj(
        h, w_in, col0, ncols, tn, epi, side, specs, head_major, out_spec, name, 56,
        casts=[_cast_job(w, lead, ncols // tn, CAST_STEPS, by_tile)])
    qk_heads = tn // RET_KEY_DIM
    q_tiles = RET_QK_W // tn
    qk_spec = pl.BlockSpec((qk_heads, TM_IN, RET_KEY_DIM), lambda j, i: (j % q_tiles, i, j // q_tiles))
    v_spec = pl.BlockSpec((tn // RET_VAL_DIM, TM_IN, RET_VAL_DIM), lambda j, i: (j, i, 0))
    zqk, (wu_bf,) = seg(0, 2 * RET_QK_W, _epi_ret_qk, [rc, rs], [tab_spec] * 2, qk_spec, "ret_in_qk",
                        w_up, layer)
    zv, (wd_bf,) = seg(2 * RET_QK_W, RET_V_W, _epi_cast, [], [], v_spec, "ret_in_v", w_down, layer)
    zg, (wout_bf,) = seg(2 * RET_QK_W + RET_V_W, RET_V_W, _epi_silu, [], [], v_spec, "ret_in_g", w_out, j)
    return (zqk, zv, zg), (wu_bf, wd_bf, wout_bf)


AB_ROWS = 512


def _sink_softmax(s, sink):
    m = jnp.maximum(jnp.max(s, axis=-1, keepdims=True), sink)
    p = jnp.exp(s - m)
    den = jnp.sum(p, axis=-1, keepdims=True) + jnp.exp(sink - m)
    return p, den


def _ab_prompt_kernel(sink_ref, q_ref, kv_ref, kvp_ref, u_ref, zv_ref, ws_ref, bs_ref, cat_ref,
                      kd_ref, vd_ref):
    r = pl.program_id(1)
    low = lax.broadcasted_iota(jnp.int32, (1, LANES), 1) < SWA_HEAD_DIM

    def both_halves(x, half):
        sw = pltpu.roll(x, SWA_HEAD_DIM, 1)
        return (jnp.where(low, x, sw) if half == 0 else jnp.where(low, sw, x)).astype(BF16)

    for kh in range(SWA_KV_HEADS):
        kcol = slice((kh // 2) * LANES, (kh // 2 + 1) * LANES)
        vcol = slice(SWA_KV_W + (kh // 2) * LANES, SWA_KV_W + (kh // 2 + 1) * LANES)
        kd_ref[kh, :WINDOW, :] = both_halves(kvp_ref[:, kcol], kh % 2)
        kd_ref[kh, WINDOW:, :] = both_halves(kv_ref[:, kcol], kh % 2)
        vd_ref[kh, :WINDOW, :] = both_halves(kvp_ref[:, vcol], kh % 2)
        vd_ref[kh, WINDOW:, :] = both_halves(kv_ref[:, vcol], kh % 2)

    t = lax.broadcasted_iota(jnp.int32, (WINDOW, 2 * WINDOW), 0)
    sk = lax.broadcasted_iota(jnp.int32, (WINDOW, 2 * WINDOW), 1)
    band = (sk <= t + WINDOW) & (sk > t)
    sink_slot = sk == t
    tril = (lax.broadcasted_iota(jnp.int32, (GMLP_CHUNK, GMLP_CHUNK), 0)
            >= lax.broadcasted_iota(jnp.int32, (GMLP_CHUNK, GMLP_CHUNK), 1))

    for sb in range(AB_ROWS // WINDOW):
        rs = slice(sb * WINDOW, (sb + 1) * WINDOW)
        ks = slice(sb * WINDOW, sb * WINDOW + 2 * WINDOW)
        valid = band & ((sk >= WINDOW) | (r * (AB_ROWS // WINDOW) + sb > 0))
        for kh in range(SWA_KV_HEADS):
            qm = []
            for g in range(SWA_GROUP):
                hd = kh * SWA_GROUP + g
                qb = q_ref[rs, (hd // 2) * LANES:(hd // 2 + 1) * LANES]
                qm.append(jnp.where(low if hd % 2 == 0 else jnp.logical_not(low), qb, jnp.zeros_like(qb)))
            s = lax.dot_general(jnp.concatenate(qm, axis=0), kd_ref[kh, ks, :], (((1,), (1,)), ((), ())),
                                preferred_element_type=F32)
            pv, den = [], []
            for g in range(SWA_GROUP):
                sg = s[g * WINDOW:(g + 1) * WINDOW]
                sg = jnp.where(valid, sg, jnp.where(sink_slot, sink_ref[kh * SWA_GROUP + g], NEG_INF))
                p = jnp.exp(sg - jnp.max(sg, axis=-1, keepdims=True))
                den.append(jnp.sum(p, axis=-1, keepdims=True))
                pv.append(jnp.where(sink_slot, 0.0, p).astype(BF16))
            o = jnp.dot(jnp.concatenate(pv, axis=0), vd_ref[kh, ks, :], preferred_element_type=F32)
            for pr in range(SWA_GROUP // 2):
                lo = o[(2 * pr) * WINDOW:(2 * pr + 1) * WINDOW] / den[2 * pr]
                hi = o[(2 * pr + 1) * WINDOW:(2 * pr + 2) * WINDOW] / den[2 * pr + 1]
                c0 = (kh * (SWA_GROUP // 2) + pr) * LANES
                cat_ref[rs, c0:c0 + LANES] = jnp.where(low, lo, hi).astype(BF16)
        zb = zv_ref[rs, :]
        ub = u_ref[rs, :]
        for g in range(GMLP_GROUPS):
            gs = slice(g * GMLP_GROUP_DIM, (g + 1) * GMLP_GROUP_DIM)
            w = jnp.where(tril, ws_ref[g], 0.0).astype(BF16)
            mixed = jnp.dot(w, zb[:, gs], preferred_element_type=F32) + bs_ref[g]
            cat_ref[rs, SWA_Q_W + g * GMLP_GROUP_DIM:SWA_Q_W + (g + 1) * GMLP_GROUP_DIM] = (
                ub[:, gs].astype(F32) * mixed).astype(BF16)


def _ab_mix_prompt(q, kv, u, zv, sinks, w_s, b_s):
    nr = SEQ // AB_ROWS
    sub = AB_ROWS // WINDOW
    row = lambda b, r: (b * nr + r, 0)
    return pl.pallas_call(
        _ab_prompt_kernel, grid=(BATCH, nr),
        in_specs=[pl.BlockSpec(memory_space=pltpu.SMEM),
                  pl.BlockSpec((AB_ROWS, SWA_Q_W), row),
                  pl.BlockSpec((AB_ROWS, 2 * SWA_KV_W), row),
                  pl.BlockSpec((WINDOW, 2 * SWA_KV_W), lambda b, r: (jnp.maximum((b * nr + r) * sub - 1, 0), 0)),
                  pl.BlockSpec((AB_ROWS, GMLP_WIDTH), row),
                  pl.BlockSpec((AB_ROWS, GMLP_WIDTH), row),
                  pl.BlockSpec((GMLP_GROUPS, GMLP_CHUNK, GMLP_CHUNK), lambda b, r: (0, 0, 0)),
                  pl.BlockSpec((GMLP_GROUPS, GMLP_CHUNK, 1), lambda b, r: (0, 0, 0))],
        out_specs=pl.BlockSpec((AB_ROWS, AB_OUT), row),
        out_shape=jax.ShapeDtypeStruct((M, AB_OUT), BF16),
        scratch_shapes=[pltpu.VMEM((SWA_KV_HEADS, WINDOW + AB_ROWS, LANES), BF16)] * 2,
        compiler_params=_cparams(("arbitrary", "arbitrary"), 40), name="ab_mix_prompt",
    )(sinks, q, kv, kv, u, zv, w_s, b_s.reshape(GMLP_GROUPS, GMLP_CHUNK, 1))


AB_NB = 16
AB_KEYS = 2 * WINDOW


def _ab_sample_kernel(sink_ref, q_ref, kv_ref, u_ref, zv_ref, ck_ref, cv_ref, ws_ref, bs_ref, cat_in_ref,
                      cat_ref, nk_ref, nv_ref):
    del cat_in_ref
    nb, t = AB_NB, DEC_SEQ
    kv3 = kv_ref[...].reshape(nb, t, 2 * SWA_KV_W)
    kn, vn = kv3[:, :, :SWA_KV_W], kv3[:, :, SWA_KV_W:]
    ck, cv = ck_ref[...], cv_ref[...]
    nk_ref[:, :WINDOW - t, :] = ck[:, t:, :]
    nk_ref[:, WINDOW - t:, :] = kn
    nv_ref[:, :WINDOW - t, :] = cv[:, t:, :]
    nv_ref[:, WINDOW - t:, :] = vn
    pad = jnp.zeros((nb, AB_KEYS - WINDOW - t, SWA_KV_W), F32)
    kall = jnp.concatenate([ck, kn, pad], axis=1).astype(BF16)
    vall = jnp.concatenate([cv, vn, pad], axis=1).astype(BF16)

    rows = SWA_GROUP * t
    tq = lax.broadcasted_iota(jnp.int32, (rows, AB_KEYS), 0) & (t - 1)
    sk = lax.broadcasted_iota(jnp.int32, (rows, AB_KEYS), 1)
    valid = ((sk < WINDOW) & (sk > tq)) | ((sk >= WINDOW) & (sk - WINDOW <= tq))

    q3 = q_ref[...].astype(F32).reshape(nb, t, SWA_Q_W)
    pieces = []
    for kh in range(SWA_KV_HEADS):
        heads = [kh * SWA_GROUP + g for g in range(SWA_GROUP)]
        qs = jnp.concatenate([q3[:, :, hd * SWA_HEAD_DIM:(hd + 1) * SWA_HEAD_DIM] for hd in heads],
                             axis=1).astype(BF16)
        cs = slice(kh * SWA_HEAD_DIM, (kh + 1) * SWA_HEAD_DIM)
        s = jnp.einsum('nqd,nkd->nqk', qs, kall[:, :, cs], preferred_element_type=F32)
        s = jnp.where(valid[None], s, NEG_INF)
        sink = jnp.concatenate([jnp.full((t, 1), sink_ref[hd], F32) for hd in heads], axis=0)[None]
        p, den = _sink_softmax(s, sink)
        o = jnp.einsum('nqk,nkd->nqd', p.astype(BF16), vall[:, :, cs], preferred_element_type=F32) / den
        pieces += [o[:, g * t:(g + 1) * t, :] for g in range(SWA_GROUP)]
    attn = jnp.concatenate(pieces, axis=2).reshape(nb * t, SWA_Q_W)
    cat_ref[:, :SWA_Q_W] = attn.astype(BF16)

    zv3 = zv_ref[...].reshape(nb, t, GMLP_WIDTH)
    u3 = u_ref[...].astype(F32).reshape(nb, t, GMLP_WIDTH)
    irow = lax.broadcasted_iota(jnp.int32, (t, GMLP_GROUP_DIM), 0)
    for g in range(GMLP_GROUPS):
        gs = slice(g * GMLP_GROUP_DIM, (g + 1) * GMLP_GROUP_DIM)
        zg = zv3[:, :, gs]
        mixed = jnp.zeros((nb, t, GMLP_GROUP_DIM), F32) + bs_ref[g][None]
        for j in range(t):
            wcol = jnp.where(irow >= j, ws_ref[g, j], 0.0)
            mixed = mixed + zg[:, j:j + 1, :] * wcol[None]
        gate = (u3[:, :, gs] * mixed).reshape(nb * t, GMLP_GROUP_DIM)
        cat_ref[:, SWA_Q_W + g * GMLP_GROUP_DIM:SWA_Q_W + (g + 1) * GMLP_GROUP_DIM] = gate.astype(BF16)


def _ab_mix_sample(cat, q, kv, u, zv_s, cache_k, cache_v, sinks, w_s, b_s):
    nb, t = AB_NB, DEC_SEQ
    rows = nb * t
    off = MP // rows
    srow = lambda i: (off + i, 0)
    wt = jnp.broadcast_to(jnp.swapaxes(w_s[:, :t, :t], 1, 2)[..., None], (GMLP_GROUPS, t, t, GMLP_GROUP_DIM))
    bt = jnp.broadcast_to(b_s[:, :t, None], (GMLP_GROUPS, t, GMLP_GROUP_DIM))
    cspec = pl.BlockSpec((nb, WINDOW, SWA_KV_W), lambda i: (i, 0, 0))
    cshape = jax.ShapeDtypeStruct((DEC_BATCH, WINDOW, SWA_KV_W), F32)
    return pl.pallas_call(
        _ab_sample_kernel, grid=(DEC_BATCH // nb,),
        in_specs=[pl.BlockSpec(memory_space=pltpu.SMEM),
                  pl.BlockSpec((rows, SWA_Q_W), srow),
                  pl.BlockSpec((rows, 2 * SWA_KV_W), srow),
                  pl.BlockSpec((rows, GMLP_WIDTH), srow),
                  pl.BlockSpec((rows, GMLP_WIDTH), lambda i: (i, 0)),
                  cspec, cspec,
                  pl.BlockSpec((GMLP_GROUPS, t, t, GMLP_GROUP_DIM), lambda i: (0, 0, 0, 0)),
                  pl.BlockSpec((GMLP_GROUPS, t, GMLP_GROUP_DIM), lambda i: (0, 0, 0)),
                  pl.BlockSpec(memory_space=pl.ANY)],
        out_specs=[pl.BlockSpec((rows, AB_OUT), srow), cspec, cspec],
        out_shape=[jax.ShapeDtypeStruct((M, AB_OUT), BF16), cshape, cshape],
        input_output_aliases={9: 0},
        compiler_params=_cparams(("arbitrary",), 48), name="ab_mix_sample",
    )(sinks, q, kv, u, zv_s, cache_k, cache_v, wt, bt, cat)


RET_ROWS = 1024
RET_BLOCK = 256


def _ret_prompt_part(lg, first_run, qk_ref, v_ref, g_ref, s_ref):
    L = RET_BLOCK

    @pl.when(first_run)
    def _():
        s_ref[...] = jnp.zeros_like(s_ref)

    diff = (lax.broadcasted_iota(jnp.int32, (L, L), 0) - lax.broadcasted_iota(jnp.int32, (L, L), 1)).astype(F32)
    dmat = jnp.where(diff >= 0, jnp.exp(lg * jnp.maximum(diff, 0.0)), 0.0)
    idx = lax.broadcasted_iota(jnp.int32, (L, 1), 0).astype(F32)
    q_dec = jnp.exp(lg * (idx + 1.0))
    k_dec = jnp.exp(lg * (L - 1.0 - idx))
    s_dec = jnp.exp(lg * jnp.full((1, 1), float(L), F32))

    ys = []
    for ci in range(RET_ROWS // L):
        rs = slice(ci * L, (ci + 1) * L)
        q, k, v = qk_ref[rs, :RET_KEY_DIM], qk_ref[rs, RET_KEY_DIM:], v_ref[rs, :]
        att = lax.dot_general(q, k, (((1,), (1,)), ((), ())), preferred_element_type=F32) * dmat
        s0 = s_ref[...]
        o = (jnp.dot(att.astype(BF16), v, preferred_element_type=F32)
             + jnp.dot(q, s0.astype(BF16), preferred_element_type=F32) * q_dec)
        kd = (k.astype(F32) * k_dec).astype(BF16)
        s_ref[...] = s0 * s_dec + lax.dot_general(kd, v, (((0,), (0,)), ((), ())), preferred_element_type=F32)
        on = o * lax.rsqrt(jnp.mean(o * o, axis=-1, keepdims=True) + EPS)
        ys.append((on * g_ref[rs, :].astype(F32)).astype(BF16))
    return jnp.concatenate(ys, axis=0)


RET_NB = DEC_BATCH // (BATCH * (SEQ // RET_ROWS))


def _ret_sample_part(lg, qk_ref, v_ref, g_ref, s0_ref, s1_ref):
    nb, t = RET_NB, DEC_SEQ
    rows = nb * t
    q, k, v = qk_ref[:, :RET_KEY_DIM], qk_ref[:, RET_KEY_DIM:], v_ref[...]
    ri = lax.broadcasted_iota(jnp.int32, (rows, rows), 0)
    ci = lax.broadcasted_iota(jnp.int32, (rows, rows), 1)
    d = ((ri & (t - 1)) - (ci & (t - 1))).astype(F32)
    same = (ri // t) == (ci // t)
    dmat = jnp.where(same & (d >= 0), jnp.exp(lg * jnp.maximum(d, 0.0)), 0.0)
    att = lax.dot_general(q, k, (((1,), (1,)), ((), ())), preferred_element_type=F32) * dmat
    o_intra = jnp.dot(att.astype(BF16), v, preferred_element_type=F32)

    rcol = lax.broadcasted_iota(jnp.int32, (rows, 1), 0)
    tt = (rcol & (t - 1)).astype(F32)
    q_dec = jnp.exp(lg * (tt + 1.0))
    k_dec = jnp.exp(lg * (t - 1.0 - tt))
    s_dec = jnp.exp(lg * jnp.full((1, 1), float(t), F32))
    kd = (k.astype(F32) * k_dec).astype(BF16)
    qf = q.astype(F32)
    vf = v.astype(F32)
    o_rows = []
    for n in range(nb):
        s0 = s0_ref[n]
        qn = qf[n * t:(n + 1) * t].astype(BF16)
        o_rows.append(jnp.dot(qn, s0.astype(BF16), preferred_element_type=F32))
        vn = jnp.where(rcol // t == n, vf, 0.0).astype(BF16)
        s1_ref[n] = s0 * s_dec + lax.dot_general(kd, vn, (((0,), (0,)), ((), ())), preferred_element_type=F32)
    o = o_intra + jnp.concatenate(o_rows, axis=0) * q_dec
    on = o * lax.rsqrt(jnp.mean(o * o, axis=-1, keepdims=True) + EPS)
    return (on * g_ref[...].astype(F32)).astype(BF16)


def _retention_kernel(lg_ref, qk_ref, v_ref, g_ref, sqk_ref, sv_ref, sg_ref, s0_ref,
                      yp_ref, sfin_ref, ys_ref, s1_ref, s_ref):
    c = pl.program_id(2)
    lg = lg_ref[pl.program_id(1)]
    yp_ref[...] = _ret_prompt_part(lg, c == 0, qk_ref, v_ref, g_ref, s_ref)

    @pl.when(c == pl.num_programs(2) - 1)
    def _():
        sfin_ref[...] = s_ref[...]

    ys_ref[...] = _ret_sample_part(lg, sqk_ref, sv_ref, sg_ref, s0_ref, s1_ref)


def _retention(zqk, zv, zg, state, lg):
    nc = SEQ // RET_ROWS
    srows = RET_NB * DEC_SEQ
    soff = MP // srows
    prow = lambda b, c: b * nc + c
    width = 2 * RET_KEY_DIM
    head_specs = lambda rows, row: [pl.BlockSpec((None, rows, width), lambda b, h, c: (h, row(b, c), 0))] * 3
    sspec = pl.BlockSpec((RET_NB, None, RET_KEY_DIM, RET_VAL_DIM), lambda b, h, c: (prow(b, c), h, 0, 0))
    return pl.pallas_call(
        _retention_kernel, grid=(BATCH, RET_HEADS, nc),
        in_specs=([pl.BlockSpec(memory_space=pltpu.SMEM)] + head_specs(RET_ROWS, prow)
                  + head_specs(srows, lambda b, c: soff + prow(b, c)) + [sspec]),
        out_specs=[pl.BlockSpec((RET_ROWS, RET_VAL_DIM), lambda b, h, c: (prow(b, c), h)),
                   pl.BlockSpec((None, None, RET_KEY_DIM, RET_VAL_DIM), lambda b, h, c: (b, h, 0, 0)),
                   pl.BlockSpec((srows, RET_VAL_DIM), lambda b, h, c: (prow(b, c), h)),
                   sspec],
        out_shape=[jax.ShapeDtypeStruct((MP, RET_V_W), BF16),
                   jax.ShapeDtypeStruct((BATCH, RET_HEADS, RET_KEY_DIM, RET_VAL_DIM), F32),
                   jax.ShapeDtypeStruct((MS, RET_V_W), BF16),
                   jax.ShapeDtypeStruct((DEC_BATCH, RET_HEADS, RET_KEY_DIM, RET_VAL_DIM), F32)],
        scratch_shapes=[pltpu.VMEM((RET_KEY_DIM, RET_VAL_DIM), F32)],
        compiler_params=_cparams(("arbitrary", "arbitrary", "arbitrary"), 56), name="retention",
    )(lg, zqk, zv, zg, zqk, zv, zg, state)


def _zero_on_first(acc_ref, k):
    @pl.when(k == 0)
    def _():
        acc_ref[...] = jnp.zeros_like(acc_ref)


def _outproj_kernel(ap_ref, as_ref, w_ref, xp_ref, xs_ref, gp_ref, gn_ref, xn_ref, hn_ref):
    is_prompt = pl.program_id(0) < MP // xn_ref.shape[0]
    a = jnp.where(is_prompt, ap_ref[...], as_ref[...])
    x = jnp.where(is_prompt, xp_ref[...], xs_ref[...])
    y = jnp.dot(a, w_ref[...], preferred_element_type=F32)
    xn, hn = _residual(y, x, gp_ref[...], gn_ref[...])
    xn_ref[...] = xn
    hn_ref[...] = hn


def _out_proj(a_parts, w_bf, x_parts, g_post, g_next, name):
    kdim = w_bf.shape[0]
    tm = TM_OUT * D_MODEL // kdim
    np_t = MP // tm

    def two_source(parts, width):
        sample_t0 = np_t if len(parts) == 1 else 0
        return [pl.BlockSpec((tm, width), lambda i: (jnp.minimum(i, np_t - 1), 0)),
                pl.BlockSpec((tm, width), lambda i: (sample_t0 + jnp.maximum(i - np_t, 0), 0))]

    row = pl.BlockSpec((tm, D_MODEL), lambda i: (i, 0))
    vec = pl.BlockSpec((1, D_MODEL), lambda i: (0, 0))
    return pl.pallas_call(
        _outproj_kernel, grid=(M // tm,),
        in_specs=(two_source(a_parts, kdim)
                  + [pl.BlockSpec((kdim, D_MODEL), lambda i: (0, 0), pipeline_mode=pl.Buffered(1))]
                  + two_source(x_parts, D_MODEL) + [vec, vec]),
        out_specs=[row, row],
        out_shape=[jax.ShapeDtypeStruct((M, D_MODEL), F32), jax.ShapeDtypeStruct((M, D_MODEL), BF16)],
        compiler_params=_cparams(("arbitrary",), 56), name=name,
    )(a_parts[0], a_parts[-1], w_bf, x_parts[0], x_parts[-1], g_post, g_next)


def _ffn_up_down(h_ref, wu_ref, wd_ref):
    a = jnp.maximum(jnp.dot(h_ref[...], wu_ref[...], preferred_element_type=F32), 0.0)
    return jnp.dot((a * a).astype(BF16), wd_ref[...], preferred_element_type=F32)


def _ffn_kernel(h_ref, wu_ref, wd_ref, x_ref, gp_ref, gn_ref, xn_ref, hn_ref):
    c = pl.program_id(1)
    _zero_on_first(xn_ref, c)
    xn_ref[...] += _ffn_up_down(h_ref, wu_ref, wd_ref)

    @pl.when(c == pl.num_programs(1) - 1)
    def _():
        xn, hn = _residual(xn_ref[...], x_ref[...], gp_ref[...], gn_ref[...])
        xn_ref[...] = xn
        hn_ref[...] = hn


def _ffn_last_kernel(h_ref, wu_ref, wd_ref, x_ref, gp_ref, yp_ref, ys_ref, acc_ref):
    i, c = pl.program_id(0), pl.program_id(1)
    _zero_on_first(acc_ref, c)
    acc_ref[...] += _ffn_up_down(h_ref, wu_ref, wd_ref)
    last = c == pl.num_programs(1) - 1
    is_prompt = i < MP // acc_ref.shape[0]

    @pl.when(last & is_prompt)
    def _():
        yp_ref[...] = x_ref[...] + _rms(acc_ref[...], gp_ref[...])

    @pl.when(last & jnp.logical_not(is_prompt))
    def _():
        ys_ref[...] = x_ref[...] + _rms(acc_ref[...], gp_ref[...])


def _ffn(h, wu_bf, wd_bf, x, g_post, g_next):
    tm, tc = TM_OUT, 1024
    row = pl.BlockSpec((tm, D_MODEL), lambda i, c: (i, 0))
    vec = pl.BlockSpec((1, D_MODEL), lambda i, c: (0, 0))
    w_specs = [pl.BlockSpec((D_MODEL, tc), lambda i, c: (0, c)), pl.BlockSpec((tc, D_MODEL), lambda i, c: (c, 0))]
    grid = (M // tm, D_FF // tc)
    if g_next is not None:
        return pl.pallas_call(
            _ffn_kernel, grid=grid, in_specs=[row] + w_specs + [row, vec, vec], out_specs=[row, row],
            out_shape=[jax.ShapeDtypeStruct((M, D_MODEL), F32), jax.ShapeDtypeStruct((M, D_MODEL), BF16)],
            compiler_params=_cparams(("arbitrary", "arbitrary"), 56), name="ffn",
        )(h, wu_bf, wd_bf, x, g_post, g_next)
    np_t = MP // tm
    return pl.pallas_call(
        _ffn_last_kernel, grid=grid, in_specs=[row] + w_specs + [row, vec],
        out_specs=[pl.BlockSpec((tm, D_MODEL), lambda i, c: (jnp.minimum(i, np_t - 1), 0)),
                   pl.BlockSpec((tm, D_MODEL), lambda i, c: (jnp.maximum(i - np_t, 0), 0))],
        out_shape=[jax.ShapeDtypeStruct((MP, D_MODEL), F32), jax.ShapeDtypeStruct((MS, D_MODEL), F32)],
        scratch_shapes=[pltpu.VMEM((tm, D_MODEL), F32)],
        compiler_params=_cparams(("arbitrary", "arbitrary"), 56), name="ffn_last",
    )(h, wu_bf, wd_bf, x, g_post)


def _mem_kv_kernel(m_ref, g_ref, wk_ref, wv_ref, k_ref, v_ref):
    m = _rms(m_ref[...], g_ref[0]).astype(BF16)
    k_ref[0] = jnp.dot(m, wk_ref[0].astype(BF16), preferred_element_type=F32)
    v_ref[0] = jnp.dot(m, wv_ref[0].astype(BF16), preferred_element_type=F32)


def _mem_kv(mem, g, w_k, w_v):
    rows = BATCH * MEM_LEN
    wspec = pl.BlockSpec((1, D_MODEL, MEM_W), lambda l: (l, 0, 0))
    ospec = pl.BlockSpec((1, rows, MEM_W), lambda l: (l, 0, 0))
    oshape = jax.ShapeDtypeStruct((DEPTH, rows, MEM_W), F32)
    return pl.pallas_call(
        _mem_kv_kernel, grid=(DEPTH,),
        in_specs=[pl.BlockSpec((rows, D_MODEL), lambda l: (0, 0)),
                  pl.BlockSpec((1, 1, D_MODEL), lambda l: (l, 0, 0)), wspec, wspec],
        out_specs=[ospec, ospec], out_shape=[oshape, oshape],
        compiler_params=_cparams(("arbitrary",), 48), name="mem_kv",
    )(mem.reshape(rows, D_MODEL), g.reshape(DEPTH, 1, D_MODEL), w_k, w_v)


def _mem_attend_prompt(q, mk_ref, mv_ref):
    q = q.astype(BF16)
    mk, mv = mk_ref[...].astype(BF16), mv_ref[...].astype(BF16)
    outs = []
    for hh in range(MEM_HEADS):
        cs = slice(hh * MEM_HEAD_DIM, (hh + 1) * MEM_HEAD_DIM)
        s = lax.dot_general(q[:, cs], mk[:, cs], (((1,), (1,)), ((), ())),
                            preferred_element_type=F32) * (MEM_HEAD_DIM ** -0.5)
        p = jnp.exp(s - jnp.max(s, axis=-1, keepdims=True))
        den = jnp.sum(p, axis=-1, keepdims=True)
        outs.append(jnp.dot(p.astype(BF16), mv[:, cs], preferred_element_type=F32) / den)
    return jnp.concatenate(outs, axis=1)


def _mem_attend_sample(q, ck_ref, cv_ref):
    nb, t = ck_ref.shape[0], DEC_SEQ
    qs = jnp.concatenate([q[:, hh * MEM_HEAD_DIM:(hh + 1) * MEM_HEAD_DIM].reshape(nb, t, MEM_HEAD_DIM)
                          for hh in range(MEM_HEADS)], axis=1).astype(BF16)
    s = jnp.einsum('nqd,nkd->nqk', qs, ck_ref[...].astype(BF16),
                   preferred_element_type=F32) * (MEM_HEAD_DIM ** -0.5)
    shape = (MEM_HEADS * t, MEM_LEN * MEM_HEADS)
    own = ((lax.broadcasted_iota(jnp.int32, shape, 1) & (MEM_HEADS - 1))
           == lax.broadcasted_iota(jnp.int32, shape, 0) // t)
    s = jnp.where(own[None], s, NEG_INF)
    p = jnp.exp(s - jnp.max(s, axis=-1, keepdims=True))
    den = jnp.sum(p, axis=-1, keepdims=True)
    o3 = jnp.einsum('nqk,nkd->nqd', p.astype(BF16), cv_ref[...].astype(BF16),
                    preferred_element_type=F32) / den
    return jnp.concatenate([o3[:, hh * t:(hh + 1) * t, :].reshape(nb * t, MEM_HEAD_DIM)
                            for hh in range(MEM_HEADS)], axis=1)


def _mem_kernel(hp_ref, hs_ref, wq_ref, wo_ref, mk_ref, mv_ref, ck_ref, cv_ref, xp_ref, xs_ref, gp_ref, gn_ref,
                xn_ref, hn_ref, os_ref):
    i = pl.program_id(0)
    tm = xn_ref.shape[0]
    n_prompt = MP // tm
    srows = hs_ref.shape[0]
    project = lambda h_ref: jnp.dot(h_ref[...], wq_ref[...], preferred_element_type=F32)

    def finish(o, x_ref):
        y = jnp.dot(o.astype(BF16), wo_ref[...], preferred_element_type=F32)
        xn, hn = _residual(y, x_ref[...], gp_ref[...], gn_ref[...])
        xn_ref[...] = xn
        hn_ref[...] = hn

    @pl.when(i < n_prompt)
    def _():
        os_ref[pl.ds(pl.multiple_of(i * srows, srows), srows), :] = _mem_attend_sample(
            project(hs_ref), ck_ref, cv_ref)
        finish(_mem_attend_prompt(project(hp_ref), mk_ref, mv_ref), xp_ref)

    @pl.when(i >= n_prompt)
    def _():
        finish(os_ref[pl.ds(pl.multiple_of((i - n_prompt) * tm, tm), tm), :], xs_ref)


MEM_TM = 256
MEM_NB = DEC_BATCH // (MP // MEM_TM)


def _mem_attention(h_parts, x_parts, wq_bf, wo_bf, mk, mv, cache_k, cache_v, layer, g_post, g_next):
    tm, nb = MEM_TM, MEM_NB
    srows = nb * DEC_SEQ
    sample_row0 = MP if len(h_parts) == 1 else 0
    per_b = SEQ // tm
    n_p = MP // tm
    pidx = lambda i: jnp.minimum(i, n_p - 1)
    row = pl.BlockSpec((tm, D_MODEL), lambda i: (i, 0))
    vec = pl.BlockSpec((1, D_MODEL), lambda i: (0, 0))
    kvspec = pl.BlockSpec((None, MEM_LEN, MEM_W), lambda i: (layer * BATCH + pidx(i) // per_b, 0, 0))
    cspec = pl.BlockSpec((None, nb, MEM_LEN * MEM_HEADS, MEM_HEAD_DIM), lambda i: (layer, pidx(i), 0, 0))
    resident = lambda shape: pl.BlockSpec(shape, lambda i: (layer, 0), pipeline_mode=pl.Buffered(1))
    return pl.pallas_call(
        _mem_kernel, grid=(M // tm,),
        in_specs=[pl.BlockSpec((tm, D_MODEL), lambda i: (pidx(i), 0)),
                  pl.BlockSpec((srows, D_MODEL), lambda i: (sample_row0 // srows + pidx(i), 0)),
                  resident((D_MODEL, MEM_W)), resident((MEM_W, D_MODEL)), kvspec, kvspec,
                  cspec, cspec,
                  pl.BlockSpec((tm, D_MODEL), lambda i: (pidx(i), 0)),
                  pl.BlockSpec((tm, D_MODEL), lambda i: (sample_row0 // tm + jnp.maximum(i - n_p, 0), 0)),
                  vec, vec],
        out_specs=[row, row],
        out_shape=[jax.ShapeDtypeStruct((M, D_MODEL), F32), jax.ShapeDtypeStruct((M, D_MODEL), BF16)],
        scratch_shapes=[pltpu.VMEM((MS, MEM_W), F32)],
        compiler_params=_cparams(("arbitrary",), 48), name="mem_attention",
    )(h_parts[0], h_parts[-1], wq_bf, wo_bf, mk, mv, cache_k, cache_v, x_parts[0], x_parts[-1], g_post, g_next)


def kernel(x_prompt, x_sample, cache_swa_k, cache_swa_v, state_ret, cache_mem_k, cache_mem_v, mem_prompt, norm_mix_pre, norm_mix_post, norm_mem, norm_x_pre, norm_x_post, norm_ffn_pre, norm_ffn_post, w_ab_in, w_ab_out, swa_sinks, gmlp_ln_g, gmlp_ln_b, gmlp_w_s, gmlp_b_s, w_ret_in, w_ret_out, w_mem_q, w_mem_k, w_mem_v, w_mem_o, w_ffn_up, w_ffn_down):
    vec = lambda g, layer: g[layer].reshape(1, D_MODEL)
    swa_c, swa_s1, swa_s2, ret_c, ret_s = _rope_tables()
    mem_k, mem_v = _mem_kv(mem_prompt, norm_mem, w_mem_k, w_mem_v)
    mem_k_b = mem_k.reshape(DEPTH * BATCH, MEM_LEN, MEM_W)
    mem_v_b = mem_v.reshape(DEPTH * BATCH, MEM_LEN, MEM_W)
    ck_mem = cache_mem_k.reshape(DEPTH, DEC_BATCH, MEM_LEN * MEM_HEADS, MEM_HEAD_DIM)
    cv_mem = cache_mem_v.reshape(DEPTH, DEC_BATCH, MEM_LEN * MEM_HEADS, MEM_HEAD_DIM)
    lg = jnp.log1p(-jnp.exp2(-5.0 - jnp.arange(RET_HEADS, dtype=F32)))

    x_parts = (x_prompt.reshape(MP, D_MODEL), x_sample.reshape(MS, D_MODEL))
    h = _stacked_prenorm(*x_parts, vec(norm_mix_pre, 0))
    outs = {}
    for layer in range(DEPTH):
        j = layer // 2
        if layer % 2 == 0:
            (q, kv, u, zv, zv_s), (wu_bf, wd_bf, wout_bf, wq_bf, wo_bf) = _ab_in_proj(
                h, w_ab_in[j], gmlp_ln_g[j].reshape(1, GMLP_WIDTH), gmlp_ln_b[j].reshape(1, GMLP_WIDTH),
                (swa_c, swa_s1, swa_s2), w_ffn_up, w_ffn_down, w_ab_out, w_mem_q, w_mem_o, layer, j)
            cat = _ab_mix_prompt(q, kv, u, zv, swa_sinks[j], gmlp_w_s[j], gmlp_b_s[j])
            cat, nk_s, nv_s = _ab_mix_sample(
                cat, q, kv, u, zv_s, cache_swa_k[j].reshape(DEC_BATCH, WINDOW, SWA_KV_W),
                cache_swa_v[j].reshape(DEC_BATCH, WINDOW, SWA_KV_W), swa_sinks[j], gmlp_w_s[j], gmlp_b_s[j])
            kv_p = jnp.stack([kv[(b + 1) * SEQ - WINDOW:(b + 1) * SEQ] for b in range(BATCH)])
            kv_p = kv_p.reshape(BATCH, WINDOW, 2, SWA_KV_HEADS, SWA_HEAD_DIM)
            outs.setdefault('swa_k_p', []).append(kv_p[:, :, 0])
            outs.setdefault('swa_v_p', []).append(kv_p[:, :, 1])
            outs.setdefault('swa_k_s', []).append(nk_s.reshape(DEC_BATCH, WINDOW, SWA_KV_HEADS, SWA_HEAD_DIM))
            outs.setdefault('swa_v_s', []).append(nv_s.reshape(DEC_BATCH, WINDOW, SWA_KV_HEADS, SWA_HEAD_DIM))
            outs.setdefault('gmlp_v_s', []).append(zv_s.reshape(DEC_BATCH, DEC_SEQ, GMLP_WIDTH))
            mixed, mix_name = (cat,), "ab_out"
        else:
            (zqk, zv, zg), (wu_bf, wd_bf, wout_bf) = _ret_in_proj(
                h, w_ret_in[j], (ret_c, ret_s), w_ffn_up, w_ffn_down, w_ret_out, layer, j)
            y_p, s_p, y_s, s_s = _retention(zqk, zv, zg, state_ret[j], lg)
            outs.setdefault('ret_p', []).append(s_p)
            outs.setdefault('ret_s', []).append(s_s)
            mixed, mix_name = (y_p, y_s), "ret_out"
        x, h = _out_proj(mixed, wout_bf, x_parts, vec(norm_mix_post, layer), vec(norm_x_pre, layer), mix_name)
        x_parts, h_parts = (x,), (h,)
        g_post = vec(norm_x_post, layer)
        g_next = vec(norm_ffn_pre, layer)
        x, h = _mem_attention(h_parts, x_parts, wq_bf, wo_bf, mem_k_b, mem_v_b, ck_mem, cv_mem, layer,
                              g_post, g_next)
        g_next = vec(norm_mix_pre, layer + 1) if layer + 1 < DEPTH else None
        x, h = _ffn(h, wu_bf, wd_bf, x, vec(norm_ffn_post, layer), g_next)
        x_parts = (x,)

    y_prompt, y_sample = x, h
    stack = lambda name: jnp.stack(outs[name])
    mem_shape = (DEPTH, BATCH, MEM_LEN, MEM_HEADS, MEM_HEAD_DIM)
    return (y_prompt.reshape(BATCH, SEQ, D_MODEL), y_sample.reshape(DEC_BATCH, DEC_SEQ, D_MODEL),
            stack('swa_k_p'), stack('swa_v_p'), stack('swa_k_s'), stack('swa_v_s'), stack('gmlp_v_s'),
            stack('ret_p'), stack('ret_s'), mem_k.reshape(mem_shape), mem_v.reshape(mem_shape))
```

```python
import functools
import math

import jax
import jax.numpy as jnp
from jax import lax
from jax.experimental import pallas as pl
from jax.experimental.pallas import tpu as pltpu

F32 = jnp.float32
BF16 = jnp.bfloat16

D_MODEL = 2048
BATCH = 2
SEQ = 4096
DEPTH = 2
DEC_BATCH = 128
DEC_SEQ = 8
PAST_LEN = 8192

SWA_HEADS = 16
SWA_KV_HEADS = 4
SWA_HEAD_DIM = 64
SWA_GROUP = SWA_HEADS // SWA_KV_HEADS
WINDOW = 128
ROPE_THETA = 500000.0
ROPE_DIM = SWA_HEAD_DIM // 4
ROPE_HALF = ROPE_DIM // 2

GMLP_GROUPS = 4
GMLP_WIDTH = 1024
GMLP_GROUP_DIM = GMLP_WIDTH // GMLP_GROUPS
GMLP_CHUNK = 128

SWA_Q_W = SWA_HEADS * SWA_HEAD_DIM
SWA_KV_W = SWA_KV_HEADS * SWA_HEAD_DIM
AB_IN = SWA_Q_W + 2 * SWA_KV_W + 2 * GMLP_WIDTH
AB_OUT = SWA_Q_W + GMLP_WIDTH

RET_HEADS = 8
RET_KEY_DIM = D_MODEL // RET_HEADS
RET_VAL_DIM = 2 * RET_KEY_DIM
RET_CHUNK = 128
RET_THETA = 10000.0
RET_QK_W = RET_HEADS * RET_KEY_DIM
RET_V_W = RET_HEADS * RET_VAL_DIM
RET_IN = 2 * RET_QK_W + 2 * RET_V_W

MEM_LEN = 256
MEM_HEADS = 4
MEM_HEAD_DIM = 128
MEM_W = MEM_HEADS * MEM_HEAD_DIM

D_FF = 4 * D_MODEL
EPS = 1e-6
NEG_INF = -1e30

MP = BATCH * SEQ
MS = DEC_BATCH * DEC_SEQ
M = MP + MS
LANES = 128
MIB = 1024 * 1024

TM_IN = 1024
TAB_ROWS = SEQ + MS
TM_OUT = 512


def _cparams(sem, vmem_mib):
    return pltpu.CompilerParams(dimension_semantics=sem, vmem_limit_bytes=vmem_mib * MIB)


def _rms(x, g):
    return x * lax.rsqrt(jnp.mean(x * x, axis=-1, keepdims=True) + EPS) * g


def _residual(y, x, g_post, g_next):
    xn = x + _rms(y, g_post)
    return xn, _rms(xn, g_next).astype(BF16)


def _rope_table_kernel(sc_ref, s1_ref, s2_ref, rc_ref, rs_ref):
    tr = sc_ref.shape[0]
    r = lax.broadcasted_iota(jnp.int32, (tr, LANES), 0) + pl.program_id(0) * tr
    pos = jnp.where(r < SEQ, r, PAST_LEN + ((r - SEQ) & (DEC_SEQ - 1))).astype(F32)
    lane = lax.broadcasted_iota(jnp.int32, (tr, LANES), 1)
    d = lane & (SWA_HEAD_DIM - 1)
    inv = jnp.exp((d & (ROPE_HALF - 1)).astype(F32) * (-math.log(ROPE_THETA) / ROPE_HALF))
    ang = pos * inv
    c, s = jnp.cos(ang), jnp.sin(ang)
    sc_ref[...] = jnp.where(d < ROPE_DIM, c, 1.0)
    s1_ref[...] = jnp.where(d < ROPE_HALF, -s, 0.0)
    s2_ref[...] = jnp.where((d >= ROPE_HALF) & (d < ROPE_DIM), s, 0.0)
    inv_r = jnp.exp(lane.astype(F32) * (-math.log(RET_THETA) / LANES))
    ang_r = pos * inv_r
    rc_ref[...] = jnp.cos(ang_r)
    rs_ref[...] = jnp.sin(ang_r)


def _rope_tables():
    tr = 512
    spec = pl.BlockSpec((tr, LANES), lambda i: (i, 0))
    shp = jax.ShapeDtypeStruct((TAB_ROWS, LANES), F32)
    return pl.pallas_call(
        _rope_table_kernel, grid=(TAB_ROWS // tr,), out_specs=[spec] * 5, out_shape=[shp] * 5,
        compiler_params=_cparams(("arbitrary",), 16), name="rope_tables")()


def _norm_kernel(xp_ref, xs_ref, g_ref, w_ref, h_ref, wbf_ref):
    def emit(src_ref):
        h_ref[...] = _rms(src_ref[...], g_ref[...]).astype(BF16)

    is_prompt = pl.program_id(0) < MP // h_ref.shape[0]
    pl.when(is_prompt)(lambda: emit(xp_ref))
    pl.when(jnp.logical_not(is_prompt))(lambda: emit(xs_ref))
    wbf_ref[...] = w_ref[...].astype(BF16)


def _stacked_prenorm(x_prompt, x_sample, g, w, layer):
    tm = TM_OUT
    w_src, w_in_spec, w_shape, w_out_spec = _row_cast_job(w, layer, MP // tm)
    return pl.pallas_call(
        _norm_kernel, grid=(M // tm,),
        in_specs=[pl.BlockSpec((tm, D_MODEL), lambda i: (jnp.minimum(i, MP // tm - 1), 0)),
                  pl.BlockSpec((tm, D_MODEL), lambda i: (jnp.maximum(i - MP // tm, 0), 0)),
                  pl.BlockSpec((1, D_MODEL), lambda i: (0, 0)), w_in_spec],
        out_specs=[pl.BlockSpec((tm, D_MODEL), lambda i: (i, 0)), w_out_spec],
        out_shape=[jax.ShapeDtypeStruct((M, D_MODEL), BF16), w_shape],
        compiler_params=_cparams(("arbitrary",), 40), name="prenorm",
    )(x_prompt, x_sample, g, w_src)


W_SUB = 512


def _inproj_kernel(h_ref, *refs, n_sub, n_side, n_cast, epilogue):
    w_refs, refs = refs[:n_sub], refs[n_sub:]
    side, refs = refs[:n_side], refs[n_side:]
    cast_src, refs = refs[:n_cast], refs[n_cast:]
    wbf_ref = refs[-1]
    outs, cast_dst = refs[:len(refs) - 1 - n_cast], refs[len(refs) - 1 - n_cast:-1]

    @pl.when(pl.program_id(1) == 0)
    def _():
        for s, w_ref in enumerate(w_refs):
            wbf_ref[:, s * W_SUB:(s + 1) * W_SUB] = w_ref[...].astype(BF16)

    acc = jnp.dot(h_ref[...], wbf_ref[...], preferred_element_type=F32)
    epilogue(acc, side, outs)
    for src, dst in zip(cast_src, cast_dst):
        dst[...] = src[...].astype(BF16)


def _tab_index(j, i):
    return (jnp.where(i < MP // TM_IN, i % (SEQ // TM_IN), SEQ // TM_IN), 0)


CAST_STEPS = MP // TM_IN


def _cast_job(w, layer, nrb, ncb, block_of):
    rows, cols = w.shape[1:]
    blk = (rows // nrb, cols // ncb)
    index = lambda j, i: block_of(j, jnp.minimum(i, CAST_STEPS - 1))
    return (w, pl.BlockSpec((None,) + blk, lambda j, i: (layer,) + tuple(index(j, i))),
            jax.ShapeDtypeStruct((rows, cols), BF16), pl.BlockSpec(blk, index))


def _inproj(h, w, col0, ncols, tn, epilogue, side, side_specs, out_shapes, out_specs, name, vmem_mib,
            casts=()):
    tm = TM_IN
    k = h.shape[1]
    n_sub = tn // W_SUB
    w_specs = [pl.BlockSpec((k, W_SUB), lambda j, i, s=s: (0, col0 // W_SUB + j * n_sub + s))
               for s in range(n_sub)]
    as_list = lambda v: list(v) if isinstance(v, (list, tuple)) else [v]
    outs = pl.pallas_call(
        functools.partial(_inproj_kernel, n_sub=n_sub, n_side=len(side), n_cast=len(casts), epilogue=epilogue),
        grid=(ncols // tn, M // tm),
        in_specs=([pl.BlockSpec((tm, k), lambda j, i: (i, 0))] + w_specs + side_specs
                  + [c[1] for c in casts]),
        out_specs=as_list(out_specs) + [c[3] for c in casts],
        out_shape=as_list(out_shapes) + [c[2] for c in casts],
        scratch_shapes=[pltpu.VMEM((k, tn), BF16)],
        compiler_params=_cparams(("arbitrary", "arbitrary"), vmem_mib), name=name,
    )(h, *([w] * n_sub), *side, *[c[0] for c in casts])
    n_main = len(outs) - len(casts)
    main = outs[0] if n_main == 1 else tuple(outs[:n_main])
    return (main, tuple(outs[n_main:])) if casts else main


def _swa_rope(x, c, s1, s2):
    return x * c + pltpu.roll(x, LANES - ROPE_HALF, 1) * s1 + pltpu.roll(x, ROPE_HALF, 1) * s2


def _epi_swa_q(acc, side, outs):
    c, s1, s2 = (t[...] for t in side)
    (q_ref,) = outs
    for cc in range(acc.shape[1] // LANES):
        sl = slice(cc * LANES, (cc + 1) * LANES)
        q_ref[:, sl] = (_swa_rope(acc[:, sl], c, s1, s2) * (SWA_HEAD_DIM ** -0.5)).astype(BF16)


def _epi_swa_kv(acc, side, outs):
    c, s1, s2 = (t[...] for t in side)
    (kv_ref,) = outs
    for cc in range(SWA_KV_W // LANES):
        sl = slice(cc * LANES, (cc + 1) * LANES)
        kv_ref[:, sl] = _swa_rope(acc[:, sl], c, s1, s2)
    kv_ref[:, SWA_KV_W:] = acc[:, SWA_KV_W:]


def _epi_gelu(acc, side, outs):
    outs[0][...] = jax.nn.gelu(acc).astype(BF16)


def _epi_ret_qk(acc, side, outs):
    cos, sin = (t[...] for t in side)
    (z_ref,) = outs
    tn = acc.shape[1]
    scale = jnp.where(pl.program_id(0) < RET_QK_W // tn, 1.0, RET_KEY_DIM ** -0.5)
    for hh in range(tn // RET_KEY_DIM):
        c0 = hh * RET_KEY_DIM
        x1, x2 = acc[:, c0:c0 + LANES], acc[:, c0 + LANES:c0 + 2 * LANES]
        z_ref[hh, :, :LANES] = ((x1 * cos - x2 * sin) * scale).astype(BF16)
        z_ref[hh, :, LANES:] = ((x2 * cos + x1 * sin) * scale).astype(BF16)


def _per_head(fn):
    def epilogue(acc, side, outs):
        (z_ref,) = outs
        for hh in range(z_ref.shape[0]):
            z_ref[hh] = fn(acc[:, hh * RET_VAL_DIM:(hh + 1) * RET_VAL_DIM]).astype(BF16)
    return epilogue


_epi_cast = _per_head(lambda a: a)
_epi_silu = _per_head(lambda a: a / (1.0 + jnp.exp(-a)))


AB_TM = 256


def _ab_in_kernel(h_ref, w_ref, sc_ref, s1_ref, s2_ref, lng_ref, lnb_ref, *refs, n_cast):
    cast_src, (q_ref, kv_ref, u_ref, zv_ref, zvs_ref), cast_dst = refs[:n_cast], refs[n_cast:n_cast + 5], refs[n_cast + 5:]
    h = h_ref[...]
    tabs = (sc_ref, s1_ref, s2_ref)
    seg = lambda c0, n: jnp.dot(h, w_ref[:, c0:c0 + n], preferred_element_type=F32)
    _epi_swa_q(seg(0, SWA_Q_W), tabs, (q_ref,))
    _epi_swa_kv(seg(SWA_Q_W, 2 * SWA_KV_W), tabs, (kv_ref,))
    _epi_gelu(seg(SWA_Q_W + 2 * SWA_KV_W, GMLP_WIDTH), (), (u_ref,))
    a = jax.nn.gelu(seg(AB_IN - GMLP_WIDTH, GMLP_WIDTH))
    xc = a - jnp.mean(a, axis=-1, keepdims=True)
    y = xc * lax.rsqrt(jnp.mean(xc * xc, axis=-1, keepdims=True) + EPS) * lng_ref[...] + lnb_ref[...]
    zv_ref[...] = y.astype(BF16)

    @pl.when(pl.program_id(0) >= MP // AB_TM)
    def _():
        zvs_ref[...] = y

    for src, dst in zip(cast_src, cast_dst):
        dst[...] = src[...].astype(BF16)


def _row_cast_job(w, layer, n_blocks):
    rows, cols = w.shape[1:]
    blk = (rows // n_blocks, cols)
    index = lambda i: (jnp.minimum(i, n_blocks - 1), 0)
    return (w, pl.BlockSpec((None,) + blk, lambda i: (layer,) + index(i)),
            jax.ShapeDtypeStruct((rows, cols), BF16), pl.BlockSpec(blk, index))


def _ab_in_proj(h, w_in_bf, ln_g, ln_b, tabs, w_up, w_down, w_out, w_mem_q, w_mem_o, layer, j):
    tm = AB_TM
    n_p = MP // tm
    stacked = lambda w: w.reshape(1, w.shape[0] * w.shape[1], w.shape[2])
    casts = [_row_cast_job(w_up, layer, n_p), _row_cast_job(w_down, layer, n_p), _row_cast_job(w_out, j, n_p),
             _row_cast_job(stacked(w_mem_q), 0, n_p), _row_cast_job(stacked(w_mem_o), 0, n_p)]
    tab_rows = SEQ // tm
    tab_spec = pl.BlockSpec((tm, LANES), lambda i: (jnp.where(i < n_p, i % tab_rows, tab_rows + i - n_p), 0))
    vec_spec = pl.BlockSpec((1, GMLP_WIDTH), lambda i: (0, 0))
    row_out = lambda n: pl.BlockSpec((tm, n), lambda i: (i, 0))
    outs = pl.pallas_call(
        functools.partial(_ab_in_kernel, n_cast=len(casts)), grid=(M // tm,),
        in_specs=([pl.BlockSpec((tm, D_MODEL), lambda i: (i, 0)),
                   pl.BlockSpec((D_MODEL, AB_IN), lambda i: (0, 0), pipeline_mode=pl.Buffered(1))]
                  + [tab_spec] * 3 + [vec_spec] * 2 + [c[1] for c in casts]),
        out_specs=[row_out(SWA_Q_W), row_out(2 * SWA_KV_W), row_out(GMLP_WIDTH), row_out(GMLP_WIDTH),
                   pl.BlockSpec((tm, GMLP_WIDTH), lambda i: (jnp.maximum(i - n_p, 0), 0))]
                  + [c[3] for c in casts],
        out_shape=[jax.ShapeDtypeStruct((M, SWA_Q_W), BF16), jax.ShapeDtypeStruct((M, 2 * SWA_KV_W), F32),
                   jax.ShapeDtypeStruct((M, GMLP_WIDTH), BF16), jax.ShapeDtypeStruct((M, GMLP_WIDTH), BF16),
                   jax.ShapeDtypeStruct((MS, GMLP_WIDTH), F32)] + [c[2] for c in casts],
        compiler_params=_cparams(("arbitrary",), 52), name="ab_in",
    )(h, w_in_bf, *tabs, ln_g, ln_b, *[c[0] for c in casts])
    return tuple(outs[:5]), tuple(outs[5:])


def _ret_in_proj(h, w_in, tabs, w_up, w_down, w_out, layer, j):
    rc, rs = tabs
    tn = 1024
    tab_spec = pl.BlockSpec((TM_IN, LANES), _tab_index)
    by_tile = lambda j, i: (j, i)
    head_major = jax.ShapeDtypeStruct((RET_HEADS, M, 2 * RET_KEY_DIM), BF16)
    seg = lambda col0, ncols, epi, side, specs, out_spec, name, w, lead: _inproj(
        h, w_in, col0, ncols, tn, epi, side, specs, head_major, out_spec, name, 56,
        casts=[_cast_job(w, lead, ncols // tn, CAST_STEPS, by_tile)])
    qk_heads = tn // RET_KEY_DIM
    q_tiles = RET_QK_W // tn
    qk_spec = pl.BlockSpec((qk_heads, TM_IN, RET_KEY_DIM), lambda j, i: (j % q_tiles, i, j // q_tiles))
    v_spec = pl.BlockSpec((tn // RET_VAL_DIM, TM_IN, RET_VAL_DIM), lambda j, i: (j, i, 0))
    zqk, (wu_bf,) = seg(0, 2 * RET_QK_W, _epi_ret_qk, [rc, rs], [tab_spec] * 2, qk_spec, "ret_in_qk",
                        w_up, layer)
    zv, (wd_bf,) = seg(2 * RET_QK_W, RET_V_W, _epi_cast, [], [], v_spec, "ret_in_v", w_down, layer)
    zg, (wout_bf,) = seg(2 * RET_QK_W + RET_V_W, RET_V_W, _epi_silu, [], [], v_spec, "ret_in_g", w_out, j)
    return (zqk, zv, zg), (wu_bf, wd_bf, wout_bf)


AB_ROWS = 512


def _sink_softmax(s, sink):
    m = jnp.maximum(jnp.max(s, axis=-1, keepdims=True), sink)
    p = jnp.exp(s - m)
    den = jnp.sum(p, axis=-1, keepdims=True) + jnp.exp(sink - m)
    return p, den


def _ab_prompt_kernel(sink_ref, q_ref, kv_ref, kvp_ref, u_ref, zv_ref, ws_ref, bs_ref, cat_ref,
                      kd_ref, vd_ref):
    r = pl.program_id(1)
    low = lax.broadcasted_iota(jnp.int32, (1, LANES), 1) < SWA_HEAD_DIM

    def both_halves(x, half):
        sw = pltpu.roll(x, SWA_HEAD_DIM, 1)
        return (jnp.where(low, x, sw) if half == 0 else jnp.where(low, sw, x)).astype(BF16)

    for kh in range(SWA_KV_HEADS):
        kcol = slice((kh // 2) * LANES, (kh // 2 + 1) * LANES)
        vcol = slice(SWA_KV_W + (kh // 2) * LANES, SWA_KV_W + (kh // 2 + 1) * LANES)
        kd_ref[kh, :WINDOW, :] = both_halves(kvp_ref[:, kcol], kh % 2)
        kd_ref[kh, WINDOW:, :] = both_halves(kv_ref[:, kcol], kh % 2)
        vd_ref[kh, :WINDOW, :] = both_halves(kvp_ref[:, vcol], kh % 2)
        vd_ref[kh, WINDOW:, :] = both_halves(kv_ref[:, vcol], kh % 2)

    t = lax.broadcasted_iota(jnp.int32, (WINDOW, 2 * WINDOW), 0)
    sk = lax.broadcasted_iota(jnp.int32, (WINDOW, 2 * WINDOW), 1)
    band = (sk <= t + WINDOW) & (sk > t)
    sink_slot = sk == t
    tril = (lax.broadcasted_iota(jnp.int32, (GMLP_CHUNK, GMLP_CHUNK), 0)
            >= lax.broadcasted_iota(jnp.int32, (GMLP_CHUNK, GMLP_CHUNK), 1))

    for sb in range(AB_ROWS // WINDOW):
        rs = slice(sb * WINDOW, (sb + 1) * WINDOW)
        ks = slice(sb * WINDOW, sb * WINDOW + 2 * WINDOW)
        valid = band & ((sk >= WINDOW) | (r * (AB_ROWS // WINDOW) + sb > 0))
        for kh in range(SWA_KV_HEADS):
            qm = []
            for g in range(SWA_GROUP):
                hd = kh * SWA_GROUP + g
                qb = q_ref[rs, (hd // 2) * LANES:(hd // 2 + 1) * LANES]
                qm.append(jnp.where(low if hd % 2 == 0 else jnp.logical_not(low), qb, jnp.zeros_like(qb)))
            s = lax.dot_general(jnp.concatenate(qm, axis=0), kd_ref[kh, ks, :], (((1,), (1,)), ((), ())),
                                preferred_element_type=F32)
            pv, den = [], []
            for g in range(SWA_GROUP):
                sg = s[g * WINDOW:(g + 1) * WINDOW]
                sg = jnp.where(valid, sg, jnp.where(sink_slot, sink_ref[kh * SWA_GROUP + g], NEG_INF))
                p = jnp.exp(sg - jnp.max(sg, axis=-1, keepdims=True))
                den.append(jnp.sum(p, axis=-1, keepdims=True))
                pv.append(jnp.where(sink_slot, 0.0, p).astype(BF16))
            o = jnp.dot(jnp.concatenate(pv, axis=0), vd_ref[kh, ks, :], preferred_element_type=F32)
            for pr in range(SWA_GROUP // 2):
                lo = o[(2 * pr) * WINDOW:(2 * pr + 1) * WINDOW] / den[2 * pr]
                hi = o[(2 * pr + 1) * WINDOW:(2 * pr + 2) * WINDOW] / den[2 * pr + 1]
                c0 = (kh * (SWA_GROUP // 2) + pr) * LANES
                cat_ref[rs, c0:c0 + LANES] = jnp.where(low, lo, hi).astype(BF16)
        zb = zv_ref[rs, :]
        ub = u_ref[rs, :]
        for g in range(GMLP_GROUPS):
            gs = slice(g * GMLP_GROUP_DIM, (g + 1) * GMLP_GROUP_DIM)
            w = jnp.where(tril, ws_ref[g], 0.0).astype(BF16)
            mixed = jnp.dot(w, zb[:, gs], preferred_element_type=F32) + bs_ref[g]
            cat_ref[rs, SWA_Q_W + g * GMLP_GROUP_DIM:SWA_Q_W + (g + 1) * GMLP_GROUP_DIM] = (
                ub[:, gs].astype(F32) * mixed).astype(BF16)


def _ab_mix_prompt(q, kv, u, zv, sinks, w_s, b_s):
    nr = SEQ // AB_ROWS
    sub = AB_ROWS // WINDOW
    row = lambda b, r: (b * nr + r, 0)
    return pl.pallas_call(
        _ab_prompt_kernel, grid=(BATCH, nr),
        in_specs=[pl.BlockSpec(memory_space=pltpu.SMEM),
                  pl.BlockSpec((AB_ROWS, SWA_Q_W), row),
                  pl.BlockSpec((AB_ROWS, 2 * SWA_KV_W), row),
                  pl.BlockSpec((WINDOW, 2 * SWA_KV_W), lambda b, r: (jnp.maximum((b * nr + r) * sub - 1, 0), 0)),
                  pl.BlockSpec((AB_ROWS, GMLP_WIDTH), row),
                  pl.BlockSpec((AB_ROWS, GMLP_WIDTH), row),
                  pl.BlockSpec((GMLP_GROUPS, GMLP_CHUNK, GMLP_CHUNK), lambda b, r: (0, 0, 0)),
                  pl.BlockSpec((GMLP_GROUPS, GMLP_CHUNK, 1), lambda b, r: (0, 0, 0))],
        out_specs=pl.BlockSpec((AB_ROWS, AB_OUT), row),
        out_shape=jax.ShapeDtypeStruct((M, AB_OUT), BF16),
        scratch_shapes=[pltpu.VMEM((SWA_KV_HEADS, WINDOW + AB_ROWS, LANES), BF16)] * 2,
        compiler_params=_cparams(("arbitrary", "arbitrary"), 40), name="ab_mix_prompt",
    )(sinks, q, kv, kv, u, zv, w_s, b_s.reshape(GMLP_GROUPS, GMLP_CHUNK, 1))


AB_NB = 16
AB_KEYS = 2 * WINDOW


def _ab_sample_kernel(sink_ref, q_ref, kv_ref, u_ref, zv_ref, ck_ref, cv_ref, ws_ref, bs_ref, cat_in_ref,
                      cat_ref, nk_ref, nv_ref):
    del cat_in_ref
    nb, t = AB_NB, DEC_SEQ
    kv3 = kv_ref[...].reshape(nb, t, 2 * SWA_KV_W)
    kn, vn = kv3[:, :, :SWA_KV_W], kv3[:, :, SWA_KV_W:]
    ck, cv = ck_ref[...], cv_ref[...]
    nk_ref[:, :WINDOW - t, :] = ck[:, t:, :]
    nk_ref[:, WINDOW - t:, :] = kn
    nv_ref[:, :WINDOW - t, :] = cv[:, t:, :]
    nv_ref[:, WINDOW - t:, :] = vn
    pad = jnp.zeros((nb, AB_KEYS - WINDOW - t, SWA_KV_W), F32)
    kall = jnp.concatenate([ck, kn, pad], axis=1).astype(BF16)
    vall = jnp.concatenate([cv, vn, pad], axis=1).astype(BF16)

    rows = SWA_GROUP * t
    tq = lax.broadcasted_iota(jnp.int32, (rows, AB_KEYS), 0) & (t - 1)
    sk = lax.broadcasted_iota(jnp.int32, (rows, AB_KEYS), 1)
    valid = ((sk < WINDOW) & (sk > tq)) | ((sk >= WINDOW) & (sk - WINDOW <= tq))

    q3 = q_ref[...].astype(F32).reshape(nb, t, SWA_Q_W)
    pieces = []
    for kh in range(SWA_KV_HEADS):
        heads = [kh * SWA_GROUP + g for g in range(SWA_GROUP)]
        qs = jnp.concatenate([q3[:, :, hd * SWA_HEAD_DIM:(hd + 1) * SWA_HEAD_DIM] for hd in heads],
                             axis=1).astype(BF16)
        cs = slice(kh * SWA_HEAD_DIM, (kh + 1) * SWA_HEAD_DIM)
        s = jnp.einsum('nqd,nkd->nqk', qs, kall[:, :, cs], preferred_element_type=F32)
        s = jnp.where(valid[None], s, NEG_INF)
        sink = jnp.concatenate([jnp.full((t, 1), sink_ref[hd], F32) for hd in heads], axis=0)[None]
        p, den = _sink_softmax(s, sink)
        o = jnp.einsum('nqk,nkd->nqd', p.astype(BF16), vall[:, :, cs], preferred_element_type=F32) / den
        pieces += [o[:, g * t:(g + 1) * t, :] for g in range(SWA_GROUP)]
    attn = jnp.concatenate(pieces, axis=2).reshape(nb * t, SWA_Q_W)
    cat_ref[:, :SWA_Q_W] = attn.astype(BF16)

    zv3 = zv_ref[...].reshape(nb, t, GMLP_WIDTH)
    u3 = u_ref[...].astype(F32).reshape(nb, t, GMLP_WIDTH)
    irow = lax.broadcasted_iota(jnp.int32, (t, GMLP_GROUP_DIM), 0)
    for g in range(GMLP_GROUPS):
        gs = slice(g * GMLP_GROUP_DIM, (g + 1) * GMLP_GROUP_DIM)
        zg = zv3[:, :, gs]
        mixed = jnp.zeros((nb, t, GMLP_GROUP_DIM), F32) + bs_ref[g][None]
        for j in range(t):
            wcol = jnp.where(irow >= j, ws_ref[g, j], 0.0)
            mixed = mixed + zg[:, j:j + 1, :] * wcol[None]
        gate = (u3[:, :, gs] * mixed).reshape(nb * t, GMLP_GROUP_DIM)
        cat_ref[:, SWA_Q_W + g * GMLP_GROUP_DIM:SWA_Q_W + (g + 1) * GMLP_GROUP_DIM] = gate.astype(BF16)


def _ab_mix_sample(cat, q, kv, u, zv_s, cache_k, cache_v, sinks, w_s, b_s):
    nb, t = AB_NB, DEC_SEQ
    rows = nb * t
    off = MP // rows
    srow = lambda i: (off + i, 0)
    wt = jnp.broadcast_to(jnp.swapaxes(w_s[:, :t, :t], 1, 2)[..., None], (GMLP_GROUPS, t, t, GMLP_GROUP_DIM))
    bt = jnp.broadcast_to(b_s[:, :t, None], (GMLP_GROUPS, t, GMLP_GROUP_DIM))
    cspec = pl.BlockSpec((nb, WINDOW, SWA_KV_W), lambda i: (i, 0, 0))
    cshape = jax.ShapeDtypeStruct((DEC_BATCH, WINDOW, SWA_KV_W), F32)
    return pl.pallas_call(
        _ab_sample_kernel, grid=(DEC_BATCH // nb,),
        in_specs=[pl.BlockSpec(memory_space=pltpu.SMEM),
                  pl.BlockSpec((rows, SWA_Q_W), srow),
                  pl.BlockSpec((rows, 2 * SWA_KV_W), srow),
                  pl.BlockSpec((rows, GMLP_WIDTH), srow),
                  pl.BlockSpec((rows, GMLP_WIDTH), lambda i: (i, 0)),
                  cspec, cspec,
                  pl.BlockSpec((GMLP_GROUPS, t, t, GMLP_GROUP_DIM), lambda i: (0, 0, 0, 0)),
                  pl.BlockSpec((GMLP_GROUPS, t, GMLP_GROUP_DIM), lambda i: (0, 0, 0)),
                  pl.BlockSpec(memory_space=pl.ANY)],
        out_specs=[pl.BlockSpec((rows, AB_OUT), srow), cspec, cspec],
        out_shape=[jax.ShapeDtypeStruct((M, AB_OUT), BF16), cshape, cshape],
        input_output_aliases={9: 0},
        compiler_params=_cparams(("arbitrary",), 48), name="ab_mix_sample",
    )(sinks, q, kv, u, zv_s, cache_k, cache_v, wt, bt, cat)


RET_ROWS = 1024
RET_BLOCK = 256


def _ret_prompt_part(lg, first_run, qk_ref, v_ref, g_ref, s_ref):
    L = RET_BLOCK

    @pl.when(first_run)
    def _():
        s_ref[...] = jnp.zeros_like(s_ref)

    diff = (lax.broadcasted_iota(jnp.int32, (L, L), 0) - lax.broadcasted_iota(jnp.int32, (L, L), 1)).astype(F32)
    dmat = jnp.where(diff >= 0, jnp.exp(lg * jnp.maximum(diff, 0.0)), 0.0)
    idx = lax.broadcasted_iota(jnp.int32, (L, 1), 0).astype(F32)
    q_dec = jnp.exp(lg * (idx + 1.0))
    k_dec = jnp.exp(lg * (L - 1.0 - idx))
    s_dec = jnp.exp(lg * jnp.full((1, 1), float(L), F32))

    ys = []
    for ci in range(RET_ROWS // L):
        rs = slice(ci * L, (ci + 1) * L)
        q, k, v = qk_ref[rs, :RET_KEY_DIM], qk_ref[rs, RET_KEY_DIM:], v_ref[rs, :]
        att = lax.dot_general(q, k, (((1,), (1,)), ((), ())), preferred_element_type=F32) * dmat
        s0 = s_ref[...]
        o = (jnp.dot(att.astype(BF16), v, preferred_element_type=F32)
             + jnp.dot(q, s0.astype(BF16), preferred_element_type=F32) * q_dec)
        kd = (k.astype(F32) * k_dec).astype(BF16)
        s_ref[...] = s0 * s_dec + lax.dot_general(kd, v, (((0,), (0,)), ((), ())), preferred_element_type=F32)
        on = o * lax.rsqrt(jnp.mean(o * o, axis=-1, keepdims=True) + EPS)
        ys.append((on * g_ref[rs, :].astype(F32)).astype(BF16))
    return jnp.concatenate(ys, axis=0)


RET_NB = DEC_BATCH // (BATCH * (SEQ // RET_ROWS))


def _ret_sample_part(lg, qk_ref, v_ref, g_ref, s0_ref, s1_ref):
    nb, t = RET_NB, DEC_SEQ
    rows = nb * t
    q, k, v = qk_ref[:, :RET_KEY_DIM], qk_ref[:, RET_KEY_DIM:], v_ref[...]
    ri = lax.broadcasted_iota(jnp.int32, (rows, rows), 0)
    ci = lax.broadcasted_iota(jnp.int32, (rows, rows), 1)
    d = ((ri & (t - 1)) - (ci & (t - 1))).astype(F32)
    same = (ri // t) == (ci // t)
    dmat = jnp.where(same & (d >= 0), jnp.exp(lg * jnp.maximum(d, 0.0)), 0.0)
    att = lax.dot_general(q, k, (((1,), (1,)), ((), ())), preferred_element_type=F32) * dmat
    o_intra = jnp.dot(att.astype(BF16), v, preferred_element_type=F32)

    rcol = lax.broadcasted_iota(jnp.int32, (rows, 1), 0)
    tt = (rcol & (t - 1)).astype(F32)
    q_dec = jnp.exp(lg * (tt + 1.0))
    k_dec = jnp.exp(lg * (t - 1.0 - tt))
    s_dec = jnp.exp(lg * jnp.full((1, 1), float(t), F32))
    kd = (k.astype(F32) * k_dec).astype(BF16)
    qf = q.astype(F32)
    vf = v.astype(F32)
    o_rows = []
    for n in range(nb):
        s0 = s0_ref[n]
        qn = qf[n * t:(n + 1) * t].astype(BF16)
        o_rows.append(jnp.dot(qn, s0.astype(BF16), preferred_element_type=F32))
        vn = jnp.where(rcol // t == n, vf, 0.0).astype(BF16)
        s1_ref[n] = s0 * s_dec + lax.dot_general(kd, vn, (((0,), (0,)), ((), ())), preferred_element_type=F32)
    o = o_intra + jnp.concatenate(o_rows, axis=0) * q_dec
    on = o * lax.rsqrt(jnp.mean(o * o, axis=-1, keepdims=True) + EPS)
    return (on * g_ref[...].astype(F32)).astype(BF16)


def _retention_kernel(lg_ref, qk_ref, v_ref, g_ref, sqk_ref, sv_ref, sg_ref, s0_ref,
                      yp_ref, sfin_ref, ys_ref, s1_ref, s_ref):
    c = pl.program_id(2)
    lg = lg_ref[pl.program_id(1)]
    yp_ref[...] = _ret_prompt_part(lg, c == 0, qk_ref, v_ref, g_ref, s_ref)

    @pl.when(c == pl.num_programs(2) - 1)
    def _():
        sfin_ref[...] = s_ref[...]

    ys_ref[...] = _ret_sample_part(lg, sqk_ref, sv_ref, sg_ref, s0_ref, s1_ref)


def _retention(zqk, zv, zg, state, lg):
    nc = SEQ // RET_ROWS
    srows = RET_NB * DEC_SEQ
    soff = MP // srows
    prow = lambda b, c: b * nc + c
    width = 2 * RET_KEY_DIM
    head_specs = lambda rows, row: [pl.BlockSpec((None, rows, width), lambda b, h, c: (h, row(b, c), 0))] * 3
    sspec = pl.BlockSpec((RET_NB, None, RET_KEY_DIM, RET_VAL_DIM), lambda b, h, c: (prow(b, c), h, 0, 0))
    return pl.pallas_call(
        _retention_kernel, grid=(BATCH, RET_HEADS, nc),
        in_specs=([pl.BlockSpec(memory_space=pltpu.SMEM)] + head_specs(RET_ROWS, prow)
                  + head_specs(srows, lambda b, c: soff + prow(b, c)) + [sspec]),
        out_specs=[pl.BlockSpec((RET_ROWS, RET_VAL_DIM), lambda b, h, c: (prow(b, c), h)),
                   pl.BlockSpec((None, None, RET_KEY_DIM, RET_VAL_DIM), lambda b, h, c: (b, h, 0, 0)),
                   pl.BlockSpec((srows, RET_VAL_DIM), lambda b, h, c: (prow(b, c), h)),
                   sspec],
        out_shape=[jax.ShapeDtypeStruct((MP, RET_V_W), BF16),
                   jax.ShapeDtypeStruct((BATCH, RET_HEADS, RET_KEY_DIM, RET_VAL_DIM), F32),
                   jax.ShapeDtypeStruct((MS, RET_V_W), BF16),
                   jax.ShapeDtypeStruct((DEC_BATCH, RET_HEADS, RET_KEY_DIM, RET_VAL_DIM), F32)],
        scratch_shapes=[pltpu.VMEM((RET_KEY_DIM, RET_VAL_DIM), F32)],
        compiler_params=_cparams(("arbitrary", "arbitrary", "arbitrary"), 56), name="retention",
    )(lg, zqk, zv, zg, zqk, zv, zg, state)


def _zero_on_first(acc_ref, k):
    @pl.when(k == 0)
    def _():
        acc_ref[...] = jnp.zeros_like(acc_ref)


def _outproj_kernel(ap_ref, as_ref, w_ref, xp_ref, xs_ref, gp_ref, gn_ref, xn_ref, hn_ref):
    is_prompt = pl.program_id(0) < MP // xn_ref.shape[0]
    a = jnp.where(is_prompt, ap_ref[...], as_ref[...])
    x = jnp.where(is_prompt, xp_ref[...], xs_ref[...])
    y = jnp.dot(a, w_ref[...], preferred_element_type=F32)
    xn, hn = _residual(y, x, gp_ref[...], gn_ref[...])
    xn_ref[...] = xn
    hn_ref[...] = hn


def _out_proj(a_parts, w_bf, x_parts, g_post, g_next, name):
    kdim = w_bf.shape[0]
    tm = TM_OUT * D_MODEL // kdim
    np_t = MP // tm

    def two_source(parts, width):
        sample_t0 = np_t if len(parts) == 1 else 0
        return [pl.BlockSpec((tm, width), lambda i: (jnp.minimum(i, np_t - 1), 0)),
                pl.BlockSpec((tm, width), lambda i: (sample_t0 + jnp.maximum(i - np_t, 0), 0))]

    row = pl.BlockSpec((tm, D_MODEL), lambda i: (i, 0))
    vec = pl.BlockSpec((1, D_MODEL), lambda i: (0, 0))
    return pl.pallas_call(
        _outproj_kernel, grid=(M // tm,),
        in_specs=(two_source(a_parts, kdim)
                  + [pl.BlockSpec((kdim, D_MODEL), lambda i: (0, 0), pipeline_mode=pl.Buffered(1))]
                  + two_source(x_parts, D_MODEL) + [vec, vec]),
        out_specs=[row, row],
        out_shape=[jax.ShapeDtypeStruct((M, D_MODEL), F32), jax.ShapeDtypeStruct((M, D_MODEL), BF16)],
        compiler_params=_cparams(("arbitrary",), 56), name=name,
    )(a_parts[0], a_parts[-1], w_bf, x_parts[0], x_parts[-1], g_post, g_next)


def _ffn_up_down(h_ref, wu_ref, wd_ref):
    a = jnp.maximum(jnp.dot(h_ref[...], wu_ref[...], preferred_element_type=F32), 0.0)
    return jnp.dot((a * a).astype(BF16), wd_ref[...], preferred_element_type=F32)


def _ffn_kernel(h_ref, wu_ref, wd_ref, x_ref, gp_ref, gn_ref, xn_ref, hn_ref):
    c = pl.program_id(1)
    _zero_on_first(xn_ref, c)
    xn_ref[...] += _ffn_up_down(h_ref, wu_ref, wd_ref)

    @pl.when(c == pl.num_programs(1) - 1)
    def _():
        xn, hn = _residual(xn_ref[...], x_ref[...], gp_ref[...], gn_ref[...])
        xn_ref[...] = xn
        hn_ref[...] = hn


def _ffn_last_kernel(h_ref, wu_ref, wd_ref, x_ref, gp_ref, yp_ref, ys_ref, acc_ref):
    i, c = pl.program_id(0), pl.program_id(1)
    _zero_on_first(acc_ref, c)
    acc_ref[...] += _ffn_up_down(h_ref, wu_ref, wd_ref)
    last = c == pl.num_programs(1) - 1
    is_prompt = i < MP // acc_ref.shape[0]

    @pl.when(last & is_prompt)
    def _():
        yp_ref[...] = x_ref[...] + _rms(acc_ref[...], gp_ref[...])

    @pl.when(last & jnp.logical_not(is_prompt))
    def _():
        ys_ref[...] = x_ref[...] + _rms(acc_ref[...], gp_ref[...])


def _ffn(h, wu_bf, wd_bf, x, g_post, g_next):
    tm, tc = TM_OUT, 1024
    row = pl.BlockSpec((tm, D_MODEL), lambda i, c: (i, 0))
    vec = pl.BlockSpec((1, D_MODEL), lambda i, c: (0, 0))
    w_specs = [pl.BlockSpec((D_MODEL, tc), lambda i, c: (0, c)), pl.BlockSpec((tc, D_MODEL), lambda i, c: (c, 0))]
    grid = (M // tm, D_FF // tc)
    if g_next is not None:
        return pl.pallas_call(
            _ffn_kernel, grid=grid, in_specs=[row] + w_specs + [row, vec, vec], out_specs=[row, row],
            out_shape=[jax.ShapeDtypeStruct((M, D_MODEL), F32), jax.ShapeDtypeStruct((M, D_MODEL), BF16)],
            compiler_params=_cparams(("arbitrary", "arbitrary"), 56), name="ffn",
        )(h, wu_bf, wd_bf, x, g_post, g_next)
    np_t = MP // tm
    return pl.pallas_call(
        _ffn_last_kernel, grid=grid, in_specs=[row] + w_specs + [row, vec],
        out_specs=[pl.BlockSpec((tm, D_MODEL), lambda i, c: (jnp.minimum(i, np_t - 1), 0)),
                   pl.BlockSpec((tm, D_MODEL), lambda i, c: (jnp.maximum(i - np_t, 0), 0))],
        out_shape=[jax.ShapeDtypeStruct((MP, D_MODEL), F32), jax.ShapeDtypeStruct((MS, D_MODEL), F32)],
        scratch_shapes=[pltpu.VMEM((tm, D_MODEL), F32)],
        compiler_params=_cparams(("arbitrary", "arbitrary"), 56), name="ffn_last",
    )(h, wu_bf, wd_bf, x, g_post)


def _mem_kv_kernel(m_ref, g_ref, wk_ref, wv_ref, k_ref, v_ref):
    m = _rms(m_ref[...], g_ref[0]).astype(BF16)
    k_ref[0] = jnp.dot(m, wk_ref[0].astype(BF16), preferred_element_type=F32)
    v_ref[0] = jnp.dot(m, wv_ref[0].astype(BF16), preferred_element_type=F32)


def _mem_kv(mem, g, w_k, w_v):
    rows = BATCH * MEM_LEN
    wspec = pl.BlockSpec((1, D_MODEL, MEM_W), lambda l: (l, 0, 0))
    ospec = pl.BlockSpec((1, rows, MEM_W), lambda l: (l, 0, 0))
    oshape = jax.ShapeDtypeStruct((DEPTH, rows, MEM_W), F32)
    return pl.pallas_call(
        _mem_kv_kernel, grid=(DEPTH,),
        in_specs=[pl.BlockSpec((rows, D_MODEL), lambda l: (0, 0)),
                  pl.BlockSpec((1, 1, D_MODEL), lambda l: (l, 0, 0)), wspec, wspec],
        out_specs=[ospec, ospec], out_shape=[oshape, oshape],
        compiler_params=_cparams(("arbitrary",), 48), name="mem_kv",
    )(mem.reshape(rows, D_MODEL), g.reshape(DEPTH, 1, D_MODEL), w_k, w_v)


def _mem_attend_prompt(q, mk_ref, mv_ref):
    q = q.astype(BF16)
    mk, mv = mk_ref[...].astype(BF16), mv_ref[...].astype(BF16)
    outs = []
    for hh in range(MEM_HEADS):
        cs = slice(hh * MEM_HEAD_DIM, (hh + 1) * MEM_HEAD_DIM)
        s = lax.dot_general(q[:, cs], mk[:, cs], (((1,), (1,)), ((), ())),
                            preferred_element_type=F32) * (MEM_HEAD_DIM ** -0.5)
        p = jnp.exp(s - jnp.max(s, axis=-1, keepdims=True))
        den = jnp.sum(p, axis=-1, keepdims=True)
        outs.append(jnp.dot(p.astype(BF16), mv[:, cs], preferred_element_type=F32) / den)
    return jnp.concatenate(outs, axis=1)


def _mem_attend_sample(q, ck_ref, cv_ref):
    nb, t = ck_ref.shape[0], DEC_SEQ
    qs = jnp.concatenate([q[:, hh * MEM_HEAD_DIM:(hh + 1) * MEM_HEAD_DIM].reshape(nb, t, MEM_HEAD_DIM)
                          for hh in range(MEM_HEADS)], axis=1).astype(BF16)
    s = jnp.einsum('nqd,nkd->nqk', qs, ck_ref[...].astype(BF16),
                   preferred_element_type=F32) * (MEM_HEAD_DIM ** -0.5)
    shape = (MEM_HEADS * t, MEM_LEN * MEM_HEADS)
    own = ((lax.broadcasted_iota(jnp.int32, shape, 1) & (MEM_HEADS - 1))
           == lax.broadcasted_iota(jnp.int32, shape, 0) // t)
    s = jnp.where(own[None], s, NEG_INF)
    p = jnp.exp(s - jnp.max(s, axis=-1, keepdims=True))
    den = jnp.sum(p, axis=-1, keepdims=True)
    o3 = jnp.einsum('nqk,nkd->nqd', p.astype(BF16), cv_ref[...].astype(BF16),
                    preferred_element_type=F32) / den
    return jnp.concatenate([o3[:, hh * t:(hh + 1) * t, :].reshape(nb * t, MEM_HEAD_DIM)
                            for hh in range(MEM_HEADS)], axis=1)


def _mem_kernel(hp_ref, hs_ref, wq_ref, wo_ref, mk_ref, mv_ref, ck_ref, cv_ref, xp_ref, xs_ref, gp_ref, gn_ref,
                xn_ref, hn_ref, os_ref):
    i = pl.program_id(0)
    tm = xn_ref.shape[0]
    n_prompt = MP // tm
    srows = hs_ref.shape[0]
    project = lambda h_ref: jnp.dot(h_ref[...], wq_ref[...], preferred_element_type=F32)

    def finish(o, x_ref):
        y = jnp.dot(o.astype(BF16), wo_ref[...], preferred_element_type=F32)
        xn, hn = _residual(y, x_ref[...], gp_ref[...], gn_ref[...])
        xn_ref[...] = xn
        hn_ref[...] = hn

    @pl.when(i < n_prompt)
    def _():
        os_ref[pl.ds(pl.multiple_of(i * srows, srows), srows), :] = _mem_attend_sample(
            project(hs_ref), ck_ref, cv_ref)
        finish(_mem_attend_prompt(project(hp_ref), mk_ref, mv_ref), xp_ref)

    @pl.when(i >= n_prompt)
    def _():
        finish(os_ref[pl.ds(pl.multiple_of((i - n_prompt) * tm, tm), tm), :], xs_ref)


MEM_TM = 256
MEM_NB = DEC_BATCH // (MP // MEM_TM)


def _mem_attention(h_parts, x_parts, wq_bf, wo_bf, mk, mv, cache_k, cache_v, layer, g_post, g_next):
    tm, nb = MEM_TM, MEM_NB
    srows = nb * DEC_SEQ
    sample_row0 = MP if len(h_parts) == 1 else 0
    per_b = SEQ // tm
    n_p = MP // tm
    pidx = lambda i: jnp.minimum(i, n_p - 1)
    row = pl.BlockSpec((tm, D_MODEL), lambda i: (i, 0))
    vec = pl.BlockSpec((1, D_MODEL), lambda i: (0, 0))
    kvspec = pl.BlockSpec((None, MEM_LEN, MEM_W), lambda i: (layer * BATCH + pidx(i) // per_b, 0, 0))
    cspec = pl.BlockSpec((None, nb, MEM_LEN * MEM_HEADS, MEM_HEAD_DIM), lambda i: (layer, pidx(i), 0, 0))
    resident = lambda shape: pl.BlockSpec(shape, lambda i: (layer, 0), pipeline_mode=pl.Buffered(1))
    return pl.pallas_call(
        _mem_kernel, grid=(M // tm,),
        in_specs=[pl.BlockSpec((tm, D_MODEL), lambda i: (pidx(i), 0)),
                  pl.BlockSpec((srows, D_MODEL), lambda i: (sample_row0 // srows + pidx(i), 0)),
                  resident((D_MODEL, MEM_W)), resident((MEM_W, D_MODEL)), kvspec, kvspec,
                  cspec, cspec,
                  pl.BlockSpec((tm, D_MODEL), lambda i: (pidx(i), 0)),
                  pl.BlockSpec((tm, D_MODEL), lambda i: (sample_row0 // tm + jnp.maximum(i - n_p, 0), 0)),
                  vec, vec],
        out_specs=[row, row],
        out_shape=[jax.ShapeDtypeStruct((M, D_MODEL), F32), jax.ShapeDtypeStruct((M, D_MODEL), BF16)],
        scratch_shapes=[pltpu.VMEM((MS, MEM_W), F32)],
        compiler_params=_cparams(("arbitrary",), 48), name="mem_attention",
    )(h_parts[0], h_parts[-1], wq_bf, wo_bf, mk, mv, cache_k, cache_v, x_parts[0], x_parts[-1], g_post, g_next)


def kernel(x_prompt, x_sample, cache_swa_k, cache_swa_v, state_ret, cache_mem_k, cache_mem_v, mem_prompt, norm_mix_pre, norm_mix_post, norm_mem, norm_x_pre, norm_x_post, norm_ffn_pre, norm_ffn_post, w_ab_in, w_ab_out, swa_sinks, gmlp_ln_g, gmlp_ln_b, gmlp_w_s, gmlp_b_s, w_ret_in, w_ret_out, w_mem_q, w_mem_k, w_mem_v, w_mem_o, w_ffn_up, w_ffn_down):
    vec = lambda g, layer: g[layer].reshape(1, D_MODEL)
    swa_c, swa_s1, swa_s2, ret_c, ret_s = _rope_tables()
    mem_k, mem_v = _mem_kv(mem_prompt, norm_mem, w_mem_k, w_mem_v)
    mem_k_b = mem_k.reshape(DEPTH * BATCH, MEM_LEN, MEM_W)
    mem_v_b = mem_v.reshape(DEPTH * BATCH, MEM_LEN, MEM_W)
    ck_mem = cache_mem_k.reshape(DEPTH, DEC_BATCH, MEM_LEN * MEM_HEADS, MEM_HEAD_DIM)
    cv_mem = cache_mem_v.reshape(DEPTH, DEC_BATCH, MEM_LEN * MEM_HEADS, MEM_HEAD_DIM)
    lg = jnp.log1p(-jnp.exp2(-5.0 - jnp.arange(RET_HEADS, dtype=F32)))

    x_parts = (x_prompt.reshape(MP, D_MODEL), x_sample.reshape(MS, D_MODEL))
    h, w_ab_in_bf = _stacked_prenorm(*x_parts, vec(norm_mix_pre, 0), w_ab_in, 0)
    outs = {}
    for layer in range(DEPTH):
        j = layer // 2
        if layer % 2 == 0:
            (q, kv, u, zv, zv_s), (wu_bf, wd_bf, wout_bf, wq_bf, wo_bf) = _ab_in_proj(
                h, w_ab_in_bf, gmlp_ln_g[j].reshape(1, GMLP_WIDTH), gmlp_ln_b[j].reshape(1, GMLP_WIDTH),
                (swa_c, swa_s1, swa_s2), w_ffn_up, w_ffn_down, w_ab_out, w_mem_q, w_mem_o, layer, j)
            cat = _ab_mix_prompt(q, kv, u, zv, swa_sinks[j], gmlp_w_s[j], gmlp_b_s[j])
            cat, nk_s, nv_s = _ab_mix_sample(
                cat, q, kv, u, zv_s, cache_swa_k[j].reshape(DEC_BATCH, WINDOW, SWA_KV_W),
                cache_swa_v[j].reshape(DEC_BATCH, WINDOW, SWA_KV_W), swa_sinks[j], gmlp_w_s[j], gmlp_b_s[j])
            kv_p = jnp.stack([kv[(b + 1) * SEQ - WINDOW:(b + 1) * SEQ] for b in range(BATCH)])
            kv_p = kv_p.reshape(BATCH, WINDOW, 2, SWA_KV_HEADS, SWA_HEAD_DIM)
            outs.setdefault('swa_k_p', []).append(kv_p[:, :, 0])
            outs.setdefault('swa_v_p', []).append(kv_p[:, :, 1])
            outs.setdefault('swa_k_s', []).append(nk_s.reshape(DEC_BATCH, WINDOW, SWA_KV_HEADS, SWA_HEAD_DIM))
            outs.setdefault('swa_v_s', []).append(nv_s.reshape(DEC_BATCH, WINDOW, SWA_KV_HEADS, SWA_HEAD_DIM))
            outs.setdefault('gmlp_v_s', []).append(zv_s.reshape(DEC_BATCH, DEC_SEQ, GMLP_WIDTH))
            mixed, mix_name = (cat,), "ab_out"
        else:
            (zqk, zv, zg), (wu_bf, wd_bf, wout_bf) = _ret_in_proj(
                h, w_ret_in[j], (ret_c, ret_s), w_ffn_up, w_ffn_down, w_ret_out, layer, j)
            y_p, s_p, y_s, s_s = _retention(zqk, zv, zg, state_ret[j], lg)
            outs.setdefault('ret_p', []).append(s_p)
            outs.setdefault('ret_s', []).append(s_s)
            mixed, mix_name = (y_p, y_s), "ret_out"
        x, h = _out_proj(mixed, wout_bf, x_parts, vec(norm_mix_post, layer), vec(norm_x_pre, layer), mix_name)
        x_parts, h_parts = (x,), (h,)
        g_post = vec(norm_x_post, layer)
        g_next = vec(norm_ffn_pre, layer)
        x, h = _mem_attention(h_parts, x_parts, wq_bf, wo_bf, mem_k_b, mem_v_b, ck_mem, cv_mem, layer,
                              g_post, g_next)
        g_next = vec(norm_mix_pre, layer + 1) if layer + 1 < DEPTH else None
        x, h = _ffn(h, wu_bf, wd_bf, x, vec(norm_ffn_post, layer), g_next)
        x_parts = (x,)

    y_prompt, y_sample = x, h
    stack = lambda name: jnp.stack(outs[name])
    mem_shape = (DEPTH, BATCH, MEM_LEN, MEM_HEADS, MEM_HEAD_DIM)
    return (y_prompt.reshape(BATCH, SEQ, D_MODEL), y_sample.reshape(DEC_BATCH, DEC_SEQ, D_MODEL),
            stack('swa_k_p'), stack('swa_v_p'), stack('swa_k_s'), stack('swa_v_s'), stack('gmlp_v_s'),
            stack('ret_p'), stack('ret_s'), mem_k.reshape(mem_shape), mem_v.reshape(mem_shape))
```

```python
import functools
import math

import jax
import jax.numpy as jnp
from jax import lax
from jax.experimental import pallas as pl
from jax.experimental.pallas import tpu as pltpu

F32 = jnp.float32
BF16 = jnp.bfloat16

D_MODEL = 2048
BATCH = 2
SEQ = 4096
DEPTH = 2
DEC_BATCH = 128
DEC_SEQ = 8
PAST_LEN = 8192

SWA_HEADS = 16
SWA_KV_HEADS = 4
SWA_HEAD_DIM = 64
SWA_GROUP = SWA_HEADS // SWA_KV_HEADS
WINDOW = 128
ROPE_THETA = 500000.0
ROPE_DIM = SWA_HEAD_DIM // 4
ROPE_HALF = ROPE_DIM // 2

GMLP_GROUPS = 4
GMLP_WIDTH = 1024
GMLP_GROUP_DIM = GMLP_WIDTH // GMLP_GROUPS
GMLP_CHUNK = 128

SWA_Q_W = SWA_HEADS * SWA_HEAD_DIM
SWA_KV_W = SWA_KV_HEADS * SWA_HEAD_DIM
AB_IN = SWA_Q_W + 2 * SWA_KV_W + 2 * GMLP_WIDTH
AB_OUT = SWA_Q_W + GMLP_WIDTH

RET_HEADS = 8
RET_KEY_DIM = D_MODEL // RET_HEADS
RET_VAL_DIM = 2 * RET_KEY_DIM
RET_CHUNK = 128
RET_THETA = 10000.0
RET_QK_W = RET_HEADS * RET_KEY_DIM
RET_V_W = RET_HEADS * RET_VAL_DIM
RET_IN = 2 * RET_QK_W + 2 * RET_V_W

MEM_LEN = 256
MEM_HEADS = 4
MEM_HEAD_DIM = 128
MEM_W = MEM_HEADS * MEM_HEAD_DIM

D_FF = 4 * D_MODEL
EPS = 1e-6
NEG_INF = -1e30

MP = BATCH * SEQ
MS = DEC_BATCH * DEC_SEQ
M = MP + MS
LANES = 128
MIB = 1024 * 1024

TM_IN = 1024
TAB_ROWS = SEQ + MS
TM_OUT = 512


def _cparams(sem, vmem_mib):
    return pltpu.CompilerParams(dimension_semantics=sem, vmem_limit_bytes=vmem_mib * MIB)


def _rms(x, g):
    return x * lax.rsqrt(jnp.mean(x * x, axis=-1, keepdims=True) + EPS) * g


def _residual(y, x, g_post, g_next):
    xn = x + _rms(y, g_post)
    return xn, _rms(xn, g_next).astype(BF16)


def _rope_table_kernel(sc_ref, s1_ref, s2_ref, rc_ref, rs_ref):
    tr = sc_ref.shape[0]
    r = lax.broadcasted_iota(jnp.int32, (tr, LANES), 0) + pl.program_id(0) * tr
    pos = jnp.where(r < SEQ, r, PAST_LEN + ((r - SEQ) & (DEC_SEQ - 1))).astype(F32)
    lane = lax.broadcasted_iota(jnp.int32, (tr, LANES), 1)
    d = lane & (SWA_HEAD_DIM - 1)
    inv = jnp.exp((d & (ROPE_HALF - 1)).astype(F32) * (-math.log(ROPE_THETA) / ROPE_HALF))
    ang = pos * inv
    c, s = jnp.cos(ang), jnp.sin(ang)
    sc_ref[...] = jnp.where(d < ROPE_DIM, c, 1.0)
    s1_ref[...] = jnp.where(d < ROPE_HALF, -s, 0.0)
    s2_ref[...] = jnp.where((d >= ROPE_HALF) & (d < ROPE_DIM), s, 0.0)
    inv_r = jnp.exp(lane.astype(F32) * (-math.log(RET_THETA) / LANES))
    ang_r = pos * inv_r
    rc_ref[...] = jnp.cos(ang_r)
    rs_ref[...] = jnp.sin(ang_r)


def _rope_tables():
    tr = 512
    spec = pl.BlockSpec((tr, LANES), lambda i: (i, 0))
    shp = jax.ShapeDtypeStruct((TAB_ROWS, LANES), F32)
    return pl.pallas_call(
        _rope_table_kernel, grid=(TAB_ROWS // tr,), out_specs=[spec] * 5, out_shape=[shp] * 5,
        compiler_params=_cparams(("arbitrary",), 16), name="rope_tables")()


def _norm_kernel(xp_ref, xs_ref, g_ref, w_ref, h_ref, wbf_ref):
    def emit(src_ref):
        h_ref[...] = _rms(src_ref[...], g_ref[...]).astype(BF16)

    is_prompt = pl.program_id(0) < MP // h_ref.shape[0]
    pl.when(is_prompt)(lambda: emit(xp_ref))
    pl.when(jnp.logical_not(is_prompt))(lambda: emit(xs_ref))
    wbf_ref[...] = w_ref[...].astype(BF16)


def _stacked_prenorm(x_prompt, x_sample, g, w, layer):
    tm = TM_OUT
    w_src, w_in_spec, w_shape, w_out_spec = _row_cast_job(w, layer, MP // tm)
    return pl.pallas_call(
        _norm_kernel, grid=(M // tm,),
        in_specs=[pl.BlockSpec((tm, D_MODEL), lambda i: (jnp.minimum(i, MP // tm - 1), 0)),
                  pl.BlockSpec((tm, D_MODEL), lambda i: (jnp.maximum(i - MP // tm, 0), 0)),
                  pl.BlockSpec((1, D_MODEL), lambda i: (0, 0)), w_in_spec],
        out_specs=[pl.BlockSpec((tm, D_MODEL), lambda i: (i, 0)), w_out_spec],
        out_shape=[jax.ShapeDtypeStruct((M, D_MODEL), BF16), w_shape],
        compiler_params=_cparams(("arbitrary",), 40), name="prenorm",
    )(x_prompt, x_sample, g, w_src)


W_SUB = 512


def _inproj_kernel(h_ref, *refs, n_sub, n_side, n_cast, epilogue):
    w_refs, refs = refs[:n_sub], refs[n_sub:]
    side, refs = refs[:n_side], refs[n_side:]
    cast_src, refs = refs[:n_cast], refs[n_cast:]
    wbf_ref = refs[-1]
    outs, cast_dst = refs[:len(refs) - 1 - n_cast], refs[len(refs) - 1 - n_cast:-1]

    @pl.when(pl.program_id(1) == 0)
    def _():
        for s, w_ref in enumerate(w_refs):
            wbf_ref[:, s * W_SUB:(s + 1) * W_SUB] = w_ref[...].astype(BF16)

    acc = jnp.dot(h_ref[...], wbf_ref[...], preferred_element_type=F32)
    epilogue(acc, side, outs)
    for src, dst in zip(cast_src, cast_dst):
        dst[...] = src[...].astype(BF16)


def _tab_index(j, i):
    return (jnp.where(i < MP // TM_IN, i % (SEQ // TM_IN), SEQ // TM_IN), 0)


CAST_STEPS = MP // TM_IN


def _cast_job(w, layer, nrb, ncb, block_of):
    rows, cols = w.shape[1:]
    blk = (rows // nrb, cols // ncb)
    index = lambda j, i: block_of(j, jnp.minimum(i, CAST_STEPS - 1))
    return (w, pl.BlockSpec((None,) + blk, lambda j, i: (layer,) + tuple(index(j, i))),
            jax.ShapeDtypeStruct((rows, cols), BF16), pl.BlockSpec(blk, index))


def _inproj(h, w, col0, ncols, tn, epilogue, side, side_specs, out_shapes, out_specs, name, vmem_mib,
            casts=()):
    tm = TM_IN
    k = h.shape[1]
    n_sub = tn // W_SUB
    w_specs = [pl.BlockSpec((k, W_SUB), lambda j, i, s=s: (0, col0 // W_SUB + j * n_sub + s))
               for s in range(n_sub)]
    as_list = lambda v: list(v) if isinstance(v, (list, tuple)) else [v]
    outs = pl.pallas_call(
        functools.partial(_inproj_kernel, n_sub=n_sub, n_side=len(side), n_cast=len(casts), epilogue=epilogue),
        grid=(ncols // tn, M // tm),
        in_specs=([pl.BlockSpec((tm, k), lambda j, i: (i, 0))] + w_specs + side_specs
                  + [c[1] for c in casts]),
        out_specs=as_list(out_specs) + [c[3] for c in casts],
        out_shape=as_list(out_shapes) + [c[2] for c in casts],
        scratch_shapes=[pltpu.VMEM((k, tn), BF16)],
        compiler_params=_cparams(("arbitrary", "arbitrary"), vmem_mib), name=name,
    )(h, *([w] * n_sub), *side, *[c[0] for c in casts])
    n_main = len(outs) - len(casts)
    main = outs[0] if n_main == 1 else tuple(outs[:n_main])
    return (main, tuple(outs[n_main:])) if casts else main


def _swa_rope(x, c, s1, s2):
    return x * c + pltpu.roll(x, LANES - ROPE_HALF, 1) * s1 + pltpu.roll(x, ROPE_HALF, 1) * s2


def _epi_swa_q(acc, side, outs):
    c, s1, s2 = (t[...] for t in side)
    (q_ref,) = outs
    for cc in range(acc.shape[1] // LANES):
        sl = slice(cc * LANES, (cc + 1) * LANES)
        q_ref[:, sl] = (_swa_rope(acc[:, sl], c, s1, s2) * (SWA_HEAD_DIM ** -0.5)).astype(BF16)


def _epi_swa_kv(acc, side, outs):
    c, s1, s2 = (t[...] for t in side)
    (kv_ref,) = outs
    for cc in range(SWA_KV_W // LANES):
        sl = slice(cc * LANES, (cc + 1) * LANES)
        kv_ref[:, sl] = _swa_rope(acc[:, sl], c, s1, s2)
    kv_ref[:, SWA_KV_W:] = acc[:, SWA_KV_W:]


def _epi_gelu(acc, side, outs):
    outs[0][...] = jax.nn.gelu(acc).astype(BF16)


def _epi_ret_qk(acc, side, outs):
    cos, sin = (t[...] for t in side)
    (z_ref,) = outs
    tn = acc.shape[1]
    scale = jnp.where(pl.program_id(0) < RET_QK_W // tn, 1.0, RET_KEY_DIM ** -0.5)
    for hh in range(tn // RET_KEY_DIM):
        c0 = hh * RET_KEY_DIM
        x1, x2 = acc[:, c0:c0 + LANES], acc[:, c0 + LANES:c0 + 2 * LANES]
        z_ref[hh, :, :LANES] = ((x1 * cos - x2 * sin) * scale).astype(BF16)
        z_ref[hh, :, LANES:] = ((x2 * cos + x1 * sin) * scale).astype(BF16)


def _per_head(fn):
    def epilogue(acc, side, outs):
        (z_ref,) = outs
        for hh in range(z_ref.shape[0]):
            z_ref[hh] = fn(acc[:, hh * RET_VAL_DIM:(hh + 1) * RET_VAL_DIM]).astype(BF16)
    return epilogue


_epi_cast = _per_head(lambda a: a)
_epi_silu = _per_head(lambda a: a / (1.0 + jnp.exp(-a)))


AB_TM = 256


def _ab_in_kernel(h_ref, w_ref, sc_ref, s1_ref, s2_ref, lng_ref, lnb_ref, *refs, n_cast):
    cast_src, (q_ref, kv_ref, u_ref, zv_ref, zvs_ref), cast_dst = refs[:n_cast], refs[n_cast:n_cast + 5], refs[n_cast + 5:]
    h = h_ref[...]
    tabs = (sc_ref, s1_ref, s2_ref)
    seg = lambda c0, n: jnp.dot(h, w_ref[:, c0:c0 + n], preferred_element_type=F32)
    _epi_swa_q(seg(0, SWA_Q_W), tabs, (q_ref,))
    _epi_swa_kv(seg(SWA_Q_W, 2 * SWA_KV_W), tabs, (kv_ref,))
    _epi_gelu(seg(SWA_Q_W + 2 * SWA_KV_W, GMLP_WIDTH), (), (u_ref,))
    a = jax.nn.gelu(seg(AB_IN - GMLP_WIDTH, GMLP_WIDTH))
    xc = a - jnp.mean(a, axis=-1, keepdims=True)
    y = xc * lax.rsqrt(jnp.mean(xc * xc, axis=-1, keepdims=True) + EPS) * lng_ref[...] + lnb_ref[...]
    zv_ref[...] = y.astype(BF16)

    @pl.when(pl.program_id(0) >= MP // AB_TM)
    def _():
        zvs_ref[...] = y

    for src, dst in zip(cast_src, cast_dst):
        dst[...] = src[...].astype(BF16)


def _row_cast_job(w, layer, n_blocks):
    rows, cols = w.shape[1:]
    blk = (rows // n_blocks, cols)
    index = lambda i: (jnp.minimum(i, n_blocks - 1), 0)
    return (w, pl.BlockSpec((None,) + blk, lambda i: (layer,) + index(i)),
            jax.ShapeDtypeStruct((rows, cols), BF16), pl.BlockSpec(blk, index))


def _ab_in_proj(h, w_in_bf, ln_g, ln_b, tabs, w_up, w_down, w_out, w_mem_q, w_mem_o, layer, j):
    tm = AB_TM
    n_p = MP // tm
    stacked = lambda w: w.reshape(1, w.shape[0] * w.shape[1], w.shape[2])
    casts = [_row_cast_job(w_up, layer, n_p), _row_cast_job(w_down, layer, n_p), _row_cast_job(w_out, j, n_p),
             _row_cast_job(stacked(w_mem_q), 0, n_p), _row_cast_job(stacked(w_mem_o), 0, n_p)]
    tab_rows = SEQ // tm
    tab_spec = pl.BlockSpec((tm, LANES), lambda i: (jnp.where(i < n_p, i % tab_rows, tab_rows + i - n_p), 0))
    vec_spec = pl.BlockSpec((1, GMLP_WIDTH), lambda i: (0, 0))
    row_out = lambda n: pl.BlockSpec((tm, n), lambda i: (i, 0))
    outs = pl.pallas_call(
        functools.partial(_ab_in_kernel, n_cast=len(casts)), grid=(M // tm,),
        in_specs=([pl.BlockSpec((tm, D_MODEL), lambda i: (i, 0)),
                   pl.BlockSpec((D_MODEL, AB_IN), lambda i: (0, 0), pipeline_mode=pl.Buffered(1))]
                  + [tab_spec] * 3 + [vec_spec] * 2 + [c[1] for c in casts]),
        out_specs=[row_out(SWA_Q_W), row_out(2 * SWA_KV_W), row_out(GMLP_WIDTH), row_out(GMLP_WIDTH),
                   pl.BlockSpec((tm, GMLP_WIDTH), lambda i: (jnp.maximum(i - n_p, 0), 0))]
                  + [c[3] for c in casts],
        out_shape=[jax.ShapeDtypeStruct((M, SWA_Q_W), BF16), jax.ShapeDtypeStruct((M, 2 * SWA_KV_W), F32),
                   jax.ShapeDtypeStruct((M, GMLP_WIDTH), BF16), jax.ShapeDtypeStruct((M, GMLP_WIDTH), BF16),
                   jax.ShapeDtypeStruct((MS, GMLP_WIDTH), F32)] + [c[2] for c in casts],
        compiler_params=_cparams(("arbitrary",), 52), name="ab_in",
    )(h, w_in_bf, *tabs, ln_g, ln_b, *[c[0] for c in casts])
    return tuple(outs[:5]), tuple(outs[5:])


def _ret_in_proj(h, w_in, tabs, w_up, w_down, w_out, layer, j):
    rc, rs = tabs
    tn = 1024
    tab_spec = pl.BlockSpec((TM_IN, LANES), _tab_index)
    by_tile = lambda j, i: (j, i)
    head_major = jax.ShapeDtypeStruct((RET_HEADS, M, 2 * RET_KEY_DIM), BF16)
    seg = lambda col0, ncols, epi, side, specs, out_spec, name, w, lead: _inproj(
        h, w_in, col0, ncols, tn, epi, side, specs, head_major, out_spec, name, 56,
        casts=[_cast_job(w, lead, ncols // tn, CAST_STEPS, by_tile)])
    qk_heads = tn // RET_KEY_DIM
    q_tiles = RET_QK_W // tn
    qk_spec = pl.BlockSpec((qk_heads, TM_IN, RET_KEY_DIM), lambda j, i: (j % q_tiles, i, j // q_tiles))
    v_spec = pl.BlockSpec((tn // RET_VAL_DIM, TM_IN, RET_VAL_DIM), lambda j, i: (j, i, 0))
    zqk, (wu_bf,) = seg(0, 2 * RET_QK_W, _epi_ret_qk, [rc, rs], [tab_spec] * 2, qk_spec, "ret_in_qk",
                        w_up, layer)
    zv, (wd_bf,) = seg(2 * RET_QK_W, RET_V_W, _epi_cast, [], [], v_spec, "ret_in_v", w_down, layer)
    zg, (wout_bf,) = seg(2 * RET_QK_W + RET_V_W, RET_V_W, _epi_silu, [], [], v_spec, "ret_in_g", w_out, j)
    return (zqk, zv, zg), (wu_bf, wd_bf, wout_bf)


AB_ROWS = 512


def _sink_softmax(s, sink):
    m = jnp.maximum(jnp.max(s, axis=-1, keepdims=True), sink)
    p = jnp.exp(s - m)
    den = jnp.sum(p, axis=-1, keepdims=True) + jnp.exp(sink - m)
    return p, den


def _ab_prompt_kernel(sink_ref, q_ref, kv_ref, kvp_ref, u_ref, zv_ref, ws_ref, bs_ref, cat_ref,
                      kd_ref, vd_ref):
    r = pl.program_id(1)
    low = lax.broadcasted_iota(jnp.int32, (1, LANES), 1) < SWA_HEAD_DIM

    def both_halves(x, half):
        sw = pltpu.roll(x, SWA_HEAD_DIM, 1)
        return (jnp.where(low, x, sw) if half == 0 else jnp.where(low, sw, x)).astype(BF16)

    for kh in range(SWA_KV_HEADS):
        kcol = slice((kh // 2) * LANES, (kh // 2 + 1) * LANES)
        vcol = slice(SWA_KV_W + (kh // 2) * LANES, SWA_KV_W + (kh // 2 + 1) * LANES)
        kd_ref[kh, :WINDOW, :] = both_halves(kvp_ref[:, kcol], kh % 2)
        kd_ref[kh, WINDOW:, :] = both_halves(kv_ref[:, kcol], kh % 2)
        vd_ref[kh, :WINDOW, :] = both_halves(kvp_ref[:, vcol], kh % 2)
        vd_ref[kh, WINDOW:, :] = both_halves(kv_ref[:, vcol], kh % 2)

    t = lax.broadcasted_iota(jnp.int32, (WINDOW, 2 * WINDOW), 0)
    sk = lax.broadcasted_iota(jnp.int32, (WINDOW, 2 * WINDOW), 1)
    band = (sk <= t + WINDOW) & (sk > t)
    sink_slot = sk == t
    tril = (lax.broadcasted_iota(jnp.int32, (GMLP_CHUNK, GMLP_CHUNK), 0)
            >= lax.broadcasted_iota(jnp.int32, (GMLP_CHUNK, GMLP_CHUNK), 1))

    for sb in range(AB_ROWS // WINDOW):
        rs = slice(sb * WINDOW, (sb + 1) * WINDOW)
        ks = slice(sb * WINDOW, sb * WINDOW + 2 * WINDOW)
        valid = band & ((sk >= WINDOW) | (r * (AB_ROWS // WINDOW) + sb > 0))
        for kh in range(SWA_KV_HEADS):
            qm = []
            for g in range(SWA_GROUP):
                hd = kh * SWA_GROUP + g
                qb = q_ref[rs, (hd // 2) * LANES:(hd // 2 + 1) * LANES]
                qm.append(jnp.where(low if hd % 2 == 0 else jnp.logical_not(low), qb, jnp.zeros_like(qb)))
            s = lax.dot_general(jnp.concatenate(qm, axis=0), kd_ref[kh, ks, :], (((1,), (1,)), ((), ())),
                                preferred_element_type=F32)
            pv, den = [], []
            for g in range(SWA_GROUP):
                sg = s[g * WINDOW:(g + 1) * WINDOW]
                sg = jnp.where(valid, sg, jnp.where(sink_slot, sink_ref[kh * SWA_GROUP + g], NEG_INF))
                p = jnp.exp(sg - jnp.max(sg, axis=-1, keepdims=True))
                den.append(jnp.sum(p, axis=-1, keepdims=True))
                pv.append(jnp.where(sink_slot, 0.0, p).astype(BF16))
            o = jnp.dot(jnp.concatenate(pv, axis=0), vd_ref[kh, ks, :], preferred_element_type=F32)
            for pr in range(SWA_GROUP // 2):
                lo = o[(2 * pr) * WINDOW:(2 * pr + 1) * WINDOW] / den[2 * pr]
                hi = o[(2 * pr + 1) * WINDOW:(2 * pr + 2) * WINDOW] / den[2 * pr + 1]
                c0 = (kh * (SWA_GROUP // 2) + pr) * LANES
                cat_ref[rs, c0:c0 + LANES] = jnp.where(low, lo, hi).astype(BF16)
        zb = zv_ref[rs, :]
        ub = u_ref[rs, :]
        for g in range(GMLP_GROUPS):
            gs = slice(g * GMLP_GROUP_DIM, (g + 1) * GMLP_GROUP_DIM)
            w = jnp.where(tril, ws_ref[g], 0.0).astype(BF16)
            mixed = jnp.dot(w, zb[:, gs], preferred_element_type=F32) + bs_ref[g]
            cat_ref[rs, SWA_Q_W + g * GMLP_GROUP_DIM:SWA_Q_W + (g + 1) * GMLP_GROUP_DIM] = (
                ub[:, gs].astype(F32) * mixed).astype(BF16)


def _ab_mix_prompt(q, kv, u, zv, sinks, w_s, b_s):
    nr = SEQ // AB_ROWS
    sub = AB_ROWS // WINDOW
    row = lambda b, r: (b * nr + r, 0)
    return pl.pallas_call(
        _ab_prompt_kernel, grid=(BATCH, nr),
        in_specs=[pl.BlockSpec(memory_space=pltpu.SMEM),
                  pl.BlockSpec((AB_ROWS, SWA_Q_W), row),
                  pl.BlockSpec((AB_ROWS, 2 * SWA_KV_W), row),
                  pl.BlockSpec((WINDOW, 2 * SWA_KV_W), lambda b, r: (jnp.maximum((b * nr + r) * sub - 1, 0), 0)),
                  pl.BlockSpec((AB_ROWS, GMLP_WIDTH), row),
                  pl.BlockSpec((AB_ROWS, GMLP_WIDTH), row),
                  pl.BlockSpec((GMLP_GROUPS, GMLP_CHUNK, GMLP_CHUNK), lambda b, r: (0, 0, 0)),
                  pl.BlockSpec((GMLP_GROUPS, GMLP_CHUNK, 1), lambda b, r: (0, 0, 0))],
        out_specs=pl.BlockSpec((AB_ROWS, AB_OUT), row),
        out_shape=jax.ShapeDtypeStruct((M, AB_OUT), BF16),
        scratch_shapes=[pltpu.VMEM((SWA_KV_HEADS, WINDOW + AB_ROWS, LANES), BF16)] * 2,
        compiler_params=_cparams(("arbitrary", "arbitrary"), 40), name="ab_mix_prompt",
    )(sinks, q, kv, kv, u, zv, w_s, b_s.reshape(GMLP_GROUPS, GMLP_CHUNK, 1))


AB_NB = 16
AB_KEYS = 2 * WINDOW


def _ab_sample_kernel(sink_ref, q_ref, kv_ref, u_ref, zv_ref, ck_ref, cv_ref, ws_ref, bs_ref, cat_in_ref,
                      cat_ref, nk_ref, nv_ref):
    del cat_in_ref
    nb, t = AB_NB, DEC_SEQ
    kv3 = kv_ref[...].reshape(nb, t, 2 * SWA_KV_W)
    kn, vn = kv3[:, :, :SWA_KV_W], kv3[:, :, SWA_KV_W:]
    ck, cv = ck_ref[...], cv_ref[...]
    nk_ref[:, :WINDOW - t, :] = ck[:, t:, :]
    nk_ref[:, WINDOW - t:, :] = kn
    nv_ref[:, :WINDOW - t, :] = cv[:, t:, :]
    nv_ref[:, WINDOW - t:, :] = vn
    pad = jnp.zeros((nb, AB_KEYS - WINDOW - t, SWA_KV_W), F32)
    kall = jnp.concatenate([ck, kn, pad], axis=1).astype(BF16)
    vall = jnp.concatenate([cv, vn, pad], axis=1).astype(BF16)

    rows = SWA_GROUP * t
    tq = lax.broadcasted_iota(jnp.int32, (rows, AB_KEYS), 0) & (t - 1)
    sk = lax.broadcasted_iota(jnp.int32, (rows, AB_KEYS), 1)
    valid = ((sk < WINDOW) & (sk > tq)) | ((sk >= WINDOW) & (sk - WINDOW <= tq))

    q3 = q_ref[...].astype(F32).reshape(nb, t, SWA_Q_W)
    pieces = []
    for kh in range(SWA_KV_HEADS):
        heads = [kh * SWA_GROUP + g for g in range(SWA_GROUP)]
        qs = jnp.concatenate([q3[:, :, hd * SWA_HEAD_DIM:(hd + 1) * SWA_HEAD_DIM] for hd in heads],
                             axis=1).astype(BF16)
        cs = slice(kh * SWA_HEAD_DIM, (kh + 1) * SWA_HEAD_DIM)
        s = jnp.einsum('nqd,nkd->nqk', qs, kall[:, :, cs], preferred_element_type=F32)
        s = jnp.where(valid[None], s, NEG_INF)
        sink = jnp.concatenate([jnp.full((t, 1), sink_ref[hd], F32) for hd in heads], axis=0)[None]
        p, den = _sink_softmax(s, sink)
        o = jnp.einsum('nqk,nkd->nqd', p.astype(BF16), vall[:, :, cs], preferred_element_type=F32) / den
        pieces += [o[:, g * t:(g + 1) * t, :] for g in range(SWA_GROUP)]
    attn = jnp.concatenate(pieces, axis=2).reshape(nb * t, SWA_Q_W)
    cat_ref[:, :SWA_Q_W] = attn.astype(BF16)

    zv3 = zv_ref[...].reshape(nb, t, GMLP_WIDTH)
    u3 = u_ref[...].astype(F32).reshape(nb, t, GMLP_WIDTH)
    irow = lax.broadcasted_iota(jnp.int32, (t, GMLP_GROUP_DIM), 0)
    for g in range(GMLP_GROUPS):
        gs = slice(g * GMLP_GROUP_DIM, (g + 1) * GMLP_GROUP_DIM)
        zg = zv3[:, :, gs]
        mixed = jnp.zeros((nb, t, GMLP_GROUP_DIM), F32) + bs_ref[g][None]
        for j in range(t):
            wcol = jnp.where(irow >= j, ws_ref[g, j], 0.0)
            mixed = mixed + zg[:, j:j + 1, :] * wcol[None]
        gate = (u3[:, :, gs] * mixed).reshape(nb * t, GMLP_GROUP_DIM)
        cat_ref[:, SWA_Q_W + g * GMLP_GROUP_DIM:SWA_Q_W + (g + 1) * GMLP_GROUP_DIM] = gate.astype(BF16)


def _ab_mix_sample(cat, q, kv, u, zv_s, cache_k, cache_v, sinks, w_s, b_s):
    nb, t = AB_NB, DEC_SEQ
    rows = nb * t
    off = MP // rows
    srow = lambda i: (off + i, 0)
    wt = jnp.broadcast_to(jnp.swapaxes(w_s[:, :t, :t], 1, 2)[..., None], (GMLP_GROUPS, t, t, GMLP_GROUP_DIM))
    bt = jnp.broadcast_to(b_s[:, :t, None], (GMLP_GROUPS, t, GMLP_GROUP_DIM))
    cspec = pl.BlockSpec((nb, WINDOW, SWA_KV_W), lambda i: (i, 0, 0))
    cshape = jax.ShapeDtypeStruct((DEC_BATCH, WINDOW, SWA_KV_W), F32)
    return pl.pallas_call(
        _ab_sample_kernel, grid=(DEC_BATCH // nb,),
        in_specs=[pl.BlockSpec(memory_space=pltpu.SMEM),
                  pl.BlockSpec((rows, SWA_Q_W), srow),
                  pl.BlockSpec((rows, 2 * SWA_KV_W), srow),
                  pl.BlockSpec((rows, GMLP_WIDTH), srow),
                  pl.BlockSpec((rows, GMLP_WIDTH), lambda i: (i, 0)),
                  cspec, cspec,
                  pl.BlockSpec((GMLP_GROUPS, t, t, GMLP_GROUP_DIM), lambda i: (0, 0, 0, 0)),
                  pl.BlockSpec((GMLP_GROUPS, t, GMLP_GROUP_DIM), lambda i: (0, 0, 0)),
                  pl.BlockSpec(memory_space=pl.ANY)],
        out_specs=[pl.BlockSpec((rows, AB_OUT), srow), cspec, cspec],
        out_shape=[jax.ShapeDtypeStruct((M, AB_OUT), BF16), cshape, cshape],
        input_output_aliases={9: 0},
        compiler_params=_cparams(("arbitrary",), 48), name="ab_mix_sample",
    )(sinks, q, kv, u, zv_s, cache_k, cache_v, wt, bt, cat)


RET_ROWS = 1024
RET_BLOCK = 256


def _ret_prompt_part(lg, first_run, qk_ref, v_ref, s_ref):
    L = RET_BLOCK

    @pl.when(first_run)
    def _():
        s_ref[...] = jnp.zeros_like(s_ref)

    diff = (lax.broadcasted_iota(jnp.int32, (L, L), 0) - lax.broadcasted_iota(jnp.int32, (L, L), 1)).astype(F32)
    dmat = jnp.where(diff >= 0, jnp.exp(lg * jnp.maximum(diff, 0.0)), 0.0)
    idx = lax.broadcasted_iota(jnp.int32, (L, 1), 0).astype(F32)
    q_dec = jnp.exp(lg * (idx + 1.0))
    k_dec = jnp.exp(lg * (L - 1.0 - idx))
    s_dec = jnp.exp(lg * jnp.full((1, 1), float(L), F32))

    ys = []
    for ci in range(RET_ROWS // L):
        rs = slice(ci * L, (ci + 1) * L)
        q, k, v = qk_ref[rs, :RET_KEY_DIM], qk_ref[rs, RET_KEY_DIM:], v_ref[rs, :]
        att = lax.dot_general(q, k, (((1,), (1,)), ((), ())), preferred_element_type=F32) * dmat
        s0 = s_ref[...]
        o = (jnp.dot(att.astype(BF16), v, preferred_element_type=F32)
             + jnp.dot(q, s0.astype(BF16), preferred_element_type=F32) * q_dec)
        kd = (k.astype(F32) * k_dec).astype(BF16)
        s_ref[...] = s0 * s_dec + lax.dot_general(kd, v, (((0,), (0,)), ((), ())), preferred_element_type=F32)
        on = o * lax.rsqrt(jnp.mean(o * o, axis=-1, keepdims=True) + EPS)
        ys.append(on.astype(BF16))
    return jnp.concatenate(ys, axis=0)


RET_NB = DEC_BATCH // (BATCH * (SEQ // RET_ROWS))


def _ret_sample_part(lg, qk_ref, v_ref, s0_ref, s1_ref):
    nb, t = RET_NB, DEC_SEQ
    rows = nb * t
    q, k, v = qk_ref[:, :RET_KEY_DIM], qk_ref[:, RET_KEY_DIM:], v_ref[...]
    ri = lax.broadcasted_iota(jnp.int32, (rows, rows), 0)
    ci = lax.broadcasted_iota(jnp.int32, (rows, rows), 1)
    d = ((ri & (t - 1)) - (ci & (t - 1))).astype(F32)
    same = (ri // t) == (ci // t)
    dmat = jnp.where(same & (d >= 0), jnp.exp(lg * jnp.maximum(d, 0.0)), 0.0)
    att = lax.dot_general(q, k, (((1,), (1,)), ((), ())), preferred_element_type=F32) * dmat
    o_intra = jnp.dot(att.astype(BF16), v, preferred_element_type=F32)

    rcol = lax.broadcasted_iota(jnp.int32, (rows, 1), 0)
    tt = (rcol & (t - 1)).astype(F32)
    q_dec = jnp.exp(lg * (tt + 1.0))
    k_dec = jnp.exp(lg * (t - 1.0 - tt))
    s_dec = jnp.exp(lg * jnp.full((1, 1), float(t), F32))
    kd = (k.astype(F32) * k_dec).astype(BF16)
    qf = q.astype(F32)
    vf = v.astype(F32)
    o_rows = []
    for n in range(nb):
        s0 = s0_ref[n]
        qn = qf[n * t:(n + 1) * t].astype(BF16)
        o_rows.append(jnp.dot(qn, s0.astype(BF16), preferred_element_type=F32))
        vn = jnp.where(rcol // t == n, vf, 0.0).astype(BF16)
        s1_ref[n] = s0 * s_dec + lax.dot_general(kd, vn, (((0,), (0,)), ((), ())), preferred_element_type=F32)
    o = o_intra + jnp.concatenate(o_rows, axis=0) * q_dec
    on = o * lax.rsqrt(jnp.mean(o * o, axis=-1, keepdims=True) + EPS)
    return on.astype(BF16)


def _retention_kernel(lg_ref, qk_ref, v_ref, sqk_ref, sv_ref, s0_ref,
                      yp_ref, sfin_ref, ys_ref, s1_ref, s_ref):
    c = pl.program_id(2)
    lg = lg_ref[pl.program_id(1)]
    yp_ref[...] = _ret_prompt_part(lg, c == 0, qk_ref, v_ref, s_ref)

    @pl.when(c == pl.num_programs(2) - 1)
    def _():
        sfin_ref[...] = s_ref[...]

    ys_ref[...] = _ret_sample_part(lg, sqk_ref, sv_ref, s0_ref, s1_ref)


def _retention(zqk, zv, state, lg):
    nc = SEQ // RET_ROWS
    srows = RET_NB * DEC_SEQ
    soff = MP // srows
    prow = lambda b, c: b * nc + c
    width = 2 * RET_KEY_DIM
    head_specs = lambda rows, row: [pl.BlockSpec((None, rows, width), lambda b, h, c: (h, row(b, c), 0))] * 2
    sspec = pl.BlockSpec((RET_NB, None, RET_KEY_DIM, RET_VAL_DIM), lambda b, h, c: (prow(b, c), h, 0, 0))
    return pl.pallas_call(
        _retention_kernel, grid=(BATCH, RET_HEADS, nc),
        in_specs=([pl.BlockSpec(memory_space=pltpu.SMEM)] + head_specs(RET_ROWS, prow)
                  + head_specs(srows, lambda b, c: soff + prow(b, c)) + [sspec]),
        out_specs=[pl.BlockSpec((RET_ROWS, RET_VAL_DIM), lambda b, h, c: (prow(b, c), h)),
                   pl.BlockSpec((None, None, RET_KEY_DIM, RET_VAL_DIM), lambda b, h, c: (b, h, 0, 0)),
                   pl.BlockSpec((srows, RET_VAL_DIM), lambda b, h, c: (prow(b, c), h)),
                   sspec],
        out_shape=[jax.ShapeDtypeStruct((MP, RET_V_W), BF16),
                   jax.ShapeDtypeStruct((BATCH, RET_HEADS, RET_KEY_DIM, RET_VAL_DIM), F32),
                   jax.ShapeDtypeStruct((MS, RET_V_W), BF16),
                   jax.ShapeDtypeStruct((DEC_BATCH, RET_HEADS, RET_KEY_DIM, RET_VAL_DIM), F32)],
        scratch_shapes=[pltpu.VMEM((RET_KEY_DIM, RET_VAL_DIM), F32)],
        compiler_params=_cparams(("arbitrary", "arbitrary", "arbitrary"), 56), name="retention",
    )(lg, zqk, zv, zqk, zv, state)


EPI_ROWS = 128


def _zero_on_first(acc_ref, k):
    @pl.when(k == 0)
    def _():
        acc_ref[...] = jnp.zeros_like(acc_ref)


def _outproj_kernel(ap_ref, as_ref, w_ref, xp_ref, xs_ref, gp_ref, gn_ref, *refs):
    gate_ref, (xn_ref, hn_ref) = (refs[0] if len(refs) == 3 else None), refs[-2:]
    tm = xn_ref.shape[0]
    is_prompt = pl.program_id(0) < MP // tm
    for r0 in range(0, tm, EPI_ROWS):
        rs = slice(r0, r0 + EPI_ROWS)
        a = jnp.where(is_prompt, ap_ref[rs, :], as_ref[rs, :])
        if gate_ref is not None:
            width = gate_ref.shape[2]
            a = jnp.concatenate(
                [(a[:, hh * width:(hh + 1) * width].astype(F32) * gate_ref[hh, rs, :].astype(F32)).astype(BF16)
                 for hh in range(gate_ref.shape[0])], axis=1)
        x = jnp.where(is_prompt, xp_ref[rs, :], xs_ref[rs, :])
        y = jnp.dot(a, w_ref[...], preferred_element_type=F32)
        xn, hn = _residual(y, x, gp_ref[...], gn_ref[...])
        xn_ref[rs, :] = xn
        hn_ref[rs, :] = hn


def _out_proj(a_parts, w_bf, x_parts, g_post, g_next, name, gate=None):
    kdim = w_bf.shape[0]
    tm = TM_OUT * D_MODEL // kdim
    np_t = MP // tm

    def two_source(parts, width):
        sample_t0 = np_t if len(parts) == 1 else 0
        return [pl.BlockSpec((tm, width), lambda i: (jnp.minimum(i, np_t - 1), 0)),
                pl.BlockSpec((tm, width), lambda i: (sample_t0 + jnp.maximum(i - np_t, 0), 0))]

    row = pl.BlockSpec((tm, D_MODEL), lambda i: (i, 0))
    vec = pl.BlockSpec((1, D_MODEL), lambda i: (0, 0))
    gate_args, gate_specs = [], []
    if gate is not None:
        gate_args = [gate]
        gate_specs = [pl.BlockSpec((gate.shape[0], tm, gate.shape[2]), lambda i: (0, i, 0))]
    return pl.pallas_call(
        _outproj_kernel, grid=(M // tm,),
        in_specs=(two_source(a_parts, kdim)
                  + [pl.BlockSpec((kdim, D_MODEL), lambda i: (0, 0), pipeline_mode=pl.Buffered(1))]
                  + two_source(x_parts, D_MODEL) + [vec, vec] + gate_specs),
        out_specs=[row, row],
        out_shape=[jax.ShapeDtypeStruct((M, D_MODEL), F32), jax.ShapeDtypeStruct((M, D_MODEL), BF16)],
        compiler_params=_cparams(("arbitrary",), 56), name=name,
    )(a_parts[0], a_parts[-1], w_bf, x_parts[0], x_parts[-1], g_post, g_next, *gate_args)


def _ffn_up_down(h, wu_ref, wd_ref):
    a = jnp.maximum(jnp.dot(h, wu_ref[...], preferred_element_type=F32), 0.0)
    return jnp.dot((a * a).astype(BF16), wd_ref[...], preferred_element_type=F32)


FFN_EPI_ROWS = 256


def _ffn_kernel(h_ref, wu_ref, wd_ref, x_ref, gp_ref, gn_ref, xn_ref, hn_ref):
    c = pl.program_id(1)
    last = c == pl.num_programs(1) - 1
    _zero_on_first(xn_ref, c)

    @pl.when(jnp.logical_not(last))
    def _():
        xn_ref[...] += _ffn_up_down(h_ref[...], wu_ref, wd_ref)

    @pl.when(last)
    def _():
        for r0 in range(0, xn_ref.shape[0], FFN_EPI_ROWS):
            rs = slice(r0, r0 + FFN_EPI_ROWS)
            y = xn_ref[rs, :] + _ffn_up_down(h_ref[rs, :], wu_ref, wd_ref)
            xn, hn = _residual(y, x_ref[rs, :], gp_ref[...], gn_ref[...])
            xn_ref[rs, :] = xn
            hn_ref[rs, :] = hn


def _ffn_last_kernel(h_ref, wu_ref, wd_ref, x_ref, gp_ref, yp_ref, ys_ref, acc_ref):
    i, c = pl.program_id(0), pl.program_id(1)
    _zero_on_first(acc_ref, c)
    last = c == pl.num_programs(1) - 1
    is_prompt = i < MP // acc_ref.shape[0]

    @pl.when(jnp.logical_not(last))
    def _():
        acc_ref[...] += _ffn_up_down(h_ref[...], wu_ref, wd_ref)

    def finish(y_ref):
        for r0 in range(0, acc_ref.shape[0], FFN_EPI_ROWS):
            rs = slice(r0, r0 + FFN_EPI_ROWS)
            y = acc_ref[rs, :] + _ffn_up_down(h_ref[rs, :], wu_ref, wd_ref)
            y_ref[rs, :] = x_ref[rs, :] + _rms(y, gp_ref[...])

    pl.when(last & is_prompt)(lambda: finish(yp_ref))
    pl.when(last & jnp.logical_not(is_prompt))(lambda: finish(ys_ref))


def _ffn(h, wu_bf, wd_bf, x, g_post, g_next):
    tm, tc = TM_OUT, 1024
    row = pl.BlockSpec((tm, D_MODEL), lambda i, c: (i, 0))
    vec = pl.BlockSpec((1, D_MODEL), lambda i, c: (0, 0))
    w_specs = [pl.BlockSpec((D_MODEL, tc), lambda i, c: (0, c)), pl.BlockSpec((tc, D_MODEL), lambda i, c: (c, 0))]
    grid = (M // tm, D_FF // tc)
    if g_next is not None:
        return pl.pallas_call(
            _ffn_kernel, grid=grid, in_specs=[row] + w_specs + [row, vec, vec], out_specs=[row, row],
            out_shape=[jax.ShapeDtypeStruct((M, D_MODEL), F32), jax.ShapeDtypeStruct((M, D_MODEL), BF16)],
            compiler_params=_cparams(("arbitrary", "arbitrary"), 56), name="ffn",
        )(h, wu_bf, wd_bf, x, g_post, g_next)
    np_t = MP // tm
    return pl.pallas_call(
        _ffn_last_kernel, grid=grid, in_specs=[row] + w_specs + [row, vec],
        out_specs=[pl.BlockSpec((tm, D_MODEL), lambda i, c: (jnp.minimum(i, np_t - 1), 0)),
                   pl.BlockSpec((tm, D_MODEL), lambda i, c: (jnp.maximum(i - np_t, 0), 0))],
        out_shape=[jax.ShapeDtypeStruct((MP, D_MODEL), F32), jax.ShapeDtypeStruct((MS, D_MODEL), F32)],
        scratch_shapes=[pltpu.VMEM((tm, D_MODEL), F32)],
        compiler_params=_cparams(("arbitrary", "arbitrary"), 56), name="ffn_last",
    )(h, wu_bf, wd_bf, x, g_post)


def _mem_kv_kernel(m_ref, g_ref, wk_ref, wv_ref, k_ref, v_ref):
    m = _rms(m_ref[...], g_ref[0]).astype(BF16)
    k_ref[0] = jnp.dot(m, wk_ref[0].astype(BF16), preferred_element_type=F32)
    v_ref[0] = jnp.dot(m, wv_ref[0].astype(BF16), preferred_element_type=F32)


def _mem_kv(mem, g, w_k, w_v):
    rows = BATCH * MEM_LEN
    wspec = pl.BlockSpec((1, D_MODEL, MEM_W), lambda l: (l, 0, 0))
    ospec = pl.BlockSpec((1, rows, MEM_W), lambda l: (l, 0, 0))
    oshape = jax.ShapeDtypeStruct((DEPTH, rows, MEM_W), F32)
    return pl.pallas_call(
        _mem_kv_kernel, grid=(DEPTH,),
        in_specs=[pl.BlockSpec((rows, D_MODEL), lambda l: (0, 0)),
                  pl.BlockSpec((1, 1, D_MODEL), lambda l: (l, 0, 0)), wspec, wspec],
        out_specs=[ospec, ospec], out_shape=[oshape, oshape],
        compiler_params=_cparams(("arbitrary",), 48), name="mem_kv",
    )(mem.reshape(rows, D_MODEL), g.reshape(DEPTH, 1, D_MODEL), w_k, w_v)


def _mem_attend_prompt(q, mk_ref, mv_ref):
    q = q.astype(BF16)
    mk, mv = mk_ref[...].astype(BF16), mv_ref[...].astype(BF16)
    outs = []
    for hh in range(MEM_HEADS):
        cs = slice(hh * MEM_HEAD_DIM, (hh + 1) * MEM_HEAD_DIM)
        s = lax.dot_general(q[:, cs], mk[:, cs], (((1,), (1,)), ((), ())),
                            preferred_element_type=F32) * (MEM_HEAD_DIM ** -0.5)
        p = jnp.exp(s - jnp.max(s, axis=-1, keepdims=True))
        den = jnp.sum(p, axis=-1, keepdims=True)
        outs.append(jnp.dot(p.astype(BF16), mv[:, cs], preferred_element_type=F32) / den)
    return jnp.concatenate(outs, axis=1)


def _mem_attend_sample(q, ck_ref, cv_ref):
    nb, t = ck_ref.shape[0], DEC_SEQ
    qs = jnp.concatenate([q[:, hh * MEM_HEAD_DIM:(hh + 1) * MEM_HEAD_DIM].reshape(nb, t, MEM_HEAD_DIM)
                          for hh in range(MEM_HEADS)], axis=1).astype(BF16)
    s = jnp.einsum('nqd,nkd->nqk', qs, ck_ref[...].astype(BF16),
                   preferred_element_type=F32) * (MEM_HEAD_DIM ** -0.5)
    shape = (MEM_HEADS * t, MEM_LEN * MEM_HEADS)
    own = ((lax.broadcasted_iota(jnp.int32, shape, 1) & (MEM_HEADS - 1))
           == lax.broadcasted_iota(jnp.int32, shape, 0) // t)
    s = jnp.where(own[None], s, NEG_INF)
    p = jnp.exp(s - jnp.max(s, axis=-1, keepdims=True))
    den = jnp.sum(p, axis=-1, keepdims=True)
    o3 = jnp.einsum('nqk,nkd->nqd', p.astype(BF16), cv_ref[...].astype(BF16),
                    preferred_element_type=F32) / den
    return jnp.concatenate([o3[:, hh * t:(hh + 1) * t, :].reshape(nb * t, MEM_HEAD_DIM)
                            for hh in range(MEM_HEADS)], axis=1)


def _mem_kernel(hp_ref, hs_ref, wq_ref, wo_ref, mk_ref, mv_ref, ck_ref, cv_ref, xp_ref, xs_ref, gp_ref, gn_ref,
                xn_ref, hn_ref, os_ref):
    i = pl.program_id(0)
    tm = xn_ref.shape[0]
    n_prompt = MP // tm
    srows = hs_ref.shape[0]
    project = lambda h_ref: jnp.dot(h_ref[...], wq_ref[...], preferred_element_type=F32)

    def finish(o, x_ref):
        for r0 in range(0, tm, EPI_ROWS):
            rs = slice(r0, r0 + EPI_ROWS)
            y = jnp.dot(o[rs].astype(BF16), wo_ref[...], preferred_element_type=F32)
            xn, hn = _residual(y, x_ref[rs, :], gp_ref[...], gn_ref[...])
            xn_ref[rs, :] = xn
            hn_ref[rs, :] = hn

    @pl.when(i < n_prompt)
    def _():
        os_ref[pl.ds(pl.multiple_of(i * srows, srows), srows), :] = _mem_attend_sample(
            project(hs_ref), ck_ref, cv_ref)
        finish(_mem_attend_prompt(project(hp_ref), mk_ref, mv_ref), xp_ref)

    @pl.when(i >= n_prompt)
    def _():
        finish(os_ref[pl.ds(pl.multiple_of((i - n_prompt) * tm, tm), tm), :], xs_ref)


MEM_TM = 256
MEM_NB = DEC_BATCH // (MP // MEM_TM)


def _mem_attention(h_parts, x_parts, wq_bf, wo_bf, mk, mv, cache_k, cache_v, layer, g_post, g_next):
    tm, nb = MEM_TM, MEM_NB
    srows = nb * DEC_SEQ
    sample_row0 = MP if len(h_parts) == 1 else 0
    per_b = SEQ // tm
    n_p = MP // tm
    pidx = lambda i: jnp.minimum(i, n_p - 1)
    row = pl.BlockSpec((tm, D_MODEL), lambda i: (i, 0))
    vec = pl.BlockSpec((1, D_MODEL), lambda i: (0, 0))
    kvspec = pl.BlockSpec((None, MEM_LEN, MEM_W), lambda i: (layer * BATCH + pidx(i) // per_b, 0, 0))
    cspec = pl.BlockSpec((None, nb, MEM_LEN * MEM_HEADS, MEM_HEAD_DIM), lambda i: (layer, pidx(i), 0, 0))
    resident = lambda shape: pl.BlockSpec(shape, lambda i: (layer, 0), pipeline_mode=pl.Buffered(1))
    return pl.pallas_call(
        _mem_kernel, grid=(M // tm,),
        in_specs=[pl.BlockSpec((tm, D_MODEL), lambda i: (pidx(i), 0)),
                  pl.BlockSpec((srows, D_MODEL), lambda i: (sample_row0 // srows + pidx(i), 0)),
                  resident((D_MODEL, MEM_W)), resident((MEM_W, D_MODEL)), kvspec, kvspec,
                  cspec, cspec,
                  pl.BlockSpec((tm, D_MODEL), lambda i: (pidx(i), 0)),
                  pl.BlockSpec((tm, D_MODEL), lambda i: (sample_row0 // tm + jnp.maximum(i - n_p, 0), 0)),
                  vec, vec],
        out_specs=[row, row],
        out_shape=[jax.ShapeDtypeStruct((M, D_MODEL), F32), jax.ShapeDtypeStruct((M, D_MODEL), BF16)],
        scratch_shapes=[pltpu.VMEM((MS, MEM_W), F32)],
        compiler_params=_cparams(("arbitrary",), 48), name="mem_attention",
    )(h_parts[0], h_parts[-1], wq_bf, wo_bf, mk, mv, cache_k, cache_v, x_parts[0], x_parts[-1], g_post, g_next)


def kernel(x_prompt, x_sample, cache_swa_k, cache_swa_v, state_ret, cache_mem_k, cache_mem_v, mem_prompt, norm_mix_pre, norm_mix_post, norm_mem, norm_x_pre, norm_x_post, norm_ffn_pre, norm_ffn_post, w_ab_in, w_ab_out, swa_sinks, gmlp_ln_g, gmlp_ln_b, gmlp_w_s, gmlp_b_s, w_ret_in, w_ret_out, w_mem_q, w_mem_k, w_mem_v, w_mem_o, w_ffn_up, w_ffn_down):
    vec = lambda g, layer: g[layer].reshape(1, D_MODEL)
    swa_c, swa_s1, swa_s2, ret_c, ret_s = _rope_tables()
    mem_k, mem_v = _mem_kv(mem_prompt, norm_mem, w_mem_k, w_mem_v)
    mem_k_b = mem_k.reshape(DEPTH * BATCH, MEM_LEN, MEM_W)
    mem_v_b = mem_v.reshape(DEPTH * BATCH, MEM_LEN, MEM_W)
    ck_mem = cache_mem_k.reshape(DEPTH, DEC_BATCH, MEM_LEN * MEM_HEADS, MEM_HEAD_DIM)
    cv_mem = cache_mem_v.reshape(DEPTH, DEC_BATCH, MEM_LEN * MEM_HEADS, MEM_HEAD_DIM)
    lg = jnp.log1p(-jnp.exp2(-5.0 - jnp.arange(RET_HEADS, dtype=F32)))

    x_parts = (x_prompt.reshape(MP, D_MODEL), x_sample.reshape(MS, D_MODEL))
    h, w_ab_in_bf = _stacked_prenorm(*x_parts, vec(norm_mix_pre, 0), w_ab_in, 0)
    outs = {}
    for layer in range(DEPTH):
        j = layer // 2
        if layer % 2 == 0:
            (q, kv, u, zv, zv_s), (wu_bf, wd_bf, wout_bf, wq_bf, wo_bf) = _ab_in_proj(
                h, w_ab_in_bf, gmlp_ln_g[j].reshape(1, GMLP_WIDTH), gmlp_ln_b[j].reshape(1, GMLP_WIDTH),
                (swa_c, swa_s1, swa_s2), w_ffn_up, w_ffn_down, w_ab_out, w_mem_q, w_mem_o, layer, j)
            cat = _ab_mix_prompt(q, kv, u, zv, swa_sinks[j], gmlp_w_s[j], gmlp_b_s[j])
            cat, nk_s, nv_s = _ab_mix_sample(
                cat, q, kv, u, zv_s, cache_swa_k[j].reshape(DEC_BATCH, WINDOW, SWA_KV_W),
                cache_swa_v[j].reshape(DEC_BATCH, WINDOW, SWA_KV_W), swa_sinks[j], gmlp_w_s[j], gmlp_b_s[j])
            kv_p = jnp.stack([kv[(b + 1) * SEQ - WINDOW:(b + 1) * SEQ] for b in range(BATCH)])
            kv_p = kv_p.reshape(BATCH, WINDOW, 2, SWA_KV_HEADS, SWA_HEAD_DIM)
            outs.setdefault('swa_k_p', []).append(kv_p[:, :, 0])
            outs.setdefault('swa_v_p', []).append(kv_p[:, :, 1])
            outs.setdefault('swa_k_s', []).append(nk_s.reshape(DEC_BATCH, WINDOW, SWA_KV_HEADS, SWA_HEAD_DIM))
            outs.setdefault('swa_v_s', []).append(nv_s.reshape(DEC_BATCH, WINDOW, SWA_KV_HEADS, SWA_HEAD_DIM))
            outs.setdefault('gmlp_v_s', []).append(zv_s.reshape(DEC_BATCH, DEC_SEQ, GMLP_WIDTH))
            mixed, mix_name, gate = (cat,), "ab_out", None
        else:
            (zqk, zv, zg), (wu_bf, wd_bf, wout_bf) = _ret_in_proj(
                h, w_ret_in[j], (ret_c, ret_s), w_ffn_up, w_ffn_down, w_ret_out, layer, j)
            y_p, s_p, y_s, s_s = _retention(zqk, zv, state_ret[j], lg)
            outs.setdefault('ret_p', []).append(s_p)
            outs.setdefault('ret_s', []).append(s_s)
            mixed, mix_name, gate = (y_p, y_s), "ret_out", zg
        x, h = _out_proj(mixed, wout_bf, x_parts, vec(norm_mix_post, layer), vec(norm_x_pre, layer), mix_name,
                         gate)
        x_parts, h_parts = (x,), (h,)
        g_post = vec(norm_x_post, layer)
        g_next = vec(norm_ffn_pre, layer)
        x, h = _mem_attention(h_parts, x_parts, wq_bf, wo_bf, mem_k_b, mem_v_b, ck_mem, cv_mem, layer,
                              g_post, g_next)
        g_next = vec(norm_mix_pre, layer + 1) if layer + 1 < DEPTH else None
        x, h = _ffn(h, wu_bf, wd_bf, x, vec(norm_ffn_post, layer), g_next)
        x_parts = (x,)

    y_prompt, y_sample = x, h
    stack = lambda name: jnp.stack(outs[name])
    mem_shape = (DEPTH, BATCH, MEM_LEN, MEM_HEADS, MEM_HEAD_DIM)
    return (y_prompt.reshape(BATCH, SEQ, D_MODEL), y_sample.reshape(DEC_BATCH, DEC_SEQ, D_MODEL),
            stack('swa_k_p'), stack('swa_v_p'), stack('swa_k_s'), stack('swa_v_s'), stack('gmlp_v_s'),
            stack('ret_p'), stack('ret_s'), mem_k.reshape(mem_shape), mem_v.reshape(mem_shape))
```

```python
import functools
import math

import jax
import jax.numpy as jnp
from jax import lax
from jax.experimental import pallas as pl
from jax.experimental.pallas import tpu as pltpu

F32 = jnp.float32
BF16 = jnp.bfloat16

D_MODEL = 2048
BATCH = 2
SEQ = 4096
DEPTH = 2
DEC_BATCH = 128
DEC_SEQ = 8
PAST_LEN = 8192

SWA_HEADS = 16
SWA_KV_HEADS = 4
SWA_HEAD_DIM = 64
SWA_GROUP = SWA_HEADS // SWA_KV_HEADS
WINDOW = 128
ROPE_THETA = 500000.0
ROPE_DIM = SWA_HEAD_DIM // 4
ROPE_HALF = ROPE_DIM // 2

GMLP_GROUPS = 4
GMLP_WIDTH = 1024
GMLP_GROUP_DIM = GMLP_WIDTH // GMLP_GROUPS
GMLP_CHUNK = 128

SWA_Q_W = SWA_HEADS * SWA_HEAD_DIM
SWA_KV_W = SWA_KV_HEADS * SWA_HEAD_DIM
AB_IN = SWA_Q_W + 2 * SWA_KV_W + 2 * GMLP_WIDTH
AB_OUT = SWA_Q_W + GMLP_WIDTH

RET_HEADS = 8
RET_KEY_DIM = D_MODEL // RET_HEADS
RET_VAL_DIM = 2 * RET_KEY_DIM
RET_CHUNK = 128
RET_THETA = 10000.0
RET_QK_W = RET_HEADS * RET_KEY_DIM
RET_V_W = RET_HEADS * RET_VAL_DIM
RET_IN = 2 * RET_QK_W + 2 * RET_V_W

MEM_LEN = 256
MEM_HEADS = 4
MEM_HEAD_DIM = 128
MEM_W = MEM_HEADS * MEM_HEAD_DIM

D_FF = 4 * D_MODEL
EPS = 1e-6
NEG_INF = -1e30

MP = BATCH * SEQ
MS = DEC_BATCH * DEC_SEQ
M = MP + MS
LANES = 128
MIB = 1024 * 1024

TM_IN = 1024
TAB_ROWS = SEQ + MS
TM_OUT = 512


def _cparams(sem, vmem_mib):
    return pltpu.CompilerParams(dimension_semantics=sem, vmem_limit_bytes=vmem_mib * MIB)


def _rms(x, g):
    return x * lax.rsqrt(jnp.mean(x * x, axis=-1, keepdims=True) + EPS) * g


def _residual(y, x, g_post, g_next):
    xn = x + _rms(y, g_post)
    return xn, _rms(xn, g_next).astype(BF16)


def _rope_table_block(block, sc_ref, s1_ref, s2_ref, rc_ref, rs_ref):
    tr = sc_ref.shape[0]
    r = lax.broadcasted_iota(jnp.int32, (tr, LANES), 0) + block * tr
    pos = jnp.where(r < SEQ, r, PAST_LEN + ((r - SEQ) & (DEC_SEQ - 1))).astype(F32)
    lane = lax.broadcasted_iota(jnp.int32, (tr, LANES), 1)
    d = lane & (SWA_HEAD_DIM - 1)
    inv = jnp.exp((d & (ROPE_HALF - 1)).astype(F32) * (-math.log(ROPE_THETA) / ROPE_HALF))
    ang = pos * inv
    c, s = jnp.cos(ang), jnp.sin(ang)
    sc_ref[...] = jnp.where(d < ROPE_DIM, c, 1.0)
    s1_ref[...] = jnp.where(d < ROPE_HALF, -s, 0.0)
    s2_ref[...] = jnp.where((d >= ROPE_HALF) & (d < ROPE_DIM), s, 0.0)
    inv_r = jnp.exp(lane.astype(F32) * (-math.log(RET_THETA) / LANES))
    ang_r = pos * inv_r
    rc_ref[...] = jnp.cos(ang_r)
    rs_ref[...] = jnp.sin(ang_r)


def _norm_kernel(xp_ref, xs_ref, g_ref, w_ref, h_ref, wbf_ref, *table_refs, n_table_blocks):
    def emit(src_ref):
        h_ref[...] = _rms(src_ref[...], g_ref[...]).astype(BF16)

    i = pl.program_id(0)
    is_prompt = i < MP // h_ref.shape[0]
    pl.when(is_prompt)(lambda: emit(xp_ref))
    pl.when(jnp.logical_not(is_prompt))(lambda: emit(xs_ref))
    wbf_ref[...] = w_ref[...].astype(BF16)
    pl.when(i < n_table_blocks)(lambda: _rope_table_block(i, *table_refs))


def _stacked_prenorm(x_prompt, x_sample, g, w, layer):
    tm = TM_OUT
    n_p = MP // tm
    w_src, w_in_spec, w_shape, w_out_spec = _row_cast_job(w, layer, n_p)
    tab_spec = pl.BlockSpec((TAB_ROWS // n_p, LANES), lambda i: (jnp.minimum(i, n_p - 1), 0))
    tab_shape = jax.ShapeDtypeStruct((TAB_ROWS, LANES), F32)
    outs = pl.pallas_call(
        functools.partial(_norm_kernel, n_table_blocks=n_p), grid=(M // tm,),
        in_specs=[pl.BlockSpec((tm, D_MODEL), lambda i: (jnp.minimum(i, n_p - 1), 0)),
                  pl.BlockSpec((tm, D_MODEL), lambda i: (jnp.maximum(i - n_p, 0), 0)),
                  pl.BlockSpec((1, D_MODEL), lambda i: (0, 0)), w_in_spec],
        out_specs=[pl.BlockSpec((tm, D_MODEL), lambda i: (i, 0)), w_out_spec] + [tab_spec] * 5,
        out_shape=[jax.ShapeDtypeStruct((M, D_MODEL), BF16), w_shape] + [tab_shape] * 5,
        compiler_params=_cparams(("arbitrary",), 40), name="prenorm",
    )(x_prompt, x_sample, g, w_src)
    return outs[0], outs[1], tuple(outs[2:])


W_SUB = 512


def _inproj_kernel(h_ref, *refs, n_sub, n_side, n_cast, epilogue):
    w_refs, refs = refs[:n_sub], refs[n_sub:]
    side, refs = refs[:n_side], refs[n_side:]
    cast_src, refs = refs[:n_cast], refs[n_cast:]
    wbf_ref = refs[-1]
    outs, cast_dst = refs[:len(refs) - 1 - n_cast], refs[len(refs) - 1 - n_cast:-1]

    @pl.when(pl.program_id(1) == 0)
    def _():
        for s, w_ref in enumerate(w_refs):
            wbf_ref[:, s * W_SUB:(s + 1) * W_SUB] = w_ref[...].astype(BF16)

    acc = jnp.dot(h_ref[...], wbf_ref[...], preferred_element_type=F32)
    epilogue(acc, side, outs)
    for src, dst in zip(cast_src, cast_dst):
        dst[...] = src[...].astype(BF16)


def _tab_index(j, i):
    return (jnp.where(i < MP // TM_IN, i % (SEQ // TM_IN), SEQ // TM_IN), 0)


CAST_STEPS = MP // TM_IN


def _cast_job(w, layer, nrb, ncb, block_of):
    rows, cols = w.shape[1:]
    blk = (rows // nrb, cols // ncb)
    index = lambda j, i: block_of(j, jnp.minimum(i, CAST_STEPS - 1))
    return (w, pl.BlockSpec((None,) + blk, lambda j, i: (layer,) + tuple(index(j, i))),
            jax.ShapeDtypeStruct((rows, cols), BF16), pl.BlockSpec(blk, index))


def _inproj(h, w, col0, ncols, tn, epilogue, side, side_specs, out_shapes, out_specs, name, vmem_mib,
            casts=()):
    tm = TM_IN
    k = h.shape[1]
    n_sub = tn // W_SUB
    w_specs = [pl.BlockSpec((k, W_SUB), lambda j, i, s=s: (0, col0 // W_SUB + j * n_sub + s))
               for s in range(n_sub)]
    as_list = lambda v: list(v) if isinstance(v, (list, tuple)) else [v]
    outs = pl.pallas_call(
        functools.partial(_inproj_kernel, n_sub=n_sub, n_side=len(side), n_cast=len(casts), epilogue=epilogue),
        grid=(ncols // tn, M // tm),
        in_specs=([pl.BlockSpec((tm, k), lambda j, i: (i, 0))] + w_specs + side_specs
                  + [c[1] for c in casts]),
        out_specs=as_list(out_specs) + [c[3] for c in casts],
        out_shape=as_list(out_shapes) + [c[2] for c in casts],
        scratch_shapes=[pltpu.VMEM((k, tn), BF16)],
        compiler_params=_cparams(("arbitrary", "arbitrary"), vmem_mib), name=name,
    )(h, *([w] * n_sub), *side, *[c[0] for c in casts])
    n_main = len(outs) - len(casts)
    main = outs[0] if n_main == 1 else tuple(outs[:n_main])
    return (main, tuple(outs[n_main:])) if casts else main


def _swa_rope(x, c, s1, s2):
    return x * c + pltpu.roll(x, LANES - ROPE_HALF, 1) * s1 + pltpu.roll(x, ROPE_HALF, 1) * s2


def _epi_swa_q(acc, side, outs):
    c, s1, s2 = (t[...] for t in side)
    (q_ref,) = outs
    for cc in range(acc.shape[1] // LANES):
        sl = slice(cc * LANES, (cc + 1) * LANES)
        q_ref[:, sl] = (_swa_rope(acc[:, sl], c, s1, s2) * (SWA_HEAD_DIM ** -0.5)).astype(BF16)


def _epi_swa_kv(acc, side, outs):
    c, s1, s2 = (t[...] for t in side)
    (kv_ref,) = outs
    for cc in range(SWA_KV_W // LANES):
        sl = slice(cc * LANES, (cc + 1) * LANES)
        kv_ref[:, sl] = _swa_rope(acc[:, sl], c, s1, s2)
    kv_ref[:, SWA_KV_W:] = acc[:, SWA_KV_W:]


def _epi_gelu(acc, side, outs):
    outs[0][...] = jax.nn.gelu(acc).astype(BF16)


def _epi_ret_qk(acc, side, outs):
    cos, sin = (t[...] for t in side)
    (z_ref,) = outs
    tn = acc.shape[1]
    scale = jnp.where(pl.program_id(0) < RET_QK_W // tn, 1.0, RET_KEY_DIM ** -0.5)
    for hh in range(tn // RET_KEY_DIM):
        c0 = hh * RET_KEY_DIM
        x1, x2 = acc[:, c0:c0 + LANES], acc[:, c0 + LANES:c0 + 2 * LANES]
        z_ref[hh, :, :LANES] = ((x1 * cos - x2 * sin) * scale).astype(BF16)
        z_ref[hh, :, LANES:] = ((x2 * cos + x1 * sin) * scale).astype(BF16)


def _per_head(fn):
    def epilogue(acc, side, outs):
        (z_ref,) = outs
        for hh in range(z_ref.shape[0]):
            z_ref[hh] = fn(acc[:, hh * RET_VAL_DIM:(hh + 1) * RET_VAL_DIM]).astype(BF16)
    return epilogue


_epi_cast = _per_head(lambda a: a)
_epi_silu = _per_head(lambda a: a / (1.0 + jnp.exp(-a)))


AB_TM = 256


def _ab_in_kernel(h_ref, w_ref, sc_ref, s1_ref, s2_ref, lng_ref, lnb_ref, *refs, n_cast):
    cast_src, (q_ref, kv_ref, u_ref, zv_ref, zvs_ref), cast_dst = refs[:n_cast], refs[n_cast:n_cast + 5], refs[n_cast + 5:]
    h = h_ref[...]
    tabs = (sc_ref, s1_ref, s2_ref)
    seg = lambda c0, n: jnp.dot(h, w_ref[:, c0:c0 + n], preferred_element_type=F32)
    _epi_swa_q(seg(0, SWA_Q_W), tabs, (q_ref,))
    _epi_swa_kv(seg(SWA_Q_W, 2 * SWA_KV_W), tabs, (kv_ref,))
    _epi_gelu(seg(SWA_Q_W + 2 * SWA_KV_W, GMLP_WIDTH), (), (u_ref,))
    a = jax.nn.gelu(seg(AB_IN - GMLP_WIDTH, GMLP_WIDTH))
    xc = a - jnp.mean(a, axis=-1, keepdims=True)
    y = xc * lax.rsqrt(jnp.mean(xc * xc, axis=-1, keepdims=True) + EPS) * lng_ref[...] + lnb_ref[...]
    zv_ref[...] = y.astype(BF16)

    @pl.when(pl.program_id(0) >= MP // AB_TM)
    def _():
        zvs_ref[...] = y

    for src, dst in zip(cast_src, cast_dst):
        dst[...] = src[...].astype(BF16)


def _row_cast_job(w, layer, n_blocks):
    rows, cols = w.shape[1:]
    blk = (rows // n_blocks, cols)
    index = lambda i: (jnp.minimum(i, n_blocks - 1), 0)
    return (w, pl.BlockSpec((None,) + blk, lambda i: (layer,) + index(i)),
            jax.ShapeDtypeStruct((rows, cols), BF16), pl.BlockSpec(blk, index))


def _ab_in_proj(h, w_in_bf, ln_g, ln_b, tabs, w_up, w_out, w_mem_q, w_mem_o, layer, j):
    tm = AB_TM
    n_p = MP // tm
    stacked = lambda w: w.reshape(1, w.shape[0] * w.shape[1], w.shape[2])
    casts = [_row_cast_job(w_up, layer, n_p), _row_cast_job(w_out, j, n_p),
             _row_cast_job(stacked(w_mem_q), 0, n_p), _row_cast_job(stacked(w_mem_o), 0, n_p)]
    tab_rows = SEQ // tm
    tab_spec = pl.BlockSpec((tm, LANES), lambda i: (jnp.where(i < n_p, i % tab_rows, tab_rows + i - n_p), 0))
    vec_spec = pl.BlockSpec((1, GMLP_WIDTH), lambda i: (0, 0))
    row_out = lambda n: pl.BlockSpec((tm, n), lambda i: (i, 0))
    outs = pl.pallas_call(
        functools.partial(_ab_in_kernel, n_cast=len(casts)), grid=(M // tm,),
        in_specs=([pl.BlockSpec((tm, D_MODEL), lambda i: (i, 0)),
                   pl.BlockSpec((D_MODEL, AB_IN), lambda i: (0, 0), pipeline_mode=pl.Buffered(1))]
                  + [tab_spec] * 3 + [vec_spec] * 2 + [c[1] for c in casts]),
        out_specs=[row_out(SWA_Q_W), row_out(2 * SWA_KV_W), row_out(GMLP_WIDTH), row_out(GMLP_WIDTH),
                   pl.BlockSpec((tm, GMLP_WIDTH), lambda i: (jnp.maximum(i - n_p, 0), 0))]
                  + [c[3] for c in casts],
        out_shape=[jax.ShapeDtypeStruct((M, SWA_Q_W), BF16), jax.ShapeDtypeStruct((M, 2 * SWA_KV_W), F32),
                   jax.ShapeDtypeStruct((M, GMLP_WIDTH), BF16), jax.ShapeDtypeStruct((M, GMLP_WIDTH), BF16),
                   jax.ShapeDtypeStruct((MS, GMLP_WIDTH), F32)] + [c[2] for c in casts],
        compiler_params=_cparams(("arbitrary",), 52), name="ab_in",
    )(h, w_in_bf, *tabs, ln_g, ln_b, *[c[0] for c in casts])
    return tuple(outs[:5]), tuple(outs[5:])


def _ret_in_proj(h, w_in, tabs, w_up, w_down, w_out, layer, j):
    rc, rs = tabs
    tn = 1024
    tab_spec = pl.BlockSpec((TM_IN, LANES), _tab_index)
    by_tile = lambda j, i: (j, i)
    head_major = jax.ShapeDtypeStruct((RET_HEADS, M, 2 * RET_KEY_DIM), BF16)
    seg = lambda col0, ncols, epi, side, specs, out_spec, name, w, lead: _inproj(
        h, w_in, col0, ncols, tn, epi, side, specs, head_major, out_spec, name, 56,
        casts=[_cast_job(w, lead, ncols // tn, CAST_STEPS, by_tile)])
    qk_heads = tn // RET_KEY_DIM
    q_tiles = RET_QK_W // tn
    qk_spec = pl.BlockSpec((qk_heads, TM_IN, RET_KEY_DIM), lambda j, i: (j % q_tiles, i, j // q_tiles))
    v_spec = pl.BlockSpec((tn // RET_VAL_DIM, TM_IN, RET_VAL_DIM), lambda j, i: (j, i, 0))
    zqk, (wu_bf,) = seg(0, 2 * RET_QK_W, _epi_ret_qk, [rc, rs], [tab_spec] * 2, qk_spec, "ret_in_qk",
                        w_up, layer)
    zv, (wd_bf,) = seg(2 * RET_QK_W, RET_V_W, _epi_cast, [], [], v_spec, "ret_in_v", w_down, layer)
    zg, (wout_bf,) = seg(2 * RET_QK_W + RET_V_W, RET_V_W, _epi_silu, [], [], v_spec, "ret_in_g", w_out, j)
    return (zqk, zv, zg), (wu_bf, wd_bf, wout_bf)


AB_ROWS = 512


def _sink_softmax(s, sink):
    m = jnp.maximum(jnp.max(s, axis=-1, keepdims=True), sink)
    p = jnp.exp(s - m)
    den = jnp.sum(p, axis=-1, keepdims=True) + jnp.exp(sink - m)
    return p, den


def _ab_prompt_kernel(sink_ref, q_ref, kv_ref, kvp_ref, u_ref, zv_ref, ws_ref, bs_ref, wsrc_ref, cat_ref,
                      wdst_ref, kd_ref, vd_ref):
    wdst_ref[...] = wsrc_ref[...].astype(BF16)
    r = pl.program_id(1)
    low = lax.broadcasted_iota(jnp.int32, (1, LANES), 1) < SWA_HEAD_DIM

    def both_halves(x, half):
        sw = pltpu.roll(x, SWA_HEAD_DIM, 1)
        return (jnp.where(low, x, sw) if half == 0 else jnp.where(low, sw, x)).astype(BF16)

    for kh in range(SWA_KV_HEADS):
        kcol = slice((kh // 2) * LANES, (kh // 2 + 1) * LANES)
        vcol = slice(SWA_KV_W + (kh // 2) * LANES, SWA_KV_W + (kh // 2 + 1) * LANES)
        kd_ref[kh, :WINDOW, :] = both_halves(kvp_ref[:, kcol], kh % 2)
        kd_ref[kh, WINDOW:, :] = both_halves(kv_ref[:, kcol], kh % 2)
        vd_ref[kh, :WINDOW, :] = both_halves(kvp_ref[:, vcol], kh % 2)
        vd_ref[kh, WINDOW:, :] = both_halves(kv_ref[:, vcol], kh % 2)

    t = lax.broadcasted_iota(jnp.int32, (WINDOW, 2 * WINDOW), 0)
    sk = lax.broadcasted_iota(jnp.int32, (WINDOW, 2 * WINDOW), 1)
    band = (sk <= t + WINDOW) & (sk > t)
    sink_slot = sk == t
    tril = (lax.broadcasted_iota(jnp.int32, (GMLP_CHUNK, GMLP_CHUNK), 0)
            >= lax.broadcasted_iota(jnp.int32, (GMLP_CHUNK, GMLP_CHUNK), 1))

    for sb in range(AB_ROWS // WINDOW):
        rs = slice(sb * WINDOW, (sb + 1) * WINDOW)
        ks = slice(sb * WINDOW, sb * WINDOW + 2 * WINDOW)
        valid = band & ((sk >= WINDOW) | (r * (AB_ROWS // WINDOW) + sb > 0))
        for kh in range(SWA_KV_HEADS):
            qm = []
            for g in range(SWA_GROUP):
                hd = kh * SWA_GROUP + g
                qb = q_ref[rs, (hd // 2) * LANES:(hd // 2 + 1) * LANES]
                qm.append(jnp.where(low if hd % 2 == 0 else jnp.logical_not(low), qb, jnp.zeros_like(qb)))
            s = lax.dot_general(jnp.concatenate(qm, axis=0), kd_ref[kh, ks, :], (((1,), (1,)), ((), ())),
                                preferred_element_type=F32)
            pv, den = [], []
            for g in range(SWA_GROUP):
                sg = s[g * WINDOW:(g + 1) * WINDOW]
                sg = jnp.where(valid, sg, jnp.where(sink_slot, sink_ref[kh * SWA_GROUP + g], NEG_INF))
                p = jnp.exp(sg - jnp.max(sg, axis=-1, keepdims=True))
                den.append(jnp.sum(p, axis=-1, keepdims=True))
                pv.append(jnp.where(sink_slot, 0.0, p).astype(BF16))
            o = jnp.dot(jnp.concatenate(pv, axis=0), vd_ref[kh, ks, :], preferred_element_type=F32)
            for pr in range(SWA_GROUP // 2):
                lo = o[(2 * pr) * WINDOW:(2 * pr + 1) * WINDOW] / den[2 * pr]
                hi = o[(2 * pr + 1) * WINDOW:(2 * pr + 2) * WINDOW] / den[2 * pr + 1]
                c0 = (kh * (SWA_GROUP // 2) + pr) * LANES
                cat_ref[rs, c0:c0 + LANES] = jnp.where(low, lo, hi).astype(BF16)
        zb = zv_ref[rs, :]
        ub = u_ref[rs, :]
        for g in range(GMLP_GROUPS):
            gs = slice(g * GMLP_GROUP_DIM, (g + 1) * GMLP_GROUP_DIM)
            w = jnp.where(tril, ws_ref[g], 0.0).astype(BF16)
            mixed = jnp.dot(w, zb[:, gs], preferred_element_type=F32) + bs_ref[g]
            cat_ref[rs, SWA_Q_W + g * GMLP_GROUP_DIM:SWA_Q_W + (g + 1) * GMLP_GROUP_DIM] = (
                ub[:, gs].astype(F32) * mixed).astype(BF16)


def _ab_mix_prompt(q, kv, u, zv, sinks, w_s, b_s, w_side, layer):
    nr = SEQ // AB_ROWS
    sub = AB_ROWS // WINDOW
    row = lambda b, r: (b * nr + r, 0)
    w_rows, w_cols = w_side.shape[1:]
    w_blk = (w_rows // (BATCH * nr), w_cols)
    return pl.pallas_call(
        _ab_prompt_kernel, grid=(BATCH, nr),
        in_specs=[pl.BlockSpec(memory_space=pltpu.SMEM),
                  pl.BlockSpec((AB_ROWS, SWA_Q_W), row),
                  pl.BlockSpec((AB_ROWS, 2 * SWA_KV_W), row),
                  pl.BlockSpec((WINDOW, 2 * SWA_KV_W), lambda b, r: (jnp.maximum((b * nr + r) * sub - 1, 0), 0)),
                  pl.BlockSpec((AB_ROWS, GMLP_WIDTH), row),
                  pl.BlockSpec((AB_ROWS, GMLP_WIDTH), row),
                  pl.BlockSpec((GMLP_GROUPS, GMLP_CHUNK, GMLP_CHUNK), lambda b, r: (0, 0, 0)),
                  pl.BlockSpec((GMLP_GROUPS, GMLP_CHUNK, 1), lambda b, r: (0, 0, 0)),
                  pl.BlockSpec((None,) + w_blk, lambda b, r: (layer, b * nr + r, 0))],
        out_specs=[pl.BlockSpec((AB_ROWS, AB_OUT), row), pl.BlockSpec(w_blk, row)],
        out_shape=[jax.ShapeDtypeStruct((M, AB_OUT), BF16), jax.ShapeDtypeStruct((w_rows, w_cols), BF16)],
        scratch_shapes=[pltpu.VMEM((SWA_KV_HEADS, WINDOW + AB_ROWS, LANES), BF16)] * 2,
        compiler_params=_cparams(("arbitrary", "arbitrary"), 48), name="ab_mix_prompt",
    )(sinks, q, kv, kv, u, zv, w_s, b_s.reshape(GMLP_GROUPS, GMLP_CHUNK, 1), w_side)


AB_NB = 16
AB_KEYS = 2 * WINDOW


def _ab_sample_kernel(sink_ref, q_ref, kv_ref, u_ref, zv_ref, ck_ref, cv_ref, ws_ref, bs_ref, cat_in_ref,
                      cat_ref, nk_ref, nv_ref):
    del cat_in_ref
    nb, t = AB_NB, DEC_SEQ
    kv3 = kv_ref[...].reshape(nb, t, 2 * SWA_KV_W)
    kn, vn = kv3[:, :, :SWA_KV_W], kv3[:, :, SWA_KV_W:]
    ck, cv = ck_ref[...], cv_ref[...]
    nk_ref[:, :WINDOW - t, :] = ck[:, t:, :]
    nk_ref[:, WINDOW - t:, :] = kn
    nv_ref[:, :WINDOW - t, :] = cv[:, t:, :]
    nv_ref[:, WINDOW - t:, :] = vn
    pad = jnp.zeros((nb, AB_KEYS - WINDOW - t, SWA_KV_W), F32)
    kall = jnp.concatenate([ck, kn, pad], axis=1).astype(BF16)
    vall = jnp.concatenate([cv, vn, pad], axis=1).astype(BF16)

    rows = SWA_GROUP * t
    tq = lax.broadcasted_iota(jnp.int32, (rows, AB_KEYS), 0) & (t - 1)
    sk = lax.broadcasted_iota(jnp.int32, (rows, AB_KEYS), 1)
    valid = ((sk < WINDOW) & (sk > tq)) | ((sk >= WINDOW) & (sk - WINDOW <= tq))

    q3 = q_ref[...].astype(F32).reshape(nb, t, SWA_Q_W)
    pieces = []
    for kh in range(SWA_KV_HEADS):
        heads = [kh * SWA_GROUP + g for g in range(SWA_GROUP)]
        qs = jnp.concatenate([q3[:, :, hd * SWA_HEAD_DIM:(hd + 1) * SWA_HEAD_DIM] for hd in heads],
                             axis=1).astype(BF16)
        cs = slice(kh * SWA_HEAD_DIM, (kh + 1) * SWA_HEAD_DIM)
        s = jnp.einsum('nqd,nkd->nqk', qs, kall[:, :, cs], preferred_element_type=F32)
        s = jnp.where(valid[None], s, NEG_INF)
        sink = jnp.concatenate([jnp.full((t, 1), sink_ref[hd], F32) for hd in heads], axis=0)[None]
        p, den = _sink_softmax(s, sink)
        o = jnp.einsum('nqk,nkd->nqd', p.astype(BF16), vall[:, :, cs], preferred_element_type=F32) / den
        pieces += [o[:, g * t:(g + 1) * t, :] for g in range(SWA_GROUP)]
    attn = jnp.concatenate(pieces, axis=2).reshape(nb * t, SWA_Q_W)
    cat_ref[:, :SWA_Q_W] = attn.astype(BF16)

    zv3 = zv_ref[...].reshape(nb, t, GMLP_WIDTH)
    u3 = u_ref[...].astype(F32).reshape(nb, t, GMLP_WIDTH)
    irow = lax.broadcasted_iota(jnp.int32, (t, GMLP_GROUP_DIM), 0)
    for g in range(GMLP_GROUPS):
        gs = slice(g * GMLP_GROUP_DIM, (g + 1) * GMLP_GROUP_DIM)
        zg = zv3[:, :, gs]
        mixed = jnp.zeros((nb, t, GMLP_GROUP_DIM), F32) + bs_ref[g][None]
        for j in range(t):
            wcol = jnp.where(irow >= j, ws_ref[g, j], 0.0)
            mixed = mixed + zg[:, j:j + 1, :] * wcol[None]
        gate = (u3[:, :, gs] * mixed).reshape(nb * t, GMLP_GROUP_DIM)
        cat_ref[:, SWA_Q_W + g * GMLP_GROUP_DIM:SWA_Q_W + (g + 1) * GMLP_GROUP_DIM] = gate.astype(BF16)


def _ab_mix_sample(cat, q, kv, u, zv_s, cache_k, cache_v, sinks, w_s, b_s):
    nb, t = AB_NB, DEC_SEQ
    rows = nb * t
    off = MP // rows
    srow = lambda i: (off + i, 0)
    wt = jnp.broadcast_to(jnp.swapaxes(w_s[:, :t, :t], 1, 2)[..., None], (GMLP_GROUPS, t, t, GMLP_GROUP_DIM))
    bt = jnp.broadcast_to(b_s[:, :t, None], (GMLP_GROUPS, t, GMLP_GROUP_DIM))
    cspec = pl.BlockSpec((nb, WINDOW, SWA_KV_W), lambda i: (i, 0, 0))
    cshape = jax.ShapeDtypeStruct((DEC_BATCH, WINDOW, SWA_KV_W), F32)
    return pl.pallas_call(
        _ab_sample_kernel, grid=(DEC_BATCH // nb,),
        in_specs=[pl.BlockSpec(memory_space=pltpu.SMEM),
                  pl.BlockSpec((rows, SWA_Q_W), srow),
                  pl.BlockSpec((rows, 2 * SWA_KV_W), srow),
                  pl.BlockSpec((rows, GMLP_WIDTH), srow),
                  pl.BlockSpec((rows, GMLP_WIDTH), lambda i: (i, 0)),
                  cspec, cspec,
                  pl.BlockSpec((GMLP_GROUPS, t, t, GMLP_GROUP_DIM), lambda i: (0, 0, 0, 0)),
                  pl.BlockSpec((GMLP_GROUPS, t, GMLP_GROUP_DIM), lambda i: (0, 0, 0)),
                  pl.BlockSpec(memory_space=pl.ANY)],
        out_specs=[pl.BlockSpec((rows, AB_OUT), srow), cspec, cspec],
        out_shape=[jax.ShapeDtypeStruct((M, AB_OUT), BF16), cshape, cshape],
        input_output_aliases={9: 0},
        compiler_params=_cparams(("arbitrary",), 48), name="ab_mix_sample",
    )(sinks, q, kv, u, zv_s, cache_k, cache_v, wt, bt, cat)


RET_ROWS = 1024
RET_BLOCK = 256


def _ret_prompt_part(lg, first_run, qk_ref, v_ref, s_ref):
    L = RET_BLOCK

    @pl.when(first_run)
    def _():
        s_ref[...] = jnp.zeros_like(s_ref)

    diff = (lax.broadcasted_iota(jnp.int32, (L, L), 0) - lax.broadcasted_iota(jnp.int32, (L, L), 1)).astype(F32)
    dmat = jnp.where(diff >= 0, jnp.exp(lg * jnp.maximum(diff, 0.0)), 0.0)
    idx = lax.broadcasted_iota(jnp.int32, (L, 1), 0).astype(F32)
    q_dec = jnp.exp(lg * (idx + 1.0))
    k_dec = jnp.exp(lg * (L - 1.0 - idx))
    s_dec = jnp.exp(lg * jnp.full((1, 1), float(L), F32))

    ys = []
    for ci in range(RET_ROWS // L):
        rs = slice(ci * L, (ci + 1) * L)
        q, k, v = qk_ref[rs, :RET_KEY_DIM], qk_ref[rs, RET_KEY_DIM:], v_ref[rs, :]
        att = lax.dot_general(q, k, (((1,), (1,)), ((), ())), preferred_element_type=F32) * dmat
        s0 = s_ref[...]
        o = (jnp.dot(att.astype(BF16), v, preferred_element_type=F32)
             + jnp.dot(q, s0.astype(BF16), preferred_element_type=F32) * q_dec)
        kd = (k.astype(F32) * k_dec).astype(BF16)
        s_ref[...] = s0 * s_dec + lax.dot_general(kd, v, (((0,), (0,)), ((), ())), preferred_element_type=F32)
        on = o * lax.rsqrt(jnp.mean(o * o, axis=-1, keepdims=True) + EPS)
        ys.append(on.astype(BF16))
    return jnp.concatenate(ys, axis=0)


RET_NB = DEC_BATCH // (BATCH * (SEQ // RET_ROWS))


def _ret_sample_part(lg, qk_ref, v_ref, s0_ref, s1_ref):
    nb, t = RET_NB, DEC_SEQ
    rows = nb * t
    q, k, v = qk_ref[:, :RET_KEY_DIM], qk_ref[:, RET_KEY_DIM:], v_ref[...]
    ri = lax.broadcasted_iota(jnp.int32, (rows, rows), 0)
    ci = lax.broadcasted_iota(jnp.int32, (rows, rows), 1)
    d = ((ri & (t - 1)) - (ci & (t - 1))).astype(F32)
    same = (ri // t) == (ci // t)
    dmat = jnp.where(same & (d >= 0), jnp.exp(lg * jnp.maximum(d, 0.0)), 0.0)
    att = lax.dot_general(q, k, (((1,), (1,)), ((), ())), preferred_element_type=F32) * dmat
    o_intra = jnp.dot(att.astype(BF16), v, preferred_element_type=F32)

    rcol = lax.broadcasted_iota(jnp.int32, (rows, 1), 0)
    tt = (rcol & (t - 1)).astype(F32)
    q_dec = jnp.exp(lg * (tt + 1.0))
    k_dec = jnp.exp(lg * (t - 1.0 - tt))
    s_dec = jnp.exp(lg * jnp.full((1, 1), float(t), F32))
    kd = (k.astype(F32) * k_dec).astype(BF16)
    qf = q.astype(F32)
    vf = v.astype(F32)
    o_rows = []
    for n in range(nb):
        s0 = s0_ref[n]
        qn = qf[n * t:(n + 1) * t].astype(BF16)
        o_rows.append(jnp.dot(qn, s0.astype(BF16), preferred_element_type=F32))
        vn = jnp.where(rcol // t == n, vf, 0.0).astype(BF16)
        s1_ref[n] = s0 * s_dec + lax.dot_general(kd, vn, (((0,), (0,)), ((), ())), preferred_element_type=F32)
    o = o_intra + jnp.concatenate(o_rows, axis=0) * q_dec
    on = o * lax.rsqrt(jnp.mean(o * o, axis=-1, keepdims=True) + EPS)
    return on.astype(BF16)


def _retention_kernel(lg_ref, qk_ref, v_ref, sqk_ref, sv_ref, s0_ref,
                      yp_ref, sfin_ref, ys_ref, s1_ref, s_ref):
    c = pl.program_id(2)
    lg = lg_ref[pl.program_id(1)]
    yp_ref[...] = _ret_prompt_part(lg, c == 0, qk_ref, v_ref, s_ref)

    @pl.when(c == pl.num_programs(2) - 1)
    def _():
        sfin_ref[...] = s_ref[...]

    ys_ref[...] = _ret_sample_part(lg, sqk_ref, sv_ref, s0_ref, s1_ref)


def _retention(zqk, zv, state, lg):
    nc = SEQ // RET_ROWS
    srows = RET_NB * DEC_SEQ
    soff = MP // srows
    prow = lambda b, c: b * nc + c
    width = 2 * RET_KEY_DIM
    head_specs = lambda rows, row: [pl.BlockSpec((None, rows, width), lambda b, h, c: (h, row(b, c), 0))] * 2
    sspec = pl.BlockSpec((RET_NB, None, RET_KEY_DIM, RET_VAL_DIM), lambda b, h, c: (prow(b, c), h, 0, 0))
    return pl.pallas_call(
        _retention_kernel, grid=(BATCH, RET_HEADS, nc),
        in_specs=([pl.BlockSpec(memory_space=pltpu.SMEM)] + head_specs(RET_ROWS, prow)
                  + head_specs(srows, lambda b, c: soff + prow(b, c)) + [sspec]),
        out_specs=[pl.BlockSpec((RET_ROWS, RET_VAL_DIM), lambda b, h, c: (prow(b, c), h)),
                   pl.BlockSpec((None, None, RET_KEY_DIM, RET_VAL_DIM), lambda b, h, c: (b, h, 0, 0)),
                   pl.BlockSpec((srows, RET_VAL_DIM), lambda b, h, c: (prow(b, c), h)),
                   sspec],
        out_shape=[jax.ShapeDtypeStruct((MP, RET_V_W), BF16),
                   jax.ShapeDtypeStruct((BATCH, RET_HEADS, RET_KEY_DIM, RET_VAL_DIM), F32),
                   jax.ShapeDtypeStruct((MS, RET_V_W), BF16),
                   jax.ShapeDtypeStruct((DEC_BATCH, RET_HEADS, RET_KEY_DIM, RET_VAL_DIM), F32)],
        scratch_shapes=[pltpu.VMEM((RET_KEY_DIM, RET_VAL_DIM), F32)],
        compiler_params=_cparams(("arbitrary", "arbitrary", "arbitrary"), 56), name="retention",
    )(lg, zqk, zv, zqk, zv, state)


EPI_ROWS = 128


def _zero_on_first(acc_ref, k):
    @pl.when(k == 0)
    def _():
        acc_ref[...] = jnp.zeros_like(acc_ref)


def _outproj_kernel(ap_ref, as_ref, w_ref, xp_ref, xs_ref, gp_ref, gn_ref, *refs):
    gate_ref, (xn_ref, hn_ref) = (refs[0] if len(refs) == 3 else None), refs[-2:]
    tm = xn_ref.shape[0]
    is_prompt = pl.program_id(0) < MP // tm
    for r0 in range(0, tm, EPI_ROWS):
        rs = slice(r0, r0 + EPI_ROWS)
        a = jnp.where(is_prompt, ap_ref[rs, :], as_ref[rs, :])
        if gate_ref is not None:
            width = gate_ref.shape[2]
            a = jnp.concatenate(
                [(a[:, hh * width:(hh + 1) * width].astype(F32) * gate_ref[hh, rs, :].astype(F32)).astype(BF16)
                 for hh in range(gate_ref.shape[0])], axis=1)
        x = jnp.where(is_prompt, xp_ref[rs, :], xs_ref[rs, :])
        y = jnp.dot(a, w_ref[...], preferred_element_type=F32)
        xn, hn = _residual(y, x, gp_ref[...], gn_ref[...])
        xn_ref[rs, :] = xn
        hn_ref[rs, :] = hn


def _out_proj(a_parts, w_bf, x_parts, g_post, g_next, name, gate=None):
    kdim = w_bf.shape[0]
    tm = TM_OUT * D_MODEL // kdim
    np_t = MP // tm

    def two_source(parts, width):
        sample_t0 = np_t if len(parts) == 1 else 0
        return [pl.BlockSpec((tm, width), lambda i: (jnp.minimum(i, np_t - 1), 0)),
                pl.BlockSpec((tm, width), lambda i: (sample_t0 + jnp.maximum(i - np_t, 0), 0))]

    row = pl.BlockSpec((tm, D_MODEL), lambda i: (i, 0))
    vec = pl.BlockSpec((1, D_MODEL), lambda i: (0, 0))
    gate_args, gate_specs = [], []
    if gate is not None:
        gate_args = [gate]
        gate_specs = [pl.BlockSpec((gate.shape[0], tm, gate.shape[2]), lambda i: (0, i, 0))]
    return pl.pallas_call(
        _outproj_kernel, grid=(M // tm,),
        in_specs=(two_source(a_parts, kdim)
                  + [pl.BlockSpec((kdim, D_MODEL), lambda i: (0, 0), pipeline_mode=pl.Buffered(1))]
                  + two_source(x_parts, D_MODEL) + [vec, vec] + gate_specs),
        out_specs=[row, row],
        out_shape=[jax.ShapeDtypeStruct((M, D_MODEL), F32), jax.ShapeDtypeStruct((M, D_MODEL), BF16)],
        compiler_params=_cparams(("arbitrary",), 56), name=name,
    )(a_parts[0], a_parts[-1], w_bf, x_parts[0], x_parts[-1], g_post, g_next, *gate_args)


def _ffn_up_down(h, wu_ref, wd_ref):
    a = jnp.maximum(jnp.dot(h, wu_ref[...], preferred_element_type=F32), 0.0)
    return jnp.dot((a * a).astype(BF16), wd_ref[...], preferred_element_type=F32)


FFN_EPI_ROWS = 256


def _ffn_kernel(h_ref, wu_ref, wd_ref, x_ref, gp_ref, gn_ref, xn_ref, hn_ref):
    c = pl.program_id(1)
    last = c == pl.num_programs(1) - 1
    _zero_on_first(xn_ref, c)

    @pl.when(jnp.logical_not(last))
    def _():
        xn_ref[...] += _ffn_up_down(h_ref[...], wu_ref, wd_ref)

    @pl.when(last)
    def _():
        for r0 in range(0, xn_ref.shape[0], FFN_EPI_ROWS):
            rs = slice(r0, r0 + FFN_EPI_ROWS)
            y = xn_ref[rs, :] + _ffn_up_down(h_ref[rs, :], wu_ref, wd_ref)
            xn, hn = _residual(y, x_ref[rs, :], gp_ref[...], gn_ref[...])
            xn_ref[rs, :] = xn
            hn_ref[rs, :] = hn


def _ffn_last_kernel(h_ref, wu_ref, wd_ref, x_ref, gp_ref, yp_ref, ys_ref, acc_ref):
    i, c = pl.program_id(0), pl.program_id(1)
    _zero_on_first(acc_ref, c)
    last = c == pl.num_programs(1) - 1
    is_prompt = i < MP // acc_ref.shape[0]

    @pl.when(jnp.logical_not(last))
    def _():
        acc_ref[...] += _ffn_up_down(h_ref[...], wu_ref, wd_ref)

    def finish(y_ref):
        for r0 in range(0, acc_ref.shape[0], FFN_EPI_ROWS):
            rs = slice(r0, r0 + FFN_EPI_ROWS)
            y = acc_ref[rs, :] + _ffn_up_down(h_ref[rs, :], wu_ref, wd_ref)
            y_ref[rs, :] = x_ref[rs, :] + _rms(y, gp_ref[...])

    pl.when(last & is_prompt)(lambda: finish(yp_ref))
    pl.when(last & jnp.logical_not(is_prompt))(lambda: finish(ys_ref))


def _ffn(h, wu_bf, wd_bf, x, g_post, g_next):
    tm, tc = TM_OUT, 1024
    row = pl.BlockSpec((tm, D_MODEL), lambda i, c: (i, 0))
    vec = pl.BlockSpec((1, D_MODEL), lambda i, c: (0, 0))
    w_specs = [pl.BlockSpec((D_MODEL, tc), lambda i, c: (0, c)), pl.BlockSpec((tc, D_MODEL), lambda i, c: (c, 0))]
    grid = (M // tm, D_FF // tc)
    if g_next is not None:
        return pl.pallas_call(
            _ffn_kernel, grid=grid, in_specs=[row] + w_specs + [row, vec, vec], out_specs=[row, row],
            out_shape=[jax.ShapeDtypeStruct((M, D_MODEL), F32), jax.ShapeDtypeStruct((M, D_MODEL), BF16)],
            compiler_params=_cparams(("arbitrary", "arbitrary"), 56), name="ffn",
        )(h, wu_bf, wd_bf, x, g_post, g_next)
    np_t = MP // tm
    return pl.pallas_call(
        _ffn_last_kernel, grid=grid, in_specs=[row] + w_specs + [row, vec],
        out_specs=[pl.BlockSpec((tm, D_MODEL), lambda i, c: (jnp.minimum(i, np_t - 1), 0)),
                   pl.BlockSpec((tm, D_MODEL), lambda i, c: (jnp.maximum(i - np_t, 0), 0))],
        out_shape=[jax.ShapeDtypeStruct((MP, D_MODEL), F32), jax.ShapeDtypeStruct((MS, D_MODEL), F32)],
        scratch_shapes=[pltpu.VMEM((tm, D_MODEL), F32)],
        compiler_params=_cparams(("arbitrary", "arbitrary"), 56), name="ffn_last",
    )(h, wu_bf, wd_bf, x, g_post)


def _mem_kv_kernel(m_ref, g_ref, wk_ref, wv_ref, k_ref, v_ref):
    m = _rms(m_ref[...], g_ref[0]).astype(BF16)
    k_ref[0] = jnp.dot(m, wk_ref[0].astype(BF16), preferred_element_type=F32)
    v_ref[0] = jnp.dot(m, wv_ref[0].astype(BF16), preferred_element_type=F32)


def _mem_kv(mem, g, w_k, w_v):
    rows = BATCH * MEM_LEN
    wspec = pl.BlockSpec((1, D_MODEL, MEM_W), lambda l: (l, 0, 0))
    ospec = pl.BlockSpec((1, rows, MEM_W), lambda l: (l, 0, 0))
    oshape = jax.ShapeDtypeStruct((DEPTH, rows, MEM_W), F32)
    return pl.pallas_call(
        _mem_kv_kernel, grid=(DEPTH,),
        in_specs=[pl.BlockSpec((rows, D_MODEL), lambda l: (0, 0)),
                  pl.BlockSpec((1, 1, D_MODEL), lambda l: (l, 0, 0)), wspec, wspec],
        out_specs=[ospec, ospec], out_shape=[oshape, oshape],
        compiler_params=_cparams(("arbitrary",), 48), name="mem_kv",
    )(mem.reshape(rows, D_MODEL), g.reshape(DEPTH, 1, D_MODEL), w_k, w_v)


def _mem_attend_prompt(q, mk_ref, mv_ref):
    q = q.astype(BF16)
    mk, mv = mk_ref[...].astype(BF16), mv_ref[...].astype(BF16)
    outs = []
    for hh in range(MEM_HEADS):
        cs = slice(hh * MEM_HEAD_DIM, (hh + 1) * MEM_HEAD_DIM)
        s = lax.dot_general(q[:, cs], mk[:, cs], (((1,), (1,)), ((), ())),
                            preferred_element_type=F32) * (MEM_HEAD_DIM ** -0.5)
        p = jnp.exp(s - jnp.max(s, axis=-1, keepdims=True))
        den = jnp.sum(p, axis=-1, keepdims=True)
        outs.append(jnp.dot(p.astype(BF16), mv[:, cs], preferred_element_type=F32) / den)
    return jnp.concatenate(outs, axis=1)


def _mem_attend_sample(q, ck_ref, cv_ref):
    nb, t = ck_ref.shape[0], DEC_SEQ
    qs = jnp.concatenate([q[:, hh * MEM_HEAD_DIM:(hh + 1) * MEM_HEAD_DIM].reshape(nb, t, MEM_HEAD_DIM)
                          for hh in range(MEM_HEADS)], axis=1).astype(BF16)
    s = jnp.einsum('nqd,nkd->nqk', qs, ck_ref[...].astype(BF16),
                   preferred_element_type=F32) * (MEM_HEAD_DIM ** -0.5)
    shape = (MEM_HEADS * t, MEM_LEN * MEM_HEADS)
    own = ((lax.broadcasted_iota(jnp.int32, shape, 1) & (MEM_HEADS - 1))
           == lax.broadcasted_iota(jnp.int32, shape, 0) // t)
    s = jnp.where(own[None], s, NEG_INF)
    p = jnp.exp(s - jnp.max(s, axis=-1, keepdims=True))
    den = jnp.sum(p, axis=-1, keepdims=True)
    o3 = jnp.einsum('nqk,nkd->nqd', p.astype(BF16), cv_ref[...].astype(BF16),
                    preferred_element_type=F32) / den
    return jnp.concatenate([o3[:, hh * t:(hh + 1) * t, :].reshape(nb * t, MEM_HEAD_DIM)
                            for hh in range(MEM_HEADS)], axis=1)


def _mem_kernel(hp_ref, hs_ref, wq_ref, wo_ref, mk_ref, mv_ref, ck_ref, cv_ref, xp_ref, xs_ref, gp_ref, gn_ref,
                xn_ref, hn_ref, os_ref):
    i = pl.program_id(0)
    tm = xn_ref.shape[0]
    n_prompt = MP // tm
    srows = hs_ref.shape[0]
    project = lambda h_ref: jnp.dot(h_ref[...], wq_ref[...], preferred_element_type=F32)

    def finish(o, x_ref):
        for r0 in range(0, tm, EPI_ROWS):
            rs = slice(r0, r0 + EPI_ROWS)
            y = jnp.dot(o[rs].astype(BF16), wo_ref[...], preferred_element_type=F32)
            xn, hn = _residual(y, x_ref[rs, :], gp_ref[...], gn_ref[...])
            xn_ref[rs, :] = xn
            hn_ref[rs, :] = hn

    @pl.when(i < n_prompt)
    def _():
        os_ref[pl.ds(pl.multiple_of(i * srows, srows), srows), :] = _mem_attend_sample(
            project(hs_ref), ck_ref, cv_ref)
        finish(_mem_attend_prompt(project(hp_ref), mk_ref, mv_ref), xp_ref)

    @pl.when(i >= n_prompt)
    def _():
        finish(os_ref[pl.ds(pl.multiple_of((i - n_prompt) * tm, tm), tm), :], xs_ref)


MEM_TM = 256
MEM_NB = DEC_BATCH // (MP // MEM_TM)


def _mem_attention(h_parts, x_parts, wq_bf, wo_bf, mk, mv, cache_k, cache_v, layer, g_post, g_next):
    tm, nb = MEM_TM, MEM_NB
    srows = nb * DEC_SEQ
    sample_row0 = MP if len(h_parts) == 1 else 0
    per_b = SEQ // tm
    n_p = MP // tm
    pidx = lambda i: jnp.minimum(i, n_p - 1)
    row = pl.BlockSpec((tm, D_MODEL), lambda i: (i, 0))
    vec = pl.BlockSpec((1, D_MODEL), lambda i: (0, 0))
    kvspec = pl.BlockSpec((None, MEM_LEN, MEM_W), lambda i: (layer * BATCH + pidx(i) // per_b, 0, 0))
    cspec = pl.BlockSpec((None, nb, MEM_LEN * MEM_HEADS, MEM_HEAD_DIM), lambda i: (layer, pidx(i), 0, 0))
    resident = lambda shape: pl.BlockSpec(shape, lambda i: (layer, 0), pipeline_mode=pl.Buffered(1))
    return pl.pallas_call(
        _mem_kernel, grid=(M // tm,),
        in_specs=[pl.BlockSpec((tm, D_MODEL), lambda i: (pidx(i), 0)),
                  pl.BlockSpec((srows, D_MODEL), lambda i: (sample_row0 // srows + pidx(i), 0)),
                  resident((D_MODEL, MEM_W)), resident((MEM_W, D_MODEL)), kvspec, kvspec,
                  cspec, cspec,
                  pl.BlockSpec((tm, D_MODEL), lambda i: (pidx(i), 0)),
                  pl.BlockSpec((tm, D_MODEL), lambda i: (sample_row0 // tm + jnp.maximum(i - n_p, 0), 0)),
                  vec, vec],
        out_specs=[row, row],
        out_shape=[jax.ShapeDtypeStruct((M, D_MODEL), F32), jax.ShapeDtypeStruct((M, D_MODEL), BF16)],
        scratch_shapes=[pltpu.VMEM((MS, MEM_W), F32)],
        compiler_params=_cparams(("arbitrary",), 48), name="mem_attention",
    )(h_parts[0], h_parts[-1], wq_bf, wo_bf, mk, mv, cache_k, cache_v, x_parts[0], x_parts[-1], g_post, g_next)


def kernel(x_prompt, x_sample, cache_swa_k, cache_swa_v, state_ret, cache_mem_k, cache_mem_v, mem_prompt, norm_mix_pre, norm_mix_post, norm_mem, norm_x_pre, norm_x_post, norm_ffn_pre, norm_ffn_post, w_ab_in, w_ab_out, swa_sinks, gmlp_ln_g, gmlp_ln_b, gmlp_w_s, gmlp_b_s, w_ret_in, w_ret_out, w_mem_q, w_mem_k, w_mem_v, w_mem_o, w_ffn_up, w_ffn_down):
    vec = lambda g, layer: g[layer].reshape(1, D_MODEL)
    mem_k, mem_v = _mem_kv(mem_prompt, norm_mem, w_mem_k, w_mem_v)
    mem_k_b = mem_k.reshape(DEPTH * BATCH, MEM_LEN, MEM_W)
    mem_v_b = mem_v.reshape(DEPTH * BATCH, MEM_LEN, MEM_W)
    ck_mem = cache_mem_k.reshape(DEPTH, DEC_BATCH, MEM_LEN * MEM_HEADS, MEM_HEAD_DIM)
    cv_mem = cache_mem_v.reshape(DEPTH, DEC_BATCH, MEM_LEN * MEM_HEADS, MEM_HEAD_DIM)
    lg = jnp.log1p(-jnp.exp2(-5.0 - jnp.arange(RET_HEADS, dtype=F32)))

    x_parts = (x_prompt.reshape(MP, D_MODEL), x_sample.reshape(MS, D_MODEL))
    h, w_ab_in_bf, (swa_c, swa_s1, swa_s2, ret_c, ret_s) = _stacked_prenorm(
        *x_parts, vec(norm_mix_pre, 0), w_ab_in, 0)
    outs = {}
    for layer in range(DEPTH):
        j = layer // 2
        if layer % 2 == 0:
            (q, kv, u, zv, zv_s), (wu_bf, wout_bf, wq_bf, wo_bf) = _ab_in_proj(
                h, w_ab_in_bf, gmlp_ln_g[j].reshape(1, GMLP_WIDTH), gmlp_ln_b[j].reshape(1, GMLP_WIDTH),
                (swa_c, swa_s1, swa_s2), w_ffn_up, w_ab_out, w_mem_q, w_mem_o, layer, j)
            cat, wd_bf = _ab_mix_prompt(q, kv, u, zv, swa_sinks[j], gmlp_w_s[j], gmlp_b_s[j], w_ffn_down, layer)
            cat, nk_s, nv_s = _ab_mix_sample(
                cat, q, kv, u, zv_s, cache_swa_k[j].reshape(DEC_BATCH, WINDOW, SWA_KV_W),
                cache_swa_v[j].reshape(DEC_BATCH, WINDOW, SWA_KV_W), swa_sinks[j], gmlp_w_s[j], gmlp_b_s[j])
            kv_p = jnp.stack([kv[(b + 1) * SEQ - WINDOW:(b + 1) * SEQ] for b in range(BATCH)])
            kv_p = kv_p.reshape(BATCH, WINDOW, 2, SWA_KV_HEADS, SWA_HEAD_DIM)
            outs.setdefault('swa_k_p', []).append(kv_p[:, :, 0])
            outs.setdefault('swa_v_p', []).append(kv_p[:, :, 1])
            outs.setdefault('swa_k_s', []).append(nk_s.reshape(DEC_BATCH, WINDOW, SWA_KV_HEADS, SWA_HEAD_DIM))
            outs.setdefault('swa_v_s', []).append(nv_s.reshape(DEC_BATCH, WINDOW, SWA_KV_HEADS, SWA_HEAD_DIM))
            outs.setdefault('gmlp_v_s', []).append(zv_s.reshape(DEC_BATCH, DEC_SEQ, GMLP_WIDTH))
            mixed, mix_name, gate = (cat,), "ab_out", None
        else:
            (zqk, zv, zg), (wu_bf, wd_bf, wout_bf) = _ret_in_proj(
                h, w_ret_in[j], (ret_c, ret_s), w_ffn_up, w_ffn_down, w_ret_out, layer, j)
            y_p, s_p, y_s, s_s = _retention(zqk, zv, state_ret[j], lg)
            outs.setdefault('ret_p', []).append(s_p)
            outs.setdefault('ret_s', []).append(s_s)
            mixed, mix_name, gate = (y_p, y_s), "ret_out", zg
        x, h = _out_proj(mixed, wout_bf, x_parts, vec(norm_mix_post, layer), vec(norm_x_pre, layer), mix_name,
                         gate)
        x_parts, h_parts = (x,), (h,)
        g_post = vec(norm_x_post, layer)
        g_next = vec(norm_ffn_pre, layer)
        x, h = _mem_attention(h_parts, x_parts, wq_bf, wo_bf, mem_k_b, mem_v_b, ck_mem, cv_mem, layer,
                              g_post, g_next)
        g_next = vec(norm_mix_pre, layer + 1) if layer + 1 < DEPTH else None
        x, h = _ffn(h, wu_bf, wd_bf, x, vec(norm_ffn_post, layer), g_next)
        x_parts = (x,)

    y_prompt, y_sample = x, h
    stack = lambda name: jnp.stack(outs[name])
    mem_shape = (DEPTH, BATCH, MEM_LEN, MEM_HEADS, MEM_HEAD_DIM)
    return (y_prompt.reshape(BATCH, SEQ, D_MODEL), y_sample.reshape(DEC_BATCH, DEC_SEQ, D_MODEL),
            stack('swa_k_p'), stack('swa_v_p'), stack('swa_k_s'), stack('swa_v_s'), stack('gmlp_v_s'),
            stack('ret_p'), stack('ret_s'), mem_k.reshape(mem_shape), mem_v.reshape(mem_shape))
```

```python
import functools
import math

import jax
import jax.numpy as jnp
from jax import lax
from jax.experimental import pallas as pl
from jax.experimental.pallas import tpu as pltpu

F32 = jnp.float32
BF16 = jnp.bfloat16

D_MODEL = 2048
BATCH = 2
SEQ = 4096
DEPTH = 2
DEC_BATCH = 128
DEC_SEQ = 8
PAST_LEN = 8192

SWA_HEADS = 16
SWA_KV_HEADS = 4
SWA_HEAD_DIM = 64
SWA_GROUP = SWA_HEADS // SWA_KV_HEADS
WINDOW = 128
ROPE_THETA = 500000.0
ROPE_DIM = SWA_HEAD_DIM // 4
ROPE_HALF = ROPE_DIM // 2

GMLP_GROUPS = 4
GMLP_WIDTH = 1024
GMLP_GROUP_DIM = GMLP_WIDTH // GMLP_GROUPS
GMLP_CHUNK = 128

SWA_Q_W = SWA_HEADS * SWA_HEAD_DIM
SWA_KV_W = SWA_KV_HEADS * SWA_HEAD_DIM
AB_IN = SWA_Q_W + 2 * SWA_KV_W + 2 * GMLP_WIDTH
AB_OUT = SWA_Q_W + GMLP_WIDTH

RET_HEADS = 8
RET_KEY_DIM = D_MODEL // RET_HEADS
RET_VAL_DIM = 2 * RET_KEY_DIM
RET_THETA = 10000.0
RET_QK_W = RET_HEADS * RET_KEY_DIM
RET_V_W = RET_HEADS * RET_VAL_DIM
RET_IN = 2 * RET_QK_W + 2 * RET_V_W

MEM_LEN = 256
MEM_HEADS = 4
MEM_HEAD_DIM = 128
MEM_W = MEM_HEADS * MEM_HEAD_DIM

D_FF = 4 * D_MODEL
EPS = 1e-6
NEG_INF = -1e30

MP = BATCH * SEQ
MS = DEC_BATCH * DEC_SEQ
M = MP + MS
LANES = 128
MIB = 1024 * 1024

TM_IN = 1024
TAB_ROWS = SEQ + MS
TM_OUT = 512


def _cparams(sem, vmem_mib):
    return pltpu.CompilerParams(dimension_semantics=sem, vmem_limit_bytes=vmem_mib * MIB)


def _rms(x, g):
    return x * lax.rsqrt(jnp.mean(x * x, axis=-1, keepdims=True) + EPS) * g


def _residual(y, x, g_post, g_next):
    xn = x + _rms(y, g_post)
    return xn, _rms(xn, g_next).astype(BF16)


def _rope_table_block(block, sc_ref, s1_ref, s2_ref, rc_ref, rs_ref):
    tr = sc_ref.shape[0]
    r = lax.broadcasted_iota(jnp.int32, (tr, LANES), 0) + block * tr
    pos = jnp.where(r < SEQ, r, PAST_LEN + ((r - SEQ) & (DEC_SEQ - 1))).astype(F32)
    lane = lax.broadcasted_iota(jnp.int32, (tr, LANES), 1)
    d = lane & (SWA_HEAD_DIM - 1)
    inv = jnp.exp((d & (ROPE_HALF - 1)).astype(F32) * (-math.log(ROPE_THETA) / ROPE_HALF))
    ang = pos * inv
    c, s = jnp.cos(ang), jnp.sin(ang)
    sc_ref[...] = jnp.where(d < ROPE_DIM, c, 1.0)
    s1_ref[...] = jnp.where(d < ROPE_HALF, -s, 0.0)
    s2_ref[...] = jnp.where((d >= ROPE_HALF) & (d < ROPE_DIM), s, 0.0)
    inv_r = jnp.exp(lane.astype(F32) * (-math.log(RET_THETA) / LANES))
    ang_r = pos * inv_r
    rc_ref[...] = jnp.cos(ang_r)
    rs_ref[...] = jnp.sin(ang_r)


def _norm_kernel(xp_ref, xs_ref, g_ref, w_ref, h_ref, wbf_ref, *table_refs, n_table_blocks):
    def emit(src_ref):
        h_ref[...] = _rms(src_ref[...], g_ref[...]).astype(BF16)

    i = pl.program_id(0)
    is_prompt = i < MP // h_ref.shape[0]
    pl.when(is_prompt)(lambda: emit(xp_ref))
    pl.when(jnp.logical_not(is_prompt))(lambda: emit(xs_ref))
    wbf_ref[...] = w_ref[...].astype(BF16)
    pl.when(i < n_table_blocks)(lambda: _rope_table_block(i, *table_refs))


def _stacked_prenorm(x_prompt, x_sample, g, w, layer):
    tm = TM_OUT
    n_p = MP // tm
    w_src, w_in_spec, w_shape, w_out_spec = _row_cast_job(w, layer, n_p)
    tab_spec = pl.BlockSpec((TAB_ROWS // n_p, LANES), lambda i: (jnp.minimum(i, n_p - 1), 0))
    tab_shape = jax.ShapeDtypeStruct((TAB_ROWS, LANES), F32)
    outs = pl.pallas_call(
        functools.partial(_norm_kernel, n_table_blocks=n_p), grid=(M // tm,),
        in_specs=[pl.BlockSpec((tm, D_MODEL), lambda i: (jnp.minimum(i, n_p - 1), 0)),
                  pl.BlockSpec((tm, D_MODEL), lambda i: (jnp.maximum(i - n_p, 0), 0)),
                  pl.BlockSpec((1, D_MODEL), lambda i: (0, 0)), w_in_spec],
        out_specs=[pl.BlockSpec((tm, D_MODEL), lambda i: (i, 0)), w_out_spec] + [tab_spec] * 5,
        out_shape=[jax.ShapeDtypeStruct((M, D_MODEL), BF16), w_shape] + [tab_shape] * 5,
        compiler_params=_cparams(("arbitrary",), 40), name="prenorm",
    )(x_prompt, x_sample, g, w_src)
    return outs[0], outs[1], tuple(outs[2:])


W_SUB = 512


def _inproj_kernel(h_ref, *refs, n_sub, n_side, n_cast, epilogue):
    w_refs, refs = refs[:n_sub], refs[n_sub:]
    side, refs = refs[:n_side], refs[n_side:]
    cast_src, refs = refs[:n_cast], refs[n_cast:]
    wbf_ref = refs[-1]
    outs, cast_dst = refs[:len(refs) - 1 - n_cast], refs[len(refs) - 1 - n_cast:-1]

    @pl.when(pl.program_id(1) == 0)
    def _():
        for s, w_ref in enumerate(w_refs):
            wbf_ref[:, s * W_SUB:(s + 1) * W_SUB] = w_ref[...].astype(BF16)

    acc = jnp.dot(h_ref[...], wbf_ref[...], preferred_element_type=F32)
    epilogue(acc, side, outs)
    for src, dst in zip(cast_src, cast_dst):
        dst[...] = src[...].astype(BF16)


def _tab_index(j, i):
    return (jnp.where(i < MP // TM_IN, i % (SEQ // TM_IN), SEQ // TM_IN), 0)


CAST_STEPS = MP // TM_IN


def _cast_job(w, layer, nrb, ncb, block_of):
    rows, cols = w.shape[1:]
    blk = (rows // nrb, cols // ncb)
    index = lambda j, i: block_of(j, jnp.minimum(i, CAST_STEPS - 1))
    return (w, pl.BlockSpec((None,) + blk, lambda j, i: (layer,) + tuple(index(j, i))),
            jax.ShapeDtypeStruct((rows, cols), BF16), pl.BlockSpec(blk, index))


def _inproj(h, w, col0, ncols, tn, epilogue, side, side_specs, out_shapes, out_specs, name, vmem_mib,
            casts=()):
    tm = TM_IN
    k = h.shape[1]
    n_sub = tn // W_SUB
    w_specs = [pl.BlockSpec((k, W_SUB), lambda j, i, s=s: (0, col0 // W_SUB + j * n_sub + s))
               for s in range(n_sub)]
    as_list = lambda v: list(v) if isinstance(v, (list, tuple)) else [v]
    outs = pl.pallas_call(
        functools.partial(_inproj_kernel, n_sub=n_sub, n_side=len(side), n_cast=len(casts), epilogue=epilogue),
        grid=(ncols // tn, M // tm),
        in_specs=([pl.BlockSpec((tm, k), lambda j, i: (i, 0))] + w_specs + side_specs
                  + [c[1] for c in casts]),
        out_specs=as_list(out_specs) + [c[3] for c in casts],
        out_shape=as_list(out_shapes) + [c[2] for c in casts],
        scratch_shapes=[pltpu.VMEM((k, tn), BF16)],
        compiler_params=_cparams(("arbitrary", "arbitrary"), vmem_mib), name=name,
    )(h, *([w] * n_sub), *side, *[c[0] for c in casts])
    n_main = len(outs) - len(casts)
    main = outs[0] if n_main == 1 else tuple(outs[:n_main])
    return (main, tuple(outs[n_main:])) if casts else main


def _swa_rope(x, c, s1, s2):
    return x * c + pltpu.roll(x, LANES - ROPE_HALF, 1) * s1 + pltpu.roll(x, ROPE_HALF, 1) * s2


def _epi_swa_q(acc, side, outs):
    c, s1, s2 = (t[...] for t in side)
    (q_ref,) = outs
    for cc in range(acc.shape[1] // LANES):
        sl = slice(cc * LANES, (cc + 1) * LANES)
        q_ref[:, sl] = (_swa_rope(acc[:, sl], c, s1, s2) * (SWA_HEAD_DIM ** -0.5)).astype(BF16)


def _epi_swa_kv(acc, side, outs):
    c, s1, s2 = (t[...] for t in side)
    (kv_ref,) = outs
    for cc in range(SWA_KV_W // LANES):
        sl = slice(cc * LANES, (cc + 1) * LANES)
        kv_ref[:, sl] = _swa_rope(acc[:, sl], c, s1, s2)
    kv_ref[:, SWA_KV_W:] = acc[:, SWA_KV_W:]


def _epi_gelu(acc, side, outs):
    outs[0][...] = jax.nn.gelu(acc).astype(BF16)


def _epi_ret_qk(acc, side, outs):
    cos, sin = (t[...] for t in side)
    (z_ref,) = outs
    tn = acc.shape[1]
    scale = jnp.where(pl.program_id(0) < RET_QK_W // tn, 1.0, RET_KEY_DIM ** -0.5)
    for hh in range(tn // RET_KEY_DIM):
        c0 = hh * RET_KEY_DIM
        x1, x2 = acc[:, c0:c0 + LANES], acc[:, c0 + LANES:c0 + 2 * LANES]
        z_ref[hh, :, :LANES] = ((x1 * cos - x2 * sin) * scale).astype(BF16)
        z_ref[hh, :, LANES:] = ((x2 * cos + x1 * sin) * scale).astype(BF16)


def _per_head(fn):
    def epilogue(acc, side, outs):
        (z_ref,) = outs
        for hh in range(z_ref.shape[0]):
            z_ref[hh] = fn(acc[:, hh * RET_VAL_DIM:(hh + 1) * RET_VAL_DIM]).astype(BF16)
    return epilogue


_epi_cast = _per_head(lambda a: a)
_epi_silu = _per_head(lambda a: a / (1.0 + jnp.exp(-a)))


AB_TM = 256


def _ab_in_kernel(h_ref, w_ref, sc_ref, s1_ref, s2_ref, lng_ref, lnb_ref, *refs, n_cast):
    cast_src, (q_ref, kv_ref, u_ref, zv_ref, zvs_ref), cast_dst = refs[:n_cast], refs[n_cast:n_cast + 5], refs[n_cast + 5:]
    h = h_ref[...]
    tabs = (sc_ref, s1_ref, s2_ref)
    seg = lambda c0, n: jnp.dot(h, w_ref[:, c0:c0 + n], preferred_element_type=F32)
    _epi_swa_q(seg(0, SWA_Q_W), tabs, (q_ref,))
    _epi_swa_kv(seg(SWA_Q_W, 2 * SWA_KV_W), tabs, (kv_ref,))
    _epi_gelu(seg(SWA_Q_W + 2 * SWA_KV_W, GMLP_WIDTH), (), (u_ref,))
    a = jax.nn.gelu(seg(AB_IN - GMLP_WIDTH, GMLP_WIDTH))
    xc = a - jnp.mean(a, axis=-1, keepdims=True)
    y = xc * lax.rsqrt(jnp.mean(xc * xc, axis=-1, keepdims=True) + EPS) * lng_ref[...] + lnb_ref[...]
    zv_ref[...] = y.astype(BF16)

    @pl.when(pl.program_id(0) >= MP // AB_TM)
    def _():
        zvs_ref[...] = y

    for src, dst in zip(cast_src, cast_dst):
        dst[...] = src[...].astype(BF16)


def _row_cast_job(w, layer, n_blocks):
    rows, cols = w.shape[1:]
    blk = (rows // n_blocks, cols)
    index = lambda i: (jnp.minimum(i, n_blocks - 1), 0)
    return (w, pl.BlockSpec((None,) + blk, lambda i: (layer,) + index(i)),
            jax.ShapeDtypeStruct((rows, cols), BF16), pl.BlockSpec(blk, index))


def _ab_in_proj(h, w_in_bf, ln_g, ln_b, tabs, w_up, w_out, w_mem_q, w_mem_o, layer, j):
    tm = AB_TM
    n_p = MP // tm
    stacked = lambda w: w.reshape(1, w.shape[0] * w.shape[1], w.shape[2])
    casts = [_row_cast_job(w_up, layer, n_p), _row_cast_job(w_out, j, n_p),
             _row_cast_job(stacked(w_mem_q), 0, n_p), _row_cast_job(stacked(w_mem_o), 0, n_p)]
    tab_rows = SEQ // tm
    tab_spec = pl.BlockSpec((tm, LANES), lambda i: (jnp.where(i < n_p, i % tab_rows, tab_rows + i - n_p), 0))
    vec_spec = pl.BlockSpec((1, GMLP_WIDTH), lambda i: (0, 0))
    row_out = lambda n: pl.BlockSpec((tm, n), lambda i: (i, 0))
    outs = pl.pallas_call(
        functools.partial(_ab_in_kernel, n_cast=len(casts)), grid=(M // tm,),
        in_specs=([pl.BlockSpec((tm, D_MODEL), lambda i: (i, 0)),
                   pl.BlockSpec((D_MODEL, AB_IN), lambda i: (0, 0), pipeline_mode=pl.Buffered(1))]
                  + [tab_spec] * 3 + [vec_spec] * 2 + [c[1] for c in casts]),
        out_specs=[row_out(SWA_Q_W), row_out(2 * SWA_KV_W), row_out(GMLP_WIDTH), row_out(GMLP_WIDTH),
                   pl.BlockSpec((tm, GMLP_WIDTH), lambda i: (jnp.maximum(i - n_p, 0), 0))]
                  + [c[3] for c in casts],
        out_shape=[jax.ShapeDtypeStruct((M, SWA_Q_W), BF16), jax.ShapeDtypeStruct((M, 2 * SWA_KV_W), F32),
                   jax.ShapeDtypeStruct((M, GMLP_WIDTH), BF16), jax.ShapeDtypeStruct((M, GMLP_WIDTH), BF16),
                   jax.ShapeDtypeStruct((MS, GMLP_WIDTH), F32)] + [c[2] for c in casts],
        compiler_params=_cparams(("arbitrary",), 52), name="ab_in",
    )(h, w_in_bf, *tabs, ln_g, ln_b, *[c[0] for c in casts])
    return tuple(outs[:5]), tuple(outs[5:])


def _ret_in_proj(h, w_in, tabs, w_up, w_down, w_out, layer, j):
    rc, rs = tabs
    tn = 1024
    tab_spec = pl.BlockSpec((TM_IN, LANES), _tab_index)
    by_tile = lambda j, i: (j, i)
    head_major = jax.ShapeDtypeStruct((RET_HEADS, M, 2 * RET_KEY_DIM), BF16)
    seg = lambda col0, ncols, epi, side, specs, out_spec, name, w, lead: _inproj(
        h, w_in, col0, ncols, tn, epi, side, specs, head_major, out_spec, name, 56,
        casts=[_cast_job(w, lead, ncols // tn, CAST_STEPS, by_tile)])
    qk_heads = tn // RET_KEY_DIM
    q_tiles = RET_QK_W // tn
    qk_spec = pl.BlockSpec((qk_heads, TM_IN, RET_KEY_DIM), lambda j, i: (j % q_tiles, i, j // q_tiles))
    v_spec = pl.BlockSpec((tn // RET_VAL_DIM, TM_IN, RET_VAL_DIM), lambda j, i: (j, i, 0))
    zqk, (wu_bf,) = seg(0, 2 * RET_QK_W, _epi_ret_qk, [rc, rs], [tab_spec] * 2, qk_spec, "ret_in_qk",
                        w_up, layer)
    zv, (wd_bf,) = seg(2 * RET_QK_W, RET_V_W, _epi_cast, [], [], v_spec, "ret_in_v", w_down, layer)
    zg, (wout_bf,) = seg(2 * RET_QK_W + RET_V_W, RET_V_W, _epi_silu, [], [], v_spec, "ret_in_g", w_out, j)
    return (zqk, zv, zg), (wu_bf, wd_bf, wout_bf)


AB_ROWS = 512


def _sink_softmax(s, sink):
    m = jnp.maximum(jnp.max(s, axis=-1, keepdims=True), sink)
    p = jnp.exp(s - m)
    den = jnp.sum(p, axis=-1, keepdims=True) + jnp.exp(sink - m)
    return p, den


def _ab_prompt_kernel(sink_ref, q_ref, kv_ref, kvp_ref, u_ref, zv_ref, ws_ref, bs_ref, wsrc_ref, cat_ref,
                      wdst_ref, kd_ref, vd_ref):
    wdst_ref[...] = wsrc_ref[...].astype(BF16)
    r = pl.program_id(1)
    low = lax.broadcasted_iota(jnp.int32, (1, LANES), 1) < SWA_HEAD_DIM

    def both_halves(x, half):
        sw = pltpu.roll(x, SWA_HEAD_DIM, 1)
        return (jnp.where(low, x, sw) if half == 0 else jnp.where(low, sw, x)).astype(BF16)

    for kh in range(SWA_KV_HEADS):
        kcol = slice((kh // 2) * LANES, (kh // 2 + 1) * LANES)
        vcol = slice(SWA_KV_W + (kh // 2) * LANES, SWA_KV_W + (kh // 2 + 1) * LANES)
        kd_ref[kh, :WINDOW, :] = both_halves(kvp_ref[:, kcol], kh % 2)
        kd_ref[kh, WINDOW:, :] = both_halves(kv_ref[:, kcol], kh % 2)
        vd_ref[kh, :WINDOW, :] = both_halves(kvp_ref[:, vcol], kh % 2)
        vd_ref[kh, WINDOW:, :] = both_halves(kv_ref[:, vcol], kh % 2)

    t = lax.broadcasted_iota(jnp.int32, (WINDOW, 2 * WINDOW), 0)
    sk = lax.broadcasted_iota(jnp.int32, (WINDOW, 2 * WINDOW), 1)
    band = (sk <= t + WINDOW) & (sk > t)
    sink_slot = sk == t
    tril = (lax.broadcasted_iota(jnp.int32, (GMLP_CHUNK, GMLP_CHUNK), 0)
            >= lax.broadcasted_iota(jnp.int32, (GMLP_CHUNK, GMLP_CHUNK), 1))

    for sb in range(AB_ROWS // WINDOW):
        rs = slice(sb * WINDOW, (sb + 1) * WINDOW)
        ks = slice(sb * WINDOW, sb * WINDOW + 2 * WINDOW)
        valid = band & ((sk >= WINDOW) | (r * (AB_ROWS // WINDOW) + sb > 0))
        for kh in range(SWA_KV_HEADS):
            qm = []
            for g in range(SWA_GROUP):
                hd = kh * SWA_GROUP + g
                qb = q_ref[rs, (hd // 2) * LANES:(hd // 2 + 1) * LANES]
                qm.append(jnp.where(low if hd % 2 == 0 else jnp.logical_not(low), qb, jnp.zeros_like(qb)))
            s = lax.dot_general(jnp.concatenate(qm, axis=0), kd_ref[kh, ks, :], (((1,), (1,)), ((), ())),
                                preferred_element_type=F32)
            pv, den = [], []
            for g in range(SWA_GROUP):
                sg = s[g * WINDOW:(g + 1) * WINDOW]
                sg = jnp.where(valid, sg, jnp.where(sink_slot, sink_ref[kh * SWA_GROUP + g], NEG_INF))
                p = jnp.exp(sg - jnp.max(sg, axis=-1, keepdims=True))
                den.append(jnp.sum(p, axis=-1, keepdims=True))
                pv.append(jnp.where(sink_slot, 0.0, p).astype(BF16))
            o = jnp.dot(jnp.concatenate(pv, axis=0), vd_ref[kh, ks, :], preferred_element_type=F32)
            for pr in range(SWA_GROUP // 2):
                lo = o[(2 * pr) * WINDOW:(2 * pr + 1) * WINDOW] / den[2 * pr]
                hi = o[(2 * pr + 1) * WINDOW:(2 * pr + 2) * WINDOW] / den[2 * pr + 1]
                c0 = (kh * (SWA_GROUP // 2) + pr) * LANES
                cat_ref[rs, c0:c0 + LANES] = jnp.where(low, lo, hi).astype(BF16)
        zb = zv_ref[rs, :]
        ub = u_ref[rs, :]
        for g in range(GMLP_GROUPS):
            gs = slice(g * GMLP_GROUP_DIM, (g + 1) * GMLP_GROUP_DIM)
            w = jnp.where(tril, ws_ref[g], 0.0).astype(BF16)
            mixed = jnp.dot(w, zb[:, gs], preferred_element_type=F32) + bs_ref[g]
            cat_ref[rs, SWA_Q_W + g * GMLP_GROUP_DIM:SWA_Q_W + (g + 1) * GMLP_GROUP_DIM] = (
                ub[:, gs].astype(F32) * mixed).astype(BF16)


def _ab_mix_prompt(q, kv, u, zv, sinks, w_s, b_s, w_side, layer):
    nr = SEQ // AB_ROWS
    sub = AB_ROWS // WINDOW
    row = lambda b, r: (b * nr + r, 0)
    w_rows, w_cols = w_side.shape[1:]
    w_blk = (w_rows // (BATCH * nr), w_cols)
    return pl.pallas_call(
        _ab_prompt_kernel, grid=(BATCH, nr),
        in_specs=[pl.BlockSpec(memory_space=pltpu.SMEM),
                  pl.BlockSpec((AB_ROWS, SWA_Q_W), row),
                  pl.BlockSpec((AB_ROWS, 2 * SWA_KV_W), row),
                  pl.BlockSpec((WINDOW, 2 * SWA_KV_W), lambda b, r: (jnp.maximum((b * nr + r) * sub - 1, 0), 0)),
                  pl.BlockSpec((AB_ROWS, GMLP_WIDTH), row),
                  pl.BlockSpec((AB_ROWS, GMLP_WIDTH), row),
                  pl.BlockSpec((GMLP_GROUPS, GMLP_CHUNK, GMLP_CHUNK), lambda b, r: (0, 0, 0)),
                  pl.BlockSpec((GMLP_GROUPS, GMLP_CHUNK, 1), lambda b, r: (0, 0, 0)),
                  pl.BlockSpec((None,) + w_blk, lambda b, r: (layer, b * nr + r, 0))],
        out_specs=[pl.BlockSpec((AB_ROWS, AB_OUT), row), pl.BlockSpec(w_blk, row)],
        out_shape=[jax.ShapeDtypeStruct((M, AB_OUT), BF16), jax.ShapeDtypeStruct((w_rows, w_cols), BF16)],
        scratch_shapes=[pltpu.VMEM((SWA_KV_HEADS, WINDOW + AB_ROWS, LANES), BF16)] * 2,
        compiler_params=_cparams(("arbitrary", "arbitrary"), 48), name="ab_mix_prompt",
    )(sinks, q, kv, kv, u, zv, w_s, b_s.reshape(GMLP_GROUPS, GMLP_CHUNK, 1), w_side)


AB_NB = 16
AB_KEYS = 2 * WINDOW


def _ab_sample_kernel(sink_ref, q_ref, kv_ref, u_ref, zv_ref, ck_ref, cv_ref, ws_ref, bs_ref, cat_in_ref,
                      cat_ref, nk_ref, nv_ref):
    del cat_in_ref
    nb, t = AB_NB, DEC_SEQ
    kv3 = kv_ref[...].reshape(nb, t, 2 * SWA_KV_W)
    kn, vn = kv3[:, :, :SWA_KV_W], kv3[:, :, SWA_KV_W:]
    ck, cv = ck_ref[...], cv_ref[...]
    nk_ref[:, :WINDOW - t, :] = ck[:, t:, :]
    nk_ref[:, WINDOW - t:, :] = kn
    nv_ref[:, :WINDOW - t, :] = cv[:, t:, :]
    nv_ref[:, WINDOW - t:, :] = vn
    pad = jnp.zeros((nb, AB_KEYS - WINDOW - t, SWA_KV_W), F32)
    kall = jnp.concatenate([ck, kn, pad], axis=1).astype(BF16)
    vall = jnp.concatenate([cv, vn, pad], axis=1).astype(BF16)

    rows = SWA_GROUP * t
    tq = lax.broadcasted_iota(jnp.int32, (rows, AB_KEYS), 0) & (t - 1)
    sk = lax.broadcasted_iota(jnp.int32, (rows, AB_KEYS), 1)
    valid = ((sk < WINDOW) & (sk > tq)) | ((sk >= WINDOW) & (sk - WINDOW <= tq))

    q3 = q_ref[...].astype(F32).reshape(nb, t, SWA_Q_W)
    pieces = []
    for kh in range(SWA_KV_HEADS):
        heads = [kh * SWA_GROUP + g for g in range(SWA_GROUP)]
        qs = jnp.concatenate([q3[:, :, hd * SWA_HEAD_DIM:(hd + 1) * SWA_HEAD_DIM] for hd in heads],
                             axis=1).astype(BF16)
        cs = slice(kh * SWA_HEAD_DIM, (kh + 1) * SWA_HEAD_DIM)
        s = jnp.einsum('nqd,nkd->nqk', qs, kall[:, :, cs], preferred_element_type=F32)
        s = jnp.where(valid[None], s, NEG_INF)
        sink = jnp.concatenate([jnp.full((t, 1), sink_ref[hd], F32) for hd in heads], axis=0)[None]
        p, den = _sink_softmax(s, sink)
        o = jnp.einsum('nqk,nkd->nqd', p.astype(BF16), vall[:, :, cs], preferred_element_type=F32) / den
        pieces += [o[:, g * t:(g + 1) * t, :] for g in range(SWA_GROUP)]
    attn = jnp.concatenate(pieces, axis=2).reshape(nb * t, SWA_Q_W)
    cat_ref[:, :SWA_Q_W] = attn.astype(BF16)

    zv3 = zv_ref[...].reshape(nb, t, GMLP_WIDTH)
    u3 = u_ref[...].astype(F32).reshape(nb, t, GMLP_WIDTH)
    irow = lax.broadcasted_iota(jnp.int32, (t, GMLP_GROUP_DIM), 0)
    for g in range(GMLP_GROUPS):
        gs = slice(g * GMLP_GROUP_DIM, (g + 1) * GMLP_GROUP_DIM)
        zg = zv3[:, :, gs]
        mixed = jnp.zeros((nb, t, GMLP_GROUP_DIM), F32) + bs_ref[g][None]
        for j in range(t):
            wcol = jnp.where(irow >= j, ws_ref[g, j], 0.0)
            mixed = mixed + zg[:, j:j + 1, :] * wcol[None]
        gate = (u3[:, :, gs] * mixed).reshape(nb * t, GMLP_GROUP_DIM)
        cat_ref[:, SWA_Q_W + g * GMLP_GROUP_DIM:SWA_Q_W + (g + 1) * GMLP_GROUP_DIM] = gate.astype(BF16)


def _ab_mix_sample(cat, q, kv, u, zv_s, cache_k, cache_v, sinks, w_s, b_s):
    nb, t = AB_NB, DEC_SEQ
    rows = nb * t
    off = MP // rows
    srow = lambda i: (off + i, 0)
    wt = jnp.broadcast_to(jnp.swapaxes(w_s[:, :t, :t], 1, 2)[..., None], (GMLP_GROUPS, t, t, GMLP_GROUP_DIM))
    bt = jnp.broadcast_to(b_s[:, :t, None], (GMLP_GROUPS, t, GMLP_GROUP_DIM))
    cspec = pl.BlockSpec((nb, WINDOW, SWA_KV_W), lambda i: (i, 0, 0))
    cshape = jax.ShapeDtypeStruct((DEC_BATCH, WINDOW, SWA_KV_W), F32)
    return pl.pallas_call(
        _ab_sample_kernel, grid=(DEC_BATCH // nb,),
        in_specs=[pl.BlockSpec(memory_space=pltpu.SMEM),
                  pl.BlockSpec((rows, SWA_Q_W), srow),
                  pl.BlockSpec((rows, 2 * SWA_KV_W), srow),
                  pl.BlockSpec((rows, GMLP_WIDTH), srow),
                  pl.BlockSpec((rows, GMLP_WIDTH), lambda i: (i, 0)),
                  cspec, cspec,
                  pl.BlockSpec((GMLP_GROUPS, t, t, GMLP_GROUP_DIM), lambda i: (0, 0, 0, 0)),
                  pl.BlockSpec((GMLP_GROUPS, t, GMLP_GROUP_DIM), lambda i: (0, 0, 0)),
                  pl.BlockSpec(memory_space=pl.ANY)],
        out_specs=[pl.BlockSpec((rows, AB_OUT), srow), cspec, cspec],
        out_shape=[jax.ShapeDtypeStruct((M, AB_OUT), BF16), cshape, cshape],
        input_output_aliases={9: 0},
        compiler_params=_cparams(("arbitrary",), 48), name="ab_mix_sample",
    )(sinks, q, kv, u, zv_s, cache_k, cache_v, wt, bt, cat)


RET_ROWS = 1024
RET_BLOCK = 256


def _ret_prompt_part(lg, first_run, qk_ref, v_ref, s_ref):
    L = RET_BLOCK

    @pl.when(first_run)
    def _():
        s_ref[...] = jnp.zeros_like(s_ref)

    diff = (lax.broadcasted_iota(jnp.int32, (L, L), 0) - lax.broadcasted_iota(jnp.int32, (L, L), 1)).astype(F32)
    dmat = jnp.where(diff >= 0, jnp.exp(lg * jnp.maximum(diff, 0.0)), 0.0)
    idx = lax.broadcasted_iota(jnp.int32, (L, 1), 0).astype(F32)
    q_dec = jnp.exp(lg * (idx + 1.0))
    k_dec = jnp.exp(lg * (L - 1.0 - idx))
    s_dec = jnp.exp(lg * jnp.full((1, 1), float(L), F32))

    ys = []
    for ci in range(RET_ROWS // L):
        rs = slice(ci * L, (ci + 1) * L)
        q, k, v = qk_ref[rs, :RET_KEY_DIM], qk_ref[rs, RET_KEY_DIM:], v_ref[rs, :]
        att = lax.dot_general(q, k, (((1,), (1,)), ((), ())), preferred_element_type=F32) * dmat
        s0 = s_ref[...]
        o = (jnp.dot(att.astype(BF16), v, preferred_element_type=F32)
             + jnp.dot(q, s0.astype(BF16), preferred_element_type=F32) * q_dec)
        kd = (k.astype(F32) * k_dec).astype(BF16)
        s_ref[...] = s0 * s_dec + lax.dot_general(kd, v, (((0,), (0,)), ((), ())), preferred_element_type=F32)
        on = o * lax.rsqrt(jnp.mean(o * o, axis=-1, keepdims=True) + EPS)
        ys.append(on.astype(BF16))
    return jnp.concatenate(ys, axis=0)


RET_NB = DEC_BATCH // (BATCH * (SEQ // RET_ROWS))


def _ret_sample_part(lg, qk_ref, v_ref, s0_ref, s1_ref):
    nb, t = RET_NB, DEC_SEQ
    rows = nb * t
    q, k, v = qk_ref[:, :RET_KEY_DIM], qk_ref[:, RET_KEY_DIM:], v_ref[...]
    ri = lax.broadcasted_iota(jnp.int32, (rows, rows), 0)
    ci = lax.broadcasted_iota(jnp.int32, (rows, rows), 1)
    d = ((ri & (t - 1)) - (ci & (t - 1))).astype(F32)
    same = (ri // t) == (ci // t)
    dmat = jnp.where(same & (d >= 0), jnp.exp(lg * jnp.maximum(d, 0.0)), 0.0)
    att = lax.dot_general(q, k, (((1,), (1,)), ((), ())), preferred_element_type=F32) * dmat
    o_intra = jnp.dot(att.astype(BF16), v, preferred_element_type=F32)

    rcol = lax.broadcasted_iota(jnp.int32, (rows, 1), 0)
    tt = (rcol & (t - 1)).astype(F32)
    q_dec = jnp.exp(lg * (tt + 1.0))
    k_dec = jnp.exp(lg * (t - 1.0 - tt))
    s_dec = jnp.exp(lg * jnp.full((1, 1), float(t), F32))
    kd = (k.astype(F32) * k_dec).astype(BF16)
    qf = q.astype(F32)
    vf = v.astype(F32)
    o_rows = []
    for n in range(nb):
        s0 = s0_ref[n]
        qn = qf[n * t:(n + 1) * t].astype(BF16)
        o_rows.append(jnp.dot(qn, s0.astype(BF16), preferred_element_type=F32))
        vn = jnp.where(rcol // t == n, vf, 0.0).astype(BF16)
        s1_ref[n] = s0 * s_dec + lax.dot_general(kd, vn, (((0,), (0,)), ((), ())), preferred_element_type=F32)
    o = o_intra + jnp.concatenate(o_rows, axis=0) * q_dec
    on = o * lax.rsqrt(jnp.mean(o * o, axis=-1, keepdims=True) + EPS)
    return on.astype(BF16)


def _retention_kernel(lg_ref, qk_ref, v_ref, sqk_ref, sv_ref, s0_ref,
                      yp_ref, sfin_ref, ys_ref, s1_ref, s_ref):
    c = pl.program_id(2)
    lg = lg_ref[pl.program_id(1)]
    yp_ref[...] = _ret_prompt_part(lg, c == 0, qk_ref, v_ref, s_ref)

    @pl.when(c == pl.num_programs(2) - 1)
    def _():
        sfin_ref[...] = s_ref[...]

    ys_ref[...] = _ret_sample_part(lg, sqk_ref, sv_ref, s0_ref, s1_ref)


def _retention(zqk, zv, state, lg):
    nc = SEQ // RET_ROWS
    srows = RET_NB * DEC_SEQ
    soff = MP // srows
    prow = lambda b, c: b * nc + c
    width = 2 * RET_KEY_DIM
    head_specs = lambda rows, row: [pl.BlockSpec((None, rows, width), lambda b, h, c: (h, row(b, c), 0))] * 2
    sspec = pl.BlockSpec((RET_NB, None, RET_KEY_DIM, RET_VAL_DIM), lambda b, h, c: (prow(b, c), h, 0, 0))
    return pl.pallas_call(
        _retention_kernel, grid=(BATCH, RET_HEADS, nc),
        in_specs=([pl.BlockSpec(memory_space=pltpu.SMEM)] + head_specs(RET_ROWS, prow)
                  + head_specs(srows, lambda b, c: soff + prow(b, c)) + [sspec]),
        out_specs=[pl.BlockSpec((RET_ROWS, RET_VAL_DIM), lambda b, h, c: (prow(b, c), h)),
                   pl.BlockSpec((None, None, RET_KEY_DIM, RET_VAL_DIM), lambda b, h, c: (b, h, 0, 0)),
                   pl.BlockSpec((srows, RET_VAL_DIM), lambda b, h, c: (prow(b, c), h)),
                   sspec],
        out_shape=[jax.ShapeDtypeStruct((MP, RET_V_W), BF16),
                   jax.ShapeDtypeStruct((BATCH, RET_HEADS, RET_KEY_DIM, RET_VAL_DIM), F32),
                   jax.ShapeDtypeStruct((MS, RET_V_W), BF16),
                   jax.ShapeDtypeStruct((DEC_BATCH, RET_HEADS, RET_KEY_DIM, RET_VAL_DIM), F32)],
        scratch_shapes=[pltpu.VMEM((RET_KEY_DIM, RET_VAL_DIM), F32)],
        compiler_params=_cparams(("arbitrary", "arbitrary", "arbitrary"), 56), name="retention",
    )(lg, zqk, zv, zqk, zv, state)


EPI_ROWS = 128


def _outproj_kernel(ap_ref, as_ref, w_ref, xp_ref, xs_ref, gp_ref, gn_ref, *refs):
    gate_ref, (xn_ref, hn_ref) = (refs[0] if len(refs) == 3 else None), refs[-2:]
    tm = xn_ref.shape[0]
    is_prompt = pl.program_id(0) < MP // tm
    for r0 in range(0, tm, EPI_ROWS):
        rs = slice(r0, r0 + EPI_ROWS)
        a = jnp.where(is_prompt, ap_ref[rs, :], as_ref[rs, :])
        if gate_ref is not None:
            width = gate_ref.shape[2]
            a = jnp.concatenate(
                [(a[:, hh * width:(hh + 1) * width].astype(F32) * gate_ref[hh, rs, :].astype(F32)).astype(BF16)
                 for hh in range(gate_ref.shape[0])], axis=1)
        x = jnp.where(is_prompt, xp_ref[rs, :], xs_ref[rs, :])
        y = jnp.dot(a, w_ref[...], preferred_element_type=F32)
        xn, hn = _residual(y, x, gp_ref[...], gn_ref[...])
        xn_ref[rs, :] = xn
        hn_ref[rs, :] = hn


def _out_proj(a_parts, w_bf, x_parts, g_post, g_next, name, gate=None):
    kdim = w_bf.shape[0]
    tm = TM_OUT * D_MODEL // kdim
    np_t = MP // tm

    def two_source(parts, width):
        sample_t0 = np_t if len(parts) == 1 else 0
        return [pl.BlockSpec((tm, width), lambda i: (jnp.minimum(i, np_t - 1), 0)),
                pl.BlockSpec((tm, width), lambda i: (sample_t0 + jnp.maximum(i - np_t, 0), 0))]

    row = pl.BlockSpec((tm, D_MODEL), lambda i: (i, 0))
    vec = pl.BlockSpec((1, D_MODEL), lambda i: (0, 0))
    gate_args, gate_specs = [], []
    if gate is not None:
        gate_args = [gate]
        gate_specs = [pl.BlockSpec((gate.shape[0], tm, gate.shape[2]), lambda i: (0, i, 0))]
    return pl.pallas_call(
        _outproj_kernel, grid=(M // tm,),
        in_specs=(two_source(a_parts, kdim)
                  + [pl.BlockSpec((kdim, D_MODEL), lambda i: (0, 0), pipeline_mode=pl.Buffered(1))]
                  + two_source(x_parts, D_MODEL) + [vec, vec] + gate_specs),
        out_specs=[row, row],
        out_shape=[jax.ShapeDtypeStruct((M, D_MODEL), F32), jax.ShapeDtypeStruct((M, D_MODEL), BF16)],
        compiler_params=_cparams(("arbitrary",), 56), name=name,
    )(a_parts[0], a_parts[-1], w_bf, x_parts[0], x_parts[-1], g_post, g_next, *gate_args)


def _ffn_up_down(h, wu_ref, wd_ref):
    a = jnp.maximum(jnp.dot(h, wu_ref[...], preferred_element_type=F32), 0.0)
    return jnp.dot((a * a).astype(BF16), wd_ref[...], preferred_element_type=F32)


FFN_EPI_ROWS = 256


def _ffn_kernel(h_ref, wu_ref, wd_ref, x_ref, gp_ref, gn_ref, xn_ref, hn_ref):
    c = pl.program_id(1)
    first, last = c == 0, c == pl.num_programs(1) - 1

    @pl.when(first)
    def _():
        xn_ref[...] = _ffn_up_down(h_ref[...], wu_ref, wd_ref)

    @pl.when(jnp.logical_not(first | last))
    def _():
        xn_ref[...] += _ffn_up_down(h_ref[...], wu_ref, wd_ref)

    @pl.when(last)
    def _():
        for r0 in range(0, xn_ref.shape[0], FFN_EPI_ROWS):
            rs = slice(r0, r0 + FFN_EPI_ROWS)
            y = xn_ref[rs, :] + _ffn_up_down(h_ref[rs, :], wu_ref, wd_ref)
            xn, hn = _residual(y, x_ref[rs, :], gp_ref[...], gn_ref[...])
            xn_ref[rs, :] = xn
            hn_ref[rs, :] = hn


def _ffn_last_kernel(h_ref, wu_ref, wd_ref, x_ref, gp_ref, yp_ref, ys_ref, acc_ref):
    i, c = pl.program_id(0), pl.program_id(1)
    first, last = c == 0, c == pl.num_programs(1) - 1
    is_prompt = i < MP // acc_ref.shape[0]

    @pl.when(first)
    def _():
        acc_ref[...] = _ffn_up_down(h_ref[...], wu_ref, wd_ref)

    @pl.when(jnp.logical_not(first | last))
    def _():
        acc_ref[...] += _ffn_up_down(h_ref[...], wu_ref, wd_ref)

    def finish(y_ref):
        for r0 in range(0, acc_ref.shape[0], FFN_EPI_ROWS):
            rs = slice(r0, r0 + FFN_EPI_ROWS)
            y = acc_ref[rs, :] + _ffn_up_down(h_ref[rs, :], wu_ref, wd_ref)
            y_ref[rs, :] = x_ref[rs, :] + _rms(y, gp_ref[...])

    pl.when(last & is_prompt)(lambda: finish(yp_ref))
    pl.when(last & jnp.logical_not(is_prompt))(lambda: finish(ys_ref))


def _ffn(h, wu_bf, wd_bf, x, g_post, g_next):
    tm, tc = TM_OUT, 1024
    row = pl.BlockSpec((tm, D_MODEL), lambda i, c: (i, 0))
    vec = pl.BlockSpec((1, D_MODEL), lambda i, c: (0, 0))
    w_specs = [pl.BlockSpec((D_MODEL, tc), lambda i, c: (0, c)), pl.BlockSpec((tc, D_MODEL), lambda i, c: (c, 0))]
    grid = (M // tm, D_FF // tc)
    if g_next is not None:
        return pl.pallas_call(
            _ffn_kernel, grid=grid, in_specs=[row] + w_specs + [row, vec, vec], out_specs=[row, row],
            out_shape=[jax.ShapeDtypeStruct((M, D_MODEL), F32), jax.ShapeDtypeStruct((M, D_MODEL), BF16)],
            compiler_params=_cparams(("arbitrary", "arbitrary"), 56), name="ffn",
        )(h, wu_bf, wd_bf, x, g_post, g_next)
    np_t = MP // tm
    return pl.pallas_call(
        _ffn_last_kernel, grid=grid, in_specs=[row] + w_specs + [row, vec],
        out_specs=[pl.BlockSpec((tm, D_MODEL), lambda i, c: (jnp.minimum(i, np_t - 1), 0)),
                   pl.BlockSpec((tm, D_MODEL), lambda i, c: (jnp.maximum(i - np_t, 0), 0))],
        out_shape=[jax.ShapeDtypeStruct((MP, D_MODEL), F32), jax.ShapeDtypeStruct((MS, D_MODEL), F32)],
        scratch_shapes=[pltpu.VMEM((tm, D_MODEL), F32)],
        compiler_params=_cparams(("arbitrary", "arbitrary"), 56), name="ffn_last",
    )(h, wu_bf, wd_bf, x, g_post)


def _mem_kv_kernel(m_ref, g_ref, wk_ref, wv_ref, k_ref, v_ref):
    m = _rms(m_ref[...], g_ref[0]).astype(BF16)
    k_ref[0] = jnp.dot(m, wk_ref[0].astype(BF16), preferred_element_type=F32)
    v_ref[0] = jnp.dot(m, wv_ref[0].astype(BF16), preferred_element_type=F32)


def _mem_kv(mem, g, w_k, w_v):
    rows = BATCH * MEM_LEN
    wspec = pl.BlockSpec((1, D_MODEL, MEM_W), lambda l: (l, 0, 0))
    ospec = pl.BlockSpec((1, rows, MEM_W), lambda l: (l, 0, 0))
    oshape = jax.ShapeDtypeStruct((DEPTH, rows, MEM_W), F32)
    return pl.pallas_call(
        _mem_kv_kernel, grid=(DEPTH,),
        in_specs=[pl.BlockSpec((rows, D_MODEL), lambda l: (0, 0)),
                  pl.BlockSpec((1, 1, D_MODEL), lambda l: (l, 0, 0)), wspec, wspec],
        out_specs=[ospec, ospec], out_shape=[oshape, oshape],
        compiler_params=_cparams(("arbitrary",), 48), name="mem_kv",
    )(mem.reshape(rows, D_MODEL), g.reshape(DEPTH, 1, D_MODEL), w_k, w_v)


def _mem_attend_prompt(q, mk_ref, mv_ref):
    q = q.astype(BF16)
    mk, mv = mk_ref[...].astype(BF16), mv_ref[...].astype(BF16)
    outs = []
    for hh in range(MEM_HEADS):
        cs = slice(hh * MEM_HEAD_DIM, (hh + 1) * MEM_HEAD_DIM)
        s = lax.dot_general(q[:, cs], mk[:, cs], (((1,), (1,)), ((), ())),
                            preferred_element_type=F32) * (MEM_HEAD_DIM ** -0.5)
        p = jnp.exp(s - jnp.max(s, axis=-1, keepdims=True))
        den = jnp.sum(p, axis=-1, keepdims=True)
        outs.append(jnp.dot(p.astype(BF16), mv[:, cs], preferred_element_type=F32) / den)
    return jnp.concatenate(outs, axis=1)


def _mem_attend_sample(q, ck_ref, cv_ref):
    nb, t = ck_ref.shape[0], DEC_SEQ
    qs = jnp.concatenate([q[:, hh * MEM_HEAD_DIM:(hh + 1) * MEM_HEAD_DIM].reshape(nb, t, MEM_HEAD_DIM)
                          for hh in range(MEM_HEADS)], axis=1).astype(BF16)
    s = jnp.einsum('nqd,nkd->nqk', qs, ck_ref[...].astype(BF16),
                   preferred_element_type=F32) * (MEM_HEAD_DIM ** -0.5)
    shape = (MEM_HEADS * t, MEM_LEN * MEM_HEADS)
    own = ((lax.broadcasted_iota(jnp.int32, shape, 1) & (MEM_HEADS - 1))
           == lax.broadcasted_iota(jnp.int32, shape, 0) // t)
    s = jnp.where(own[None], s, NEG_INF)
    p = jnp.exp(s - jnp.max(s, axis=-1, keepdims=True))
    den = jnp.sum(p, axis=-1, keepdims=True)
    o3 = jnp.einsum('nqk,nkd->nqd', p.astype(BF16), cv_ref[...].astype(BF16),
                    preferred_element_type=F32) / den
    return jnp.concatenate([o3[:, hh * t:(hh + 1) * t, :].reshape(nb * t, MEM_HEAD_DIM)
                            for hh in range(MEM_HEADS)], axis=1)


def _mem_kernel(hp_ref, hs_ref, wq_ref, wo_ref, mk_ref, mv_ref, ck_ref, cv_ref, xp_ref, xs_ref, gp_ref, gn_ref,
                xn_ref, hn_ref, os_ref):
    i = pl.program_id(0)
    tm = xn_ref.shape[0]
    n_prompt = MP // tm
    srows = hs_ref.shape[0]
    project = lambda h_ref: jnp.dot(h_ref[...], wq_ref[...], preferred_element_type=F32)

    def finish(o, x_ref):
        for r0 in range(0, tm, EPI_ROWS):
            rs = slice(r0, r0 + EPI_ROWS)
            y = jnp.dot(o[rs].astype(BF16), wo_ref[...], preferred_element_type=F32)
            xn, hn = _residual(y, x_ref[rs, :], gp_ref[...], gn_ref[...])
            xn_ref[rs, :] = xn
            hn_ref[rs, :] = hn

    @pl.when(i < n_prompt)
    def _():
        os_ref[pl.ds(pl.multiple_of(i * srows, srows), srows), :] = _mem_attend_sample(
            project(hs_ref), ck_ref, cv_ref)
        finish(_mem_attend_prompt(project(hp_ref), mk_ref, mv_ref), xp_ref)

    @pl.when(i >= n_prompt)
    def _():
        finish(os_ref[pl.ds(pl.multiple_of((i - n_prompt) * tm, tm), tm), :], xs_ref)


MEM_TM = 256
MEM_NB = DEC_BATCH // (MP // MEM_TM)


def _mem_attention(h_parts, x_parts, wq_bf, wo_bf, mk, mv, cache_k, cache_v, layer, g_post, g_next):
    tm, nb = MEM_TM, MEM_NB
    srows = nb * DEC_SEQ
    sample_row0 = MP if len(h_parts) == 1 else 0
    per_b = SEQ // tm
    n_p = MP // tm
    pidx = lambda i: jnp.minimum(i, n_p - 1)
    row = pl.BlockSpec((tm, D_MODEL), lambda i: (i, 0))
    vec = pl.BlockSpec((1, D_MODEL), lambda i: (0, 0))
    kvspec = pl.BlockSpec((None, MEM_LEN, MEM_W), lambda i: (layer * BATCH + pidx(i) // per_b, 0, 0))
    cspec = pl.BlockSpec((None, nb, MEM_LEN * MEM_HEADS, MEM_HEAD_DIM), lambda i: (layer, pidx(i), 0, 0))
    resident = lambda shape: pl.BlockSpec(shape, lambda i: (layer, 0), pipeline_mode=pl.Buffered(1))
    return pl.pallas_call(
        _mem_kernel, grid=(M // tm,),
        in_specs=[pl.BlockSpec((tm, D_MODEL), lambda i: (pidx(i), 0)),
                  pl.BlockSpec((srows, D_MODEL), lambda i: (sample_row0 // srows + pidx(i), 0)),
                  resident((D_MODEL, MEM_W)), resident((MEM_W, D_MODEL)), kvspec, kvspec,
                  cspec, cspec,
                  pl.BlockSpec((tm, D_MODEL), lambda i: (pidx(i), 0)),
                  pl.BlockSpec((tm, D_MODEL), lambda i: (sample_row0 // tm + jnp.maximum(i - n_p, 0), 0)),
                  vec, vec],
        out_specs=[row, row],
        out_shape=[jax.ShapeDtypeStruct((M, D_MODEL), F32), jax.ShapeDtypeStruct((M, D_MODEL), BF16)],
        scratch_shapes=[pltpu.VMEM((MS, MEM_W), F32)],
        compiler_params=_cparams(("arbitrary",), 48), name="mem_attention",
    )(h_parts[0], h_parts[-1], wq_bf, wo_bf, mk, mv, cache_k, cache_v, x_parts[0], x_parts[-1], g_post, g_next)


def kernel(x_prompt, x_sample, cache_swa_k, cache_swa_v, state_ret, cache_mem_k, cache_mem_v, mem_prompt, norm_mix_pre, norm_mix_post, norm_mem, norm_x_pre, norm_x_post, norm_ffn_pre, norm_ffn_post, w_ab_in, w_ab_out, swa_sinks, gmlp_ln_g, gmlp_ln_b, gmlp_w_s, gmlp_b_s, w_ret_in, w_ret_out, w_mem_q, w_mem_k, w_mem_v, w_mem_o, w_ffn_up, w_ffn_down):
    vec = lambda g, layer: g[layer].reshape(1, D_MODEL)
    mem_k, mem_v = _mem_kv(mem_prompt, norm_mem, w_mem_k, w_mem_v)
    mem_k_b = mem_k.reshape(DEPTH * BATCH, MEM_LEN, MEM_W)
    mem_v_b = mem_v.reshape(DEPTH * BATCH, MEM_LEN, MEM_W)
    ck_mem = cache_mem_k.reshape(DEPTH, DEC_BATCH, MEM_LEN * MEM_HEADS, MEM_HEAD_DIM)
    cv_mem = cache_mem_v.reshape(DEPTH, DEC_BATCH, MEM_LEN * MEM_HEADS, MEM_HEAD_DIM)
    lg = jnp.log1p(-jnp.exp2(-5.0 - jnp.arange(RET_HEADS, dtype=F32)))

    x_parts = (x_prompt.reshape(MP, D_MODEL), x_sample.reshape(MS, D_MODEL))
    h, w_ab_in_bf, (swa_c, swa_s1, swa_s2, ret_c, ret_s) = _stacked_prenorm(
        *x_parts, vec(norm_mix_pre, 0), w_ab_in, 0)
    outs = {}
    for layer in range(DEPTH):
        j = layer // 2
        if layer % 2 == 0:
            (q, kv, u, zv, zv_s), (wu_bf, wout_bf, wq_bf, wo_bf) = _ab_in_proj(
                h, w_ab_in_bf, gmlp_ln_g[j].reshape(1, GMLP_WIDTH), gmlp_ln_b[j].reshape(1, GMLP_WIDTH),
                (swa_c, swa_s1, swa_s2), w_ffn_up, w_ab_out, w_mem_q, w_mem_o, layer, j)
            cat, wd_bf = _ab_mix_prompt(q, kv, u, zv, swa_sinks[j], gmlp_w_s[j], gmlp_b_s[j], w_ffn_down, layer)
            cat, nk_s, nv_s = _ab_mix_sample(
                cat, q, kv, u, zv_s, cache_swa_k[j].reshape(DEC_BATCH, WINDOW, SWA_KV_W),
                cache_swa_v[j].reshape(DEC_BATCH, WINDOW, SWA_KV_W), swa_sinks[j], gmlp_w_s[j], gmlp_b_s[j])
            kv_p = jnp.stack([kv[(b + 1) * SEQ - WINDOW:(b + 1) * SEQ] for b in range(BATCH)])
            kv_p = kv_p.reshape(BATCH, WINDOW, 2, SWA_KV_HEADS, SWA_HEAD_DIM)
            outs.setdefault('swa_k_p', []).append(kv_p[:, :, 0])
            outs.setdefault('swa_v_p', []).append(kv_p[:, :, 1])
            outs.setdefault('swa_k_s', []).append(nk_s.reshape(DEC_BATCH, WINDOW, SWA_KV_HEADS, SWA_HEAD_DIM))
            outs.setdefault('swa_v_s', []).append(nv_s.reshape(DEC_BATCH, WINDOW, SWA_KV_HEADS, SWA_HEAD_DIM))
            outs.setdefault('gmlp_v_s', []).append(zv_s.reshape(DEC_BATCH, DEC_SEQ, GMLP_WIDTH))
            mixed, mix_name, gate = (cat,), "ab_out", None
        else:
            (zqk, zv, zg), (wu_bf, wd_bf, wout_bf) = _ret_in_proj(
                h, w_ret_in[j], (ret_c, ret_s), w_ffn_up, w_ffn_down, w_ret_out, layer, j)
            y_p, s_p, y_s, s_s = _retention(zqk, zv, state_ret[j], lg)
            outs.setdefault('ret_p', []).append(s_p)
            outs.setdefault('ret_s', []).append(s_s)
            mixed, mix_name, gate = (y_p, y_s), "ret_out", zg
        x, h = _out_proj(mixed, wout_bf, x_parts, vec(norm_mix_post, layer), vec(norm_x_pre, layer), mix_name,
                         gate)
        x_parts, h_parts = (x,), (h,)
        g_post = vec(norm_x_post, layer)
        g_next = vec(norm_ffn_pre, layer)
        x, h = _mem_attention(h_parts, x_parts, wq_bf, wo_bf, mem_k_b, mem_v_b, ck_mem, cv_mem, layer,
                              g_post, g_next)
        g_next = vec(norm_mix_pre, layer + 1) if layer + 1 < DEPTH else None
        x, h = _ffn(h, wu_bf, wd_bf, x, vec(norm_ffn_post, layer), g_next)
        x_parts = (x,)

    y_prompt, y_sample = x, h
    stack = lambda name: jnp.stack(outs[name])
    mem_shape = (DEPTH, BATCH, MEM_LEN, MEM_HEADS, MEM_HEAD_DIM)
    return (y_prompt.reshape(BATCH, SEQ, D_MODEL), y_sample.reshape(DEC_BATCH, DEC_SEQ, D_MODEL),
            stack('swa_k_p'), stack('swa_v_p'), stack('swa_k_s'), stack('swa_v_s'), stack('gmlp_v_s'),
            stack('ret_p'), stack('ret_s'), mem_k.reshape(mem_shape), mem_v.reshape(mem_shape))
```

```python
import functools
import math

import jax
import jax.numpy as jnp
from jax import lax
from jax.experimental import pallas as pl
from jax.experimental.pallas import tpu as pltpu

F32 = jnp.float32
BF16 = jnp.bfloat16

D_MODEL = 2048
BATCH = 2
SEQ = 4096
DEPTH = 2
DEC_BATCH = 128
DEC_SEQ = 8
PAST_LEN = 8192

SWA_HEADS = 16
SWA_KV_HEADS = 4
SWA_HEAD_DIM = 64
SWA_GROUP = SWA_HEADS // SWA_KV_HEADS
WINDOW = 128
ROPE_THETA = 500000.0
ROPE_DIM = SWA_HEAD_DIM // 4
ROPE_HALF = ROPE_DIM // 2

GMLP_GROUPS = 4
GMLP_WIDTH = 1024
GMLP_GROUP_DIM = GMLP_WIDTH // GMLP_GROUPS
GMLP_CHUNK = 128

SWA_Q_W = SWA_HEADS * SWA_HEAD_DIM
SWA_KV_W = SWA_KV_HEADS * SWA_HEAD_DIM
AB_IN = SWA_Q_W + 2 * SWA_KV_W + 2 * GMLP_WIDTH
AB_OUT = SWA_Q_W + GMLP_WIDTH

RET_HEADS = 8
RET_KEY_DIM = D_MODEL // RET_HEADS
RET_VAL_DIM = 2 * RET_KEY_DIM
RET_THETA = 10000.0
RET_QK_W = RET_HEADS * RET_KEY_DIM
RET_V_W = RET_HEADS * RET_VAL_DIM
RET_IN = 2 * RET_QK_W + 2 * RET_V_W

MEM_LEN = 256
MEM_HEADS = 4
MEM_HEAD_DIM = 128
MEM_W = MEM_HEADS * MEM_HEAD_DIM

D_FF = 4 * D_MODEL
EPS = 1e-6
NEG_INF = -1e30

MP = BATCH * SEQ
MS = DEC_BATCH * DEC_SEQ
M = MP + MS
LANES = 128
MIB = 1024 * 1024

TM_IN = 1024
TAB_ROWS = SEQ + MS
TM_OUT = 512


def _cparams(sem, vmem_mib):
    return pltpu.CompilerParams(dimension_semantics=sem, vmem_limit_bytes=vmem_mib * MIB)


def _rms(x, g):
    return x * lax.rsqrt(jnp.mean(x * x, axis=-1, keepdims=True) + EPS) * g


def _residual(y, x, g_post, g_next):
    xn = x + _rms(y, g_post)
    return xn, _rms(xn, g_next).astype(BF16)


def _table_positions(block, tr):
    r = lax.broadcasted_iota(jnp.int32, (tr, LANES), 0) + block * tr
    pos = jnp.where(r < SEQ, r, PAST_LEN + ((r - SEQ) & (DEC_SEQ - 1))).astype(F32)
    return pos, lax.broadcasted_iota(jnp.int32, (tr, LANES), 1)


def _swa_table_block(block, sc_ref, s1_ref, s2_ref):
    pos, lane = _table_positions(block, sc_ref.shape[0])
    d = lane & (SWA_HEAD_DIM - 1)
    inv = jnp.exp((d & (ROPE_HALF - 1)).astype(F32) * (-math.log(ROPE_THETA) / ROPE_HALF))
    ang = pos * inv
    c, s = jnp.cos(ang), jnp.sin(ang)
    sc_ref[...] = jnp.where(d < ROPE_DIM, c, 1.0)
    s1_ref[...] = jnp.where(d < ROPE_HALF, -s, 0.0)
    s2_ref[...] = jnp.where((d >= ROPE_HALF) & (d < ROPE_DIM), s, 0.0)


def _ret_table_block(block, rc_ref, rs_ref):
    pos, lane = _table_positions(block, rc_ref.shape[0])
    inv_r = jnp.exp(lane.astype(F32) * (-math.log(RET_THETA) / LANES))
    ang_r = pos * inv_r
    rc_ref[...] = jnp.cos(ang_r)
    rs_ref[...] = jnp.sin(ang_r)


def _norm_kernel(xp_ref, xs_ref, g_ref, w_ref, h_ref, wbf_ref, *table_refs, n_table_blocks):
    def emit(src_ref):
        h_ref[...] = _rms(src_ref[...], g_ref[...]).astype(BF16)

    i = pl.program_id(0)
    is_prompt = i < MP // h_ref.shape[0]
    pl.when(is_prompt)(lambda: emit(xp_ref))
    pl.when(jnp.logical_not(is_prompt))(lambda: emit(xs_ref))
    wbf_ref[...] = w_ref[...].astype(BF16)
    pl.when(i < n_table_blocks)(lambda: _swa_table_block(i, *table_refs))


def _stacked_prenorm(x_prompt, x_sample, g, w, layer):
    tm = TM_OUT
    n_p = MP // tm
    w_src, w_in_spec, w_shape, w_out_spec = _row_cast_job(w, layer, n_p)
    tab_spec = pl.BlockSpec((TAB_ROWS // n_p, LANES), lambda i: (jnp.minimum(i, n_p - 1), 0))
    tab_shape = jax.ShapeDtypeStruct((TAB_ROWS, LANES), F32)
    outs = pl.pallas_call(
        functools.partial(_norm_kernel, n_table_blocks=n_p), grid=(M // tm,),
        in_specs=[pl.BlockSpec((tm, D_MODEL), lambda i: (jnp.minimum(i, n_p - 1), 0)),
                  pl.BlockSpec((tm, D_MODEL), lambda i: (jnp.maximum(i - n_p, 0), 0)),
                  pl.BlockSpec((1, D_MODEL), lambda i: (0, 0)), w_in_spec],
        out_specs=[pl.BlockSpec((tm, D_MODEL), lambda i: (i, 0)), w_out_spec] + [tab_spec] * 3,
        out_shape=[jax.ShapeDtypeStruct((M, D_MODEL), BF16), w_shape] + [tab_shape] * 3,
        compiler_params=_cparams(("arbitrary",), 40), name="prenorm",
    )(x_prompt, x_sample, g, w_src)
    return outs[0], outs[1], tuple(outs[2:])


W_SUB = 512


def _inproj_kernel(h_ref, *refs, n_sub, n_side, n_cast, epilogue):
    w_refs, refs = refs[:n_sub], refs[n_sub:]
    side, refs = refs[:n_side], refs[n_side:]
    cast_src, refs = refs[:n_cast], refs[n_cast:]
    wbf_ref = refs[-1]
    outs, cast_dst = refs[:len(refs) - 1 - n_cast], refs[len(refs) - 1 - n_cast:-1]

    @pl.when(pl.program_id(1) == 0)
    def _():
        for s, w_ref in enumerate(w_refs):
            wbf_ref[:, s * W_SUB:(s + 1) * W_SUB] = w_ref[...].astype(BF16)

    acc = jnp.dot(h_ref[...], wbf_ref[...], preferred_element_type=F32)
    epilogue(acc, side, outs)
    for src, dst in zip(cast_src, cast_dst):
        dst[...] = src[...].astype(BF16)


def _tab_index(j, i):
    return (jnp.where(i < MP // TM_IN, i % (SEQ // TM_IN), SEQ // TM_IN), 0)


CAST_STEPS = MP // TM_IN


def _cast_job(w, layer, nrb, ncb, block_of):
    rows, cols = w.shape[1:]
    blk = (rows // nrb, cols // ncb)
    index = lambda j, i: block_of(j, jnp.minimum(i, CAST_STEPS - 1))
    return (w, pl.BlockSpec((None,) + blk, lambda j, i: (layer,) + tuple(index(j, i))),
            jax.ShapeDtypeStruct((rows, cols), BF16), pl.BlockSpec(blk, index))


def _inproj(h, w, col0, ncols, tn, epilogue, side, side_specs, out_shapes, out_specs, name, vmem_mib,
            casts=()):
    tm = TM_IN
    k = h.shape[1]
    n_sub = tn // W_SUB
    w_specs = [pl.BlockSpec((k, W_SUB), lambda j, i, s=s: (0, col0 // W_SUB + j * n_sub + s))
               for s in range(n_sub)]
    as_list = lambda v: list(v) if isinstance(v, (list, tuple)) else [v]
    outs = pl.pallas_call(
        functools.partial(_inproj_kernel, n_sub=n_sub, n_side=len(side), n_cast=len(casts), epilogue=epilogue),
        grid=(ncols // tn, M // tm),
        in_specs=([pl.BlockSpec((tm, k), lambda j, i: (i, 0))] + w_specs + side_specs
                  + [c[1] for c in casts]),
        out_specs=as_list(out_specs) + [c[3] for c in casts],
        out_shape=as_list(out_shapes) + [c[2] for c in casts],
        scratch_shapes=[pltpu.VMEM((k, tn), BF16)],
        compiler_params=_cparams(("arbitrary", "arbitrary"), vmem_mib), name=name,
    )(h, *([w] * n_sub), *side, *[c[0] for c in casts])
    n_main = len(outs) - len(casts)
    main = outs[0] if n_main == 1 else tuple(outs[:n_main])
    return (main, tuple(outs[n_main:])) if casts else main


def _swa_rope(x, c, s1, s2):
    return x * c + pltpu.roll(x, LANES - ROPE_HALF, 1) * s1 + pltpu.roll(x, ROPE_HALF, 1) * s2


def _epi_swa_q(acc, side, outs):
    c, s1, s2 = (t[...] for t in side)
    (q_ref,) = outs
    for cc in range(acc.shape[1] // LANES):
        sl = slice(cc * LANES, (cc + 1) * LANES)
        q_ref[:, sl] = (_swa_rope(acc[:, sl], c, s1, s2) * (SWA_HEAD_DIM ** -0.5)).astype(BF16)


def _epi_swa_kv(acc, side, outs):
    c, s1, s2 = (t[...] for t in side)
    (kv_ref,) = outs
    for cc in range(SWA_KV_W // LANES):
        sl = slice(cc * LANES, (cc + 1) * LANES)
        kv_ref[:, sl] = _swa_rope(acc[:, sl], c, s1, s2)
    kv_ref[:, SWA_KV_W:] = acc[:, SWA_KV_W:]


def _epi_gelu(acc, side, outs):
    outs[0][...] = jax.nn.gelu(acc).astype(BF16)


def _epi_ret_qk(acc, side, outs):
    cos, sin = (t[...] for t in side)
    (z_ref,) = outs
    tn = acc.shape[1]
    scale = jnp.where(pl.program_id(0) < RET_QK_W // tn, 1.0, RET_KEY_DIM ** -0.5)
    for hh in range(tn // RET_KEY_DIM):
        c0 = hh * RET_KEY_DIM
        x1, x2 = acc[:, c0:c0 + LANES], acc[:, c0 + LANES:c0 + 2 * LANES]
        z_ref[hh, :, :LANES] = ((x1 * cos - x2 * sin) * scale).astype(BF16)
        z_ref[hh, :, LANES:] = ((x2 * cos + x1 * sin) * scale).astype(BF16)


def _per_head(fn):
    def epilogue(acc, side, outs):
        (z_ref,) = outs
        for hh in range(z_ref.shape[0]):
            z_ref[hh] = fn(acc[:, hh * RET_VAL_DIM:(hh + 1) * RET_VAL_DIM]).astype(BF16)
    return epilogue


_epi_cast = _per_head(lambda a: a)
_epi_silu = _per_head(lambda a: a / (1.0 + jnp.exp(-a)))


AB_TM = 256


def _ab_in_kernel(h_ref, w_ref, sc_ref, s1_ref, s2_ref, lng_ref, lnb_ref, *refs, n_cast):
    cast_src, (q_ref, kv_ref, u_ref, zv_ref, zvs_ref), cast_dst = refs[:n_cast], refs[n_cast:n_cast + 5], refs[n_cast + 5:]
    h = h_ref[...]
    tabs = (sc_ref, s1_ref, s2_ref)
    seg = lambda c0, n: jnp.dot(h, w_ref[:, c0:c0 + n], preferred_element_type=F32)
    _epi_swa_q(seg(0, SWA_Q_W), tabs, (q_ref,))
    _epi_swa_kv(seg(SWA_Q_W, 2 * SWA_KV_W), tabs, (kv_ref,))
    _epi_gelu(seg(SWA_Q_W + 2 * SWA_KV_W, GMLP_WIDTH), (), (u_ref,))
    a = jax.nn.gelu(seg(AB_IN - GMLP_WIDTH, GMLP_WIDTH))
    xc = a - jnp.mean(a, axis=-1, keepdims=True)
    y = xc * lax.rsqrt(jnp.mean(xc * xc, axis=-1, keepdims=True) + EPS) * lng_ref[...] + lnb_ref[...]
    zv_ref[...] = y.astype(BF16)

    @pl.when(pl.program_id(0) >= MP // AB_TM)
    def _():
        zvs_ref[...] = y

    for src, dst in zip(cast_src, cast_dst):
        dst[...] = src[...].astype(BF16)


def _row_cast_job(w, layer, n_blocks):
    rows, cols = w.shape[1:]
    blk = (rows // n_blocks, cols)
    index = lambda i: (jnp.minimum(i, n_blocks - 1), 0)
    return (w, pl.BlockSpec((None,) + blk, lambda i: (layer,) + index(i)),
            jax.ShapeDtypeStruct((rows, cols), BF16), pl.BlockSpec(blk, index))


def _ab_in_proj(h, w_in_bf, ln_g, ln_b, tabs, w_up, w_out, w_mem_q, w_mem_o, layer, j):
    tm = AB_TM
    n_p = MP // tm
    stacked = lambda w: w.reshape(1, w.shape[0] * w.shape[1], w.shape[2])
    casts = [_row_cast_job(w_up, layer, n_p), _row_cast_job(w_out, j, n_p),
             _row_cast_job(stacked(w_mem_q), 0, n_p), _row_cast_job(stacked(w_mem_o), 0, n_p)]
    tab_rows = SEQ // tm
    tab_spec = pl.BlockSpec((tm, LANES), lambda i: (jnp.where(i < n_p, i % tab_rows, tab_rows + i - n_p), 0))
    vec_spec = pl.BlockSpec((1, GMLP_WIDTH), lambda i: (0, 0))
    row_out = lambda n: pl.BlockSpec((tm, n), lambda i: (i, 0))
    outs = pl.pallas_call(
        functools.partial(_ab_in_kernel, n_cast=len(casts)), grid=(M // tm,),
        in_specs=([pl.BlockSpec((tm, D_MODEL), lambda i: (i, 0)),
                   pl.BlockSpec((D_MODEL, AB_IN), lambda i: (0, 0), pipeline_mode=pl.Buffered(1))]
                  + [tab_spec] * 3 + [vec_spec] * 2 + [c[1] for c in casts]),
        out_specs=[row_out(SWA_Q_W), row_out(2 * SWA_KV_W), row_out(GMLP_WIDTH), row_out(GMLP_WIDTH),
                   pl.BlockSpec((tm, GMLP_WIDTH), lambda i: (jnp.maximum(i - n_p, 0), 0))]
                  + [c[3] for c in casts],
        out_shape=[jax.ShapeDtypeStruct((M, SWA_Q_W), BF16), jax.ShapeDtypeStruct((M, 2 * SWA_KV_W), F32),
                   jax.ShapeDtypeStruct((M, GMLP_WIDTH), BF16), jax.ShapeDtypeStruct((M, GMLP_WIDTH), BF16),
                   jax.ShapeDtypeStruct((MS, GMLP_WIDTH), F32)] + [c[2] for c in casts],
        compiler_params=_cparams(("arbitrary",), 52), name="ab_in",
    )(h, w_in_bf, *tabs, ln_g, ln_b, *[c[0] for c in casts])
    return tuple(outs[:5]), tuple(outs[5:])


def _ret_in_proj(h, w_in, tabs, w_up, w_down, w_out, layer, j):
    rc, rs = tabs
    tn = 1024
    tab_spec = pl.BlockSpec((TM_IN, LANES), _tab_index)
    by_tile = lambda j, i: (j, i)
    head_major = jax.ShapeDtypeStruct((RET_HEADS, M, 2 * RET_KEY_DIM), BF16)
    seg = lambda col0, ncols, epi, side, specs, out_spec, name, w, lead: _inproj(
        h, w_in, col0, ncols, tn, epi, side, specs, head_major, out_spec, name, 56,
        casts=[_cast_job(w, lead, ncols // tn, CAST_STEPS, by_tile)])
    qk_heads = tn // RET_KEY_DIM
    q_tiles = RET_QK_W // tn
    qk_spec = pl.BlockSpec((qk_heads, TM_IN, RET_KEY_DIM), lambda j, i: (j % q_tiles, i, j // q_tiles))
    v_spec = pl.BlockSpec((tn // RET_VAL_DIM, TM_IN, RET_VAL_DIM), lambda j, i: (j, i, 0))
    zqk, (wu_bf,) = seg(0, 2 * RET_QK_W, _epi_ret_qk, [rc, rs], [tab_spec] * 2, qk_spec, "ret_in_qk",
                        w_up, layer)
    zv, (wd_bf,) = seg(2 * RET_QK_W, RET_V_W, _epi_cast, [], [], v_spec, "ret_in_v", w_down, layer)
    zg, (wout_bf,) = seg(2 * RET_QK_W + RET_V_W, RET_V_W, _epi_silu, [], [], v_spec, "ret_in_g", w_out, j)
    return (zqk, zv, zg), (wu_bf, wd_bf, wout_bf)


AB_ROWS = 512


def _sink_softmax(s, sink):
    m = jnp.maximum(jnp.max(s, axis=-1, keepdims=True), sink)
    p = jnp.exp(s - m)
    den = jnp.sum(p, axis=-1, keepdims=True) + jnp.exp(sink - m)
    return p, den


def _ab_prompt_kernel(sink_ref, q_ref, kv_ref, kvp_ref, u_ref, zv_ref, ws_ref, bs_ref, wsrc_ref, cat_ref,
                      wdst_ref, kd_ref, vd_ref):
    wdst_ref[...] = wsrc_ref[...].astype(BF16)
    r = pl.program_id(1)
    low = lax.broadcasted_iota(jnp.int32, (1, LANES), 1) < SWA_HEAD_DIM

    def both_halves(x, half):
        sw = pltpu.roll(x, SWA_HEAD_DIM, 1)
        return (jnp.where(low, x, sw) if half == 0 else jnp.where(low, sw, x)).astype(BF16)

    for kh in range(SWA_KV_HEADS):
        kcol = slice((kh // 2) * LANES, (kh // 2 + 1) * LANES)
        vcol = slice(SWA_KV_W + (kh // 2) * LANES, SWA_KV_W + (kh // 2 + 1) * LANES)
        kd_ref[kh, :WINDOW, :] = both_halves(kvp_ref[:, kcol], kh % 2)
        kd_ref[kh, WINDOW:, :] = both_halves(kv_ref[:, kcol], kh % 2)
        vd_ref[kh, :WINDOW, :] = both_halves(kvp_ref[:, vcol], kh % 2)
        vd_ref[kh, WINDOW:, :] = both_halves(kv_ref[:, vcol], kh % 2)

    t = lax.broadcasted_iota(jnp.int32, (WINDOW, 2 * WINDOW), 0)
    sk = lax.broadcasted_iota(jnp.int32, (WINDOW, 2 * WINDOW), 1)
    band = (sk <= t + WINDOW) & (sk > t)
    sink_slot = sk == t
    tril = (lax.broadcasted_iota(jnp.int32, (GMLP_CHUNK, GMLP_CHUNK), 0)
            >= lax.broadcasted_iota(jnp.int32, (GMLP_CHUNK, GMLP_CHUNK), 1))

    for sb in range(AB_ROWS // WINDOW):
        rs = slice(sb * WINDOW, (sb + 1) * WINDOW)
        ks = slice(sb * WINDOW, sb * WINDOW + 2 * WINDOW)
        valid = band & ((sk >= WINDOW) | (r * (AB_ROWS // WINDOW) + sb > 0))
        for kh in range(SWA_KV_HEADS):
            qm = []
            for g in range(SWA_GROUP):
                hd = kh * SWA_GROUP + g
                qb = q_ref[rs, (hd // 2) * LANES:(hd // 2 + 1) * LANES]
                qm.append(jnp.where(low if hd % 2 == 0 else jnp.logical_not(low), qb, jnp.zeros_like(qb)))
            s = lax.dot_general(jnp.concatenate(qm, axis=0), kd_ref[kh, ks, :], (((1,), (1,)), ((), ())),
                                preferred_element_type=F32)
            pv, den = [], []
            for g in range(SWA_GROUP):
                sg = s[g * WINDOW:(g + 1) * WINDOW]
                sg = jnp.where(valid, sg, jnp.where(sink_slot, sink_ref[kh * SWA_GROUP + g], NEG_INF))
                p = jnp.exp(sg - jnp.max(sg, axis=-1, keepdims=True))
                den.append(jnp.sum(p, axis=-1, keepdims=True))
                pv.append(jnp.where(sink_slot, 0.0, p).astype(BF16))
            o = jnp.dot(jnp.concatenate(pv, axis=0), vd_ref[kh, ks, :], preferred_element_type=F32)
            for pr in range(SWA_GROUP // 2):
                lo = o[(2 * pr) * WINDOW:(2 * pr + 1) * WINDOW] / den[2 * pr]
                hi = o[(2 * pr + 1) * WINDOW:(2 * pr + 2) * WINDOW] / den[2 * pr + 1]
                c0 = (kh * (SWA_GROUP // 2) + pr) * LANES
                cat_ref[rs, c0:c0 + LANES] = jnp.where(low, lo, hi).astype(BF16)
        zb = zv_ref[rs, :]
        ub = u_ref[rs, :]
        for g in range(GMLP_GROUPS):
            gs = slice(g * GMLP_GROUP_DIM, (g + 1) * GMLP_GROUP_DIM)
            w = jnp.where(tril, ws_ref[g], 0.0).astype(BF16)
            mixed = jnp.dot(w, zb[:, gs], preferred_element_type=F32) + bs_ref[g]
            cat_ref[rs, SWA_Q_W + g * GMLP_GROUP_DIM:SWA_Q_W + (g + 1) * GMLP_GROUP_DIM] = (
                ub[:, gs].astype(F32) * mixed).astype(BF16)


def _ab_mix_prompt(q, kv, u, zv, sinks, w_s, b_s, w_side, layer):
    nr = SEQ // AB_ROWS
    sub = AB_ROWS // WINDOW
    row = lambda b, r: (b * nr + r, 0)
    w_rows, w_cols = w_side.shape[1:]
    w_blk = (w_rows // (BATCH * nr), w_cols)
    return pl.pallas_call(
        _ab_prompt_kernel, grid=(BATCH, nr),
        in_specs=[pl.BlockSpec(memory_space=pltpu.SMEM),
                  pl.BlockSpec((AB_ROWS, SWA_Q_W), row),
                  pl.BlockSpec((AB_ROWS, 2 * SWA_KV_W), row),
                  pl.BlockSpec((WINDOW, 2 * SWA_KV_W), lambda b, r: (jnp.maximum((b * nr + r) * sub - 1, 0), 0)),
                  pl.BlockSpec((AB_ROWS, GMLP_WIDTH), row),
                  pl.BlockSpec((AB_ROWS, GMLP_WIDTH), row),
                  pl.BlockSpec((GMLP_GROUPS, GMLP_CHUNK, GMLP_CHUNK), lambda b, r: (0, 0, 0)),
                  pl.BlockSpec((GMLP_GROUPS, GMLP_CHUNK, 1), lambda b, r: (0, 0, 0)),
                  pl.BlockSpec((None,) + w_blk, lambda b, r: (layer, b * nr + r, 0))],
        out_specs=[pl.BlockSpec((AB_ROWS, AB_OUT), row), pl.BlockSpec(w_blk, row)],
        out_shape=[jax.ShapeDtypeStruct((M, AB_OUT), BF16), jax.ShapeDtypeStruct((w_rows, w_cols), BF16)],
        scratch_shapes=[pltpu.VMEM((SWA_KV_HEADS, WINDOW + AB_ROWS, LANES), BF16)] * 2,
        compiler_params=_cparams(("arbitrary", "arbitrary"), 48), name="ab_mix_prompt",
    )(sinks, q, kv, kv, u, zv, w_s, b_s.reshape(GMLP_GROUPS, GMLP_CHUNK, 1), w_side)


AB_NB = 16
AB_KEYS = 2 * WINDOW


def _ab_sample_kernel(sink_ref, q_ref, kv_ref, u_ref, zv_ref, ck_ref, cv_ref, ws_ref, bs_ref, cat_in_ref,
                      cat_ref, nk_ref, nv_ref):
    del cat_in_ref
    nb, t = AB_NB, DEC_SEQ
    kv3 = kv_ref[...].reshape(nb, t, 2 * SWA_KV_W)
    kn, vn = kv3[:, :, :SWA_KV_W], kv3[:, :, SWA_KV_W:]
    ck, cv = ck_ref[...], cv_ref[...]
    nk_ref[:, :WINDOW - t, :] = ck[:, t:, :]
    nk_ref[:, WINDOW - t:, :] = kn
    nv_ref[:, :WINDOW - t, :] = cv[:, t:, :]
    nv_ref[:, WINDOW - t:, :] = vn
    pad = jnp.zeros((nb, AB_KEYS - WINDOW - t, SWA_KV_W), F32)
    kall = jnp.concatenate([ck, kn, pad], axis=1).astype(BF16)
    vall = jnp.concatenate([cv, vn, pad], axis=1).astype(BF16)

    rows = SWA_GROUP * t
    tq = lax.broadcasted_iota(jnp.int32, (rows, AB_KEYS), 0) & (t - 1)
    sk = lax.broadcasted_iota(jnp.int32, (rows, AB_KEYS), 1)
    valid = ((sk < WINDOW) & (sk > tq)) | ((sk >= WINDOW) & (sk - WINDOW <= tq))

    q3 = q_ref[...].astype(F32).reshape(nb, t, SWA_Q_W)
    pieces = []
    for kh in range(SWA_KV_HEADS):
        heads = [kh * SWA_GROUP + g for g in range(SWA_GROUP)]
        qs = jnp.concatenate([q3[:, :, hd * SWA_HEAD_DIM:(hd + 1) * SWA_HEAD_DIM] for hd in heads],
                             axis=1).astype(BF16)
        cs = slice(kh * SWA_HEAD_DIM, (kh + 1) * SWA_HEAD_DIM)
        s = jnp.einsum('nqd,nkd->nqk', qs, kall[:, :, cs], preferred_element_type=F32)
        s = jnp.where(valid[None], s, NEG_INF)
        sink = jnp.concatenate([jnp.full((t, 1), sink_ref[hd], F32) for hd in heads], axis=0)[None]
        p, den = _sink_softmax(s, sink)
        o = jnp.einsum('nqk,nkd->nqd', p.astype(BF16), vall[:, :, cs], preferred_element_type=F32) / den
        pieces += [o[:, g * t:(g + 1) * t, :] for g in range(SWA_GROUP)]
    attn = jnp.concatenate(pieces, axis=2).reshape(nb * t, SWA_Q_W)
    cat_ref[:, :SWA_Q_W] = attn.astype(BF16)

    zv3 = zv_ref[...].reshape(nb, t, GMLP_WIDTH)
    u3 = u_ref[...].astype(F32).reshape(nb, t, GMLP_WIDTH)
    irow = lax.broadcasted_iota(jnp.int32, (t, GMLP_GROUP_DIM), 0)
    for g in range(GMLP_GROUPS):
        gs = slice(g * GMLP_GROUP_DIM, (g + 1) * GMLP_GROUP_DIM)
        zg = zv3[:, :, gs]
        mixed = jnp.zeros((nb, t, GMLP_GROUP_DIM), F32) + bs_ref[g][None]
        for j in range(t):
            wcol = jnp.where(irow >= j, ws_ref[g, j], 0.0)
            mixed = mixed + zg[:, j:j + 1, :] * wcol[None]
        gate = (u3[:, :, gs] * mixed).reshape(nb * t, GMLP_GROUP_DIM)
        cat_ref[:, SWA_Q_W + g * GMLP_GROUP_DIM:SWA_Q_W + (g + 1) * GMLP_GROUP_DIM] = gate.astype(BF16)


def _ab_mix_sample(cat, q, kv, u, zv_s, cache_k, cache_v, sinks, w_s, b_s):
    nb, t = AB_NB, DEC_SEQ
    rows = nb * t
    off = MP // rows
    srow = lambda i: (off + i, 0)
    wt = jnp.broadcast_to(jnp.swapaxes(w_s[:, :t, :t], 1, 2)[..., None], (GMLP_GROUPS, t, t, GMLP_GROUP_DIM))
    bt = jnp.broadcast_to(b_s[:, :t, None], (GMLP_GROUPS, t, GMLP_GROUP_DIM))
    cspec = pl.BlockSpec((nb, WINDOW, SWA_KV_W), lambda i: (i, 0, 0))
    cshape = jax.ShapeDtypeStruct((DEC_BATCH, WINDOW, SWA_KV_W), F32)
    return pl.pallas_call(
        _ab_sample_kernel, grid=(DEC_BATCH // nb,),
        in_specs=[pl.BlockSpec(memory_space=pltpu.SMEM),
                  pl.BlockSpec((rows, SWA_Q_W), srow),
                  pl.BlockSpec((rows, 2 * SWA_KV_W), srow),
                  pl.BlockSpec((rows, GMLP_WIDTH), srow),
                  pl.BlockSpec((rows, GMLP_WIDTH), lambda i: (i, 0)),
                  cspec, cspec,
                  pl.BlockSpec((GMLP_GROUPS, t, t, GMLP_GROUP_DIM), lambda i: (0, 0, 0, 0)),
                  pl.BlockSpec((GMLP_GROUPS, t, GMLP_GROUP_DIM), lambda i: (0, 0, 0)),
                  pl.BlockSpec(memory_space=pl.ANY)],
        out_specs=[pl.BlockSpec((rows, AB_OUT), srow), cspec, cspec],
        out_shape=[jax.ShapeDtypeStruct((M, AB_OUT), BF16), cshape, cshape],
        input_output_aliases={9: 0},
        compiler_params=_cparams(("arbitrary",), 48), name="ab_mix_sample",
    )(sinks, q, kv, u, zv_s, cache_k, cache_v, wt, bt, cat)


RET_ROWS = 1024
RET_BLOCK = 256


def _ret_prompt_part(lg, first_run, qk_ref, v_ref, s_ref):
    L = RET_BLOCK

    @pl.when(first_run)
    def _():
        s_ref[...] = jnp.zeros_like(s_ref)

    diff = (lax.broadcasted_iota(jnp.int32, (L, L), 0) - lax.broadcasted_iota(jnp.int32, (L, L), 1)).astype(F32)
    dmat = jnp.where(diff >= 0, jnp.exp(lg * jnp.maximum(diff, 0.0)), 0.0)
    idx = lax.broadcasted_iota(jnp.int32, (L, 1), 0).astype(F32)
    q_dec = jnp.exp(lg * (idx + 1.0))
    k_dec = jnp.exp(lg * (L - 1.0 - idx))
    s_dec = jnp.exp(lg * jnp.full((1, 1), float(L), F32))

    ys = []
    for ci in range(RET_ROWS // L):
        rs = slice(ci * L, (ci + 1) * L)
        q, k, v = qk_ref[rs, :RET_KEY_DIM], qk_ref[rs, RET_KEY_DIM:], v_ref[rs, :]
        att = lax.dot_general(q, k, (((1,), (1,)), ((), ())), preferred_element_type=F32) * dmat
        s0 = s_ref[...]
        o = (jnp.dot(att.astype(BF16), v, preferred_element_type=F32)
             + jnp.dot(q, s0.astype(BF16), preferred_element_type=F32) * q_dec)
        kd = (k.astype(F32) * k_dec).astype(BF16)
        s_ref[...] = s0 * s_dec + lax.dot_general(kd, v, (((0,), (0,)), ((), ())), preferred_element_type=F32)
        on = o * lax.rsqrt(jnp.mean(o * o, axis=-1, keepdims=True) + EPS)
        ys.append(on.astype(BF16))
    return jnp.concatenate(ys, axis=0)


RET_NB = DEC_BATCH // (BATCH * (SEQ // RET_ROWS))


def _ret_sample_part(lg, qk_ref, v_ref, s0_ref, s1_ref):
    nb, t = RET_NB, DEC_SEQ
    rows = nb * t
    q, k, v = qk_ref[:, :RET_KEY_DIM], qk_ref[:, RET_KEY_DIM:], v_ref[...]
    ri = lax.broadcasted_iota(jnp.int32, (rows, rows), 0)
    ci = lax.broadcasted_iota(jnp.int32, (rows, rows), 1)
    d = ((ri & (t - 1)) - (ci & (t - 1))).astype(F32)
    same = (ri // t) == (ci // t)
    dmat = jnp.where(same & (d >= 0), jnp.exp(lg * jnp.maximum(d, 0.0)), 0.0)
    att = lax.dot_general(q, k, (((1,), (1,)), ((), ())), preferred_element_type=F32) * dmat
    o_intra = jnp.dot(att.astype(BF16), v, preferred_element_type=F32)

    rcol = lax.broadcasted_iota(jnp.int32, (rows, 1), 0)
    tt = (rcol & (t - 1)).astype(F32)
    q_dec = jnp.exp(lg * (tt + 1.0))
    k_dec = jnp.exp(lg * (t - 1.0 - tt))
    s_dec = jnp.exp(lg * jnp.full((1, 1), float(t), F32))
    kd = (k.astype(F32) * k_dec).astype(BF16)
    qf = q.astype(F32)
    vf = v.astype(F32)
    o_rows = []
    for n in range(nb):
        s0 = s0_ref[n]
        qn = qf[n * t:(n + 1) * t].astype(BF16)
        o_rows.append(jnp.dot(qn, s0.astype(BF16), preferred_element_type=F32))
        vn = jnp.where(rcol // t == n, vf, 0.0).astype(BF16)
        s1_ref[n] = s0 * s_dec + lax.dot_general(kd, vn, (((0,), (0,)), ((), ())), preferred_element_type=F32)
    o = o_intra + jnp.concatenate(o_rows, axis=0) * q_dec
    on = o * lax.rsqrt(jnp.mean(o * o, axis=-1, keepdims=True) + EPS)
    return on.astype(BF16)


def _retention_kernel(lg_ref, qk_ref, v_ref, sqk_ref, sv_ref, s0_ref,
                      yp_ref, sfin_ref, ys_ref, s1_ref, s_ref):
    c = pl.program_id(2)
    lg = lg_ref[pl.program_id(1)]
    yp_ref[...] = _ret_prompt_part(lg, c == 0, qk_ref, v_ref, s_ref)

    @pl.when(c == pl.num_programs(2) - 1)
    def _():
        sfin_ref[...] = s_ref[...]

    ys_ref[...] = _ret_sample_part(lg, sqk_ref, sv_ref, s0_ref, s1_ref)


def _retention(zqk, zv, state, lg):
    nc = SEQ // RET_ROWS
    srows = RET_NB * DEC_SEQ
    soff = MP // srows
    prow = lambda b, c: b * nc + c
    width = 2 * RET_KEY_DIM
    head_specs = lambda rows, row: [pl.BlockSpec((None, rows, width), lambda b, h, c: (h, row(b, c), 0))] * 2
    sspec = pl.BlockSpec((RET_NB, None, RET_KEY_DIM, RET_VAL_DIM), lambda b, h, c: (prow(b, c), h, 0, 0))
    return pl.pallas_call(
        _retention_kernel, grid=(BATCH, RET_HEADS, nc),
        in_specs=([pl.BlockSpec(memory_space=pltpu.SMEM)] + head_specs(RET_ROWS, prow)
                  + head_specs(srows, lambda b, c: soff + prow(b, c)) + [sspec]),
        out_specs=[pl.BlockSpec((RET_ROWS, RET_VAL_DIM), lambda b, h, c: (prow(b, c), h)),
                   pl.BlockSpec((None, None, RET_KEY_DIM, RET_VAL_DIM), lambda b, h, c: (b, h, 0, 0)),
                   pl.BlockSpec((srows, RET_VAL_DIM), lambda b, h, c: (prow(b, c), h)),
                   sspec],
        out_shape=[jax.ShapeDtypeStruct((MP, RET_V_W), BF16),
                   jax.ShapeDtypeStruct((BATCH, RET_HEADS, RET_KEY_DIM, RET_VAL_DIM), F32),
                   jax.ShapeDtypeStruct((MS, RET_V_W), BF16),
                   jax.ShapeDtypeStruct((DEC_BATCH, RET_HEADS, RET_KEY_DIM, RET_VAL_DIM), F32)],
        scratch_shapes=[pltpu.VMEM((RET_KEY_DIM, RET_VAL_DIM), F32)],
        compiler_params=_cparams(("arbitrary", "arbitrary", "arbitrary"), 56), name="retention",
    )(lg, zqk, zv, zqk, zv, state)


EPI_ROWS = 128


def _outproj_kernel(ap_ref, as_ref, w_ref, xp_ref, xs_ref, gp_ref, gn_ref, *refs):
    gate_ref, (xn_ref, hn_ref) = (refs[0] if len(refs) == 3 else None), refs[-2:]
    tm = xn_ref.shape[0]
    is_prompt = pl.program_id(0) < MP // tm
    for r0 in range(0, tm, EPI_ROWS):
        rs = slice(r0, r0 + EPI_ROWS)
        a = jnp.where(is_prompt, ap_ref[rs, :], as_ref[rs, :])
        if gate_ref is not None:
            width = gate_ref.shape[2]
            a = jnp.concatenate(
                [(a[:, hh * width:(hh + 1) * width].astype(F32) * gate_ref[hh, rs, :].astype(F32)).astype(BF16)
                 for hh in range(gate_ref.shape[0])], axis=1)
        x = jnp.where(is_prompt, xp_ref[rs, :], xs_ref[rs, :])
        y = jnp.dot(a, w_ref[...], preferred_element_type=F32)
        xn, hn = _residual(y, x, gp_ref[...], gn_ref[...])
        xn_ref[rs, :] = xn
        hn_ref[rs, :] = hn


def _out_proj(a_parts, w_bf, x_parts, g_post, g_next, name, gate=None):
    kdim = w_bf.shape[0]
    tm = TM_OUT * D_MODEL // kdim
    np_t = MP // tm

    def two_source(parts, width):
        sample_t0 = np_t if len(parts) == 1 else 0
        return [pl.BlockSpec((tm, width), lambda i: (jnp.minimum(i, np_t - 1), 0)),
                pl.BlockSpec((tm, width), lambda i: (sample_t0 + jnp.maximum(i - np_t, 0), 0))]

    row = pl.BlockSpec((tm, D_MODEL), lambda i: (i, 0))
    vec = pl.BlockSpec((1, D_MODEL), lambda i: (0, 0))
    gate_args, gate_specs = [], []
    if gate is not None:
        gate_args = [gate]
        gate_specs = [pl.BlockSpec((gate.shape[0], tm, gate.shape[2]), lambda i: (0, i, 0))]
    return pl.pallas_call(
        _outproj_kernel, grid=(M // tm,),
        in_specs=(two_source(a_parts, kdim)
                  + [pl.BlockSpec((kdim, D_MODEL), lambda i: (0, 0), pipeline_mode=pl.Buffered(1))]
                  + two_source(x_parts, D_MODEL) + [vec, vec] + gate_specs),
        out_specs=[row, row],
        out_shape=[jax.ShapeDtypeStruct((M, D_MODEL), F32), jax.ShapeDtypeStruct((M, D_MODEL), BF16)],
        compiler_params=_cparams(("arbitrary",), 56), name=name,
    )(a_parts[0], a_parts[-1], w_bf, x_parts[0], x_parts[-1], g_post, g_next, *gate_args)


def _ffn_up_down(h, wu_ref, wd_ref):
    a = jnp.maximum(jnp.dot(h, wu_ref[...], preferred_element_type=F32), 0.0)
    return jnp.dot((a * a).astype(BF16), wd_ref[...], preferred_element_type=F32)


FFN_EPI_ROWS = 256


def _ffn_kernel(h_ref, wu_ref, wd_ref, x_ref, gp_ref, gn_ref, xn_ref, hn_ref):
    c = pl.program_id(1)
    first, last = c == 0, c == pl.num_programs(1) - 1

    @pl.when(first)
    def _():
        xn_ref[...] = _ffn_up_down(h_ref[...], wu_ref, wd_ref)

    @pl.when(jnp.logical_not(first | last))
    def _():
        xn_ref[...] += _ffn_up_down(h_ref[...], wu_ref, wd_ref)

    @pl.when(last)
    def _():
        for r0 in range(0, xn_ref.shape[0], FFN_EPI_ROWS):
            rs = slice(r0, r0 + FFN_EPI_ROWS)
            y = xn_ref[rs, :] + _ffn_up_down(h_ref[rs, :], wu_ref, wd_ref)
            xn, hn = _residual(y, x_ref[rs, :], gp_ref[...], gn_ref[...])
            xn_ref[rs, :] = xn
            hn_ref[rs, :] = hn


def _ffn_last_kernel(h_ref, wu_ref, wd_ref, x_ref, gp_ref, yp_ref, ys_ref, acc_ref):
    i, c = pl.program_id(0), pl.program_id(1)
    first, last = c == 0, c == pl.num_programs(1) - 1
    is_prompt = i < MP // acc_ref.shape[0]

    @pl.when(first)
    def _():
        acc_ref[...] = _ffn_up_down(h_ref[...], wu_ref, wd_ref)

    @pl.when(jnp.logical_not(first | last))
    def _():
        acc_ref[...] += _ffn_up_down(h_ref[...], wu_ref, wd_ref)

    def finish(y_ref):
        for r0 in range(0, acc_ref.shape[0], FFN_EPI_ROWS):
            rs = slice(r0, r0 + FFN_EPI_ROWS)
            y = acc_ref[rs, :] + _ffn_up_down(h_ref[rs, :], wu_ref, wd_ref)
            y_ref[rs, :] = x_ref[rs, :] + _rms(y, gp_ref[...])

    pl.when(last & is_prompt)(lambda: finish(yp_ref))
    pl.when(last & jnp.logical_not(is_prompt))(lambda: finish(ys_ref))


def _ffn(h, wu_bf, wd_bf, x, g_post, g_next):
    tm, tc = TM_OUT, 1024
    row = pl.BlockSpec((tm, D_MODEL), lambda i, c: (i, 0))
    vec = pl.BlockSpec((1, D_MODEL), lambda i, c: (0, 0))
    w_specs = [pl.BlockSpec((D_MODEL, tc), lambda i, c: (0, c)), pl.BlockSpec((tc, D_MODEL), lambda i, c: (c, 0))]
    grid = (M // tm, D_FF // tc)
    if g_next is not None:
        return pl.pallas_call(
            _ffn_kernel, grid=grid, in_specs=[row] + w_specs + [row, vec, vec], out_specs=[row, row],
            out_shape=[jax.ShapeDtypeStruct((M, D_MODEL), F32), jax.ShapeDtypeStruct((M, D_MODEL), BF16)],
            compiler_params=_cparams(("arbitrary", "arbitrary"), 56), name="ffn",
        )(h, wu_bf, wd_bf, x, g_post, g_next)
    np_t = MP // tm
    return pl.pallas_call(
        _ffn_last_kernel, grid=grid, in_specs=[row] + w_specs + [row, vec],
        out_specs=[pl.BlockSpec((tm, D_MODEL), lambda i, c: (jnp.minimum(i, np_t - 1), 0)),
                   pl.BlockSpec((tm, D_MODEL), lambda i, c: (jnp.maximum(i - np_t, 0), 0))],
        out_shape=[jax.ShapeDtypeStruct((MP, D_MODEL), F32), jax.ShapeDtypeStruct((MS, D_MODEL), F32)],
        scratch_shapes=[pltpu.VMEM((tm, D_MODEL), F32)],
        compiler_params=_cparams(("arbitrary", "arbitrary"), 56), name="ffn_last",
    )(h, wu_bf, wd_bf, x, g_post)


def _mem_kv_kernel(m_ref, g_ref, wk_ref, wv_ref, k_ref, v_ref, rc_ref, rs_ref):
    m = _rms(m_ref[...], g_ref[0]).astype(BF16)
    k_ref[0] = jnp.dot(m, wk_ref[0].astype(BF16), preferred_element_type=F32)
    v_ref[0] = jnp.dot(m, wv_ref[0].astype(BF16), preferred_element_type=F32)
    _ret_table_block(pl.program_id(0), rc_ref, rs_ref)


def _mem_kv(mem, g, w_k, w_v):
    rows = BATCH * MEM_LEN
    wspec = pl.BlockSpec((1, D_MODEL, MEM_W), lambda l: (l, 0, 0))
    ospec = pl.BlockSpec((1, rows, MEM_W), lambda l: (l, 0, 0))
    oshape = jax.ShapeDtypeStruct((DEPTH, rows, MEM_W), F32)
    tab_spec = pl.BlockSpec((TAB_ROWS // DEPTH, LANES), lambda l: (l, 0))
    tab_shape = jax.ShapeDtypeStruct((TAB_ROWS, LANES), F32)
    return pl.pallas_call(
        _mem_kv_kernel, grid=(DEPTH,),
        in_specs=[pl.BlockSpec((rows, D_MODEL), lambda l: (0, 0)),
                  pl.BlockSpec((1, 1, D_MODEL), lambda l: (l, 0, 0)), wspec, wspec],
        out_specs=[ospec, ospec, tab_spec, tab_spec], out_shape=[oshape, oshape, tab_shape, tab_shape],
        compiler_params=_cparams(("arbitrary",), 48), name="mem_kv",
    )(mem.reshape(rows, D_MODEL), g.reshape(DEPTH, 1, D_MODEL), w_k, w_v)


def _mem_attend_prompt(q, mk_ref, mv_ref):
    q = q.astype(BF16)
    mk, mv = mk_ref[...].astype(BF16), mv_ref[...].astype(BF16)
    outs = []
    for hh in range(MEM_HEADS):
        cs = slice(hh * MEM_HEAD_DIM, (hh + 1) * MEM_HEAD_DIM)
        s = lax.dot_general(q[:, cs], mk[:, cs], (((1,), (1,)), ((), ())),
                            preferred_element_type=F32) * (MEM_HEAD_DIM ** -0.5)
        p = jnp.exp(s - jnp.max(s, axis=-1, keepdims=True))
        den = jnp.sum(p, axis=-1, keepdims=True)
        outs.append(jnp.dot(p.astype(BF16), mv[:, cs], preferred_element_type=F32) / den)
    return jnp.concatenate(outs, axis=1)


def _mem_attend_sample(q, ck_ref, cv_ref):
    nb, t = ck_ref.shape[0], DEC_SEQ
    qs = jnp.concatenate([q[:, hh * MEM_HEAD_DIM:(hh + 1) * MEM_HEAD_DIM].reshape(nb, t, MEM_HEAD_DIM)
                          for hh in range(MEM_HEADS)], axis=1).astype(BF16)
    s = jnp.einsum('nqd,nkd->nqk', qs, ck_ref[...].astype(BF16),
                   preferred_element_type=F32) * (MEM_HEAD_DIM ** -0.5)
    shape = (MEM_HEADS * t, MEM_LEN * MEM_HEADS)
    own = ((lax.broadcasted_iota(jnp.int32, shape, 1) & (MEM_HEADS - 1))
           == lax.broadcasted_iota(jnp.int32, shape, 0) // t)
    s = jnp.where(own[None], s, NEG_INF)
    p = jnp.exp(s - jnp.max(s, axis=-1, keepdims=True))
    den = jnp.sum(p, axis=-1, keepdims=True)
    o3 = jnp.einsum('nqk,nkd->nqd', p.astype(BF16), cv_ref[...].astype(BF16),
                    preferred_element_type=F32) / den
    return jnp.concatenate([o3[:, hh * t:(hh + 1) * t, :].reshape(nb * t, MEM_HEAD_DIM)
                            for hh in range(MEM_HEADS)], axis=1)


def _mem_kernel(hp_ref, hs_ref, wq_ref, wo_ref, mk_ref, mv_ref, ck_ref, cv_ref, xp_ref, xs_ref, gp_ref, gn_ref,
                xn_ref, hn_ref, os_ref):
    i = pl.program_id(0)
    tm = xn_ref.shape[0]
    n_prompt = MP // tm
    srows = hs_ref.shape[0]
    project = lambda h_ref: jnp.dot(h_ref[...], wq_ref[...], preferred_element_type=F32)

    def finish(o, x_ref):
        for r0 in range(0, tm, EPI_ROWS):
            rs = slice(r0, r0 + EPI_ROWS)
            y = jnp.dot(o[rs].astype(BF16), wo_ref[...], preferred_element_type=F32)
            xn, hn = _residual(y, x_ref[rs, :], gp_ref[...], gn_ref[...])
            xn_ref[rs, :] = xn
            hn_ref[rs, :] = hn

    @pl.when(i < n_prompt)
    def _():
        os_ref[pl.ds(pl.multiple_of(i * srows, srows), srows), :] = _mem_attend_sample(
            project(hs_ref), ck_ref, cv_ref)
        finish(_mem_attend_prompt(project(hp_ref), mk_ref, mv_ref), xp_ref)

    @pl.when(i >= n_prompt)
    def _():
        finish(os_ref[pl.ds(pl.multiple_of((i - n_prompt) * tm, tm), tm), :], xs_ref)


MEM_TM = 256
MEM_NB = DEC_BATCH // (MP // MEM_TM)


def _mem_attention(h_parts, x_parts, wq_bf, wo_bf, mk, mv, cache_k, cache_v, layer, g_post, g_next):
    tm, nb = MEM_TM, MEM_NB
    srows = nb * DEC_SEQ
    sample_row0 = MP if len(h_parts) == 1 else 0
    per_b = SEQ // tm
    n_p = MP // tm
    pidx = lambda i: jnp.minimum(i, n_p - 1)
    row = pl.BlockSpec((tm, D_MODEL), lambda i: (i, 0))
    vec = pl.BlockSpec((1, D_MODEL), lambda i: (0, 0))
    kvspec = pl.BlockSpec((None, MEM_LEN, MEM_W), lambda i: (layer * BATCH + pidx(i) // per_b, 0, 0))
    cspec = pl.BlockSpec((None, nb, MEM_LEN * MEM_HEADS, MEM_HEAD_DIM), lambda i: (layer, pidx(i), 0, 0))
    resident = lambda shape: pl.BlockSpec(shape, lambda i: (layer, 0), pipeline_mode=pl.Buffered(1))
    return pl.pallas_call(
        _mem_kernel, grid=(M // tm,),
        in_specs=[pl.BlockSpec((tm, D_MODEL), lambda i: (pidx(i), 0)),
                  pl.BlockSpec((srows, D_MODEL), lambda i: (sample_row0 // srows + pidx(i), 0)),
                  resident((D_MODEL, MEM_W)), resident((MEM_W, D_MODEL)), kvspec, kvspec,
                  cspec, cspec,
                  pl.BlockSpec((tm, D_MODEL), lambda i: (pidx(i), 0)),
                  pl.BlockSpec((tm, D_MODEL), lambda i: (sample_row0 // tm + jnp.maximum(i - n_p, 0), 0)),
                  vec, vec],
        out_specs=[row, row],
        out_shape=[jax.ShapeDtypeStruct((M, D_MODEL), F32), jax.ShapeDtypeStruct((M, D_MODEL), BF16)],
        scratch_shapes=[pltpu.VMEM((MS, MEM_W), F32)],
        compiler_params=_cparams(("arbitrary",), 48), name="mem_attention",
    )(h_parts[0], h_parts[-1], wq_bf, wo_bf, mk, mv, cache_k, cache_v, x_parts[0], x_parts[-1], g_post, g_next)


def kernel(x_prompt, x_sample, cache_swa_k, cache_swa_v, state_ret, cache_mem_k, cache_mem_v, mem_prompt, norm_mix_pre, norm_mix_post, norm_mem, norm_x_pre, norm_x_post, norm_ffn_pre, norm_ffn_post, w_ab_in, w_ab_out, swa_sinks, gmlp_ln_g, gmlp_ln_b, gmlp_w_s, gmlp_b_s, w_ret_in, w_ret_out, w_mem_q, w_mem_k, w_mem_v, w_mem_o, w_ffn_up, w_ffn_down):
    vec = lambda g, layer: g[layer].reshape(1, D_MODEL)
    mem_k, mem_v, ret_c, ret_s = _mem_kv(mem_prompt, norm_mem, w_mem_k, w_mem_v)
    mem_k_b = mem_k.reshape(DEPTH * BATCH, MEM_LEN, MEM_W)
    mem_v_b = mem_v.reshape(DEPTH * BATCH, MEM_LEN, MEM_W)
    ck_mem = cache_mem_k.reshape(DEPTH, DEC_BATCH, MEM_LEN * MEM_HEADS, MEM_HEAD_DIM)
    cv_mem = cache_mem_v.reshape(DEPTH, DEC_BATCH, MEM_LEN * MEM_HEADS, MEM_HEAD_DIM)
    lg = jnp.log1p(-jnp.exp2(-5.0 - jnp.arange(RET_HEADS, dtype=F32)))

    x_parts = (x_prompt.reshape(MP, D_MODEL), x_sample.reshape(MS, D_MODEL))
    h, w_ab_in_bf, (swa_c, swa_s1, swa_s2) = _stacked_prenorm(
        *x_parts, vec(norm_mix_pre, 0), w_ab_in, 0)
    outs = {}
    for layer in range(DEPTH):
        j = layer // 2
        if layer % 2 == 0:
            (q, kv, u, zv, zv_s), (wu_bf, wout_bf, wq_bf, wo_bf) = _ab_in_proj(
                h, w_ab_in_bf, gmlp_ln_g[j].reshape(1, GMLP_WIDTH), gmlp_ln_b[j].reshape(1, GMLP_WIDTH),
                (swa_c, swa_s1, swa_s2), w_ffn_up, w_ab_out, w_mem_q, w_mem_o, layer, j)
            cat, wd_bf = _ab_mix_prompt(q, kv, u, zv, swa_sinks[j], gmlp_w_s[j], gmlp_b_s[j], w_ffn_down, layer)
            cat, nk_s, nv_s = _ab_mix_sample(
                cat, q, kv, u, zv_s, cache_swa_k[j].reshape(DEC_BATCH, WINDOW, SWA_KV_W),
                cache_swa_v[j].reshape(DEC_BATCH, WINDOW, SWA_KV_W), swa_sinks[j], gmlp_w_s[j], gmlp_b_s[j])
            kv_p = jnp.stack([kv[(b + 1) * SEQ - WINDOW:(b + 1) * SEQ] for b in range(BATCH)])
            kv_p = kv_p.reshape(BATCH, WINDOW, 2, SWA_KV_HEADS, SWA_HEAD_DIM)
            outs.setdefault('swa_k_p', []).append(kv_p[:, :, 0])
            outs.setdefault('swa_v_p', []).append(kv_p[:, :, 1])
            outs.setdefault('swa_k_s', []).append(nk_s.reshape(DEC_BATCH, WINDOW, SWA_KV_HEADS, SWA_HEAD_DIM))
            outs.setdefault('swa_v_s', []).append(nv_s.reshape(DEC_BATCH, WINDOW, SWA_KV_HEADS, SWA_HEAD_DIM))
            outs.setdefault('gmlp_v_s', []).append(zv_s.reshape(DEC_BATCH, DEC_SEQ, GMLP_WIDTH))
            mixed, mix_name, gate = (cat,), "ab_out", None
        else:
            (zqk, zv, zg), (wu_bf, wd_bf, wout_bf) = _ret_in_proj(
                h, w_ret_in[j], (ret_c, ret_s), w_ffn_up, w_ffn_down, w_ret_out, layer, j)
            y_p, s_p, y_s, s_s = _retention(zqk, zv, state_ret[j], lg)
            outs.setdefault('ret_p', []).append(s_p)
            outs.setdefault('ret_s', []).append(s_s)
            mixed, mix_name, gate = (y_p, y_s), "ret_out", zg
        x, h = _out_proj(mixed, wout_bf, x_parts, vec(norm_mix_post, layer), vec(norm_x_pre, layer), mix_name,
                         gate)
        x_parts, h_parts = (x,), (h,)
        g_post = vec(norm_x_post, layer)
        g_next = vec(norm_ffn_pre, layer)
        x, h = _mem_attention(h_parts, x_parts, wq_bf, wo_bf, mem_k_b, mem_v_b, ck_mem, cv_mem, layer,
                              g_post, g_next)
        g_next = vec(norm_mix_pre, layer + 1) if layer + 1 < DEPTH else None
        x, h = _ffn(h, wu_bf, wd_bf, x, vec(norm_ffn_post, layer), g_next)
        x_parts = (x,)

    y_prompt, y_sample = x, h
    stack = lambda name: jnp.stack(outs[name])
    mem_shape = (DEPTH, BATCH, MEM_LEN, MEM_HEADS, MEM_HEAD_DIM)
    return (y_prompt.reshape(BATCH, SEQ, D_MODEL), y_sample.reshape(DEC_BATCH, DEC_SEQ, D_MODEL),
            stack('swa_k_p'), stack('swa_v_p'), stack('swa_k_s'), stack('swa_v_s'), stack('gmlp_v_s'),
            stack('ret_p'), stack('ret_s'), mem_k.reshape(mem_shape), mem_v.reshape(mem_shape))
```

```python
import functools
import math

import jax
import jax.numpy as jnp
from jax import lax
from jax.experimental import pallas as pl
from jax.experimental.pallas import tpu as pltpu

F32 = jnp.float32
BF16 = jnp.bfloat16

D_MODEL = 2048
BATCH = 2
SEQ = 4096
DEPTH = 2
DEC_BATCH = 128
DEC_SEQ = 8
PAST_LEN = 8192

SWA_HEADS = 16
SWA_KV_HEADS = 4
SWA_HEAD_DIM = 64
SWA_GROUP = SWA_HEADS // SWA_KV_HEADS
WINDOW = 128
ROPE_THETA = 500000.0
ROPE_DIM = SWA_HEAD_DIM // 4
ROPE_HALF = ROPE_DIM // 2

GMLP_GROUPS = 4
GMLP_WIDTH = 1024
GMLP_GROUP_DIM = GMLP_WIDTH // GMLP_GROUPS
GMLP_CHUNK = 128

SWA_Q_W = SWA_HEADS * SWA_HEAD_DIM
SWA_KV_W = SWA_KV_HEADS * SWA_HEAD_DIM
AB_IN = SWA_Q_W + 2 * SWA_KV_W + 2 * GMLP_WIDTH
AB_OUT = SWA_Q_W + GMLP_WIDTH

RET_HEADS = 8
RET_KEY_DIM = D_MODEL // RET_HEADS
RET_VAL_DIM = 2 * RET_KEY_DIM
RET_THETA = 10000.0
RET_QK_W = RET_HEADS * RET_KEY_DIM
RET_V_W = RET_HEADS * RET_VAL_DIM
RET_IN = 2 * RET_QK_W + 2 * RET_V_W

MEM_LEN = 256
MEM_HEADS = 4
MEM_HEAD_DIM = 128
MEM_W = MEM_HEADS * MEM_HEAD_DIM

D_FF = 4 * D_MODEL
EPS = 1e-6
NEG_INF = -1e30

MP = BATCH * SEQ
MS = DEC_BATCH * DEC_SEQ
M = MP + MS
LANES = 128
MIB = 1024 * 1024

TM_IN = 1024
TAB_ROWS = SEQ + MS
TM_OUT = 512


def _cparams(sem, vmem_mib):
    return pltpu.CompilerParams(dimension_semantics=sem, vmem_limit_bytes=vmem_mib * MIB)


def _rms(x, g):
    return x * lax.rsqrt(jnp.mean(x * x, axis=-1, keepdims=True) + EPS) * g


def _residual(y, x, g_post, g_next):
    xn = x + _rms(y, g_post)
    return xn, _rms(xn, g_next).astype(BF16)


def _rope_table_block(block, sc_ref, s1_ref, s2_ref, rc_ref, rs_ref):
    tr = sc_ref.shape[0]
    r = lax.broadcasted_iota(jnp.int32, (tr, LANES), 0) + block * tr
    pos = jnp.where(r < SEQ, r, PAST_LEN + ((r - SEQ) & (DEC_SEQ - 1))).astype(F32)
    lane = lax.broadcasted_iota(jnp.int32, (tr, LANES), 1)
    d = lane & (SWA_HEAD_DIM - 1)
    inv = jnp.exp((d & (ROPE_HALF - 1)).astype(F32) * (-math.log(ROPE_THETA) / ROPE_HALF))
    ang = pos * inv
    c, s = jnp.cos(ang), jnp.sin(ang)
    sc_ref[...] = jnp.where(d < ROPE_DIM, c, 1.0)
    s1_ref[...] = jnp.where(d < ROPE_HALF, -s, 0.0)
    s2_ref[...] = jnp.where((d >= ROPE_HALF) & (d < ROPE_DIM), s, 0.0)
    inv_r = jnp.exp(lane.astype(F32) * (-math.log(RET_THETA) / LANES))
    ang_r = pos * inv_r
    rc_ref[...] = jnp.cos(ang_r)
    rs_ref[...] = jnp.sin(ang_r)


def _norm_kernel(xp_ref, xs_ref, g_ref, w_ref, h_ref, wbf_ref, *table_refs, n_table_blocks):
    def emit(src_ref):
        h_ref[...] = _rms(src_ref[...], g_ref[...]).astype(BF16)

    i = pl.program_id(0)
    is_prompt = i < MP // h_ref.shape[0]
    pl.when(is_prompt)(lambda: emit(xp_ref))
    pl.when(jnp.logical_not(is_prompt))(lambda: emit(xs_ref))
    wbf_ref[...] = w_ref[...].astype(BF16)
    pl.when(i < n_table_blocks)(lambda: _rope_table_block(i, *table_refs))


def _stacked_prenorm(x_prompt, x_sample, g, w, layer):
    tm = TM_OUT
    n_p = MP // tm
    w_src, w_in_spec, w_shape, w_out_spec = _row_cast_job(w, layer, n_p)
    tab_spec = pl.BlockSpec((TAB_ROWS // n_p, LANES), lambda i: (jnp.minimum(i, n_p - 1), 0))
    tab_shape = jax.ShapeDtypeStruct((TAB_ROWS, LANES), F32)
    outs = pl.pallas_call(
        functools.partial(_norm_kernel, n_table_blocks=n_p), grid=(M // tm,),
        in_specs=[pl.BlockSpec((tm, D_MODEL), lambda i: (jnp.minimum(i, n_p - 1), 0)),
                  pl.BlockSpec((tm, D_MODEL), lambda i: (jnp.maximum(i - n_p, 0), 0)),
                  pl.BlockSpec((1, D_MODEL), lambda i: (0, 0)), w_in_spec],
        out_specs=[pl.BlockSpec((tm, D_MODEL), lambda i: (i, 0)), w_out_spec] + [tab_spec] * 5,
        out_shape=[jax.ShapeDtypeStruct((M, D_MODEL), BF16), w_shape] + [tab_shape] * 5,
        compiler_params=_cparams(("arbitrary",), 40), name="prenorm",
    )(x_prompt, x_sample, g, w_src)
    return outs[0], outs[1], tuple(outs[2:])


W_SUB = 512


def _inproj_kernel(h_ref, *refs, n_sub, n_side, n_cast, epilogue):
    w_refs, refs = refs[:n_sub], refs[n_sub:]
    side, refs = refs[:n_side], refs[n_side:]
    cast_src, refs = refs[:n_cast], refs[n_cast:]
    wbf_ref = refs[-1]
    outs, cast_dst = refs[:len(refs) - 1 - n_cast], refs[len(refs) - 1 - n_cast:-1]

    @pl.when(pl.program_id(1) == 0)
    def _():
        for s, w_ref in enumerate(w_refs):
            wbf_ref[:, s * W_SUB:(s + 1) * W_SUB] = w_ref[...].astype(BF16)

    acc = jnp.dot(h_ref[...], wbf_ref[...], preferred_element_type=F32)
    epilogue(acc, side, outs)
    for src, dst in zip(cast_src, cast_dst):
        dst[...] = src[...].astype(BF16)


def _tab_index(j, i):
    return (jnp.where(i < MP // TM_IN, i % (SEQ // TM_IN), SEQ // TM_IN), 0)


CAST_STEPS = MP // TM_IN


def _cast_job(w, layer, nrb, ncb, block_of):
    rows, cols = w.shape[1:]
    blk = (rows // nrb, cols // ncb)
    index = lambda j, i: block_of(j, jnp.minimum(i, CAST_STEPS - 1))
    return (w, pl.BlockSpec((None,) + blk, lambda j, i: (layer,) + tuple(index(j, i))),
            jax.ShapeDtypeStruct((rows, cols), BF16), pl.BlockSpec(blk, index))


def _inproj(h, w, col0, ncols, tn, epilogue, side, side_specs, out_shapes, out_specs, name, vmem_mib,
            casts=()):
    tm = TM_IN
    k = h.shape[1]
    n_sub = tn // W_SUB
    w_specs = [pl.BlockSpec((k, W_SUB), lambda j, i, s=s: (0, col0 // W_SUB + j * n_sub + s))
               for s in range(n_sub)]
    as_list = lambda v: list(v) if isinstance(v, (list, tuple)) else [v]
    outs = pl.pallas_call(
        functools.partial(_inproj_kernel, n_sub=n_sub, n_side=len(side), n_cast=len(casts), epilogue=epilogue),
        grid=(ncols // tn, M // tm),
        in_specs=([pl.BlockSpec((tm, k), lambda j, i: (i, 0))] + w_specs + side_specs
                  + [c[1] for c in casts]),
        out_specs=as_list(out_specs) + [c[3] for c in casts],
        out_shape=as_list(out_shapes) + [c[2] for c in casts],
        scratch_shapes=[pltpu.VMEM((k, tn), BF16)],
        compiler_params=_cparams(("arbitrary", "arbitrary"), vmem_mib), name=name,
    )(h, *([w] * n_sub), *side, *[c[0] for c in casts])
    n_main = len(outs) - len(casts)
    main = outs[0] if n_main == 1 else tuple(outs[:n_main])
    return (main, tuple(outs[n_main:])) if casts else main


def _swa_rope(x, c, s1, s2):
    return x * c + pltpu.roll(x, LANES - ROPE_HALF, 1) * s1 + pltpu.roll(x, ROPE_HALF, 1) * s2


def _epi_swa_q(acc, side, outs):
    c, s1, s2 = (t[...] for t in side)
    (q_ref,) = outs
    for cc in range(acc.shape[1] // LANES):
        sl = slice(cc * LANES, (cc + 1) * LANES)
        q_ref[:, sl] = (_swa_rope(acc[:, sl], c, s1, s2) * (SWA_HEAD_DIM ** -0.5)).astype(BF16)


def _epi_swa_kv(acc, side, outs):
    c, s1, s2 = (t[...] for t in side)
    (kv_ref,) = outs
    for cc in range(SWA_KV_W // LANES):
        sl = slice(cc * LANES, (cc + 1) * LANES)
        kv_ref[:, sl] = _swa_rope(acc[:, sl], c, s1, s2)
    kv_ref[:, SWA_KV_W:] = acc[:, SWA_KV_W:]


def _epi_gelu(acc, side, outs):
    outs[0][...] = jax.nn.gelu(acc).astype(BF16)


def _epi_ret_qk(acc, side, outs):
    cos, sin = (t[...] for t in side)
    (z_ref,) = outs
    tn = acc.shape[1]
    scale = jnp.where(pl.program_id(0) < RET_QK_W // tn, 1.0, RET_KEY_DIM ** -0.5)
    for hh in range(tn // RET_KEY_DIM):
        c0 = hh * RET_KEY_DIM
        x1, x2 = acc[:, c0:c0 + LANES], acc[:, c0 + LANES:c0 + 2 * LANES]
        z_ref[hh, :, :LANES] = ((x1 * cos - x2 * sin) * scale).astype(BF16)
        z_ref[hh, :, LANES:] = ((x2 * cos + x1 * sin) * scale).astype(BF16)


def _per_head(fn):
    def epilogue(acc, side, outs):
        (z_ref,) = outs
        for hh in range(z_ref.shape[0]):
            z_ref[hh] = fn(acc[:, hh * RET_VAL_DIM:(hh + 1) * RET_VAL_DIM]).astype(BF16)
    return epilogue


_epi_cast = _per_head(lambda a: a)
_epi_silu = _per_head(lambda a: a / (1.0 + jnp.exp(-a)))


AB_TM = 256


def _ab_in_kernel(h_ref, w_ref, sc_ref, s1_ref, s2_ref, lng_ref, lnb_ref, *refs, n_cast):
    cast_src, (q_ref, kv_ref, u_ref, zv_ref, zvs_ref), cast_dst = refs[:n_cast], refs[n_cast:n_cast + 5], refs[n_cast + 5:]
    h = h_ref[...]
    tabs = (sc_ref, s1_ref, s2_ref)
    seg = lambda c0, n: jnp.dot(h, w_ref[:, c0:c0 + n], preferred_element_type=F32)
    _epi_swa_q(seg(0, SWA_Q_W), tabs, (q_ref,))
    _epi_swa_kv(seg(SWA_Q_W, 2 * SWA_KV_W), tabs, (kv_ref,))
    _epi_gelu(seg(SWA_Q_W + 2 * SWA_KV_W, GMLP_WIDTH), (), (u_ref,))
    a = jax.nn.gelu(seg(AB_IN - GMLP_WIDTH, GMLP_WIDTH))
    xc = a - jnp.mean(a, axis=-1, keepdims=True)
    y = xc * lax.rsqrt(jnp.mean(xc * xc, axis=-1, keepdims=True) + EPS) * lng_ref[...] + lnb_ref[...]
    zv_ref[...] = y.astype(BF16)

    @pl.when(pl.program_id(0) >= MP // AB_TM)
    def _():
        zvs_ref[...] = y

    for src, dst in zip(cast_src, cast_dst):
        dst[...] = src[...].astype(BF16)


def _row_cast_job(w, layer, n_blocks):
    rows, cols = w.shape[1:]
    blk = (rows // n_blocks, cols)
    index = lambda i: (jnp.minimum(i, n_blocks - 1), 0)
    return (w, pl.BlockSpec((None,) + blk, lambda i: (layer,) + index(i)),
            jax.ShapeDtypeStruct((rows, cols), BF16), pl.BlockSpec(blk, index))


def _ab_in_proj(h, w_in_bf, ln_g, ln_b, tabs, w_up, w_out, w_mem_q, w_mem_o, layer, j):
    tm = AB_TM
    n_p = MP // tm
    stacked = lambda w: w.reshape(1, w.shape[0] * w.shape[1], w.shape[2])
    casts = [_row_cast_job(w_up, layer, n_p), _row_cast_job(w_out, j, n_p),
             _row_cast_job(stacked(w_mem_q), 0, n_p), _row_cast_job(stacked(w_mem_o), 0, n_p)]
    tab_rows = SEQ // tm
    tab_spec = pl.BlockSpec((tm, LANES), lambda i: (jnp.where(i < n_p, i % tab_rows, tab_rows + i - n_p), 0))
    vec_spec = pl.BlockSpec((1, GMLP_WIDTH), lambda i: (0, 0))
    row_out = lambda n: pl.BlockSpec((tm, n), lambda i: (i, 0))
    outs = pl.pallas_call(
        functools.partial(_ab_in_kernel, n_cast=len(casts)), grid=(M // tm,),
        in_specs=([pl.BlockSpec((tm, D_MODEL), lambda i: (i, 0)),
                   pl.BlockSpec((D_MODEL, AB_IN), lambda i: (0, 0), pipeline_mode=pl.Buffered(1))]
                  + [tab_spec] * 3 + [vec_spec] * 2 + [c[1] for c in casts]),
        out_specs=[row_out(SWA_Q_W), row_out(2 * SWA_KV_W), row_out(GMLP_WIDTH), row_out(GMLP_WIDTH),
                   pl.BlockSpec((tm, GMLP_WIDTH), lambda i: (jnp.maximum(i - n_p, 0), 0))]
                  + [c[3] for c in casts],
        out_shape=[jax.ShapeDtypeStruct((M, SWA_Q_W), BF16), jax.ShapeDtypeStruct((M, 2 * SWA_KV_W), F32),
                   jax.ShapeDtypeStruct((M, GMLP_WIDTH), BF16), jax.ShapeDtypeStruct((M, GMLP_WIDTH), BF16),
                   jax.ShapeDtypeStruct((MS, GMLP_WIDTH), F32)] + [c[2] for c in casts],
        compiler_params=_cparams(("arbitrary",), 52), name="ab_in",
    )(h, w_in_bf, *tabs, ln_g, ln_b, *[c[0] for c in casts])
    return tuple(outs[:5]), tuple(outs[5:])


def _ret_in_proj(h, w_in, tabs, w_up, w_down, w_out, layer, j):
    rc, rs = tabs
    tn = 1024
    tab_spec = pl.BlockSpec((TM_IN, LANES), _tab_index)
    by_tile = lambda j, i: (j, i)
    head_major = jax.ShapeDtypeStruct((RET_HEADS, M, 2 * RET_KEY_DIM), BF16)
    seg = lambda col0, ncols, epi, side, specs, out_spec, name, w, lead: _inproj(
        h, w_in, col0, ncols, tn, epi, side, specs, head_major, out_spec, name, 56,
        casts=[_cast_job(w, lead, ncols // tn, CAST_STEPS, by_tile)])
    qk_heads = tn // RET_KEY_DIM
    q_tiles = RET_QK_W // tn
    qk_spec = pl.BlockSpec((qk_heads, TM_IN, RET_KEY_DIM), lambda j, i: (j % q_tiles, i, j // q_tiles))
    v_spec = pl.BlockSpec((tn // RET_VAL_DIM, TM_IN, RET_VAL_DIM), lambda j, i: (j, i, 0))
    zqk, (wu_bf,) = seg(0, 2 * RET_QK_W, _epi_ret_qk, [rc, rs], [tab_spec] * 2, qk_spec, "ret_in_qk",
                        w_up, layer)
    zv, (wd_bf,) = seg(2 * RET_QK_W, RET_V_W, _epi_cast, [], [], v_spec, "ret_in_v", w_down, layer)
    zg, (wout_bf,) = seg(2 * RET_QK_W + RET_V_W, RET_V_W, _epi_silu, [], [], v_spec, "ret_in_g", w_out, j)
    return (zqk, zv, zg), (wu_bf, wd_bf, wout_bf)


AB_ROWS = 512


def _sink_softmax(s, sink):
    m = jnp.maximum(jnp.max(s, axis=-1, keepdims=True), sink)
    p = jnp.exp(s - m)
    den = jnp.sum(p, axis=-1, keepdims=True) + jnp.exp(sink - m)
    return p, den


def _ab_prompt_kernel(sink_ref, q_ref, kv_ref, kvp_ref, u_ref, zv_ref, ws_ref, bs_ref, wsrc_ref, cat_ref,
                      wdst_ref, kd_ref, vd_ref):
    wdst_ref[...] = wsrc_ref[...].astype(BF16)
    r = pl.program_id(1)
    low = lax.broadcasted_iota(jnp.int32, (1, LANES), 1) < SWA_HEAD_DIM

    def both_halves(x, half):
        sw = pltpu.roll(x, SWA_HEAD_DIM, 1)
        return (jnp.where(low, x, sw) if half == 0 else jnp.where(low, sw, x)).astype(BF16)

    for kh in range(SWA_KV_HEADS):
        kcol = slice((kh // 2) * LANES, (kh // 2 + 1) * LANES)
        vcol = slice(SWA_KV_W + (kh // 2) * LANES, SWA_KV_W + (kh // 2 + 1) * LANES)
        kd_ref[kh, :WINDOW, :] = both_halves(kvp_ref[:, kcol], kh % 2)
        kd_ref[kh, WINDOW:, :] = both_halves(kv_ref[:, kcol], kh % 2)
        vd_ref[kh, :WINDOW, :] = both_halves(kvp_ref[:, vcol], kh % 2)
        vd_ref[kh, WINDOW:, :] = both_halves(kv_ref[:, vcol], kh % 2)

    t = lax.broadcasted_iota(jnp.int32, (WINDOW, 2 * WINDOW), 0)
    sk = lax.broadcasted_iota(jnp.int32, (WINDOW, 2 * WINDOW), 1)
    band = (sk <= t + WINDOW) & (sk > t)
    sink_slot = sk == t
    tril = (lax.broadcasted_iota(jnp.int32, (GMLP_CHUNK, GMLP_CHUNK), 0)
            >= lax.broadcasted_iota(jnp.int32, (GMLP_CHUNK, GMLP_CHUNK), 1))

    for sb in range(AB_ROWS // WINDOW):
        rs = slice(sb * WINDOW, (sb + 1) * WINDOW)
        ks = slice(sb * WINDOW, sb * WINDOW + 2 * WINDOW)
        valid = band & ((sk >= WINDOW) | (r * (AB_ROWS // WINDOW) + sb > 0))
        for kh in range(SWA_KV_HEADS):
            qm = []
            for g in range(SWA_GROUP):
                hd = kh * SWA_GROUP + g
                qb = q_ref[rs, (hd // 2) * LANES:(hd // 2 + 1) * LANES]
                qm.append(jnp.where(low if hd % 2 == 0 else jnp.logical_not(low), qb, jnp.zeros_like(qb)))
            s = lax.dot_general(jnp.concatenate(qm, axis=0), kd_ref[kh, ks, :], (((1,), (1,)), ((), ())),
                                preferred_element_type=F32)
            pv, den = [], []
            for g in range(SWA_GROUP):
                sg = s[g * WINDOW:(g + 1) * WINDOW]
                sg = jnp.where(valid, sg, jnp.where(sink_slot, sink_ref[kh * SWA_GROUP + g], NEG_INF))
                p = jnp.exp(sg - jnp.max(sg, axis=-1, keepdims=True))
                den.append(jnp.sum(p, axis=-1, keepdims=True))
                pv.append(jnp.where(sink_slot, 0.0, p).astype(BF16))
            o = jnp.dot(jnp.concatenate(pv, axis=0), vd_ref[kh, ks, :], preferred_element_type=F32)
            for pr in range(SWA_GROUP // 2):
                lo = o[(2 * pr) * WINDOW:(2 * pr + 1) * WINDOW] / den[2 * pr]
                hi = o[(2 * pr + 1) * WINDOW:(2 * pr + 2) * WINDOW] / den[2 * pr + 1]
                c0 = (kh * (SWA_GROUP // 2) + pr) * LANES
                cat_ref[rs, c0:c0 + LANES] = jnp.where(low, lo, hi).astype(BF16)
        zb = zv_ref[rs, :]
        ub = u_ref[rs, :]
        for g in range(GMLP_GROUPS):
            gs = slice(g * GMLP_GROUP_DIM, (g + 1) * GMLP_GROUP_DIM)
            w = jnp.where(tril, ws_ref[g], 0.0).astype(BF16)
            mixed = jnp.dot(w, zb[:, gs], preferred_element_type=F32) + bs_ref[g]
            cat_ref[rs, SWA_Q_W + g * GMLP_GROUP_DIM:SWA_Q_W + (g + 1) * GMLP_GROUP_DIM] = (
                ub[:, gs].astype(F32) * mixed).astype(BF16)


def _ab_mix_prompt(q, kv, u, zv, sinks, w_s, b_s, w_side, layer):
    nr = SEQ // AB_ROWS
    sub = AB_ROWS // WINDOW
    row = lambda b, r: (b * nr + r, 0)
    w_rows, w_cols = w_side.shape[1:]
    w_blk = (w_rows // (BATCH * nr), w_cols)
    return pl.pallas_call(
        _ab_prompt_kernel, grid=(BATCH, nr),
        in_specs=[pl.BlockSpec(memory_space=pltpu.SMEM),
                  pl.BlockSpec((AB_ROWS, SWA_Q_W), row),
                  pl.BlockSpec((AB_ROWS, 2 * SWA_KV_W), row),
                  pl.BlockSpec((WINDOW, 2 * SWA_KV_W), lambda b, r: (jnp.maximum((b * nr + r) * sub - 1, 0), 0)),
                  pl.BlockSpec((AB_ROWS, GMLP_WIDTH), row),
                  pl.BlockSpec((AB_ROWS, GMLP_WIDTH), row),
                  pl.BlockSpec((GMLP_GROUPS, GMLP_CHUNK, GMLP_CHUNK), lambda b, r: (0, 0, 0)),
                  pl.BlockSpec((GMLP_GROUPS, GMLP_CHUNK, 1), lambda b, r: (0, 0, 0)),
                  pl.BlockSpec((None,) + w_blk, lambda b, r: (layer, b * nr + r, 0))],
        out_specs=[pl.BlockSpec((AB_ROWS, AB_OUT), row), pl.BlockSpec(w_blk, row)],
        out_shape=[jax.ShapeDtypeStruct((M, AB_OUT), BF16), jax.ShapeDtypeStruct((w_rows, w_cols), BF16)],
        scratch_shapes=[pltpu.VMEM((SWA_KV_HEADS, WINDOW + AB_ROWS, LANES), BF16)] * 2,
        compiler_params=_cparams(("arbitrary", "arbitrary"), 48), name="ab_mix_prompt",
    )(sinks, q, kv, kv, u, zv, w_s, b_s.reshape(GMLP_GROUPS, GMLP_CHUNK, 1), w_side)


AB_NB = 16
AB_KEYS = 2 * WINDOW


def _ab_sample_kernel(sink_ref, q_ref, kv_ref, u_ref, zv_ref, ck_ref, cv_ref, ws_ref, bs_ref, cat_in_ref,
                      cat_ref, nk_ref, nv_ref):
    del cat_in_ref
    nb, t = AB_NB, DEC_SEQ
    kv3 = kv_ref[...].reshape(nb, t, 2 * SWA_KV_W)
    kn, vn = kv3[:, :, :SWA_KV_W], kv3[:, :, SWA_KV_W:]
    ck, cv = ck_ref[...], cv_ref[...]
    nk_ref[:, :WINDOW - t, :] = ck[:, t:, :]
    nk_ref[:, WINDOW - t:, :] = kn
    nv_ref[:, :WINDOW - t, :] = cv[:, t:, :]
    nv_ref[:, WINDOW - t:, :] = vn
    pad = jnp.zeros((nb, AB_KEYS - WINDOW - t, SWA_KV_W), F32)
    kall = jnp.concatenate([ck, kn, pad], axis=1).astype(BF16)
    vall = jnp.concatenate([cv, vn, pad], axis=1).astype(BF16)

    rows = SWA_GROUP * t
    tq = lax.broadcasted_iota(jnp.int32, (rows, AB_KEYS), 0) & (t - 1)
    sk = lax.broadcasted_iota(jnp.int32, (rows, AB_KEYS), 1)
    valid = ((sk < WINDOW) & (sk > tq)) | ((sk >= WINDOW) & (sk - WINDOW <= tq))

    q3 = q_ref[...].astype(F32).reshape(nb, t, SWA_Q_W)
    pieces = []
    for kh in range(SWA_KV_HEADS):
        heads = [kh * SWA_GROUP + g for g in range(SWA_GROUP)]
        qs = jnp.concatenate([q3[:, :, hd * SWA_HEAD_DIM:(hd + 1) * SWA_HEAD_DIM] for hd in heads],
                             axis=1).astype(BF16)
        cs = slice(kh * SWA_HEAD_DIM, (kh + 1) * SWA_HEAD_DIM)
        s = jnp.einsum('nqd,nkd->nqk', qs, kall[:, :, cs], preferred_element_type=F32)
        s = jnp.where(valid[None], s, NEG_INF)
        sink = jnp.concatenate([jnp.full((t, 1), sink_ref[hd], F32) for hd in heads], axis=0)[None]
        p, den = _sink_softmax(s, sink)
        o = jnp.einsum('nqk,nkd->nqd', p.astype(BF16), vall[:, :, cs], preferred_element_type=F32) / den
        pieces += [o[:, g * t:(g + 1) * t, :] for g in range(SWA_GROUP)]
    attn = jnp.concatenate(pieces, axis=2).reshape(nb * t, SWA_Q_W)
    cat_ref[:, :SWA_Q_W] = attn.astype(BF16)

    zv3 = zv_ref[...].reshape(nb, t, GMLP_WIDTH)
    u3 = u_ref[...].astype(F32).reshape(nb, t, GMLP_WIDTH)
    irow = lax.broadcasted_iota(jnp.int32, (t, GMLP_GROUP_DIM), 0)
    for g in range(GMLP_GROUPS):
        gs = slice(g * GMLP_GROUP_DIM, (g + 1) * GMLP_GROUP_DIM)
        zg = zv3[:, :, gs]
        mixed = jnp.zeros((nb, t, GMLP_GROUP_DIM), F32) + bs_ref[g][None]
        for j in range(t):
            wcol = jnp.where(irow >= j, ws_ref[g, j], 0.0)
            mixed = mixed + zg[:, j:j + 1, :] * wcol[None]
        gate = (u3[:, :, gs] * mixed).reshape(nb * t, GMLP_GROUP_DIM)
        cat_ref[:, SWA_Q_W + g * GMLP_GROUP_DIM:SWA_Q_W + (g + 1) * GMLP_GROUP_DIM] = gate.astype(BF16)


def _ab_mix_sample(cat, q, kv, u, zv_s, cache_k, cache_v, sinks, w_s, b_s):
    nb, t = AB_NB, DEC_SEQ
    rows = nb * t
    off = MP // rows
    srow = lambda i: (off + i, 0)
    wt = jnp.broadcast_to(jnp.swapaxes(w_s[:, :t, :t], 1, 2)[..., None], (GMLP_GROUPS, t, t, GMLP_GROUP_DIM))
    bt = jnp.broadcast_to(b_s[:, :t, None], (GMLP_GROUPS, t, GMLP_GROUP_DIM))
    cspec = pl.BlockSpec((nb, WINDOW, SWA_KV_W), lambda i: (i, 0, 0))
    cshape = jax.ShapeDtypeStruct((DEC_BATCH, WINDOW, SWA_KV_W), F32)
    return pl.pallas_call(
        _ab_sample_kernel, grid=(DEC_BATCH // nb,),
        in_specs=[pl.BlockSpec(memory_space=pltpu.SMEM),
                  pl.BlockSpec((rows, SWA_Q_W), srow),
                  pl.BlockSpec((rows, 2 * SWA_KV_W), srow),
                  pl.BlockSpec((rows, GMLP_WIDTH), srow),
                  pl.BlockSpec((rows, GMLP_WIDTH), lambda i: (i, 0)),
                  cspec, cspec,
                  pl.BlockSpec((GMLP_GROUPS, t, t, GMLP_GROUP_DIM), lambda i: (0, 0, 0, 0)),
                  pl.BlockSpec((GMLP_GROUPS, t, GMLP_GROUP_DIM), lambda i: (0, 0, 0)),
                  pl.BlockSpec(memory_space=pl.ANY)],
        out_specs=[pl.BlockSpec((rows, AB_OUT), srow), cspec, cspec],
        out_shape=[jax.ShapeDtypeStruct((M, AB_OUT), BF16), cshape, cshape],
        input_output_aliases={9: 0},
        compiler_params=_cparams(("arbitrary",), 48), name="ab_mix_sample",
    )(sinks, q, kv, u, zv_s, cache_k, cache_v, wt, bt, cat)


RET_ROWS = 1024
RET_BLOCK = 256


def _ret_prompt_part(lg, first_run, qk_ref, v_ref, s_ref):
    L = RET_BLOCK

    @pl.when(first_run)
    def _():
        s_ref[...] = jnp.zeros_like(s_ref)

    diff = (lax.broadcasted_iota(jnp.int32, (L, L), 0) - lax.broadcasted_iota(jnp.int32, (L, L), 1)).astype(F32)
    dmat = jnp.where(diff >= 0, jnp.exp(lg * jnp.maximum(diff, 0.0)), 0.0)
    idx = lax.broadcasted_iota(jnp.int32, (L, 1), 0).astype(F32)
    q_dec = jnp.exp(lg * (idx + 1.0))
    k_dec = jnp.exp(lg * (L - 1.0 - idx))
    s_dec = jnp.exp(lg * jnp.full((1, 1), float(L), F32))

    ys = []
    for ci in range(RET_ROWS // L):
        rs = slice(ci * L, (ci + 1) * L)
        q, k, v = qk_ref[rs, :RET_KEY_DIM], qk_ref[rs, RET_KEY_DIM:], v_ref[rs, :]
        att = lax.dot_general(q, k, (((1,), (1,)), ((), ())), preferred_element_type=F32) * dmat
        s0 = s_ref[...]
        o = (jnp.dot(att.astype(BF16), v, preferred_element_type=F32)
             + jnp.dot(q, s0.astype(BF16), preferred_element_type=F32) * q_dec)
        kd = (k.astype(F32) * k_dec).astype(BF16)
        s_ref[...] = s0 * s_dec + lax.dot_general(kd, v, (((0,), (0,)), ((), ())), preferred_element_type=F32)
        on = o * lax.rsqrt(jnp.mean(o * o, axis=-1, keepdims=True) + EPS)
        ys.append(on.astype(BF16))
    return jnp.concatenate(ys, axis=0)


RET_NB = DEC_BATCH // (BATCH * (SEQ // RET_ROWS))


def _ret_sample_part(lg, qk_ref, v_ref, s0_ref, s1_ref):
    nb, t = RET_NB, DEC_SEQ
    rows = nb * t
    q, k, v = qk_ref[:, :RET_KEY_DIM], qk_ref[:, RET_KEY_DIM:], v_ref[...]
    ri = lax.broadcasted_iota(jnp.int32, (rows, rows), 0)
    ci = lax.broadcasted_iota(jnp.int32, (rows, rows), 1)
    d = ((ri & (t - 1)) - (ci & (t - 1))).astype(F32)
    same = (ri // t) == (ci // t)
    dmat = jnp.where(same & (d >= 0), jnp.exp(lg * jnp.maximum(d, 0.0)), 0.0)
    att = lax.dot_general(q, k, (((1,), (1,)), ((), ())), preferred_element_type=F32) * dmat
    o_intra = jnp.dot(att.astype(BF16), v, preferred_element_type=F32)

    rcol = lax.broadcasted_iota(jnp.int32, (rows, 1), 0)
    tt = (rcol & (t - 1)).astype(F32)
    q_dec = jnp.exp(lg * (tt + 1.0))
    k_dec = jnp.exp(lg * (t - 1.0 - tt))
    s_dec = jnp.exp(lg * jnp.full((1, 1), float(t), F32))
    kd = (k.astype(F32) * k_dec).astype(BF16)
    qf = q.astype(F32)
    vf = v.astype(F32)
    o_rows = []
    for n in range(nb):
        s0 = s0_ref[n]
        qn = qf[n * t:(n + 1) * t].astype(BF16)
        o_rows.append(jnp.dot(qn, s0.astype(BF16), preferred_element_type=F32))
        vn = jnp.where(rcol // t == n, vf, 0.0).astype(BF16)
        s1_ref[n] = s0 * s_dec + lax.dot_general(kd, vn, (((0,), (0,)), ((), ())), preferred_element_type=F32)
    o = o_intra + jnp.concatenate(o_rows, axis=0) * q_dec
    on = o * lax.rsqrt(jnp.mean(o * o, axis=-1, keepdims=True) + EPS)
    return on.astype(BF16)


S0_SLOTS = 3


def _retention_kernel(lg_ref, qk_ref, v_ref, sqk_ref, sv_ref, state_ref,
                      yp_ref, sfin_ref, ys_ref, s1_ref, s_ref, s0_buf, s0_sem):
    nh, nc = RET_HEADS, SEQ // RET_ROWS
    b, h, c = pl.program_id(0), pl.program_id(1), pl.program_id(2)
    step = (b * nh + h) * nc + c
    n_steps = BATCH * nh * nc

    def s0_copy(s):
        s_c, s_bh = lax.rem(s, nc), s // nc
        s_h, s_b = lax.rem(s_bh, nh), s_bh // nh
        slot = lax.rem(s, S0_SLOTS)
        return pltpu.make_async_copy(
            state_ref.at[pl.ds((s_b * nc + s_c) * RET_NB, RET_NB), s_h], s0_buf.at[slot], s0_sem.at[slot])

    @pl.when(step == 0)
    def _():
        s0_copy(step).start()
        s0_copy(step + 1).start()

    @pl.when(step + 2 < n_steps)
    def _():
        s0_copy(step + 2).start()

    lg = lg_ref[h]
    yp_ref[...] = _ret_prompt_part(lg, c == 0, qk_ref, v_ref, s_ref)

    @pl.when(c == nc - 1)
    def _():
        sfin_ref[...] = s_ref[...]

    s0_copy(step).wait()
    ys_ref[...] = _ret_sample_part(lg, sqk_ref, sv_ref, s0_buf.at[lax.rem(step, S0_SLOTS)], s1_ref)


def _retention(zqk, zv, state, lg):
    nc = SEQ // RET_ROWS
    srows = RET_NB * DEC_SEQ
    soff = MP // srows
    prow = lambda b, c: b * nc + c
    width = 2 * RET_KEY_DIM
    head_specs = lambda rows, row: [pl.BlockSpec((None, rows, width), lambda b, h, c: (h, row(b, c), 0))] * 2
    sspec = pl.BlockSpec((RET_NB, None, RET_KEY_DIM, RET_VAL_DIM), lambda b, h, c: (prow(b, c), h, 0, 0))
    return pl.pallas_call(
        _retention_kernel, grid=(BATCH, RET_HEADS, nc),
        in_specs=([pl.BlockSpec(memory_space=pltpu.SMEM)] + head_specs(RET_ROWS, prow)
                  + head_specs(srows, lambda b, c: soff + prow(b, c)) + [pl.BlockSpec(memory_space=pl.ANY)]),
        out_specs=[pl.BlockSpec((RET_ROWS, RET_VAL_DIM), lambda b, h, c: (prow(b, c), h)),
                   pl.BlockSpec((None, None, RET_KEY_DIM, RET_VAL_DIM), lambda b, h, c: (b, h, 0, 0)),
                   pl.BlockSpec((srows, RET_VAL_DIM), lambda b, h, c: (prow(b, c), h)),
                   sspec],
        out_shape=[jax.ShapeDtypeStruct((MP, RET_V_W), BF16),
                   jax.ShapeDtypeStruct((BATCH, RET_HEADS, RET_KEY_DIM, RET_VAL_DIM), F32),
                   jax.ShapeDtypeStruct((MS, RET_V_W), BF16),
                   jax.ShapeDtypeStruct((DEC_BATCH, RET_HEADS, RET_KEY_DIM, RET_VAL_DIM), F32)],
        scratch_shapes=[pltpu.VMEM((RET_KEY_DIM, RET_VAL_DIM), F32),
                        pltpu.VMEM((S0_SLOTS, RET_NB, RET_KEY_DIM, RET_VAL_DIM), F32),
                        pltpu.SemaphoreType.DMA((S0_SLOTS,))],
        compiler_params=_cparams(("arbitrary", "arbitrary", "arbitrary"), 56), name="retention",
    )(lg, zqk, zv, zqk, zv, state)


EPI_ROWS = 128


def _outproj_kernel(ap_ref, as_ref, w_ref, xp_ref, xs_ref, gp_ref, gn_ref, *refs):
    gate_ref, (xn_ref, hn_ref) = (refs[0] if len(refs) == 3 else None), refs[-2:]
    tm = xn_ref.shape[0]
    is_prompt = pl.program_id(0) < MP // tm
    for r0 in range(0, tm, EPI_ROWS):
        rs = slice(r0, r0 + EPI_ROWS)
        a = jnp.where(is_prompt, ap_ref[rs, :], as_ref[rs, :])
        if gate_ref is not None:
            width = gate_ref.shape[2]
            a = jnp.concatenate(
                [(a[:, hh * width:(hh + 1) * width].astype(F32) * gate_ref[hh, rs, :].astype(F32)).astype(BF16)
                 for hh in range(gate_ref.shape[0])], axis=1)
        x = jnp.where(is_prompt, xp_ref[rs, :], xs_ref[rs, :])
        y = jnp.dot(a, w_ref[...], preferred_element_type=F32)
        xn, hn = _residual(y, x, gp_ref[...], gn_ref[...])
        xn_ref[rs, :] = xn
        hn_ref[rs, :] = hn


def _out_proj(a_parts, w_bf, x_parts, g_post, g_next, name, gate=None):
    kdim = w_bf.shape[0]
    tm = TM_OUT * D_MODEL // kdim
    np_t = MP // tm

    def two_source(parts, width):
        sample_t0 = np_t if len(parts) == 1 else 0
        return [pl.BlockSpec((tm, width), lambda i: (jnp.minimum(i, np_t - 1), 0)),
                pl.BlockSpec((tm, width), lambda i: (sample_t0 + jnp.maximum(i - np_t, 0), 0))]

    row = pl.BlockSpec((tm, D_MODEL), lambda i: (i, 0))
    vec = pl.BlockSpec((1, D_MODEL), lambda i: (0, 0))
    gate_args, gate_specs = [], []
    if gate is not None:
        gate_args = [gate]
        gate_specs = [pl.BlockSpec((gate.shape[0], tm, gate.shape[2]), lambda i: (0, i, 0))]
    return pl.pallas_call(
        _outproj_kernel, grid=(M // tm,),
        in_specs=(two_source(a_parts, kdim)
                  + [pl.BlockSpec((kdim, D_MODEL), lambda i: (0, 0), pipeline_mode=pl.Buffered(1))]
                  + two_source(x_parts, D_MODEL) + [vec, vec] + gate_specs),
        out_specs=[row, row],
        out_shape=[jax.ShapeDtypeStruct((M, D_MODEL), F32), jax.ShapeDtypeStruct((M, D_MODEL), BF16)],
        compiler_params=_cparams(("arbitrary",), 56), name=name,
    )(a_parts[0], a_parts[-1], w_bf, x_parts[0], x_parts[-1], g_post, g_next, *gate_args)


def _ffn_up_down(h, wu_ref, wd_ref):
    a = jnp.maximum(jnp.dot(h, wu_ref[...], preferred_element_type=F32), 0.0)
    return jnp.dot((a * a).astype(BF16), wd_ref[...], preferred_element_type=F32)


FFN_EPI_ROWS = 256


def _ffn_kernel(h_ref, wu_ref, wd_ref, x_ref, gp_ref, gn_ref, xn_ref, hn_ref):
    c = pl.program_id(1)
    first, last = c == 0, c == pl.num_programs(1) - 1

    @pl.when(first)
    def _():
        xn_ref[...] = _ffn_up_down(h_ref[...], wu_ref, wd_ref)

    @pl.when(jnp.logical_not(first | last))
    def _():
        xn_ref[...] += _ffn_up_down(h_ref[...], wu_ref, wd_ref)

    @pl.when(last)
    def _():
        for r0 in range(0, xn_ref.shape[0], FFN_EPI_ROWS):
            rs = slice(r0, r0 + FFN_EPI_ROWS)
            y = xn_ref[rs, :] + _ffn_up_down(h_ref[rs, :], wu_ref, wd_ref)
            xn, hn = _residual(y, x_ref[rs, :], gp_ref[...], gn_ref[...])
            xn_ref[rs, :] = xn
            hn_ref[rs, :] = hn


def _ffn_last_kernel(h_ref, wu_ref, wd_ref, x_ref, gp_ref, yp_ref, ys_ref, acc_ref):
    i, c = pl.program_id(0), pl.program_id(1)
    first, last = c == 0, c == pl.num_programs(1) - 1
    is_prompt = i < MP // acc_ref.shape[0]

    @pl.when(first)
    def _():
        acc_ref[...] = _ffn_up_down(h_ref[...], wu_ref, wd_ref)

    @pl.when(jnp.logical_not(first | last))
    def _():
        acc_ref[...] += _ffn_up_down(h_ref[...], wu_ref, wd_ref)

    def finish(y_ref):
        for r0 in range(0, acc_ref.shape[0], FFN_EPI_ROWS):
            rs = slice(r0, r0 + FFN_EPI_ROWS)
            y = acc_ref[rs, :] + _ffn_up_down(h_ref[rs, :], wu_ref, wd_ref)
            y_ref[rs, :] = x_ref[rs, :] + _rms(y, gp_ref[...])

    pl.when(last & is_prompt)(lambda: finish(yp_ref))
    pl.when(last & jnp.logical_not(is_prompt))(lambda: finish(ys_ref))


def _ffn(h, wu_bf, wd_bf, x, g_post, g_next):
    tm, tc = TM_OUT, 1024
    row = pl.BlockSpec((tm, D_MODEL), lambda i, c: (i, 0))
    vec = pl.BlockSpec((1, D_MODEL), lambda i, c: (0, 0))
    w_specs = [pl.BlockSpec((D_MODEL, tc), lambda i, c: (0, c)), pl.BlockSpec((tc, D_MODEL), lambda i, c: (c, 0))]
    grid = (M // tm, D_FF // tc)
    if g_next is not None:
        return pl.pallas_call(
            _ffn_kernel, grid=grid, in_specs=[row] + w_specs + [row, vec, vec], out_specs=[row, row],
            out_shape=[jax.ShapeDtypeStruct((M, D_MODEL), F32), jax.ShapeDtypeStruct((M, D_MODEL), BF16)],
            compiler_params=_cparams(("arbitrary", "arbitrary"), 56), name="ffn",
        )(h, wu_bf, wd_bf, x, g_post, g_next)
    np_t = MP // tm
    return pl.pallas_call(
        _ffn_last_kernel, grid=grid, in_specs=[row] + w_specs + [row, vec],
        out_specs=[pl.BlockSpec((tm, D_MODEL), lambda i, c: (jnp.minimum(i, np_t - 1), 0)),
                   pl.BlockSpec((tm, D_MODEL), lambda i, c: (jnp.maximum(i - np_t, 0), 0))],
        out_shape=[jax.ShapeDtypeStruct((MP, D_MODEL), F32), jax.ShapeDtypeStruct((MS, D_MODEL), F32)],
        scratch_shapes=[pltpu.VMEM((tm, D_MODEL), F32)],
        compiler_params=_cparams(("arbitrary", "arbitrary"), 56), name="ffn_last",
    )(h, wu_bf, wd_bf, x, g_post)


def _mem_kv_kernel(m_ref, g_ref, wk_ref, wv_ref, k_ref, v_ref):
    m = _rms(m_ref[...], g_ref[0]).astype(BF16)
    k_ref[0] = jnp.dot(m, wk_ref[0].astype(BF16), preferred_element_type=F32)
    v_ref[0] = jnp.dot(m, wv_ref[0].astype(BF16), preferred_element_type=F32)


def _mem_kv(mem, g, w_k, w_v):
    rows = BATCH * MEM_LEN
    wspec = pl.BlockSpec((1, D_MODEL, MEM_W), lambda l: (l, 0, 0))
    ospec = pl.BlockSpec((1, rows, MEM_W), lambda l: (l, 0, 0))
    oshape = jax.ShapeDtypeStruct((DEPTH, rows, MEM_W), F32)
    return pl.pallas_call(
        _mem_kv_kernel, grid=(DEPTH,),
        in_specs=[pl.BlockSpec((rows, D_MODEL), lambda l: (0, 0)),
                  pl.BlockSpec((1, 1, D_MODEL), lambda l: (l, 0, 0)), wspec, wspec],
        out_specs=[ospec, ospec], out_shape=[oshape, oshape],
        compiler_params=_cparams(("arbitrary",), 48), name="mem_kv",
    )(mem.reshape(rows, D_MODEL), g.reshape(DEPTH, 1, D_MODEL), w_k, w_v)


def _mem_attend_prompt(q, mk_ref, mv_ref):
    q = q.astype(BF16)
    mk, mv = mk_ref[...].astype(BF16), mv_ref[...].astype(BF16)
    outs = []
    for hh in range(MEM_HEADS):
        cs = slice(hh * MEM_HEAD_DIM, (hh + 1) * MEM_HEAD_DIM)
        s = lax.dot_general(q[:, cs], mk[:, cs], (((1,), (1,)), ((), ())),
                            preferred_element_type=F32) * (MEM_HEAD_DIM ** -0.5)
        p = jnp.exp(s - jnp.max(s, axis=-1, keepdims=True))
        den = jnp.sum(p, axis=-1, keepdims=True)
        outs.append(jnp.dot(p.astype(BF16), mv[:, cs], preferred_element_type=F32) / den)
    return jnp.concatenate(outs, axis=1)


def _mem_attend_sample(q, ck_ref, cv_ref):
    nb, t = ck_ref.shape[0], DEC_SEQ
    qs = jnp.concatenate([q[:, hh * MEM_HEAD_DIM:(hh + 1) * MEM_HEAD_DIM].reshape(nb, t, MEM_HEAD_DIM)
                          for hh in range(MEM_HEADS)], axis=1).astype(BF16)
    s = jnp.einsum('nqd,nkd->nqk', qs, ck_ref[...].astype(BF16),
                   preferred_element_type=F32) * (MEM_HEAD_DIM ** -0.5)
    shape = (MEM_HEADS * t, MEM_LEN * MEM_HEADS)
    own = ((lax.broadcasted_iota(jnp.int32, shape, 1) & (MEM_HEADS - 1))
           == lax.broadcasted_iota(jnp.int32, shape, 0) // t)
    s = jnp.where(own[None], s, NEG_INF)
    p = jnp.exp(s - jnp.max(s, axis=-1, keepdims=True))
    den = jnp.sum(p, axis=-1, keepdims=True)
    o3 = jnp.einsum('nqk,nkd->nqd', p.astype(BF16), cv_ref[...].astype(BF16),
                    preferred_element_type=F32) / den
    return jnp.concatenate([o3[:, hh * t:(hh + 1) * t, :].reshape(nb * t, MEM_HEAD_DIM)
                            for hh in range(MEM_HEADS)], axis=1)


def _mem_kernel(hp_ref, hs_ref, wq_ref, wo_ref, mk_ref, mv_ref, ck_ref, cv_ref, xp_ref, xs_ref, gp_ref, gn_ref,
                xn_ref, hn_ref, os_ref):
    i = pl.program_id(0)
    tm = xn_ref.shape[0]
    n_prompt = MP // tm
    srows = hs_ref.shape[0]
    project = lambda h_ref: jnp.dot(h_ref[...], wq_ref[...], preferred_element_type=F32)

    def finish(o, x_ref):
        for r0 in range(0, tm, EPI_ROWS):
            rs = slice(r0, r0 + EPI_ROWS)
            y = jnp.dot(o[rs].astype(BF16), wo_ref[...], preferred_element_type=F32)
            xn, hn = _residual(y, x_ref[rs, :], gp_ref[...], gn_ref[...])
            xn_ref[rs, :] = xn
            hn_ref[rs, :] = hn

    @pl.when(i < n_prompt)
    def _():
        os_ref[pl.ds(pl.multiple_of(i * srows, srows), srows), :] = _mem_attend_sample(
            project(hs_ref), ck_ref, cv_ref)
        finish(_mem_attend_prompt(project(hp_ref), mk_ref, mv_ref), xp_ref)

    @pl.when(i >= n_prompt)
    def _():
        finish(os_ref[pl.ds(pl.multiple_of((i - n_prompt) * tm, tm), tm), :], xs_ref)


MEM_TM = 256
MEM_NB = DEC_BATCH // (MP // MEM_TM)


def _mem_attention(h_parts, x_parts, wq_bf, wo_bf, mk, mv, cache_k, cache_v, layer, g_post, g_next):
    tm, nb = MEM_TM, MEM_NB
    srows = nb * DEC_SEQ
    sample_row0 = MP if len(h_parts) == 1 else 0
    per_b = SEQ // tm
    n_p = MP // tm
    pidx = lambda i: jnp.minimum(i, n_p - 1)
    row = pl.BlockSpec((tm, D_MODEL), lambda i: (i, 0))
    vec = pl.BlockSpec((1, D_MODEL), lambda i: (0, 0))
    kvspec = pl.BlockSpec((None, MEM_LEN, MEM_W), lambda i: (layer * BATCH + pidx(i) // per_b, 0, 0))
    cspec = pl.BlockSpec((None, nb, MEM_LEN * MEM_HEADS, MEM_HEAD_DIM), lambda i: (layer, pidx(i), 0, 0))
    resident = lambda shape: pl.BlockSpec(shape, lambda i: (layer, 0), pipeline_mode=pl.Buffered(1))
    return pl.pallas_call(
        _mem_kernel, grid=(M // tm,),
        in_specs=[pl.BlockSpec((tm, D_MODEL), lambda i: (pidx(i), 0)),
                  pl.BlockSpec((srows, D_MODEL), lambda i: (sample_row0 // srows + pidx(i), 0)),
                  resident((D_MODEL, MEM_W)), resident((MEM_W, D_MODEL)), kvspec, kvspec,
                  cspec, cspec,
                  pl.BlockSpec((tm, D_MODEL), lambda i: (pidx(i), 0)),
                  pl.BlockSpec((tm, D_MODEL), lambda i: (sample_row0 // tm + jnp.maximum(i - n_p, 0), 0)),
                  vec, vec],
        out_specs=[row, row],
        out_shape=[jax.ShapeDtypeStruct((M, D_MODEL), F32), jax.ShapeDtypeStruct((M, D_MODEL), BF16)],
        scratch_shapes=[pltpu.VMEM((MS, MEM_W), F32)],
        compiler_params=_cparams(("arbitrary",), 48), name="mem_attention",
    )(h_parts[0], h_parts[-1], wq_bf, wo_bf, mk, mv, cache_k, cache_v, x_parts[0], x_parts[-1], g_post, g_next)


def kernel(x_prompt, x_sample, cache_swa_k, cache_swa_v, state_ret, cache_mem_k, cache_mem_v, mem_prompt, norm_mix_pre, norm_mix_post, norm_mem, norm_x_pre, norm_x_post, norm_ffn_pre, norm_ffn_post, w_ab_in, w_ab_out, swa_sinks, gmlp_ln_g, gmlp_ln_b, gmlp_w_s, gmlp_b_s, w_ret_in, w_ret_out, w_mem_q, w_mem_k, w_mem_v, w_mem_o, w_ffn_up, w_ffn_down):
    vec = lambda g, layer: g[layer].reshape(1, D_MODEL)
    mem_k, mem_v = _mem_kv(mem_prompt, norm_mem, w_mem_k, w_mem_v)
    mem_k_b = mem_k.reshape(DEPTH * BATCH, MEM_LEN, MEM_W)
    mem_v_b = mem_v.reshape(DEPTH * BATCH, MEM_LEN, MEM_W)
    ck_mem = cache_mem_k.reshape(DEPTH, DEC_BATCH, MEM_LEN * MEM_HEADS, MEM_HEAD_DIM)
    cv_mem = cache_mem_v.reshape(DEPTH, DEC_BATCH, MEM_LEN * MEM_HEADS, MEM_HEAD_DIM)
    lg = jnp.log1p(-jnp.exp2(-5.0 - jnp.arange(RET_HEADS, dtype=F32)))

    x_parts = (x_prompt.reshape(MP, D_MODEL), x_sample.reshape(MS, D_MODEL))
    h, w_ab_in_bf, (swa_c, swa_s1, swa_s2, ret_c, ret_s) = _stacked_prenorm(
        *x_parts, vec(norm_mix_pre, 0), w_ab_in, 0)
    outs = {}
    for layer in range(DEPTH):
        j = layer // 2
        if layer % 2 == 0:
            (q, kv, u, zv, zv_s), (wu_bf, wout_bf, wq_bf, wo_bf) = _ab_in_proj(
                h, w_ab_in_bf, gmlp_ln_g[j].reshape(1, GMLP_WIDTH), gmlp_ln_b[j].reshape(1, GMLP_WIDTH),
                (swa_c, swa_s1, swa_s2), w_ffn_up, w_ab_out, w_mem_q, w_mem_o, layer, j)
            cat, wd_bf = _ab_mix_prompt(q, kv, u, zv, swa_sinks[j], gmlp_w_s[j], gmlp_b_s[j], w_ffn_down, layer)
            cat, nk_s, nv_s = _ab_mix_sample(
                cat, q, kv, u, zv_s, cache_swa_k[j].reshape(DEC_BATCH, WINDOW, SWA_KV_W),
                cache_swa_v[j].reshape(DEC_BATCH, WINDOW, SWA_KV_W), swa_sinks[j], gmlp_w_s[j], gmlp_b_s[j])
            kv_p = jnp.stack([kv[(b + 1) * SEQ - WINDOW:(b + 1) * SEQ] for b in range(BATCH)])
            kv_p = kv_p.reshape(BATCH, WINDOW, 2, SWA_KV_HEADS, SWA_HEAD_DIM)
            outs.setdefault('swa_k_p', []).append(kv_p[:, :, 0])
            outs.setdefault('swa_v_p', []).append(kv_p[:, :, 1])
            outs.setdefault('swa_k_s', []).append(nk_s.reshape(DEC_BATCH, WINDOW, SWA_KV_HEADS, SWA_HEAD_DIM))
            outs.setdefault('swa_v_s', []).append(nv_s.reshape(DEC_BATCH, WINDOW, SWA_KV_HEADS, SWA_HEAD_DIM))
            outs.setdefault('gmlp_v_s', []).append(zv_s.reshape(DEC_BATCH, DEC_SEQ, GMLP_WIDTH))
            mixed, mix_name, gate = (cat,), "ab_out", None
        else:
            (zqk, zv, zg), (wu_bf, wd_bf, wout_bf) = _ret_in_proj(
                h, w_ret_in[j], (ret_c, ret_s), w_ffn_up, w_ffn_down, w_ret_out, layer, j)
            y_p, s_p, y_s, s_s = _retention(zqk, zv, state_ret[j], lg)
            outs.setdefault('ret_p', []).append(s_p)
            outs.setdefault('ret_s', []).append(s_s)
            mixed, mix_name, gate = (y_p, y_s), "ret_out", zg
        x, h = _out_proj(mixed, wout_bf, x_parts, vec(norm_mix_post, layer), vec(norm_x_pre, layer), mix_name,
                         gate)
        x_parts, h_parts = (x,), (h,)
        g_post = vec(norm_x_post, layer)
        g_next = vec(norm_ffn_pre, layer)
        x, h = _mem_attention(h_parts, x_parts, wq_bf, wo_bf, mem_k_b, mem_v_b, ck_mem, cv_mem, layer,
                              g_post, g_next)
        g_next = vec(norm_mix_pre, layer + 1) if layer + 1 < DEPTH else None
        x, h = _ffn(h, wu_bf, wd_bf, x, vec(norm_ffn_post, layer), g_next)
        x_parts = (x,)

    y_prompt, y_sample = x, h
    stack = lambda name: jnp.stack(outs[name])
    mem_shape = (DEPTH, BATCH, MEM_LEN, MEM_HEADS, MEM_HEAD_DIM)
    return (y_prompt.reshape(BATCH, SEQ, D_MODEL), y_sample.reshape(DEC_BATCH, DEC_SEQ, D_MODEL),
            stack('swa_k_p'), stack('swa_v_p'), stack('swa_k_s'), stack('swa_v_s'), stack('gmlp_v_s'),
            stack('ret_p'), stack('ret_s'), mem_k.reshape(mem_shape), mem_v.reshape(mem_shape))
```
